```python
import jax
import jax.numpy as jnp
from jax import lax
import numpy as np

D_MODEL = 1024
BATCH = 8
SEQ = 2048
DEPTH = 4

GRID_W = 64
CTX_LEN = 256
HEAD_DIM = 64
NA_HEADS = 8
NA_WIN_ROWS = 8
NA_WIN_COLS = 16
GQA_Q_HEADS = 8
GQA_KV_HEADS = 2
Q_BLOCK = 128
ROPE_THETA = 10000.0
ROPE_AXIS_DIM = HEAD_DIM // 2
NA_DIM = NA_HEADS * HEAD_DIM
GQA_Q_DIM = GQA_Q_HEADS * HEAD_DIM
GQA_KV_DIM = GQA_KV_HEADS * HEAD_DIM
ATT_SPLITS = (NA_DIM, 2 * NA_DIM, 3 * NA_DIM, 3 * NA_DIM + GQA_Q_DIM, 3 * NA_DIM + GQA_Q_DIM + GQA_KV_DIM)
ATT_IN = 3 * NA_DIM + GQA_Q_DIM + 2 * GQA_KV_DIM
RWKV_HEADS = 8
RWKV_HEAD = 64
RWKV_DIM = RWKV_HEADS * RWKV_HEAD
DECAY_LORA = 32
ICLR_LORA = 32
GATE_LORA = 96
RWKV_GN_EPS = 64e-5
RWKV_IN = 4 * RWKV_DIM
MLSTM_HEADS = 4
MLSTM_HEAD = 128
MLSTM_DIM = MLSTM_HEADS * MLSTM_HEAD
MLSTM_CHUNK = 64
MLSTM_IN = 4 * MLSTM_DIM + 4 * MLSTM_HEADS
REC_IN = RWKV_IN + MLSTM_IN
D_MIX = NA_DIM + GQA_Q_DIM
N_EXPERTS = 64
TOP_K = 6
EXPERT_FF = 256
SHARED_FF = 256
ROUTED_SCALE = 2.5
MOE_BLOCK = 128
N_ATT_LAYERS = (DEPTH + 1) // 2
N_REC_LAYERS = DEPTH // 2
DN_ALPHA = (2 * DEPTH) ** 0.25
DN_BETA = (8 * DEPTH) ** -0.25
LN_EPS = 1e-5
NORM_EPS = 1e-6
F32 = jnp.float32

kernel_name = 'hybrid_na_gqa_rwkv7_mlstm_moe_dit'


def layer_norm(x, g, b):
    xf = x.astype(F32)
    mu = jnp.mean(xf, -1, keepdims=True)
    var = jnp.mean(jnp.square(xf - mu), -1, keepdims=True)
    return ((xf - mu) * lax.rsqrt(var + LN_EPS)).astype(x.dtype) * g + b


def rms_norm(x, g):
    xf = x.astype(F32)
    return (xf * lax.rsqrt(jnp.mean(jnp.square(xf), -1, keepdims=True) + NORM_EPS)).astype(x.dtype) * g


def swiglu(x, w1, w3, w2):
    return (jax.nn.silu(x @ w1) * (x @ w3)) @ w2


def centred_shift(x):
    xp = jnp.pad(x, ((0, 0), (1, 1), (0, 0)))
    return 0.5 * (xp[:, :-2] + xp[:, 2:])


def seg_flip(x, n_ctx, axis):
    a, b = jnp.split(x, [n_ctx], axis=axis)
    return jnp.concatenate([jnp.flip(a, axis), jnp.flip(b, axis)], axis=axis)


def both_dirs(x, n_ctx):
    return jnp.stack([x, seg_flip(x, n_ctx, 1)])


def own_dirs(x, n_ctx):
    return jnp.stack([x[0], seg_flip(x[1], n_ctx, 1)])


def axial_rope(n_tokens):
    t = jnp.arange(n_tokens)
    row = (t // GRID_W).astype(F32)
    col = (t % GRID_W).astype(F32)
    inv = ROPE_THETA ** (-jnp.arange(0, ROPE_AXIS_DIM, 2, dtype=F32) / ROPE_AXIS_DIM)
    ang = jnp.concatenate([row[:, None] * inv, col[:, None] * inv], -1)
    return jnp.cos(ang), jnp.sin(ang)


def apply_rope(x, cos, sin):
    xp = x.reshape(x.shape[:-1] + (HEAD_DIM // 2, 2))
    x0, x1 = xp[..., 0], xp[..., 1]
    c, s = cos[:, None, :], sin[:, None, :]
    return jnp.stack([x0 * c - x1 * s, x0 * s + x1 * c], -1).reshape(x.shape).astype(x.dtype)


def attend(q, k, v):
    B, Tq, Hq, Dh = q.shape
    hkv = k.shape[2]
    qg = q.reshape(B, Tq, hkv, Hq // hkv, Dh)
    s = jnp.einsum('bqhgd,bkhd->bhgqk', qg, k).astype(F32) * Dh ** -0.5
    p = jax.nn.softmax(s, -1).astype(v.dtype)
    return jnp.einsum('bhgqk,bkhd->bqhgd', p, v).reshape(B, Tq, Hq * Dh)


def blocked_attention(q, k, v):
    B, S, Hq, Dh = q.shape
    nb = S // Q_BLOCK
    qb = jnp.moveaxis(q.reshape(B, nb, Q_BLOCK, Hq, Dh), 1, 0)
    out = lax.map(lambda qi: attend(qi, k, v), qb)
    return jnp.moveaxis(out, 0, 1).reshape(B, S, Hq * Dh)


def neighbourhood_attention(q, k, v, k_ctx, v_ctx, rpb):
    B, S, H, Dh = q.shape
    rows = S // GRID_W
    kr, kc = min(NA_WIN_ROWS, rows), NA_WIN_COLS
    qg = q.reshape(B, rows, GRID_W, H, Dh)
    kg = k.reshape(B, rows, GRID_W, H, Dh)
    vg = v.reshape(B, rows, GRID_W, H, Dh)
    r = jnp.arange(rows)
    row_idx = jnp.clip(r - kr // 2, 0, rows - kr)[:, None] + jnp.arange(kr)[None, :]
    k_band = kg[:, row_idx]
    v_band = vg[:, row_idx]
    cidx = jnp.arange(GRID_W)
    col_start = jnp.clip(cidx - kc // 2, 0, GRID_W - kc)
    col_in = (cidx[None, :] >= col_start[:, None]) & (cidx[None, :] < col_start[:, None] + kc)
    d_row = row_idx - r[:, None] + NA_WIN_ROWS - 1
    d_col = jnp.clip(cidx[None, :] - cidx[:, None], -(kc - 1), kc - 1) + kc - 1
    bias = rpb[:, d_row[:, None, :, None], d_col[None, :, None, :]]
    scale = Dh ** -0.5
    s_lat = jnp.einsum('brqhd,brkwhd->bhrqkw', qg, k_band).astype(F32) * scale + bias
    s_lat = jnp.where(col_in[None, None, None, :, None, :], s_lat, -jnp.inf)
    s_ctx = jnp.einsum('brqhd,bchd->bhrqc', qg, k_ctx).astype(F32) * scale
    n_lat = kr * GRID_W
    s = jnp.concatenate([s_lat.reshape(B, H, rows, GRID_W, n_lat), s_ctx], -1)
    p = jax.nn.softmax(s, -1).astype(v.dtype)
    p_lat = p[..., :n_lat].reshape(B, H, rows, GRID_W, kr, GRID_W)
    out = jnp.einsum('bhrqkw,brkwhd->brqhd', p_lat, v_band) + jnp.einsum('bhrqc,bchd->brqhd', p[..., n_lat:], v_ctx)
    return out.reshape(B, S, H * Dh)


def attention_mixers(h_ctx, h_lat, w_in, rpb, qk_gain, keep_ctx):
    S = h_lat.shape[1]
    heads = lambda t, n: t.reshape(t.shape[:2] + (n, HEAD_DIM))

    def project(h):
        q_a, k_a, v_a, q_b, k_b, v_b = jnp.split(h @ w_in, ATT_SPLITS, axis=-1)
        return (heads(q_a, NA_HEADS), heads(k_a, NA_HEADS), heads(v_a, NA_HEADS),
                rms_norm(heads(q_b, GQA_Q_HEADS), qk_gain[0]), rms_norm(heads(k_b, GQA_KV_HEADS), qk_gain[1]),
                heads(v_b, GQA_KV_HEADS))

    qa_c, ka_c, va_c, qb_c, kb_c, vb_c = project(h_ctx)
    qa, ka, va, qb, kb, vb = project(h_lat)
    cos, sin = axial_rope(S)
    qb, kb = apply_rope(qb, cos, sin), apply_rope(kb, cos, sin)
    out_a = neighbourhood_attention(qa, ka, va, ka_c, va_c, rpb)
    out_b = blocked_attention(qb, jnp.concatenate([kb, kb_c], 1), jnp.concatenate([vb, vb_c], 1))
    out_lat = jnp.concatenate([out_a, out_b], -1)
    if not keep_ctx:
        return None, out_lat
    out_ctx = jnp.concatenate([attend(qa_c, ka_c, va_c), attend(qb_c, kb_c, vb_c)], -1)
    return out_ctx, out_lat


def rwkv7_step(state, inp):
    r, w, k, v, kk, a = inp
    sa = jnp.einsum('...vk,...k->...v', state, -kk)
    state = state * w[..., None, :] + sa[..., :, None] * (kk * a)[..., None, :] + v[..., :, None] * k[..., None, :]
    return state, jnp.einsum('...vk,...k->...v', state, r)


def rwkv7_bidirectional(p_ctx, p_lat, mu, w0, w1, w2, a0, a1, a2, g1, g2, kvec, r_k, gn, keep_ctx):
    n_ctx = p_ctx.shape[1]

    def shift_mix(p):
        d = centred_shift(p) - p
        (r, k, v, z), (dr, dk, dv, dz) = jnp.split(p, 4, -1), jnp.split(d, 4, -1)
        return r + dr * mu[0], k + dk * mu[1], v + dv * mu[2], z + dz * mu[3], z + dz * mu[4], z + dz * mu[5]

    r, k, v, z_w, z_a, z_g = [jnp.concatenate(pair, axis=1) for pair in zip(shift_mix(p_ctx), shift_mix(p_lat))]
    B, T, C = r.shape
    heads = lambda t: t.reshape(t.shape[:-1] + (RWKV_HEADS, RWKV_HEAD)).astype(F32)
    w_pre = w0[:, None, None] + jnp.einsum('dbtr,drc->dbtc', jnp.tanh(jnp.einsum('btc,dcr->dbtr', z_w, w1)), w2)
    decay = jnp.exp(-jnp.exp(-jax.nn.softplus(-w_pre.astype(F32)) - 0.5))
    iclr = jax.nn.sigmoid((a0[:, None, None] + jnp.einsum('dbtr,drc->dbtc', jnp.einsum('btc,dcr->dbtr', z_a, a1), a2)).astype(F32))
    gate = jax.nn.sigmoid(z_g @ g1) @ g2
    kk = heads(k * kvec[0])
    kk = kk * lax.rsqrt(jnp.maximum(jnp.sum(jnp.square(kk), -1, keepdims=True), 1e-24))
    k_eff = heads(k)[None] * (1.0 + (heads(iclr) - 1.0) * heads(kvec[1]))
    xs = (both_dirs(heads(r), n_ctx), own_dirs(heads(decay), n_ctx), own_dirs(k_eff, n_ctx),
          both_dirs(heads(v), n_ctx), both_dirs(kk, n_ctx), own_dirs(heads(iclr), n_ctx))
    xs = tuple(jnp.moveaxis(t, 2, 0) for t in xs)
    state0 = jnp.zeros((2, B, RWKV_HEADS, RWKV_HEAD, RWKV_HEAD), F32)
    _, y = lax.scan(rwkv7_step, state0, xs)
    y = own_dirs(jnp.moveaxis(y, 0, 2), n_ctx).sum(0)
    t0 = 0 if keep_ctx else n_ctx
    y, r_h, v_h, k_h, gate = y[:, t0:], heads(r)[:, t0:], heads(v)[:, t0:], k_eff[:, :, t0:], gate[:, t0:]
    mu_y = jnp.mean(y, -1, keepdims=True)
    var_y = jnp.mean(jnp.square(y - mu_y), -1, keepdims=True)
    y = (y - mu_y) * lax.rsqrt(var_y + RWKV_GN_EPS) * heads(gn[0]) + heads(gn[1])
    bonus = jnp.sum(r_h[None] * k_h * r_k, axis=-1, keepdims=True).sum(0) * v_h
    out = ((y + bonus).reshape(B, T - t0, C) * gate).astype(p_lat.dtype)
    if keep_ctx:
        return out[:, :n_ctx], out[:, n_ctx:]
    return None, out


def mlstm_chunk(carry, inp):
    c_state, n_state, m_state = carry
    q, k, v, ig, lf = inp
    L = q.shape[-2]
    in_order = jnp.tril(jnp.ones((L, L), bool))
    b = jnp.cumsum(lf, -1)
    d_intra = jnp.where(in_order, b[..., :, None] - b[..., None, :] + ig[..., None, :], -jnp.inf)
    d_inter = b + m_state[..., None]
    m_t = jnp.maximum(d_inter, jnp.max(d_intra, -1))
    s = jnp.einsum('...td,...sd->...ts', q, k) * jnp.exp(d_intra - m_t[..., None])
    w_inter = jnp.exp(d_inter - m_t)
    num = jnp.einsum('...ts,...sv->...tv', s, v) + w_inter[..., None] * jnp.einsum('...vd,...td->...tv', c_state, q)
    den = jnp.sum(s, -1) + w_inter * jnp.einsum('...d,...td->...t', n_state, q)
    h = num / jnp.maximum(jnp.abs(den), jnp.exp(-m_t))[..., None]
    b_end = b[..., -1]
    d_state = b_end[..., None] - b + ig
    m_new = jnp.maximum(b_end + m_state, jnp.max(d_state, -1))
    w_s = jnp.exp(d_state - m_new[..., None])
    w_c = jnp.exp(b_end + m_state - m_new)
    c_new = w_c[..., None, None] * c_state + jnp.einsum('...s,...sv,...sd->...vd', w_s, v, k)
    n_new = w_c[..., None] * n_state + jnp.einsum('...s,...sd->...d', w_s, k)
    return (c_new, n_new, m_new), h


def mlstm_bidirectional(p_ctx, p_lat, gate_b, norm_g, keep_ctx):
    n_ctx = p_ctx.shape[1]
    p = jnp.concatenate([p_ctx, p_lat], axis=1)
    B, T, _ = p.shape
    L, nc = MLSTM_CHUNK, T // MLSTM_CHUNK
    q, k, v, o, g = jnp.split(p, [MLSTM_DIM, 2 * MLSTM_DIM, 3 * MLSTM_DIM, 4 * MLSTM_DIM], axis=-1)
    heads = lambda t: t.reshape(B, T, MLSTM_HEADS, MLSTM_HEAD).astype(F32)
    g = g.reshape(B, T, 4, MLSTM_HEADS).astype(F32) + gate_b
    ig = jnp.stack([g[:, :, 0], seg_flip(g[:, :, 2], n_ctx, 1)])
    lf = jax.nn.log_sigmoid(jnp.stack([g[:, :, 1], seg_flip(g[:, :, 3], n_ctx, 1)]))

    def chunks(t):
        t = t.reshape((2, B, nc, L) + t.shape[3:])
        return jnp.moveaxis(jnp.moveaxis(t, 2, 0), 3, 4)

    xs = (chunks(both_dirs(heads(q) * MLSTM_HEAD ** -0.5, n_ctx)), chunks(both_dirs(heads(k), n_ctx)),
          chunks(both_dirs(heads(v), n_ctx)), chunks(ig), chunks(lf))
    carry0 = (jnp.zeros((2, B, MLSTM_HEADS, MLSTM_HEAD, MLSTM_HEAD), F32),
              jnp.zeros((2, B, MLSTM_HEADS, MLSTM_HEAD), F32), jnp.zeros((2, B, MLSTM_HEADS), F32))
    _, h = lax.scan(mlstm_chunk, carry0, xs)
    h = jnp.moveaxis(jnp.moveaxis(h, 4, 3), 0, 2).reshape(2, B, T, MLSTM_HEADS, MLSTM_HEAD)
    h = own_dirs(h, n_ctx).sum(0)
    t0 = 0 if keep_ctx else n_ctx
    h = rms_norm(h[:, t0:], norm_g.reshape(MLSTM_HEADS, MLSTM_HEAD))
    out = (h.reshape(B, T - t0, MLSTM_DIM) * jax.nn.sigmoid(o[:, t0:])).astype(p_lat.dtype)
    if keep_ctx:
        return out[:, :n_ctx], out[:, n_ctx:]
    return None, out


def recurrent_mixers(h_ctx, h_lat, w_in, mu, w0, w1, w2, a0, a1, a2, g1, g2, kvec, r_k, gn, gate_b, norm_g, keep_ctx):
    p_ctx, p_lat = h_ctx @ w_in, h_lat @ w_in
    rc, rl = rwkv7_bidirectional(p_ctx[..., :RWKV_IN], p_lat[..., :RWKV_IN], mu, w0, w1, w2, a0, a1, a2,
                                 g1, g2, kvec, r_k, gn, keep_ctx)
    mc, ml = mlstm_bidirectional(p_ctx[..., RWKV_IN:], p_lat[..., RWKV_IN:], gate_b, norm_g, keep_ctx)
    out_lat = jnp.concatenate([rl, ml], -1)
    if not keep_ctx:
        return None, out_lat
    return jnp.concatenate([rc, mc], -1), out_lat


def grouped_expert_ffn(xt, idx, gates, w1, w3, w2):
    n_tok, D = xt.shape
    n_assign = n_tok * TOP_K
    flat_e = idx.reshape(-1)
    flat_tok = jnp.repeat(jnp.arange(n_tok, dtype=jnp.int32), TOP_K)
    flat_g = gates.reshape(-1)
    order = jnp.argsort(flat_e)
    e_sorted = flat_e[order]
    counts = jnp.bincount(flat_e, length=N_EXPERTS)
    padded = (counts + MOE_BLOCK - 1) // MOE_BLOCK * MOE_BLOCK
    start = jnp.cumsum(counts) - counts
    p_end = jnp.cumsum(padded)
    dest = (p_end - padded)[e_sorted] + jnp.arange(n_assign) - start[e_sorted]
    n_blocks = -(-n_assign // MOE_BLOCK) + N_EXPERTS
    n_rows = n_blocks * MOE_BLOCK
    row_tok = jnp.full((n_rows,), n_tok, jnp.int32).at[dest].set(flat_tok[order])
    row_gate = jnp.zeros((n_rows,), gates.dtype).at[dest].set(flat_g[order])
    block_expert = jnp.minimum(jnp.searchsorted(p_end, jnp.arange(n_blocks) * MOE_BLOCK, side='right'), N_EXPERTS - 1)
    x_pad = jnp.concatenate([xt, jnp.zeros((1, D), xt.dtype)], 0)
    xb = x_pad[row_tok].reshape(n_blocks, MOE_BLOCK, D)
    yb = lax.map(lambda a: swiglu(a[0], w1[a[1]], w3[a[1]], w2[a[1]]), (xb, block_expert))
    y = yb.reshape(n_rows, D) * row_gate[:, None]
    return jax.ops.segment_sum(y, row_tok, num_segments=n_tok + 1)[:n_tok]


def moe_ffn(xt, router_w, router_b, w1, w3, w2, sw1, sw3, sw2):
    scores = jax.nn.sigmoid((xt @ router_w).astype(F32))
    _, idx = lax.top_k(scores + router_b.astype(F32), TOP_K)
    gates = jnp.take_along_axis(scores, idx, -1)
    gates = (gates / jnp.sum(gates, -1, keepdims=True) * ROUTED_SCALE).astype(xt.dtype)
    return grouped_expert_ffn(xt, idx, gates, w1, w3, w2) + swiglu(xt, sw1, sw3, sw2)


def setup_inputs(seed: int = 0) -> dict:
    key = jax.random.key(seed)
    ks = iter(jax.random.split(key, 40))
    nrm = lambda shape, scale: scale * jax.random.normal(next(ks), shape, F32)
    D = D_MODEL
    return {
        'x': nrm((BATCH, SEQ, D), 1.0),
        'c': nrm((BATCH, D), 1.0),
        'ctx': nrm((BATCH, CTX_LEN, D), 1.0),
        'c_ctx': nrm((D,), 1.0),
        'ada_w': nrm((DEPTH, D, 6 * D), 0.5 * D ** -0.5),
        'ada_b': nrm((DEPTH, 6 * D), 0.02),
        'ln_g': 1.0 + nrm((DEPTH, 2, D), 0.02),
        'ln_b': nrm((DEPTH, 2, D), 0.02),
        'mix_w_out': nrm((DEPTH, D_MIX, D), DN_BETA * D_MIX ** -0.5),
        'att_w_in': nrm((N_ATT_LAYERS, D, ATT_IN), D ** -0.5),
        'na_rpb': nrm((N_ATT_LAYERS, NA_HEADS, 2 * NA_WIN_ROWS - 1, 2 * NA_WIN_COLS - 1), 0.1),
        'qk_gain': 1.0 + nrm((N_ATT_LAYERS, 2, HEAD_DIM), 0.02),
        'rec_w_in': nrm((N_REC_LAYERS, D, REC_IN), D ** -0.5),
        'rwkv_mu': jax.random.uniform(next(ks), (N_REC_LAYERS, 6, RWKV_DIM), F32),
        'rwkv_w0': jax.random.uniform(next(ks), (N_REC_LAYERS, 2, RWKV_DIM), F32, minval=-5.5, maxval=0.5),
        'rwkv_w1': nrm((N_REC_LAYERS, 2, RWKV_DIM, DECAY_LORA), RWKV_DIM ** -0.5),
        'rwkv_w2': nrm((N_REC_LAYERS, 2, DECAY_LORA, RWKV_DIM), 0.5 * DECAY_LORA ** -0.5),
        'rwkv_a0': nrm((N_REC_LAYERS, 2, RWKV_DIM), 0.1),
        'rwkv_a1': nrm((N_REC_LAYERS, 2, RWKV_DIM, ICLR_LORA), RWKV_DIM ** -0.5),
        'rwkv_a2': nrm((N_REC_LAYERS, 2, ICLR_LORA, RWKV_DIM), ICLR_LORA ** -0.5),
        'rwkv_g1': nrm((N_REC_LAYERS, RWKV_DIM, GATE_LORA), RWKV_DIM ** -0.5),
        'rwkv_g2': nrm((N_REC_LAYERS, GATE_LORA, RWKV_DIM), GATE_LORA ** -0.5),
        'rwkv_kvec': jnp.array([0.85, 1.0], F32)[None, :, None] + nrm((N_REC_LAYERS, 2, RWKV_DIM), 0.05),
        'rwkv_rk': nrm((N_REC_LAYERS, RWKV_HEADS, RWKV_HEAD), 0.1),
        'rwkv_gn': jnp.array([1.0, 0.0], F32)[None, :, None] + nrm((N_REC_LAYERS, 2, RWKV_DIM), 0.02),
        'mlstm_gate_b': jnp.array([-2.0, 3.0, -2.0, 3.0], F32)[None, :, None] + nrm((N_REC_LAYERS, 4, MLSTM_HEADS), 0.3),
        'mlstm_norm': 1.0 + nrm((N_REC_LAYERS, MLSTM_DIM), 0.02),
        'moe_router': nrm((DEPTH, D, N_EXPERTS), D ** -0.5),
        'moe_bias': nrm((DEPTH, N_EXPERTS), 0.01),
        'moe_w1': nrm((DEPTH, N_EXPERTS, D, EXPERT_FF), D ** -0.5),
        'moe_w3': nrm((DEPTH, N_EXPERTS, D, EXPERT_FF), D ** -0.5),
        'moe_w2': nrm((DEPTH, N_EXPERTS, EXPERT_FF, D), DN_BETA * EXPERT_FF ** -0.5),
        'shared_w1': nrm((DEPTH, D, SHARED_FF), D ** -0.5),
        'shared_w3': nrm((DEPTH, D, SHARED_FF), D ** -0.5),
        'shared_w2': nrm((DEPTH, SHARED_FF, D), DN_BETA * SHARED_FF ** -0.5),
    }


def reference(x, c, ctx, c_ctx, ada_w, ada_b, ln_g, ln_b, mix_w_out, att_w_in, na_rpb, qk_gain,
              rec_w_in, rwkv_mu, rwkv_w0, rwkv_w1, rwkv_w2, rwkv_a0, rwkv_a1, rwkv_a2, rwkv_g1, rwkv_g2,
              rwkv_kvec, rwkv_rk, rwkv_gn, mlstm_gate_b, mlstm_norm, moe_router, moe_bias, moe_w1, moe_w3,
              moe_w2, shared_w1, shared_w3, shared_w2):
    B, S, D = x.shape
    xc = ctx
    for i in range(DEPTH):
        keep_ctx = i < DEPTH - 1
        j = i // 2
        mod = jax.nn.silu(c) @ ada_w[i] + ada_b[i]
        mod_c = jax.nn.silu(c_ctx) @ ada_w[i] + ada_b[i]
        sh1, sc1, g1, sh2, sc2, g2 = [t[:, None] for t in jnp.split(mod, 6, -1)]
        sh1c, sc1c, g1c, sh2c, sc2c, g2c = jnp.split(mod_c, 6, -1)
        h_lat = x * (1.0 + sc1) + sh1
        h_ctx = xc * (1.0 + sc1c) + sh1c
        if i % 2 == 0:
            m_ctx, m_lat = attention_mixers(h_ctx, h_lat, att_w_in[j], na_rpb[j], qk_gain[j], keep_ctx)
        else:
            m_ctx, m_lat = recurrent_mixers(h_ctx, h_lat, rec_w_in[j], rwkv_mu[j], rwkv_w0[j], rwkv_w1[j], rwkv_w2[j],
                                            rwkv_a0[j], rwkv_a1[j], rwkv_a2[j], rwkv_g1[j], rwkv_g2[j], rwkv_kvec[j],
                                            rwkv_rk[j], rwkv_gn[j], mlstm_gate_b[j], mlstm_norm[j], keep_ctx)
        x = layer_norm(DN_ALPHA * x + g1 * (m_lat @ mix_w_out[i]), ln_g[i, 0], ln_b[i, 0])
        h_lat = x * (1.0 + sc2) + sh2
        if keep_ctx:
            xc = layer_norm(DN_ALPHA * xc + g1c * (m_ctx @ mix_w_out[i]), ln_g[i, 0], ln_b[i, 0])
            h_ctx = xc * (1.0 + sc2c) + sh2c
            tokens = jnp.concatenate([h_lat.reshape(-1, D), h_ctx.reshape(-1, D)], 0)
        else:
            tokens = h_lat.reshape(-1, D)
        y = moe_ffn(tokens, moe_router[i], moe_bias[i], moe_w1[i], moe_w3[i], moe_w2[i],
                    shared_w1[i], shared_w3[i], shared_w2[i])
        x = layer_norm(DN_ALPHA * x + g2 * y[:B * S].reshape(B, S, D), ln_g[i, 1], ln_b[i, 1])
        if keep_ctx:
            xc = layer_norm(DN_ALPHA * xc + g2c * y[B * S:].reshape(xc.shape), ln_g[i, 1], ln_b[i, 1])
    return x
```

```python
import functools

import jax
import jax.numpy as jnp
from jax import lax
from jax.experimental import pallas as pl
from jax.experimental.pallas import tpu as pltpu

D_MODEL = 1024
DEPTH = 4
GRID_W = 64
HEAD_DIM = 64
NA_HEADS = 8
NA_WIN_ROWS = 8
NA_WIN_COLS = 16
GQA_Q_HEADS = 8
GQA_KV_HEADS = 2
Q_BLOCK = 128
ROPE_THETA = 10000.0
ROPE_AXIS_DIM = HEAD_DIM // 2
NA_DIM = NA_HEADS * HEAD_DIM
GQA_Q_DIM = GQA_Q_HEADS * HEAD_DIM
GQA_KV_DIM = GQA_KV_HEADS * HEAD_DIM
ATT_SPLITS = (NA_DIM, 2 * NA_DIM, 3 * NA_DIM, 3 * NA_DIM + GQA_Q_DIM, 3 * NA_DIM + GQA_Q_DIM + GQA_KV_DIM)
RWKV_HEADS = 8
RWKV_HEAD = 64
RWKV_DIM = RWKV_HEADS * RWKV_HEAD
RWKV_GN_EPS = 64e-5
RWKV_IN = 4 * RWKV_DIM
MLSTM_HEADS = 4
MLSTM_HEAD = 128
MLSTM_DIM = MLSTM_HEADS * MLSTM_HEAD
MLSTM_CHUNK = 64
N_EXPERTS = 64
TOP_K = 6
ROUTED_SCALE = 2.5
DN_ALPHA = (2 * DEPTH) ** 0.25
LN_EPS = 1e-5
NORM_EPS = 1e-6
F32 = jnp.float32
BF16 = jnp.bfloat16

LANES = 128
RWKV_TIME_BLOCK = 16
MOE_TOKEN_TILE = 256
MOE_ROW_BLOCK = 256
MOE_SLOTS = 8
VMEM_LIMIT = 48 * 1024 * 1024


def _rwkv_scan_body(rf, wf, kf, vf, nf, bf, rb, wb, kb, vb, nb, bb, yf_ref, yb_ref, state_ref, *, tc):
    @pl.when(pl.program_id(0) == 0)
    def _():
        state_ref[...] = jnp.zeros_like(state_ref)

    fwd_lane = lax.broadcasted_iota(jnp.int32, (RWKV_HEAD, LANES), 1) < LANES // 2

    def step(j, carry):
        jb = tc - 1 - j

        def sel(f, b):
            return jnp.where(fwd_lane, f[j], b[jb])

        r, w, k, v, kkn, bv = sel(rf, rb), sel(wf, wb), sel(kf, kb), sel(vf, vb), sel(nf, nb), sel(bf, bb)
        for vi in range(RWKV_HEAD):
            s = state_ref[vi]
            sa = jnp.sum(s * kkn, axis=0, keepdims=True)
            s2 = s * w + sa * bv + v[vi:vi + 1, :] * k
            state_ref[vi] = s2
            yrow = jnp.sum(s2 * r, axis=0, keepdims=True)
            yf_ref[j, pl.ds(vi, 1), :] = yrow
            yb_ref[jb, pl.ds(vi, 1), :] = yrow
        return carry

    lax.fori_loop(0, tc, step, 0)


def _rwkv_scan(xs, n_ctx):
    T = xs[0].shape[0]
    tc = RWKV_TIME_BLOCK
    assert T % tc == 0 and n_ctx % tc == 0
    nc, ncc = T // tc, n_ctx // tc
    blk = (tc, RWKV_HEAD, LANES)
    fwd = lambda c: (c, 0, 0)
    bwd = lambda c: (jnp.where(c < ncc, ncc - 1 - c, nc - 1 - (c - ncc)), 0, 0)
    out_sds = jax.ShapeDtypeStruct((T, RWKV_HEAD, LANES), F32)
    return pl.pallas_call(
        functools.partial(_rwkv_scan_body, tc=tc),
        out_shape=(out_sds, out_sds),
        grid=(nc,),
        in_specs=[pl.BlockSpec(blk, fwd)] * 6 + [pl.BlockSpec(blk, bwd)] * 6,
        out_specs=(pl.BlockSpec(blk, fwd), pl.BlockSpec(blk, bwd)),
        scratch_shapes=[pltpu.VMEM((RWKV_HEAD, RWKV_HEAD, LANES), F32)],
        compiler_params=pltpu.CompilerParams(dimension_semantics=("arbitrary",), vmem_limit_bytes=VMEM_LIMIT),
        name="rwkv_scan",
    )(*xs, *xs)


def _router_body(x_ref, rw_ref, rb_ref, tri_ref, idx_ref, rank_ref, gate_ref, cnt_ref, run_ref):
    @pl.when(pl.program_id(0) == 0)
    def _():
        run_ref[...] = jnp.zeros_like(run_ref)

    tm = x_ref.shape[0]
    logits = jnp.dot(x_ref[...], rw_ref[...], preferred_element_type=F32, precision=lax.Precision.HIGHEST)
    scores = jax.nn.sigmoid(logits)
    sel = scores + rb_ref[...]
    lane = lax.broadcasted_iota(jnp.int32, (tm, N_EXPERTS), 1)
    slot = lax.broadcasted_iota(jnp.int32, (tm, MOE_SLOTS), 1)
    idx8 = jnp.zeros((tm, MOE_SLOTS), jnp.int32)
    onehots = []
    for j in range(TOP_K):
        m = jnp.max(sel, axis=-1, keepdims=True)
        ij = jnp.min(jnp.where(sel == m, lane, N_EXPERTS), axis=-1, keepdims=True)
        oh = lane == ij
        onehots.append(oh)
        sel = jnp.where(oh, -jnp.inf, sel)
        idx8 = jnp.where(slot == j, ij, idx8)
    mask = onehots[0]
    for oh in onehots[1:]:
        mask = mask | oh
    maskf = jnp.where(mask, 1.0, 0.0)
    gsum = jnp.sum(jnp.where(mask, scores, 0.0), axis=-1, keepdims=True)
    gates = scores / gsum * ROUTED_SCALE
    within = jnp.dot(tri_ref[...], maskf.astype(BF16), preferred_element_type=F32)
    rank = within + run_ref[...]
    run_ref[...] += jnp.sum(maskf, axis=0, keepdims=True)
    rank8 = jnp.zeros((tm, MOE_SLOTS), F32)
    gate8 = jnp.zeros((tm, MOE_SLOTS), F32)
    for j in range(TOP_K):
        rj = jnp.sum(jnp.where(onehots[j], rank, 0.0), axis=-1, keepdims=True)
        gj = jnp.sum(jnp.where(onehots[j], gates, 0.0), axis=-1, keepdims=True)
        rank8 = jnp.where(slot == j, rj, rank8)
        gate8 = jnp.where(slot == j, gj, gate8)
    idx_ref[...] = idx8
    rank_ref[...] = rank8.astype(jnp.int32)
    gate_ref[...] = gate8
    cnt_ref[...] = run_ref[...]


def _moe_route(tokens, router_w, router_b):
    n, d = tokens.shape
    tm = MOE_TOKEN_TILE
    tri = (jnp.arange(tm)[:, None] > jnp.arange(tm)[None, :]).astype(BF16)
    slots = lambda dt: jax.ShapeDtypeStruct((n, MOE_SLOTS), dt)
    return pl.pallas_call(
        _router_body,
        out_shape=(slots(jnp.int32), slots(jnp.int32), slots(F32), jax.ShapeDtypeStruct((1, N_EXPERTS), F32)),
        grid=(n // tm,),
        in_specs=[pl.BlockSpec((tm, d), lambda i: (i, 0)),
                  pl.BlockSpec((d, N_EXPERTS), lambda i: (0, 0)),
                  pl.BlockSpec((1, N_EXPERTS), lambda i: (0, 0)),
                  pl.BlockSpec((tm, tm), lambda i: (0, 0))],
        out_specs=(pl.BlockSpec((tm, MOE_SLOTS), lambda i: (i, 0)),
                   pl.BlockSpec((tm, MOE_SLOTS), lambda i: (i, 0)),
                   pl.BlockSpec((tm, MOE_SLOTS), lambda i: (i, 0)),
                   pl.BlockSpec((1, N_EXPERTS), lambda i: (0, 0))),
        scratch_shapes=[pltpu.VMEM((1, N_EXPERTS), F32)],
        compiler_params=pltpu.CompilerParams(dimension_semantics=("arbitrary",), vmem_limit_bytes=VMEM_LIMIT),
        name="moe_router",
    )(tokens, router_w, router_b.reshape(1, N_EXPERTS), tri)


def _load_tables(i, idx_hbm, rank_hbm, idx_s, rank_s, sem_in):
    c1 = pltpu.make_async_copy(idx_hbm.at[i], idx_s, sem_in.at[0])
    c2 = pltpu.make_async_copy(rank_hbm.at[i], rank_s, sem_in.at[1])
    c1.start()
    c2.start()
    c1.wait()
    c2.wait()


def _dispatch_body(offs_ref, bv_ref, x_ref, idx_hbm, rank_hbm, xs_hbm, idx_s, rank_s, zero_ref, sem_in, sem):
    tm = x_ref.shape[0]
    br = zero_ref.shape[0]
    n_blocks = xs_hbm.shape[0] // br

    @pl.when(pl.program_id(0) == 0)
    def _():
        zero_ref[...] = jnp.zeros_like(zero_ref)

        def fill_copy(b):
            return pltpu.make_async_copy(zero_ref, xs_hbm.at[pl.ds(b * br, br)], sem_in.at[0])

        def fill(b, carry):
            @pl.when(bv_ref[b] < br)
            def _():
                fill_copy(b).start()
            return carry

        def fill_wait(b, carry):
            @pl.when(bv_ref[b] < br)
            def _():
                fill_copy(b).wait()
            return carry

        lax.fori_loop(0, n_blocks, fill, 0)
        lax.fori_loop(0, n_blocks, fill_wait, 0)

    _load_tables(pl.program_id(0), idx_hbm, rank_hbm, idx_s, rank_s, sem_in)

    def row_copy(t, d):
        return pltpu.make_async_copy(x_ref.at[pl.ds(t, 1)], xs_hbm.at[pl.ds(d, 1)], sem)

    def issue(t, carry):
        for j in range(TOP_K):
            d = offs_ref[idx_s[t * MOE_SLOTS + j]] + rank_s[t * MOE_SLOTS + j]
            row_copy(t, d).start()
        return carry

    lax.fori_loop(0, tm, issue, 0)

    def drain(t, carry):
        for j in range(TOP_K):
            row_copy(0, 0).wait()
        return carry

    lax.fori_loop(0, tm, drain, 0)


def _moe_dispatch(tokens, offs, block_valid, idx_t, rank_t):
    n, d = tokens.shape
    tm, br = MOE_TOKEN_TILE, MOE_ROW_BLOCK
    n_rows = block_valid.shape[0] * br
    return pl.pallas_call(
        _dispatch_body,
        out_shape=jax.ShapeDtypeStruct((n_rows, d), tokens.dtype),
        grid_spec=pltpu.PrefetchScalarGridSpec(
            num_scalar_prefetch=2,
            grid=(n // tm,),
            in_specs=[pl.BlockSpec((tm, d), lambda i, offs, bv: (i, 0)),
                      pl.BlockSpec(memory_space=pl.ANY),
                      pl.BlockSpec(memory_space=pl.ANY)],
            out_specs=pl.BlockSpec(memory_space=pl.ANY),
            scratch_shapes=[pltpu.SMEM((tm * MOE_SLOTS,), jnp.int32), pltpu.SMEM((tm * MOE_SLOTS,), jnp.int32),
                            pltpu.VMEM((br, d), tokens.dtype),
                            pltpu.SemaphoreType.DMA((2,)), pltpu.SemaphoreType.DMA]),
        compiler_params=pltpu.CompilerParams(dimension_semantics=("arbitrary",), vmem_limit_bytes=VMEM_LIMIT),
        name="moe_dispatch",
    )(offs, block_valid, tokens, idx_t, rank_t)


def _expert_body(be_ref, bv_ref, x_ref, w1_ref, w3_ref, w2_ref, y_ref):
    valid = bv_ref[pl.program_id(0)]

    @pl.when(valid > 0)
    def _():
        row = lax.broadcasted_iota(jnp.int32, (x_ref.shape[0], 1), 0)
        x = jnp.where(row < valid, x_ref[...], 0.0).astype(BF16)
        h1 = jnp.dot(x, w1_ref[0], preferred_element_type=F32)
        h3 = jnp.dot(x, w3_ref[0], preferred_element_type=F32)
        a = (h1 * jax.nn.sigmoid(h1) * h3).astype(BF16)
        y_ref[...] = jnp.dot(a, w2_ref[0], preferred_element_type=F32)

    @pl.when(valid <= 0)
    def _():
        y_ref[...] = jnp.zeros_like(y_ref)


def _moe_experts(xs, block_expert, block_valid, w1, w3, w2):
    n_rows, d = xs.shape
    br = MOE_ROW_BLOCK
    ff = w1.shape[-1]
    return pl.pallas_call(
        _expert_body,
        out_shape=jax.ShapeDtypeStruct((n_rows, d), F32),
        grid_spec=pltpu.PrefetchScalarGridSpec(
            num_scalar_prefetch=2,
            grid=(n_rows // br,),
            in_specs=[pl.BlockSpec((br, d), lambda i, be, bv: (i, 0)),
                      pl.BlockSpec((1, d, ff), lambda i, be, bv: (be[i], 0, 0)),
                      pl.BlockSpec((1, d, ff), lambda i, be, bv: (be[i], 0, 0)),
                      pl.BlockSpec((1, ff, d), lambda i, be, bv: (be[i], 0, 0))],
            out_specs=pl.BlockSpec((br, d), lambda i, be, bv: (i, 0))),
        compiler_params=pltpu.CompilerParams(dimension_semantics=("arbitrary",), vmem_limit_bytes=VMEM_LIMIT),
        name="moe_experts",
    )(block_expert, block_valid, xs, w1, w3, w2)


def _combine_body(offs_ref, x_ref, gate_ref, idx_hbm, rank_hbm, ys_hbm, sw1_ref, sw3_ref, sw2_ref, out_ref,
                  idx_s, rank_s, buf, sem_in, sem):
    tm = x_ref.shape[0]
    _load_tables(pl.program_id(0), idx_hbm, rank_hbm, idx_s, rank_s, sem_in)

    def row_copy(t, j, d):
        return pltpu.make_async_copy(ys_hbm.at[pl.ds(d, 1)], buf.at[j, pl.ds(t, 1)], sem)

    def issue(t, carry):
        for j in range(TOP_K):
            d = offs_ref[idx_s[t * MOE_SLOTS + j]] + rank_s[t * MOE_SLOTS + j]
            row_copy(t, j, d).start()
        return carry

    lax.fori_loop(0, tm, issue, 0)

    x = x_ref[...].astype(BF16)
    h1 = jnp.dot(x, sw1_ref[...], preferred_element_type=F32)
    h3 = jnp.dot(x, sw3_ref[...], preferred_element_type=F32)
    a = (h1 * jax.nn.sigmoid(h1) * h3).astype(BF16)
    acc = jnp.dot(a, sw2_ref[...], preferred_element_type=F32)

    def drain(t, carry):
        for j in range(TOP_K):
            row_copy(0, j, 0).wait()
        return carry

    lax.fori_loop(0, tm, drain, 0)

    g = gate_ref[...]
    for j in range(TOP_K):
        acc = acc + g[:, j:j + 1] * buf[j]
    out_ref[...] = acc


def _moe_combine(tokens, offs, gate8, idx_t, rank_t, ys, sw1, sw3, sw2):
    n, d = tokens.shape
    tm = MOE_TOKEN_TILE
    ff = sw1.shape[-1]
    whole = lambda shape: pl.BlockSpec(shape, lambda i, offs: (0,) * len(shape))
    return pl.pallas_call(
        _combine_body,
        out_shape=jax.ShapeDtypeStruct((n, d), F32),
        grid_spec=pltpu.PrefetchScalarGridSpec(
            num_scalar_prefetch=1,
            grid=(n // tm,),
            in_specs=[pl.BlockSpec((tm, d), lambda i, offs: (i, 0)),
                      pl.BlockSpec((tm, MOE_SLOTS), lambda i, offs: (i, 0)),
                      pl.BlockSpec(memory_space=pl.ANY),
                      pl.BlockSpec(memory_space=pl.ANY),
                      pl.BlockSpec(memory_space=pl.ANY),
                      whole((d, ff)), whole((d, ff)), whole((ff, d))],
            out_specs=pl.BlockSpec((tm, d), lambda i, offs: (i, 0)),
            scratch_shapes=[pltpu.SMEM((tm * MOE_SLOTS,), jnp.int32), pltpu.SMEM((tm * MOE_SLOTS,), jnp.int32),
                            pltpu.VMEM((TOP_K, tm, d), F32),
                            pltpu.SemaphoreType.DMA((2,)), pltpu.SemaphoreType.DMA]),
        compiler_params=pltpu.CompilerParams(dimension_semantics=("arbitrary",), vmem_limit_bytes=VMEM_LIMIT),
        name="moe_combine",
    )(offs, tokens, gate8, idx_t, rank_t, ys, sw1, sw3, sw2)


def _moe_ffn(tokens, router_w, router_b, w1, w3, w2, sw1, sw3, sw2):
    n, d = tokens.shape
    tm, br = MOE_TOKEN_TILE, MOE_ROW_BLOCK
    assert n % tm == 0
    idx8, rank8, gate8, counts = _moe_route(tokens, router_w, router_b)
    counts = counts.reshape(N_EXPERTS).astype(jnp.int32)
    padded = (counts + br - 1) // br * br
    p_end = jnp.cumsum(padded)
    offs = (p_end - padded).astype(jnp.int32)
    n_blocks = -(-(n * TOP_K + N_EXPERTS * (br - 1)) // br)
    blk_start = jnp.arange(n_blocks, dtype=jnp.int32) * br
    block_expert = jnp.minimum(jnp.searchsorted(p_end, blk_start, side='right'), N_EXPERTS - 1).astype(jnp.int32)
    block_valid = jnp.clip(counts[block_expert] - (blk_start - offs[block_expert]), 0, br).astype(jnp.int32)
    idx_t = idx8.reshape(n // tm, tm * MOE_SLOTS)
    rank_t = rank8.reshape(n // tm, tm * MOE_SLOTS)
    xs = _moe_dispatch(tokens, offs, block_valid, idx_t, rank_t)
    ys = _moe_experts(xs, block_expert, block_valid, w1.astype(BF16), w3.astype(BF16), w2.astype(BF16))
    return _moe_combine(tokens, offs, gate8, idx_t, rank_t, ys, sw1.astype(BF16), sw3.astype(BF16), sw2.astype(BF16))


def _layer_norm(x, g, b):
    mu = jnp.mean(x, -1, keepdims=True)
    var = jnp.mean(jnp.square(x - mu), -1, keepdims=True)
    return (x - mu) * lax.rsqrt(var + LN_EPS) * g + b


def _rms_norm(x, g):
    return x * lax.rsqrt(jnp.mean(jnp.square(x), -1, keepdims=True) + NORM_EPS) * g


def _swiglu(x, w1, w3, w2):
    return (jax.nn.silu(x @ w1) * (x @ w3)) @ w2


def _centred_shift(x):
    xp = jnp.pad(x, ((0, 0), (1, 1), (0, 0)))
    return 0.5 * (xp[:, :-2] + xp[:, 2:])


def _seg_flip(x, n_ctx, axis):
    a, b = jnp.split(x, [n_ctx], axis=axis)
    return jnp.concatenate([jnp.flip(a, axis), jnp.flip(b, axis)], axis=axis)


def _both_dirs(x, n_ctx):
    return jnp.stack([x, _seg_flip(x, n_ctx, 1)])


def _own_dirs(x, n_ctx):
    return jnp.stack([x[0], _seg_flip(x[1], n_ctx, 1)])


def _axial_rope(n_tokens):
    t = jnp.arange(n_tokens)
    row = (t // GRID_W).astype(F32)
    col = (t % GRID_W).astype(F32)
    inv = ROPE_THETA ** (-jnp.arange(0, ROPE_AXIS_DIM, 2, dtype=F32) / ROPE_AXIS_DIM)
    ang = jnp.concatenate([row[:, None] * inv, col[:, None] * inv], -1)
    return jnp.cos(ang), jnp.sin(ang)


def _apply_rope(x, cos, sin):
    xp = x.reshape(x.shape[:-1] + (HEAD_DIM // 2, 2))
    x0, x1 = xp[..., 0], xp[..., 1]
    c, s = cos[:, None, :], sin[:, None, :]
    return jnp.stack([x0 * c - x1 * s, x0 * s + x1 * c], -1).reshape(x.shape)


def _attend(q, k, v):
    B, Tq, Hq, Dh = q.shape
    hkv = k.shape[2]
    qg = q.reshape(B, Tq, hkv, Hq // hkv, Dh)
    s = jnp.einsum('bqhgd,bkhd->bhgqk', qg, k) * Dh ** -0.5
    p = jax.nn.softmax(s, -1)
    return jnp.einsum('bhgqk,bkhd->bqhgd', p, v).reshape(B, Tq, Hq * Dh)


def _blocked_attention(q, k, v):
    B, S, Hq, Dh = q.shape
    nb = S // Q_BLOCK
    qb = jnp.moveaxis(q.reshape(B, nb, Q_BLOCK, Hq, Dh), 1, 0)
    out = lax.map(lambda qi: _attend(qi, k, v), qb)
    return jnp.moveaxis(out, 0, 1).reshape(B, S, Hq * Dh)


def _neighbourhood_attention(q, k, v, k_ctx, v_ctx, rpb):
    B, S, H, Dh = q.shape
    rows = S // GRID_W
    kr, kc = min(NA_WIN_ROWS, rows), NA_WIN_COLS
    qg = q.reshape(B, rows, GRID_W, H, Dh)
    kg = k.reshape(B, rows, GRID_W, H, Dh)
    vg = v.reshape(B, rows, GRID_W, H, Dh)
    r = jnp.arange(rows)
    row_idx = jnp.clip(r - kr // 2, 0, rows - kr)[:, None] + jnp.arange(kr)[None, :]
    k_band = kg[:, row_idx]
    v_band = vg[:, row_idx]
    cidx = jnp.arange(GRID_W)
    col_start = jnp.clip(cidx - kc // 2, 0, GRID_W - kc)
    col_in = (cidx[None, :] >= col_start[:, None]) & (cidx[None, :] < col_start[:, None] + kc)
    d_row = row_idx - r[:, None] + NA_WIN_ROWS - 1
    d_col = jnp.clip(cidx[None, :] - cidx[:, None], -(kc - 1), kc - 1) + kc - 1
    bias = rpb[:, d_row[:, None, :, None], d_col[None, :, None, :]]
    scale = Dh ** -0.5
    s_lat = jnp.einsum('brqhd,brkwhd->bhrqkw', qg, k_band) * scale + bias
    s_lat = jnp.where(col_in[None, None, None, :, None, :], s_lat, -jnp.inf)
    s_ctx = jnp.einsum('brqhd,bchd->bhrqc', qg, k_ctx) * scale
    n_lat = kr * GRID_W
    s = jnp.concatenate([s_lat.reshape(B, H, rows, GRID_W, n_lat), s_ctx], -1)
    p = jax.nn.softmax(s, -1)
    p_lat = p[..., :n_lat].reshape(B, H, rows, GRID_W, kr, GRID_W)
    out = jnp.einsum('bhrqkw,brkwhd->brqhd', p_lat, v_band) + jnp.einsum('bhrqc,bchd->brqhd', p[..., n_lat:], v_ctx)
    return out.reshape(B, S, H * Dh)


def _attention_mixers(h_ctx, h_lat, w_in, rpb, qk_gain, keep_ctx):
    S = h_lat.shape[1]
    heads = lambda t, n: t.reshape(t.shape[:2] + (n, HEAD_DIM))

    def project(h):
        q_a, k_a, v_a, q_b, k_b, v_b = jnp.split(h @ w_in, ATT_SPLITS, axis=-1)
        return (heads(q_a, NA_HEADS), heads(k_a, NA_HEADS), heads(v_a, NA_HEADS),
                _rms_norm(heads(q_b, GQA_Q_HEADS), qk_gain[0]), _rms_norm(heads(k_b, GQA_KV_HEADS), qk_gain[1]),
                heads(v_b, GQA_KV_HEADS))

    qa_c, ka_c, va_c, qb_c, kb_c, vb_c = project(h_ctx)
    qa, ka, va, qb, kb, vb = project(h_lat)
    cos, sin = _axial_rope(S)
    qb, kb = _apply_rope(qb, cos, sin), _apply_rope(kb, cos, sin)
    out_a = _neighbourhood_attention(qa, ka, va, ka_c, va_c, rpb)
    out_b = _blocked_attention(qb, jnp.concatenate([kb, kb_c], 1), jnp.concatenate([vb, vb_c], 1))
    out_lat = jnp.concatenate([out_a, out_b], -1)
    if not keep_ctx:
        return None, out_lat
    out_ctx = jnp.concatenate([_attend(qa_c, ka_c, va_c), _attend(qb_c, kb_c, vb_c)], -1)
    return out_ctx, out_lat


def _to_state_lanes(x):
    nd, B, T, _ = x.shape
    y = jnp.transpose(x.reshape(nd, B, T, RWKV_HEADS, RWKV_HEAD), (2, 4, 0, 1, 3)).reshape(T, RWKV_HEAD, nd * B * RWKV_HEADS)
    return jnp.concatenate([y, y], -1) if nd == 1 else y


def _rwkv7_bidirectional(p_ctx, p_lat, mu, w0, w1, w2, a0, a1, a2, g1, g2, kvec, r_k, gn, keep_ctx):
    n_ctx = p_ctx.shape[1]

    def shift_mix(p):
        d = _centred_shift(p) - p
        (r, k, v, z), (dr, dk, dv, dz) = jnp.split(p, 4, -1), jnp.split(d, 4, -1)
        return r + dr * mu[0], k + dk * mu[1], v + dv * mu[2], z + dz * mu[3], z + dz * mu[4], z + dz * mu[5]

    r, k, v, z_w, z_a, z_g = [jnp.concatenate(pair, axis=1) for pair in zip(shift_mix(p_ctx), shift_mix(p_lat))]
    B, T, C = r.shape
    assert 2 * B * RWKV_HEADS == LANES
    heads = lambda t: t.reshape(t.shape[:-1] + (RWKV_HEADS, RWKV_HEAD))
    w_pre = w0[:, None, None] + jnp.einsum('dbtr,drc->dbtc', jnp.tanh(jnp.einsum('btc,dcr->dbtr', z_w, w1)), w2)
    decay = jnp.exp(-jnp.exp(-jax.nn.softplus(-w_pre) - 0.5))
    iclr = jax.nn.sigmoid(a0[:, None, None] + jnp.einsum('dbtr,drc->dbtc', jnp.einsum('btc,dcr->dbtr', z_a, a1), a2))
    gate = jax.nn.sigmoid(z_g @ g1) @ g2
    kk = heads(k * kvec[0])
    kk = (kk * lax.rsqrt(jnp.maximum(jnp.sum(jnp.square(kk), -1, keepdims=True), 1e-24))).reshape(B, T, C)
    k_eff = k[None] * (1.0 + (iclr - 1.0) * kvec[1])
    xs = (_to_state_lanes(r[None]), _to_state_lanes(decay), _to_state_lanes(k_eff), _to_state_lanes(v[None]),
          _to_state_lanes(-kk[None]), _to_state_lanes(kk[None] * iclr))
    yf, yb = _rwkv_scan(xs, n_ctx)
    half = LANES // 2
    y = yf[:, :, :half] + yb[:, :, half:]
    y = jnp.transpose(y.reshape(T, RWKV_HEAD, B, RWKV_HEADS), (2, 0, 3, 1))
    t0 = 0 if keep_ctx else n_ctx
    y, r_h, v_h, k_h, gate = y[:, t0:], heads(r)[:, t0:], heads(v)[:, t0:], heads(k_eff)[:, :, t0:], gate[:, t0:]
    mu_y = jnp.mean(y, -1, keepdims=True)
    var_y = jnp.mean(jnp.square(y - mu_y), -1, keepdims=True)
    y = (y - mu_y) * lax.rsqrt(var_y + RWKV_GN_EPS) * heads(gn[0]) + heads(gn[1])
    bonus = jnp.sum(r_h[None] * k_h * r_k, axis=-1, keepdims=True).sum(0) * v_h
    out = (y + bonus).reshape(B, T - t0, C) * gate
    if keep_ctx:
        return out[:, :n_ctx], out[:, n_ctx:]
    return None, out


def _mlstm_chunk(carry, inp):
    c_state, n_state, m_state = carry
    q, k, v, ig, lf = inp
    L = q.shape[-2]
    in_order = jnp.tril(jnp.ones((L, L), bool))
    b = jnp.cumsum(lf, -1)
    d_intra = jnp.where(in_order, b[..., :, None] - b[..., None, :] + ig[..., None, :], -jnp.inf)
    d_inter = b + m_state[..., None]
    m_t = jnp.maximum(d_inter, jnp.max(d_intra, -1))
    s = jnp.einsum('...td,...sd->...ts', q, k) * jnp.exp(d_intra - m_t[..., None])
    w_inter = jnp.exp(d_inter - m_t)
    num = jnp.einsum('...ts,...sv->...tv', s, v) + w_inter[..., None] * jnp.einsum('...vd,...td->...tv', c_state, q)
    den = jnp.sum(s, -1) + w_inter * jnp.einsum('...d,...td->...t', n_state, q)
    h = num / jnp.maximum(jnp.abs(den), jnp.exp(-m_t))[..., None]
    b_end = b[..., -1]
    d_state = b_end[..., None] - b + ig
    m_new = jnp.maximum(b_end + m_state, jnp.max(d_state, -1))
    w_s = jnp.exp(d_state - m_new[..., None])
    w_c = jnp.exp(b_end + m_state - m_new)
    c_new = w_c[..., None, None] * c_state + jnp.einsum('...s,...sv,...sd->...vd', w_s, v, k)
    n_new = w_c[..., None] * n_state + jnp.einsum('...s,...sd->...d', w_s, k)
    return (c_new, n_new, m_new), h


def _mlstm_bidirectional(p_ctx, p_lat, gate_b, norm_g, keep_ctx):
    n_ctx = p_ctx.shape[1]
    p = jnp.concatenate([p_ctx, p_lat], axis=1)
    B, T, _ = p.shape
    L, nc = MLSTM_CHUNK, T // MLSTM_CHUNK
    q, k, v, o, g = jnp.split(p, [MLSTM_DIM, 2 * MLSTM_DIM, 3 * MLSTM_DIM, 4 * MLSTM_DIM], axis=-1)
    heads = lambda t: t.reshape(B, T, MLSTM_HEADS, MLSTM_HEAD)
    g = g.reshape(B, T, 4, MLSTM_HEADS) + gate_b
    ig = jnp.stack([g[:, :, 0], _seg_flip(g[:, :, 2], n_ctx, 1)])
    lf = jax.nn.log_sigmoid(jnp.stack([g[:, :, 1], _seg_flip(g[:, :, 3], n_ctx, 1)]))

    def chunks(t):
        t = t.reshape((2, B, nc, L) + t.shape[3:])
        return jnp.moveaxis(jnp.moveaxis(t, 2, 0), 3, 4)

    xs = (chunks(_both_dirs(heads(q) * MLSTM_HEAD ** -0.5, n_ctx)), chunks(_both_dirs(heads(k), n_ctx)),
          chunks(_both_dirs(heads(v), n_ctx)), chunks(ig), chunks(lf))
    carry0 = (jnp.zeros((2, B, MLSTM_HEADS, MLSTM_HEAD, MLSTM_HEAD), F32),
              jnp.zeros((2, B, MLSTM_HEADS, MLSTM_HEAD), F32), jnp.zeros((2, B, MLSTM_HEADS), F32))
    _, h = lax.scan(_mlstm_chunk, carry0, xs)
    h = jnp.moveaxis(jnp.moveaxis(h, 4, 3), 0, 2).reshape(2, B, T, MLSTM_HEADS, MLSTM_HEAD)
    h = _own_dirs(h, n_ctx).sum(0)
    t0 = 0 if keep_ctx else n_ctx
    h = _rms_norm(h[:, t0:], norm_g.reshape(MLSTM_HEADS, MLSTM_HEAD))
    out = h.reshape(B, T - t0, MLSTM_DIM) * jax.nn.sigmoid(o[:, t0:])
    if keep_ctx:
        return out[:, :n_ctx], out[:, n_ctx:]
    return None, out


def _recurrent_mixers(h_ctx, h_lat, w_in, mu, w0, w1, w2, a0, a1, a2, g1, g2, kvec, r_k, gn, gate_b, norm_g, keep_ctx):
    p_ctx, p_lat = h_ctx @ w_in, h_lat @ w_in
    rc, rl = _rwkv7_bidirectional(p_ctx[..., :RWKV_IN], p_lat[..., :RWKV_IN], mu, w0, w1, w2, a0, a1, a2,
                                  g1, g2, kvec, r_k, gn, keep_ctx)
    mc, ml = _mlstm_bidirectional(p_ctx[..., RWKV_IN:], p_lat[..., RWKV_IN:], gate_b, norm_g, keep_ctx)
    out_lat = jnp.concatenate([rl, ml], -1)
    if not keep_ctx:
        return None, out_lat
    return jnp.concatenate([rc, mc], -1), out_lat


def kernel(x, c, ctx, c_ctx, ada_w, ada_b, ln_g, ln_b, mix_w_out, att_w_in, na_rpb, qk_gain, rec_w_in, rwkv_mu, rwkv_w0, rwkv_w1, rwkv_w2, rwkv_a0, rwkv_a1, rwkv_a2, rwkv_g1, rwkv_g2, rwkv_kvec, rwkv_rk, rwkv_gn, mlstm_gate_b, mlstm_norm, moe_router, moe_bias, moe_w1, moe_w3, moe_w2, shared_w1, shared_w3, shared_w2):
    B, S, D = x.shape
    xc = ctx
    for i in range(DEPTH):
        keep_ctx = i < DEPTH - 1
        j = i // 2
        mod = jax.nn.silu(c) @ ada_w[i] + ada_b[i]
        mod_c = jax.nn.silu(c_ctx) @ ada_w[i] + ada_b[i]
        sh1, sc1, g1, sh2, sc2, g2 = [t[:, None] for t in jnp.split(mod, 6, -1)]
        sh1c, sc1c, g1c, sh2c, sc2c, g2c = jnp.split(mod_c, 6, -1)
        h_lat = x * (1.0 + sc1) + sh1
        h_ctx = xc * (1.0 + sc1c) + sh1c
        if i % 2 == 0:
            m_ctx, m_lat = _attention_mixers(h_ctx, h_lat, att_w_in[j], na_rpb[j], qk_gain[j], keep_ctx)
        else:
            m_ctx, m_lat = _recurrent_mixers(h_ctx, h_lat, rec_w_in[j], rwkv_mu[j], rwkv_w0[j], rwkv_w1[j], rwkv_w2[j],
                                             rwkv_a0[j], rwkv_a1[j], rwkv_a2[j], rwkv_g1[j], rwkv_g2[j], rwkv_kvec[j],
                                             rwkv_rk[j], rwkv_gn[j], mlstm_gate_b[j], mlstm_norm[j], keep_ctx)
        x = _layer_norm(DN_ALPHA * x + g1 * (m_lat @ mix_w_out[i]), ln_g[i, 0], ln_b[i, 0])
        h_lat = x * (1.0 + sc2) + sh2
        if keep_ctx:
            xc = _layer_norm(DN_ALPHA * xc + g1c * (m_ctx @ mix_w_out[i]), ln_g[i, 0], ln_b[i, 0])
            h_ctx = xc * (1.0 + sc2c) + sh2c
            tokens = jnp.concatenate([h_lat.reshape(-1, D), h_ctx.reshape(-1, D)], 0)
        else:
            tokens = h_lat.reshape(-1, D)
        y = _moe_ffn(tokens, moe_router[i], moe_bias[i], moe_w1[i], moe_w3[i], moe_w2[i],
                     shared_w1[i], shared_w3[i], shared_w2[i])
        x = _layer_norm(DN_ALPHA * x + g2 * y[:B * S].reshape(B, S, D), ln_g[i, 1], ln_b[i, 1])
        if keep_ctx:
            xc = _layer_norm(DN_ALPHA * xc + g2c * y[B * S:].reshape(xc.shape), ln_g[i, 1], ln_b[i, 1])
    return x
```

```python
import functools

import jax
import jax.numpy as jnp
from jax import lax
from jax.experimental import pallas as pl
from jax.experimental.pallas import tpu as pltpu

D_MODEL = 1024
DEPTH = 4
GRID_W = 64
HEAD_DIM = 64
NA_HEADS = 8
NA_WIN_ROWS = 8
NA_WIN_COLS = 16
GQA_Q_HEADS = 8
GQA_KV_HEADS = 2
Q_BLOCK = 128
ROPE_THETA = 10000.0
ROPE_AXIS_DIM = HEAD_DIM // 2
NA_DIM = NA_HEADS * HEAD_DIM
GQA_Q_DIM = GQA_Q_HEADS * HEAD_DIM
GQA_KV_DIM = GQA_KV_HEADS * HEAD_DIM
ATT_SPLITS = (NA_DIM, 2 * NA_DIM, 3 * NA_DIM, 3 * NA_DIM + GQA_Q_DIM, 3 * NA_DIM + GQA_Q_DIM + GQA_KV_DIM)
RWKV_HEADS = 8
RWKV_HEAD = 64
RWKV_DIM = RWKV_HEADS * RWKV_HEAD
RWKV_GN_EPS = 64e-5
RWKV_IN = 4 * RWKV_DIM
MLSTM_HEADS = 4
MLSTM_HEAD = 128
MLSTM_DIM = MLSTM_HEADS * MLSTM_HEAD
MLSTM_CHUNK = 64
N_EXPERTS = 64
TOP_K = 6
ROUTED_SCALE = 2.5
DN_ALPHA = (2 * DEPTH) ** 0.25
LN_EPS = 1e-5
NORM_EPS = 1e-6
F32 = jnp.float32
BF16 = jnp.bfloat16

LANES = 128
RWKV_TIME_BLOCK = 16
MOE_TOKEN_TILE = 256
MOE_ROW_BLOCK = 256
MOE_SLOTS = 8
VMEM_LIMIT = 48 * 1024 * 1024


def _rwkv_scan_body(rf, wf, kf, vf, nf, bf, rb, wb, kb, vb, nb, bb, yf_ref, yb_ref, state_ref, *, tc):
    @pl.when(pl.program_id(0) == 0)
    def _():
        state_ref[...] = jnp.zeros_like(state_ref)

    fwd_lane = lax.broadcasted_iota(jnp.int32, (RWKV_HEAD, LANES), 1) < LANES // 2

    def step(j, carry):
        jb = tc - 1 - j

        def sel(f, b):
            return jnp.where(fwd_lane, f[j], b[jb])

        r, w, k, v, kkn, bv = sel(rf, rb), sel(wf, wb), sel(kf, kb), sel(vf, vb), sel(nf, nb), sel(bf, bb)
        for vi in range(RWKV_HEAD):
            s = state_ref[vi]
            sa = jnp.sum(s * kkn, axis=0, keepdims=True)
            s2 = s * w + sa * bv + v[vi:vi + 1, :] * k
            state_ref[vi] = s2
            yrow = jnp.sum(s2 * r, axis=0, keepdims=True)
            yf_ref[j, pl.ds(vi, 1), :] = yrow
            yb_ref[jb, pl.ds(vi, 1), :] = yrow
        return carry

    lax.fori_loop(0, tc, step, 0)


def _rwkv_scan(xs, n_ctx):
    T = xs[0].shape[0]
    tc = RWKV_TIME_BLOCK
    assert T % tc == 0 and n_ctx % tc == 0
    nc, ncc = T // tc, n_ctx // tc
    blk = (tc, RWKV_HEAD, LANES)
    fwd = lambda c: (c, 0, 0)
    bwd = lambda c: (jnp.where(c < ncc, ncc - 1 - c, nc - 1 - (c - ncc)), 0, 0)
    out_sds = jax.ShapeDtypeStruct((T, RWKV_HEAD, LANES), F32)
    return pl.pallas_call(
        functools.partial(_rwkv_scan_body, tc=tc),
        out_shape=(out_sds, out_sds),
        grid=(nc,),
        in_specs=[pl.BlockSpec(blk, fwd)] * 6 + [pl.BlockSpec(blk, bwd)] * 6,
        out_specs=(pl.BlockSpec(blk, fwd), pl.BlockSpec(blk, bwd)),
        scratch_shapes=[pltpu.VMEM((RWKV_HEAD, RWKV_HEAD, LANES), F32)],
        compiler_params=pltpu.CompilerParams(dimension_semantics=("arbitrary",), vmem_limit_bytes=VMEM_LIMIT),
        name="rwkv_scan",
    )(*xs, *xs)


def _router_body(x_ref, rw_ref, rb_ref, tri_ref, idx_ref, rank_ref, gate_ref, cnt_ref, run_ref):
    @pl.when(pl.program_id(0) == 0)
    def _():
        run_ref[...] = jnp.zeros_like(run_ref)

    tm = x_ref.shape[0]
    logits = jnp.dot(x_ref[...], rw_ref[...], preferred_element_type=F32, precision=lax.Precision.HIGHEST)
    scores = jax.nn.sigmoid(logits)
    sel = scores + rb_ref[...]
    lane = lax.broadcasted_iota(jnp.int32, (tm, N_EXPERTS), 1)
    slot = lax.broadcasted_iota(jnp.int32, (tm, MOE_SLOTS), 1)
    idx8 = jnp.zeros((tm, MOE_SLOTS), jnp.int32)
    onehots = []
    for j in range(TOP_K):
        m = jnp.max(sel, axis=-1, keepdims=True)
        ij = jnp.min(jnp.where(sel == m, lane, N_EXPERTS), axis=-1, keepdims=True)
        oh = lane == ij
        onehots.append(oh)
        sel = jnp.where(oh, -jnp.inf, sel)
        idx8 = jnp.where(slot == j, ij, idx8)
    mask = onehots[0]
    for oh in onehots[1:]:
        mask = mask | oh
    maskf = jnp.where(mask, 1.0, 0.0)
    gsum = jnp.sum(jnp.where(mask, scores, 0.0), axis=-1, keepdims=True)
    gates = scores / gsum * ROUTED_SCALE
    within = jnp.dot(tri_ref[...], maskf.astype(BF16), preferred_element_type=F32)
    rank = within + run_ref[...]
    run_ref[...] += jnp.sum(maskf, axis=0, keepdims=True)
    rank8 = jnp.zeros((tm, MOE_SLOTS), F32)
    gate8 = jnp.zeros((tm, MOE_SLOTS), F32)
    for j in range(TOP_K):
        rj = jnp.sum(jnp.where(onehots[j], rank, 0.0), axis=-1, keepdims=True)
        gj = jnp.sum(jnp.where(onehots[j], gates, 0.0), axis=-1, keepdims=True)
        rank8 = jnp.where(slot == j, rj, rank8)
        gate8 = jnp.where(slot == j, gj, gate8)
    idx_ref[...] = idx8
    rank_ref[...] = rank8.astype(jnp.int32)
    gate_ref[...] = gate8
    cnt_ref[...] = run_ref[...]


def _moe_route(tokens, router_w, router_b):
    n, d = tokens.shape
    tm = MOE_TOKEN_TILE
    tri = (jnp.arange(tm)[:, None] > jnp.arange(tm)[None, :]).astype(BF16)
    slots = lambda dt: jax.ShapeDtypeStruct((n, MOE_SLOTS), dt)
    return pl.pallas_call(
        _router_body,
        out_shape=(slots(jnp.int32), slots(jnp.int32), slots(F32), jax.ShapeDtypeStruct((1, N_EXPERTS), F32)),
        grid=(n // tm,),
        in_specs=[pl.BlockSpec((tm, d), lambda i: (i, 0)),
                  pl.BlockSpec((d, N_EXPERTS), lambda i: (0, 0)),
                  pl.BlockSpec((1, N_EXPERTS), lambda i: (0, 0)),
                  pl.BlockSpec((tm, tm), lambda i: (0, 0))],
        out_specs=(pl.BlockSpec((tm, MOE_SLOTS), lambda i: (i, 0)),
                   pl.BlockSpec((tm, MOE_SLOTS), lambda i: (i, 0)),
                   pl.BlockSpec((tm, MOE_SLOTS), lambda i: (i, 0)),
                   pl.BlockSpec((1, N_EXPERTS), lambda i: (0, 0))),
        scratch_shapes=[pltpu.VMEM((1, N_EXPERTS), F32)],
        compiler_params=pltpu.CompilerParams(dimension_semantics=("arbitrary",), vmem_limit_bytes=VMEM_LIMIT),
        name="moe_router",
    )(tokens, router_w, router_b.reshape(1, N_EXPERTS), tri)


def _load_tables(i, idx_hbm, rank_hbm, idx_s, rank_s, sem_in):
    c1 = pltpu.make_async_copy(idx_hbm.at[i], idx_s, sem_in.at[0])
    c2 = pltpu.make_async_copy(rank_hbm.at[i], rank_s, sem_in.at[1])
    c1.start()
    c2.start()
    c1.wait()
    c2.wait()


def _dispatch_body(offs_ref, bv_ref, x_ref, idx_hbm, rank_hbm, xs_hbm, idx_s, rank_s, zero_ref, sem_in, sem):
    tm = x_ref.shape[0]
    br = zero_ref.shape[0]
    n_blocks = xs_hbm.shape[0] // br

    @pl.when(pl.program_id(0) == 0)
    def _():
        zero_ref[...] = jnp.zeros_like(zero_ref)

        def fill_copy(b):
            return pltpu.make_async_copy(zero_ref, xs_hbm.at[pl.ds(b * br, br)], sem_in.at[0])

        def fill(b, carry):
            @pl.when(bv_ref[b] < br)
            def _():
                fill_copy(b).start()
            return carry

        def fill_wait(b, carry):
            @pl.when(bv_ref[b] < br)
            def _():
                fill_copy(b).wait()
            return carry

        lax.fori_loop(0, n_blocks, fill, 0)
        lax.fori_loop(0, n_blocks, fill_wait, 0)

    _load_tables(pl.program_id(0), idx_hbm, rank_hbm, idx_s, rank_s, sem_in)

    def row_copy(t, d):
        return pltpu.make_async_copy(x_ref.at[pl.ds(t, 1)], xs_hbm.at[pl.ds(d, 1)], sem)

    def issue(t, carry):
        for j in range(TOP_K):
            d = offs_ref[idx_s[t * MOE_SLOTS + j]] + rank_s[t * MOE_SLOTS + j]
            row_copy(t, d).start(priority=j % 2)
        return carry

    lax.fori_loop(0, tm, issue, 0)

    def drain(t, carry):
        for j in range(TOP_K):
            row_copy(0, 0).wait()
        return carry

    lax.fori_loop(0, tm, drain, 0)


def _moe_dispatch(tokens, offs, block_valid, idx_t, rank_t):
    n, d = tokens.shape
    tm, br = MOE_TOKEN_TILE, MOE_ROW_BLOCK
    n_rows = block_valid.shape[0] * br
    return pl.pallas_call(
        _dispatch_body,
        out_shape=jax.ShapeDtypeStruct((n_rows, d), tokens.dtype),
        grid_spec=pltpu.PrefetchScalarGridSpec(
            num_scalar_prefetch=2,
            grid=(n // tm,),
            in_specs=[pl.BlockSpec((tm, d), lambda i, offs, bv: (i, 0)),
                      pl.BlockSpec(memory_space=pl.ANY),
                      pl.BlockSpec(memory_space=pl.ANY)],
            out_specs=pl.BlockSpec(memory_space=pl.ANY),
            scratch_shapes=[pltpu.SMEM((tm * MOE_SLOTS,), jnp.int32), pltpu.SMEM((tm * MOE_SLOTS,), jnp.int32),
                            pltpu.VMEM((br, d), tokens.dtype),
                            pltpu.SemaphoreType.DMA((2,)), pltpu.SemaphoreType.DMA]),
        compiler_params=pltpu.CompilerParams(dimension_semantics=("arbitrary",), vmem_limit_bytes=VMEM_LIMIT),
        name="moe_dispatch",
    )(offs, block_valid, tokens, idx_t, rank_t)


def _expert_body(be_ref, bv_ref, x_ref, w1_ref, w3_ref, w2_ref, y_ref):
    valid = bv_ref[pl.program_id(0)]

    @pl.when(valid > 0)
    def _():
        row = lax.broadcasted_iota(jnp.int32, (x_ref.shape[0], 1), 0)
        x = jnp.where(row < valid, x_ref[...], 0.0).astype(BF16)
        h1 = jnp.dot(x, w1_ref[0], preferred_element_type=F32)
        h3 = jnp.dot(x, w3_ref[0], preferred_element_type=F32)
        a = (h1 * jax.nn.sigmoid(h1) * h3).astype(BF16)
        y_ref[...] = jnp.dot(a, w2_ref[0], preferred_element_type=F32)

    @pl.when(valid <= 0)
    def _():
        y_ref[...] = jnp.zeros_like(y_ref)


def _moe_experts(xs, block_expert, block_valid, w1, w3, w2):
    n_rows, d = xs.shape
    br = MOE_ROW_BLOCK
    ff = w1.shape[-1]
    return pl.pallas_call(
        _expert_body,
        out_shape=jax.ShapeDtypeStruct((n_rows, d), F32),
        grid_spec=pltpu.PrefetchScalarGridSpec(
            num_scalar_prefetch=2,
            grid=(n_rows // br,),
            in_specs=[pl.BlockSpec((br, d), lambda i, be, bv: (i, 0)),
                      pl.BlockSpec((1, d, ff), lambda i, be, bv: (be[i], 0, 0)),
                      pl.BlockSpec((1, d, ff), lambda i, be, bv: (be[i], 0, 0)),
                      pl.BlockSpec((1, ff, d), lambda i, be, bv: (be[i], 0, 0))],
            out_specs=pl.BlockSpec((br, d), lambda i, be, bv: (i, 0))),
        compiler_params=pltpu.CompilerParams(dimension_semantics=("arbitrary",), vmem_limit_bytes=VMEM_LIMIT),
        name="moe_experts",
    )(block_expert, block_valid, xs, w1, w3, w2)


def _combine_body(offs_ref, x_ref, gate_ref, idx_hbm, rank_hbm, ys_hbm, sw1_ref, sw3_ref, sw2_ref, out_ref,
                  idx_s, rank_s, buf, sem_in, sem):
    tm = x_ref.shape[0]
    _load_tables(pl.program_id(0), idx_hbm, rank_hbm, idx_s, rank_s, sem_in)

    def row_copy(t, j, d):
        return pltpu.make_async_copy(ys_hbm.at[pl.ds(d, 1)], buf.at[j, pl.ds(t, 1)], sem)

    def issue(t, carry):
        for j in range(TOP_K):
            d = offs_ref[idx_s[t * MOE_SLOTS + j]] + rank_s[t * MOE_SLOTS + j]
            row_copy(t, j, d).start(priority=j % 2)
        return carry

    lax.fori_loop(0, tm, issue, 0)

    x = x_ref[...].astype(BF16)
    h1 = jnp.dot(x, sw1_ref[...], preferred_element_type=F32)
    h3 = jnp.dot(x, sw3_ref[...], preferred_element_type=F32)
    a = (h1 * jax.nn.sigmoid(h1) * h3).astype(BF16)
    acc = jnp.dot(a, sw2_ref[...], preferred_element_type=F32)

    def drain(t, carry):
        for j in range(TOP_K):
            row_copy(0, j, 0).wait()
        return carry

    lax.fori_loop(0, tm, drain, 0)

    g = gate_ref[...]
    for j in range(TOP_K):
        acc = acc + g[:, j:j + 1] * buf[j]
    out_ref[...] = acc


def _moe_combine(tokens, offs, gate8, idx_t, rank_t, ys, sw1, sw3, sw2):
    n, d = tokens.shape
    tm = MOE_TOKEN_TILE
    ff = sw1.shape[-1]
    whole = lambda shape: pl.BlockSpec(shape, lambda i, offs: (0,) * len(shape))
    return pl.pallas_call(
        _combine_body,
        out_shape=jax.ShapeDtypeStruct((n, d), F32),
        grid_spec=pltpu.PrefetchScalarGridSpec(
            num_scalar_prefetch=1,
            grid=(n // tm,),
            in_specs=[pl.BlockSpec((tm, d), lambda i, offs: (i, 0)),
                      pl.BlockSpec((tm, MOE_SLOTS), lambda i, offs: (i, 0)),
                      pl.BlockSpec(memory_space=pl.ANY),
                      pl.BlockSpec(memory_space=pl.ANY),
                      pl.BlockSpec(memory_space=pl.ANY),
                      whole((d, ff)), whole((d, ff)), whole((ff, d))],
            out_specs=pl.BlockSpec((tm, d), lambda i, offs: (i, 0)),
            scratch_shapes=[pltpu.SMEM((tm * MOE_SLOTS,), jnp.int32), pltpu.SMEM((tm * MOE_SLOTS,), jnp.int32),
                            pltpu.VMEM((TOP_K, tm, d), F32),
                            pltpu.SemaphoreType.DMA((2,)), pltpu.SemaphoreType.DMA]),
        compiler_params=pltpu.CompilerParams(dimension_semantics=("arbitrary",), vmem_limit_bytes=VMEM_LIMIT),
        name="moe_combine",
    )(offs, tokens, gate8, idx_t, rank_t, ys, sw1, sw3, sw2)


def _moe_ffn(tokens, router_w, router_b, w1, w3, w2, sw1, sw3, sw2):
    n, d = tokens.shape
    tm, br = MOE_TOKEN_TILE, MOE_ROW_BLOCK
    assert n % tm == 0
    idx8, rank8, gate8, counts = _moe_route(tokens, router_w, router_b)
    counts = counts.reshape(N_EXPERTS).astype(jnp.int32)
    padded = (counts + br - 1) // br * br
    p_end = jnp.cumsum(padded)
    offs = (p_end - padded).astype(jnp.int32)
    n_blocks = -(-(n * TOP_K + N_EXPERTS * (br - 1)) // br)
    blk_start = jnp.arange(n_blocks, dtype=jnp.int32) * br
    block_expert = jnp.minimum(jnp.sum(blk_start[:, None] >= p_end[None, :], axis=1), N_EXPERTS - 1).astype(jnp.int32)
    block_valid = jnp.clip(counts[block_expert] - (blk_start - offs[block_expert]), 0, br).astype(jnp.int32)
    idx_t = idx8.reshape(n // tm, tm * MOE_SLOTS)
    rank_t = rank8.reshape(n // tm, tm * MOE_SLOTS)
    xs = _moe_dispatch(tokens, offs, block_valid, idx_t, rank_t)
    ys = _moe_experts(xs, block_expert, block_valid, w1.astype(BF16), w3.astype(BF16), w2.astype(BF16))
    return _moe_combine(tokens, offs, gate8, idx_t, rank_t, ys, sw1.astype(BF16), sw3.astype(BF16), sw2.astype(BF16))


def _low_half():
    return lax.broadcasted_iota(jnp.int32, (1, LANES), 1) < LANES // 2


def _pair_attention(q, parts):
    low = _low_half()
    outs = []
    for use_low in (True, False):
        qm = jnp.where(low == use_low, q, jnp.zeros_like(q))
        scores = []
        for k, _, b_lo, b_hi in parts:
            s = lax.dot_general(qm, k, (((1,), (1,)), ((), ())), preferred_element_type=F32) * HEAD_DIM ** -0.5
            b = b_lo if use_low else b_hi
            scores.append(s if b is None else s + b)
        m = functools.reduce(jnp.maximum, [jnp.max(s, axis=-1, keepdims=True) for s in scores])
        den = 0.0
        num = 0.0
        for s, (_, v, _, _) in zip(scores, parts):
            p = jnp.exp(s - m)
            den = den + jnp.sum(p, axis=-1, keepdims=True)
            num = num + jnp.dot(p.astype(BF16), v, preferred_element_type=F32)
        outs.append(num / den)
    return jnp.where(low, outs[0], outs[1])


def _na_body(q_ref, k_ref, v_ref, bias_ref, o_ref, *, n_ctx, rows):
    s = pl.program_id(1)
    ctx_blocks = n_ctx // GRID_W
    n_tiles = NA_DIM // LANES
    tile = lambda t: slice(t * LANES, (t + 1) * LANES)

    @pl.when(s < ctx_blocks)
    def _():
        for t in range(n_tiles):
            part = (k_ref[0, 0:n_ctx, tile(t)], v_ref[0, 0:n_ctx, tile(t)], None, None)
            o_ref[0, :, tile(t)] = _pair_attention(q_ref[0, :, tile(t)], [part])

    @pl.when(s >= ctx_blocks)
    def _():
        r = s - ctx_blocks
        start = jnp.clip(r - NA_WIN_ROWS // 2, 0, rows - NA_WIN_ROWS)
        off = pl.multiple_of(n_ctx + start * GRID_W, GRID_W)
        band = pl.ds(off, NA_WIN_ROWS * GRID_W)
        for t in range(n_tiles):
            parts = [(k_ref[0, band, tile(t)], v_ref[0, band, tile(t)], bias_ref[0, 2 * t], bias_ref[0, 2 * t + 1]),
                     (k_ref[0, 0:n_ctx, tile(t)], v_ref[0, 0:n_ctx, tile(t)], None, None)]
            o_ref[0, :, tile(t)] = _pair_attention(q_ref[0, :, tile(t)], parts)


def _na_bias_table(rpb):
    kc = NA_WIN_COLS
    cidx = jnp.arange(GRID_W)
    col_start = jnp.clip(cidx - kc // 2, 0, GRID_W - kc)
    col_in = (cidx[None, :] >= col_start[:, None]) & (cidx[None, :] < col_start[:, None] + kc)
    d_col = jnp.clip(cidx[None, :] - cidx[:, None], -(kc - 1), kc - 1) + kc - 1
    tab = jnp.where(col_in, rpb[:, :, d_col], -jnp.inf)
    d_rows = jnp.arange(NA_WIN_ROWS)[:, None] + jnp.arange(NA_WIN_ROWS)[None, :]
    tab = tab[:, d_rows]
    return jnp.transpose(tab, (1, 0, 3, 2, 4)).reshape(NA_WIN_ROWS, NA_HEADS, GRID_W, NA_WIN_ROWS * GRID_W)


def _na_attention(pa, rpb, n_ctx):
    B, T, _ = pa.shape
    rows = (T - n_ctx) // GRID_W
    ctx_blocks = n_ctx // GRID_W

    def bias_idx(b, s):
        r = jnp.maximum(s - ctx_blocks, 0)
        return (jnp.clip(r - NA_WIN_ROWS // 2, 0, rows - NA_WIN_ROWS) - r + NA_WIN_ROWS - 1, 0, 0, 0)

    return pl.pallas_call(
        functools.partial(_na_body, n_ctx=n_ctx, rows=rows),
        out_shape=jax.ShapeDtypeStruct((B, T, NA_DIM), F32),
        grid=(B, T // GRID_W),
        in_specs=[pl.BlockSpec((1, GRID_W, NA_DIM), lambda b, s: (b, s, 0)),
                  pl.BlockSpec((1, T, NA_DIM), lambda b, s: (b, 0, 1)),
                  pl.BlockSpec((1, T, NA_DIM), lambda b, s: (b, 0, 2)),
                  pl.BlockSpec((1, NA_HEADS, GRID_W, NA_WIN_ROWS * GRID_W), bias_idx)],
        out_specs=pl.BlockSpec((1, GRID_W, NA_DIM), lambda b, s: (b, s, 0)),
        compiler_params=pltpu.CompilerParams(dimension_semantics=("arbitrary", "arbitrary"), vmem_limit_bytes=VMEM_LIMIT),
        name="na_attention",
    )(pa, pa, pa, _na_bias_table(rpb))


def _rms_pair(x, gain):
    low = _low_half()
    sq = x * x
    s_lo = jnp.sum(jnp.where(low, sq, 0.0), axis=-1, keepdims=True)
    s_hi = jnp.sum(jnp.where(low, 0.0, sq), axis=-1, keepdims=True)
    ms = jnp.where(low, s_lo, s_hi) * (1.0 / HEAD_DIM)
    return x * lax.rsqrt(ms + NORM_EPS) * gain


def _rope_pair(x, cos, sin_signed):
    even = lax.broadcasted_iota(jnp.int32, (1, LANES), 1) % 2 == 0
    partner = jnp.where(even, pltpu.roll(x, LANES - 1, axis=1), pltpu.roll(x, 1, axis=1))
    return x * cos + partner * sin_signed


def _gqa_body(q_ref, k_ref, v_ref, cos_q, sin_q, cos_k, sin_k, gain_ref, o_ref, kn_ref, vn_ref, *, n_ctx):
    s = pl.program_id(1)
    tq = q_ref.shape[1]

    @pl.when(s == 0)
    def _():
        kn_ref[...] = _rope_pair(_rms_pair(k_ref[0], gain_ref[1:2, :]), cos_k[...], sin_k[...]).astype(BF16)
        vn_ref[...] = v_ref[0].astype(BF16)

    def run(n_keys):
        k, v = kn_ref[0:n_keys, :], vn_ref[0:n_keys, :]
        for t in range(GQA_Q_DIM // LANES):
            q = q_ref[0, :, t * LANES:(t + 1) * LANES]
            qn = _rope_pair(_rms_pair(q, gain_ref[0:1, :]), cos_q[...], sin_q[...]).astype(BF16)
            o_ref[0, :, t * LANES:(t + 1) * LANES] = _pair_attention(qn, [(k, v, None, None)])

    @pl.when(s < n_ctx // tq)
    def _():
        run(n_ctx)

    @pl.when(s >= n_ctx // tq)
    def _():
        run(kn_ref.shape[0])


GQA_Q_TILE = 128
GQA_HEAD_ORDER = (0, 4, 1, 5, 2, 6, 3, 7)


def _gqa_rope_tables(T, n_ctx):
    cos, sin = _axial_rope(T - n_ctx)
    cos = jnp.concatenate([jnp.ones((n_ctx, ROPE_AXIS_DIM), F32), cos], 0)
    sin = jnp.concatenate([jnp.zeros((n_ctx, ROPE_AXIS_DIM), F32), sin], 0)
    cos = jnp.tile(jnp.repeat(cos, 2, axis=-1), (1, 2))
    sign = jnp.tile(jnp.array([-1.0, 1.0], F32), LANES // 2)
    sin = jnp.tile(jnp.repeat(sin, 2, axis=-1), (1, 2)) * sign
    return cos, sin


def _gqa_attention(pb, qk_gain, n_ctx):
    B, T, _ = pb.shape
    tq = GQA_Q_TILE
    cos, sin = _gqa_rope_tables(T, n_ctx)
    gain = jnp.tile(qk_gain, (1, 2))
    kv_blk = GQA_Q_DIM // GQA_KV_DIM
    return pl.pallas_call(
        functools.partial(_gqa_body, n_ctx=n_ctx),
        out_shape=jax.ShapeDtypeStruct((B, T, GQA_Q_DIM), F32),
        grid=(B, T // tq),
        in_specs=[pl.BlockSpec((1, tq, GQA_Q_DIM), lambda b, s: (b, s, 0)),
                  pl.BlockSpec((1, T, GQA_KV_DIM), lambda b, s: (b, 0, kv_blk)),
                  pl.BlockSpec((1, T, GQA_KV_DIM), lambda b, s: (b, 0, kv_blk + 1)),
                  pl.BlockSpec((tq, LANES), lambda b, s: (s, 0)),
                  pl.BlockSpec((tq, LANES), lambda b, s: (s, 0)),
                  pl.BlockSpec((T, LANES), lambda b, s: (0, 0)),
                  pl.BlockSpec((T, LANES), lambda b, s: (0, 0)),
                  pl.BlockSpec((2, LANES), lambda b, s: (0, 0))],
        out_specs=pl.BlockSpec((1, tq, GQA_Q_DIM), lambda b, s: (b, s, 0)),
        scratch_shapes=[pltpu.VMEM((T, GQA_KV_DIM), BF16), pltpu.VMEM((T, GQA_KV_DIM), BF16)],
        compiler_params=pltpu.CompilerParams(dimension_semantics=("arbitrary", "arbitrary"), vmem_limit_bytes=VMEM_LIMIT),
        name="gqa_attention",
    )(pb, pb, pb, cos, sin, cos, sin, gain)


def _layer_norm(x, g, b):
    mu = jnp.mean(x, -1, keepdims=True)
    var = jnp.mean(jnp.square(x - mu), -1, keepdims=True)
    return (x - mu) * lax.rsqrt(var + LN_EPS) * g + b


def _rms_norm(x, g):
    return x * lax.rsqrt(jnp.mean(jnp.square(x), -1, keepdims=True) + NORM_EPS) * g


def _swiglu(x, w1, w3, w2):
    return (jax.nn.silu(x @ w1) * (x @ w3)) @ w2


def _centred_shift(x):
    xp = jnp.pad(x, ((0, 0), (1, 1), (0, 0)))
    return 0.5 * (xp[:, :-2] + xp[:, 2:])


def _seg_flip(x, n_ctx, axis):
    a, b = jnp.split(x, [n_ctx], axis=axis)
    return jnp.concatenate([jnp.flip(a, axis), jnp.flip(b, axis)], axis=axis)


def _both_dirs(x, n_ctx):
    return jnp.stack([x, _seg_flip(x, n_ctx, 1)])


def _own_dirs(x, n_ctx):
    return jnp.stack([x[0], _seg_flip(x[1], n_ctx, 1)])


def _axial_rope(n_tokens):
    t = jnp.arange(n_tokens)
    row = (t // GRID_W).astype(F32)
    col = (t % GRID_W).astype(F32)
    inv = ROPE_THETA ** (-jnp.arange(0, ROPE_AXIS_DIM, 2, dtype=F32) / ROPE_AXIS_DIM)
    ang = jnp.concatenate([row[:, None] * inv, col[:, None] * inv], -1)
    return jnp.cos(ang), jnp.sin(ang)


def _apply_rope(x, cos, sin):
    xp = x.reshape(x.shape[:-1] + (HEAD_DIM // 2, 2))
    x0, x1 = xp[..., 0], xp[..., 1]
    c, s = cos[:, None, :], sin[:, None, :]
    return jnp.stack([x0 * c - x1 * s, x0 * s + x1 * c], -1).reshape(x.shape)


def _attend(q, k, v):
    B, Tq, Hq, Dh = q.shape
    hkv = k.shape[2]
    qg = q.reshape(B, Tq, hkv, Hq // hkv, Dh)
    s = jnp.einsum('bqhgd,bkhd->bhgqk', qg, k) * Dh ** -0.5
    p = jax.nn.softmax(s, -1)
    return jnp.einsum('bhgqk,bkhd->bqhgd', p, v).reshape(B, Tq, Hq * Dh)


def _blocked_attention(q, k, v):
    B, S, Hq, Dh = q.shape
    nb = S // Q_BLOCK
    qb = jnp.moveaxis(q.reshape(B, nb, Q_BLOCK, Hq, Dh), 1, 0)
    out = lax.map(lambda qi: _attend(qi, k, v), qb)
    return jnp.moveaxis(out, 0, 1).reshape(B, S, Hq * Dh)


def _neighbourhood_attention(q, k, v, k_ctx, v_ctx, rpb):
    B, S, H, Dh = q.shape
    rows = S // GRID_W
    kr, kc = min(NA_WIN_ROWS, rows), NA_WIN_COLS
    qg = q.reshape(B, rows, GRID_W, H, Dh)
    kg = k.reshape(B, rows, GRID_W, H, Dh)
    vg = v.reshape(B, rows, GRID_W, H, Dh)
    r = jnp.arange(rows)
    row_idx = jnp.clip(r - kr // 2, 0, rows - kr)[:, None] + jnp.arange(kr)[None, :]
    k_band = kg[:, row_idx]
    v_band = vg[:, row_idx]
    cidx = jnp.arange(GRID_W)
    col_start = jnp.clip(cidx - kc // 2, 0, GRID_W - kc)
    col_in = (cidx[None, :] >= col_start[:, None]) & (cidx[None, :] < col_start[:, None] + kc)
    d_row = row_idx - r[:, None] + NA_WIN_ROWS - 1
    d_col = jnp.clip(cidx[None, :] - cidx[:, None], -(kc - 1), kc - 1) + kc - 1
    bias = rpb[:, d_row[:, None, :, None], d_col[None, :, None, :]]
    scale = Dh ** -0.5
    s_lat = jnp.einsum('brqhd,brkwhd->bhrqkw', qg, k_band) * scale + bias
    s_lat = jnp.where(col_in[None, None, None, :, None, :], s_lat, -jnp.inf)
    s_ctx = jnp.einsum('brqhd,bchd->bhrqc', qg, k_ctx) * scale
    n_lat = kr * GRID_W
    s = jnp.concatenate([s_lat.reshape(B, H, rows, GRID_W, n_lat), s_ctx], -1)
    p = jax.nn.softmax(s, -1)
    p_lat = p[..., :n_lat].reshape(B, H, rows, GRID_W, kr, GRID_W)
    out = jnp.einsum('bhrqkw,brkwhd->brqhd', p_lat, v_band) + jnp.einsum('bhrqc,bchd->brqhd', p[..., n_lat:], v_ctx)
    return out.reshape(B, S, H * Dh)


def _attention_mixers(h_ctx, h_lat, w_in, rpb, qk_gain, keep_ctx):
    n_ctx = h_ctx.shape[1]
    h_all = jnp.concatenate([h_ctx, h_lat], axis=1)
    B, T, _ = h_all.shape
    order = jnp.array(GQA_HEAD_ORDER)
    qb_cols = 3 * NA_DIM + (order[:, None] * HEAD_DIM + jnp.arange(HEAD_DIM)[None, :]).reshape(-1)
    cols = jnp.concatenate([jnp.arange(3 * NA_DIM), qb_cols, jnp.arange(3 * NA_DIM + GQA_Q_DIM, w_in.shape[1])])
    p = h_all @ w_in[:, cols]
    out_a = _na_attention(p[..., :3 * NA_DIM].astype(BF16), rpb, n_ctx)
    out_b = _gqa_attention(p[..., 3 * NA_DIM:], qk_gain, n_ctx)
    inverse = jnp.array([GQA_HEAD_ORDER.index(h) for h in range(GQA_Q_HEADS)])
    out_b = out_b.reshape(B, T, GQA_Q_HEADS, HEAD_DIM)[:, :, inverse].reshape(B, T, GQA_Q_DIM)
    out = jnp.concatenate([out_a, out_b], -1)
    return (out[:, :n_ctx] if keep_ctx else None), out[:, n_ctx:]


def _to_state_lanes(x):
    nd, B, T, _ = x.shape
    y = jnp.transpose(x.reshape(nd, B, T, RWKV_HEADS, RWKV_HEAD), (2, 4, 0, 1, 3)).reshape(T, RWKV_HEAD, nd * B * RWKV_HEADS)
    return jnp.concatenate([y, y], -1) if nd == 1 else y


def _rwkv7_bidirectional(p_ctx, p_lat, mu, w0, w1, w2, a0, a1, a2, g1, g2, kvec, r_k, gn, keep_ctx):
    n_ctx = p_ctx.shape[1]

    def shift_mix(p):
        d = _centred_shift(p) - p
        (r, k, v, z), (dr, dk, dv, dz) = jnp.split(p, 4, -1), jnp.split(d, 4, -1)
        return r + dr * mu[0], k + dk * mu[1], v + dv * mu[2], z + dz * mu[3], z + dz * mu[4], z + dz * mu[5]

    r, k, v, z_w, z_a, z_g = [jnp.concatenate(pair, axis=1) for pair in zip(shift_mix(p_ctx), shift_mix(p_lat))]
    B, T, C = r.shape
    assert 2 * B * RWKV_HEADS == LANES
    heads = lambda t: t.reshape(t.shape[:-1] + (RWKV_HEADS, RWKV_HEAD))
    w_pre = w0[:, None, None] + jnp.einsum('dbtr,drc->dbtc', jnp.tanh(jnp.einsum('btc,dcr->dbtr', z_w, w1)), w2)
    decay = jnp.exp(-jnp.exp(-jax.nn.softplus(-w_pre) - 0.5))
    iclr = jax.nn.sigmoid(a0[:, None, None] + jnp.einsum('dbtr,drc->dbtc', jnp.einsum('btc,dcr->dbtr', z_a, a1), a2))
    gate = jax.nn.sigmoid(z_g @ g1) @ g2
    kk = heads(k * kvec[0])
    kk = (kk * lax.rsqrt(jnp.maximum(jnp.sum(jnp.square(kk), -1, keepdims=True), 1e-24))).reshape(B, T, C)
    k_eff = k[None] * (1.0 + (iclr - 1.0) * kvec[1])
    xs = (_to_state_lanes(r[None]), _to_state_lanes(decay), _to_state_lanes(k_eff), _to_state_lanes(v[None]),
          _to_state_lanes(-kk[None]), _to_state_lanes(kk[None] * iclr))
    yf, yb = _rwkv_scan(xs, n_ctx)
    half = LANES // 2
    y = yf[:, :, :half] + yb[:, :, half:]
    y = jnp.transpose(y.reshape(T, RWKV_HEAD, B, RWKV_HEADS), (2, 0, 3, 1))
    t0 = 0 if keep_ctx else n_ctx
    y, r_h, v_h, k_h, gate = y[:, t0:], heads(r)[:, t0:], heads(v)[:, t0:], heads(k_eff)[:, :, t0:], gate[:, t0:]
    mu_y = jnp.mean(y, -1, keepdims=True)
    var_y = jnp.mean(jnp.square(y - mu_y), -1, keepdims=True)
    y = (y - mu_y) * lax.rsqrt(var_y + RWKV_GN_EPS) * heads(gn[0]) + heads(gn[1])
    bonus = jnp.sum(r_h[None] * k_h * r_k, axis=-1, keepdims=True).sum(0) * v_h
    out = (y + bonus).reshape(B, T - t0, C) * gate
    if keep_ctx:
        return out[:, :n_ctx], out[:, n_ctx:]
    return None, out


def _mlstm_chunk(carry, inp):
    c_state, n_state, m_state = carry
    q, k, v, ig, lf = inp
    L = q.shape[-2]
    in_order = jnp.tril(jnp.ones((L, L), bool))
    b = jnp.cumsum(lf, -1)
    d_intra = jnp.where(in_order, b[..., :, None] - b[..., None, :] + ig[..., None, :], -jnp.inf)
    d_inter = b + m_state[..., None]
    m_t = jnp.maximum(d_inter, jnp.max(d_intra, -1))
    s = jnp.einsum('...td,...sd->...ts', q, k) * jnp.exp(d_intra - m_t[..., None])
    w_inter = jnp.exp(d_inter - m_t)
    num = jnp.einsum('...ts,...sv->...tv', s, v) + w_inter[..., None] * jnp.einsum('...vd,...td->...tv', c_state, q)
    den = jnp.sum(s, -1) + w_inter * jnp.einsum('...d,...td->...t', n_state, q)
    h = num / jnp.maximum(jnp.abs(den), jnp.exp(-m_t))[..., None]
    b_end = b[..., -1]
    d_state = b_end[..., None] - b + ig
    m_new = jnp.maximum(b_end + m_state, jnp.max(d_state, -1))
    w_s = jnp.exp(d_state - m_new[..., None])
    w_c = jnp.exp(b_end + m_state - m_new)
    c_new = w_c[..., None, None] * c_state + jnp.einsum('...s,...sv,...sd->...vd', w_s, v, k)
    n_new = w_c[..., None] * n_state + jnp.einsum('...s,...sd->...d', w_s, k)
    return (c_new, n_new, m_new), h


def _mlstm_bidirectional(p_ctx, p_lat, gate_b, norm_g, keep_ctx):
    n_ctx = p_ctx.shape[1]
    p = jnp.concatenate([p_ctx, p_lat], axis=1)
    B, T, _ = p.shape
    L, nc = MLSTM_CHUNK, T // MLSTM_CHUNK
    q, k, v, o, g = jnp.split(p, [MLSTM_DIM, 2 * MLSTM_DIM, 3 * MLSTM_DIM, 4 * MLSTM_DIM], axis=-1)
    heads = lambda t: t.reshape(B, T, MLSTM_HEADS, MLSTM_HEAD)
    g = g.reshape(B, T, 4, MLSTM_HEADS) + gate_b
    ig = jnp.stack([g[:, :, 0], _seg_flip(g[:, :, 2], n_ctx, 1)])
    lf = jax.nn.log_sigmoid(jnp.stack([g[:, :, 1], _seg_flip(g[:, :, 3], n_ctx, 1)]))

    def chunks(t):
        t = t.reshape((2, B, nc, L) + t.shape[3:])
        return jnp.moveaxis(jnp.moveaxis(t, 2, 0), 3, 4)

    xs = (chunks(_both_dirs(heads(q) * MLSTM_HEAD ** -0.5, n_ctx)), chunks(_both_dirs(heads(k), n_ctx)),
          chunks(_both_dirs(heads(v), n_ctx)), chunks(ig), chunks(lf))
    carry0 = (jnp.zeros((2, B, MLSTM_HEADS, MLSTM_HEAD, MLSTM_HEAD), F32),
              jnp.zeros((2, B, MLSTM_HEADS, MLSTM_HEAD), F32), jnp.zeros((2, B, MLSTM_HEADS), F32))
    _, h = lax.scan(_mlstm_chunk, carry0, xs)
    h = jnp.moveaxis(jnp.moveaxis(h, 4, 3), 0, 2).reshape(2, B, T, MLSTM_HEADS, MLSTM_HEAD)
    h = _own_dirs(h, n_ctx).sum(0)
    t0 = 0 if keep_ctx else n_ctx
    h = _rms_norm(h[:, t0:], norm_g.reshape(MLSTM_HEADS, MLSTM_HEAD))
    out = h.reshape(B, T - t0, MLSTM_DIM) * jax.nn.sigmoid(o[:, t0:])
    if keep_ctx:
        return out[:, :n_ctx], out[:, n_ctx:]
    return None, out


def _recurrent_mixers(h_ctx, h_lat, w_in, mu, w0, w1, w2, a0, a1, a2, g1, g2, kvec, r_k, gn, gate_b, norm_g, keep_ctx):
    p_ctx, p_lat = h_ctx @ w_in, h_lat @ w_in
    rc, rl = _rwkv7_bidirectional(p_ctx[..., :RWKV_IN], p_lat[..., :RWKV_IN], mu, w0, w1, w2, a0, a1, a2,
                                  g1, g2, kvec, r_k, gn, keep_ctx)
    mc, ml = _mlstm_bidirectional(p_ctx[..., RWKV_IN:], p_lat[..., RWKV_IN:], gate_b, norm_g, keep_ctx)
    out_lat = jnp.concatenate([rl, ml], -1)
    if not keep_ctx:
        return None, out_lat
    return jnp.concatenate([rc, mc], -1), out_lat


def kernel(x, c, ctx, c_ctx, ada_w, ada_b, ln_g, ln_b, mix_w_out, att_w_in, na_rpb, qk_gain, rec_w_in, rwkv_mu, rwkv_w0, rwkv_w1, rwkv_w2, rwkv_a0, rwkv_a1, rwkv_a2, rwkv_g1, rwkv_g2, rwkv_kvec, rwkv_rk, rwkv_gn, mlstm_gate_b, mlstm_norm, moe_router, moe_bias, moe_w1, moe_w3, moe_w2, shared_w1, shared_w3, shared_w2):
    B, S, D = x.shape
    xc = ctx
    for i in range(DEPTH):
        keep_ctx = i < DEPTH - 1
        j = i // 2
        mod = jax.nn.silu(c) @ ada_w[i] + ada_b[i]
        mod_c = jax.nn.silu(c_ctx) @ ada_w[i] + ada_b[i]
        sh1, sc1, g1, sh2, sc2, g2 = [t[:, None] for t in jnp.split(mod, 6, -1)]
        sh1c, sc1c, g1c, sh2c, sc2c, g2c = jnp.split(mod_c, 6, -1)
        h_lat = x * (1.0 + sc1) + sh1
        h_ctx = xc * (1.0 + sc1c) + sh1c
        if i % 2 == 0:
            m_ctx, m_lat = _attention_mixers(h_ctx, h_lat, att_w_in[j], na_rpb[j], qk_gain[j], keep_ctx)
        else:
            m_ctx, m_lat = _recurrent_mixers(h_ctx, h_lat, rec_w_in[j], rwkv_mu[j], rwkv_w0[j], rwkv_w1[j], rwkv_w2[j],
                                             rwkv_a0[j], rwkv_a1[j], rwkv_a2[j], rwkv_g1[j], rwkv_g2[j], rwkv_kvec[j],
                                             rwkv_rk[j], rwkv_gn[j], mlstm_gate_b[j], mlstm_norm[j], keep_ctx)
        x = _layer_norm(DN_ALPHA * x + g1 * (m_lat @ mix_w_out[i]), ln_g[i, 0], ln_b[i, 0])
        h_lat = x * (1.0 + sc2) + sh2
        if keep_ctx:
            xc = _layer_norm(DN_ALPHA * xc + g1c * (m_ctx @ mix_w_out[i]), ln_g[i, 0], ln_b[i, 0])
            h_ctx = xc * (1.0 + sc2c) + sh2c
            tokens = jnp.concatenate([h_lat.reshape(-1, D), h_ctx.reshape(-1, D)], 0)
        else:
            tokens = h_lat.reshape(-1, D)
        y = _moe_ffn(tokens, moe_router[i], moe_bias[i], moe_w1[i], moe_w3[i], moe_w2[i],
                     shared_w1[i], shared_w3[i], shared_w2[i])
        x = _layer_norm(DN_ALPHA * x + g2 * y[:B * S].reshape(B, S, D), ln_g[i, 1], ln_b[i, 1])
        if keep_ctx:
            xc = _layer_norm(DN_ALPHA * xc + g2c * y[B * S:].reshape(xc.shape), ln_g[i, 1], ln_b[i, 1])
    return x
```

```python
import functools

import jax
import jax.numpy as jnp
from jax import lax
from jax.experimental import pallas as pl
from jax.experimental.pallas import tpu as pltpu

D_MODEL = 1024
DEPTH = 4
GRID_W = 64
HEAD_DIM = 64
NA_HEADS = 8
NA_WIN_ROWS = 8
NA_WIN_COLS = 16
GQA_Q_HEADS = 8
GQA_KV_HEADS = 2
ROPE_THETA = 10000.0
ROPE_AXIS_DIM = HEAD_DIM // 2
NA_DIM = NA_HEADS * HEAD_DIM
GQA_Q_DIM = GQA_Q_HEADS * HEAD_DIM
GQA_KV_DIM = GQA_KV_HEADS * HEAD_DIM
RWKV_HEADS = 8
RWKV_HEAD = 64
RWKV_DIM = RWKV_HEADS * RWKV_HEAD
RWKV_GN_EPS = 64e-5
RWKV_IN = 4 * RWKV_DIM
MLSTM_HEADS = 4
MLSTM_HEAD = 128
MLSTM_DIM = MLSTM_HEADS * MLSTM_HEAD
MLSTM_CHUNK = 64
N_EXPERTS = 64
TOP_K = 6
ROUTED_SCALE = 2.5
DN_ALPHA = (2 * DEPTH) ** 0.25
LN_EPS = 1e-5
NORM_EPS = 1e-6
F32 = jnp.float32
BF16 = jnp.bfloat16

LANES = 128
SUBLANES = 8
VMEM_LIMIT = 48 * 1024 * 1024
TOKEN_TILE = 256
RWKV_TIME_BLOCK = 16
GQA_Q_TILE = 128
MOE_ROW_BLOCK = 256
MOE_SLOTS = 8
MOE_CHUNK = SUBLANES
MOE_STAGE_ROWS = TOKEN_TILE * TOP_K + N_EXPERTS * MOE_CHUNK
MOD_ROWS = 8


def _params(*semantics):
    return pltpu.CompilerParams(dimension_semantics=semantics, vmem_limit_bytes=VMEM_LIMIT)


def _mod_spec(tiles_per_batch, first_tile, flat):
    if flat:
        idx = lambda i, *_: (i // tiles_per_batch, jnp.minimum(i % tiles_per_batch + first_tile, 1), 0, 0)
    else:
        idx = lambda b, s, *_: (b, jnp.minimum(s + first_tile, 1), 0, 0)
    return pl.BlockSpec((1, 1, MOD_ROWS, D_MODEL), idx)


def _layer_norm_rows(z, g, b):
    mu = jnp.mean(z, axis=-1, keepdims=True)
    zc = z - mu
    var = jnp.mean(zc * zc, axis=-1, keepdims=True)
    return zc * lax.rsqrt(var + LN_EPS) * g + b


def _in_proj_body(x_ref, mod_ref, w_ref, *out_refs, splits):
    h = (x_ref[0] * (1.0 + mod_ref[0, 0, 1:2, :]) + mod_ref[0, 0, 0:1, :]).astype(BF16)
    for o_ref, (c0, c1) in zip(out_refs, splits):
        o_ref[0] = jnp.dot(h, w_ref[:, c0:c1], preferred_element_type=F32).astype(o_ref.dtype)


def _in_proj(x, modtab, w, splits, dtypes):
    B, T, D = x.shape
    tm = TOKEN_TILE
    outs = tuple(jax.ShapeDtypeStruct((B, T, c1 - c0), dt) for (c0, c1), dt in zip(splits, dtypes))
    return pl.pallas_call(
        functools.partial(_in_proj_body, splits=splits),
        out_shape=outs,
        grid=(B, T // tm),
        in_specs=[pl.BlockSpec((1, tm, D), lambda b, s: (b, s, 0)), _mod_spec(T // tm, 0, False),
                  pl.BlockSpec(w.shape, lambda b, s: (0, 0))],
        out_specs=tuple(pl.BlockSpec((1, tm, c1 - c0), lambda b, s: (b, s, 0)) for c0, c1 in splits),
        compiler_params=_params("arbitrary", "arbitrary"),
        name="in_proj",
    )(x, modtab, w)


def _out_proj_body(m_ref, x_ref, mod_ref, w_ref, ln_ref, x1_ref, h2_ref):
    y = jnp.dot(m_ref[0], w_ref[...], preferred_element_type=F32)
    mod = mod_ref[0, 0]
    x1 = _layer_norm_rows(DN_ALPHA * x_ref[0] + mod[2:3, :] * y, ln_ref[0:1, :], ln_ref[1:2, :])
    x1_ref[0] = x1
    h2_ref[0] = x1 * (1.0 + mod[4:5, :]) + mod[3:4, :]


def _out_proj(m, x, modtab, w, ln, first_tile):
    B, T, D = x.shape
    tm = TOKEN_TILE
    n_tiles = T // tm - first_tile
    rows = lambda b, s: (b, s + first_tile, 0)
    out_sds = jax.ShapeDtypeStruct((B, n_tiles * tm, D), F32)
    return pl.pallas_call(
        _out_proj_body,
        out_shape=(out_sds, out_sds),
        grid=(B, n_tiles),
        in_specs=[pl.BlockSpec((1, tm, m.shape[-1]), rows), pl.BlockSpec((1, tm, D), rows),
                  _mod_spec(T // tm, first_tile, False),
                  pl.BlockSpec(w.shape, lambda b, s: (0, 0)), pl.BlockSpec(ln.shape, lambda b, s: (0, 0))],
        out_specs=(pl.BlockSpec((1, tm, D), lambda b, s: (b, s, 0)),) * 2,
        compiler_params=_params("arbitrary", "arbitrary"),
        name="out_proj",
    )(m, x, modtab, w, ln)


def _rwkv_scan_body(rf, wf, kf, vf, nf, bf, rb, wb, kb, vb, nb, bb, yf_ref, yb_ref, state_ref, *, tc):
    @pl.when(pl.program_id(0) == 0)
    def _():
        state_ref[...] = jnp.zeros_like(state_ref)

    fwd_lane = lax.broadcasted_iota(jnp.int32, (RWKV_HEAD, LANES), 1) < LANES // 2

    def step(j, carry):
        jb = tc - 1 - j

        def sel(f, b):
            return jnp.where(fwd_lane, f[j], b[jb])

        r, w, k, v, kkn, bv = sel(rf, rb), sel(wf, wb), sel(kf, kb), sel(vf, vb), sel(nf, nb), sel(bf, bb)
        for vi in range(RWKV_HEAD):
            s = state_ref[vi]
            sa = jnp.sum(s * kkn, axis=0, keepdims=True)
            s2 = s * w + sa * bv + v[vi:vi + 1, :] * k
            state_ref[vi] = s2
            yrow = jnp.sum(s2 * r, axis=0, keepdims=True)
            yf_ref[j, pl.ds(vi, 1), :] = yrow
            yb_ref[jb, pl.ds(vi, 1), :] = yrow
        return carry

    lax.fori_loop(0, tc, step, 0)


def _rwkv_scan(xs, n_ctx):
    T = xs[0].shape[0]
    tc = RWKV_TIME_BLOCK
    assert T % tc == 0 and n_ctx % tc == 0
    nc, ncc = T // tc, n_ctx // tc
    blk = (tc, RWKV_HEAD, LANES)
    fwd = lambda c: (c, 0, 0)
    bwd = lambda c: (jnp.where(c < ncc, ncc - 1 - c, nc - 1 - (c - ncc)), 0, 0)
    out_sds = jax.ShapeDtypeStruct((T, RWKV_HEAD, LANES), F32)
    return pl.pallas_call(
        functools.partial(_rwkv_scan_body, tc=tc),
        out_shape=(out_sds, out_sds),
        grid=(nc,),
        in_specs=[pl.BlockSpec(blk, fwd)] * 6 + [pl.BlockSpec(blk, bwd)] * 6,
        out_specs=(pl.BlockSpec(blk, fwd), pl.BlockSpec(blk, bwd)),
        scratch_shapes=[pltpu.VMEM((RWKV_HEAD, RWKV_HEAD, LANES), F32)],
        compiler_params=_params("arbitrary"),
        name="rwkv_scan",
    )(*xs, *xs)


def _route_body(x_ref, rwt_ref, rb_ref, upper_ref, lower_ref, q_ref, g_ref, cnt_ref):
    tm = x_ref.shape[0]
    logits = lax.dot_general(rwt_ref[...], x_ref[...], (((1,), (1,)), ((), ())), preferred_element_type=F32,
                             precision=lax.Precision.HIGHEST)
    scores = jax.nn.sigmoid(logits)
    sel = scores + rb_ref[...]
    eidx = lax.broadcasted_iota(jnp.int32, (N_EXPERTS, tm), 0)
    slot = lax.broadcasted_iota(jnp.int32, (MOE_SLOTS, tm), 0)
    onehots = []
    for _ in range(TOP_K):
        m = jnp.max(sel, axis=0, keepdims=True)
        ij = jnp.min(jnp.where(sel == m, eidx, N_EXPERTS), axis=0, keepdims=True)
        oh = eidx == ij
        onehots.append(oh)
        sel = jnp.where(oh, -jnp.inf, sel)
    mask = functools.reduce(jnp.logical_or, onehots)
    maskf = jnp.where(mask, 1.0, 0.0)
    gsum = jnp.sum(jnp.where(mask, scores, 0.0), axis=0, keepdims=True)
    gates = scores / gsum * ROUTED_SCALE
    cnt = jnp.sum(maskf, axis=1, keepdims=True)
    cnt_pad = jnp.ceil(cnt * (1.0 / MOE_CHUNK)) * MOE_CHUNK
    lrank = jnp.dot(maskf.astype(BF16), upper_ref[...], preferred_element_type=F32)
    loff = jnp.dot(lower_ref[...], jnp.broadcast_to(cnt_pad, (N_EXPERTS, LANES)).astype(BF16),
                   preferred_element_type=F32)[:, 0:1]
    q = loff + lrank
    q8 = jnp.full((MOE_SLOTS, tm), -1.0, F32)
    g8 = jnp.zeros((MOE_SLOTS, tm), F32)
    for j, oh in enumerate(onehots):
        q8 = jnp.where(slot == j, jnp.sum(jnp.where(oh, q, 0.0), axis=0, keepdims=True), q8)
        g8 = jnp.where(slot == j, jnp.sum(jnp.where(oh, gates, 0.0), axis=0, keepdims=True), g8)
    q_ref[0] = q8.astype(jnp.int32)
    g_ref[0] = g8
    cnt_ref[0] = jnp.broadcast_to(cnt, (N_EXPERTS, LANES))


def _moe_route(tokens, router_w, router_b):
    n, d = tokens.shape
    tm = TOKEN_TILE
    nt = n // tm
    ar = jnp.arange(tm)
    upper = (ar[:, None] < ar[None, :]).astype(BF16)
    ae = jnp.arange(N_EXPERTS)
    lower = (ae[:, None] > ae[None, :]).astype(BF16)
    const = lambda shape: pl.BlockSpec(shape, lambda i: (0,) * len(shape))
    return pl.pallas_call(
        _route_body,
        out_shape=(jax.ShapeDtypeStruct((nt, MOE_SLOTS, tm), jnp.int32), jax.ShapeDtypeStruct((nt, MOE_SLOTS, tm), F32),
                   jax.ShapeDtypeStruct((nt, N_EXPERTS, LANES), F32)),
        grid=(nt,),
        in_specs=[pl.BlockSpec((tm, d), lambda i: (i, 0)), const((N_EXPERTS, d)), const((N_EXPERTS, 1)),
                  const((tm, tm)), const((N_EXPERTS, N_EXPERTS))],
        out_specs=(pl.BlockSpec((1, MOE_SLOTS, tm), lambda i: (i, 0, 0)),
                   pl.BlockSpec((1, MOE_SLOTS, tm), lambda i: (i, 0, 0)),
                   pl.BlockSpec((1, N_EXPERTS, LANES), lambda i: (i, 0, 0))),
        compiler_params=_params("arbitrary"),
        name="moe_router",
    )(tokens, router_w.T, router_b.reshape(N_EXPERTS, 1), upper, lower)


def _chunk_loops(i, base_ref, nchunk_ref, loff_ref, copy):
    def per_expert(e, total):
        k = i * N_EXPERTS + e
        n, base, lo = nchunk_ref[k], base_ref[k], loff_ref[k]

        def piece(c, carry):
            copy(pl.multiple_of(lo + c * MOE_CHUNK, MOE_CHUNK), pl.multiple_of(base + c * MOE_CHUNK, MOE_CHUNK)).start()
            return carry

        lax.fori_loop(0, n, piece, 0)
        return total + n

    return lax.fori_loop(0, N_EXPERTS, per_expert, 0)


def _dispatch_body(base_ref, nchunk_ref, loff_ref, bv_ref, x_ref, q_ref, xs_hbm, stage_ref, zero_ref, sem_z, sem):
    i = pl.program_id(0)
    br = zero_ref.shape[0]
    n_blocks = xs_hbm.shape[0] // br

    @pl.when(i == 0)
    def _():
        zero_ref[...] = jnp.zeros_like(zero_ref)

        def fill_copy(b):
            return pltpu.make_async_copy(zero_ref, xs_hbm.at[pl.ds(b * br, br)], sem_z)

        def fill(b, carry):
            @pl.when(bv_ref[b] < br)
            def _():
                fill_copy(b).start()
            return carry

        def fill_wait(b, carry):
            @pl.when(bv_ref[b] < br)
            def _():
                fill_copy(b).wait()
            return carry

        lax.fori_loop(0, n_blocks, fill, 0)
        lax.fori_loop(0, n_blocks, fill_wait, 0)

    rows, tm = stage_ref.shape[0], x_ref.shape[0]
    q8 = q_ref[0]
    pos = lax.broadcasted_iota(jnp.int32, (rows, tm), 0)
    hit = functools.reduce(jnp.logical_or, [pos == q8[j:j + 1, :] for j in range(TOP_K)])
    perm = jnp.where(hit, 1.0, 0.0).astype(BF16)
    stage_ref[...] = jnp.dot(perm, x_ref[...].astype(BF16), preferred_element_type=F32)

    def copy(local_row, global_row):
        return pltpu.make_async_copy(stage_ref.at[pl.ds(local_row, MOE_CHUNK)], xs_hbm.at[pl.ds(global_row, MOE_CHUNK)], sem)

    total = _chunk_loops(i, base_ref, nchunk_ref, loff_ref, copy)

    def drain(c, carry):
        copy(0, 0).wait()
        return carry

    lax.fori_loop(0, total, drain, 0)


def _moe_dispatch(tokens, q_rows, dest_base, nchunk, loff, block_valid):
    n, d = tokens.shape
    tm, br = TOKEN_TILE, MOE_ROW_BLOCK
    n_rows = block_valid.shape[0] * br
    return pl.pallas_call(
        _dispatch_body,
        out_shape=jax.ShapeDtypeStruct((n_rows, d), F32),
        grid_spec=pltpu.PrefetchScalarGridSpec(
            num_scalar_prefetch=4,
            grid=(n // tm,),
            in_specs=[pl.BlockSpec((tm, d), lambda i, *_: (i, 0)),
                      pl.BlockSpec((1, MOE_SLOTS, tm), lambda i, *_: (i, 0, 0))],
            out_specs=pl.BlockSpec(memory_space=pl.ANY),
            scratch_shapes=[pltpu.VMEM((MOE_STAGE_ROWS, d), F32), pltpu.VMEM((br, d), F32),
                            pltpu.SemaphoreType.DMA, pltpu.SemaphoreType.DMA]),
        compiler_params=_params("arbitrary"),
        name="moe_dispatch",
    )(dest_base, nchunk, loff, block_valid, tokens, q_rows)


def _expert_body(be_ref, bv_ref, x_ref, w1_ref, w3_ref, w2_ref, y_ref):
    valid = bv_ref[pl.program_id(0)]

    @pl.when(valid > 0)
    def _():
        row = lax.broadcasted_iota(jnp.int32, (x_ref.shape[0], 1), 0)
        x = jnp.where(row < valid, x_ref[...], 0.0).astype(BF16)
        h1 = jnp.dot(x, w1_ref[0], preferred_element_type=F32)
        h3 = jnp.dot(x, w3_ref[0], preferred_element_type=F32)
        a = (h1 * jax.nn.sigmoid(h1) * h3).astype(BF16)
        y_ref[...] = jnp.dot(a, w2_ref[0], preferred_element_type=F32)

    @pl.when(valid <= 0)
    def _():
        y_ref[...] = jnp.zeros_like(y_ref)


def _moe_experts(xs, block_expert, block_valid, w1, w3, w2):
    n_rows, d = xs.shape
    br = MOE_ROW_BLOCK
    ff = w1.shape[-1]
    return pl.pallas_call(
        _expert_body,
        out_shape=jax.ShapeDtypeStruct((n_rows, d), F32),
        grid_spec=pltpu.PrefetchScalarGridSpec(
            num_scalar_prefetch=2,
            grid=(n_rows // br,),
            in_specs=[pl.BlockSpec((br, d), lambda i, be, bv: (i, 0)),
                      pl.BlockSpec((1, d, ff), lambda i, be, bv: (be[i], 0, 0)),
                      pl.BlockSpec((1, d, ff), lambda i, be, bv: (be[i], 0, 0)),
                      pl.BlockSpec((1, ff, d), lambda i, be, bv: (be[i], 0, 0))],
            out_specs=pl.BlockSpec((br, d), lambda i, be, bv: (i, 0))),
        compiler_params=_params("arbitrary"),
        name="moe_experts",
    )(block_expert, block_valid, xs, w1, w3, w2)


def _combine_body(base_ref, nchunk_ref, loff_ref, h_ref, q_ref, g_ref, ys_hbm, sw1_ref, sw3_ref, sw2_ref,
                  x1_ref, mod_ref, ln_ref, out_ref, stage_ref, sem):
    i = pl.program_id(0)

    @pl.when(i == 0)
    def _():
        stage_ref[...] = jnp.zeros_like(stage_ref)

    def copy(local_row, global_row):
        return pltpu.make_async_copy(ys_hbm.at[pl.ds(global_row, MOE_CHUNK)], stage_ref.at[pl.ds(local_row, MOE_CHUNK)], sem)

    total = _chunk_loops(i, base_ref, nchunk_ref, loff_ref, copy)

    h = h_ref[...].astype(BF16)
    h1 = jnp.dot(h, sw1_ref[...], preferred_element_type=F32)
    h3 = jnp.dot(h, sw3_ref[...], preferred_element_type=F32)
    y = jnp.dot((h1 * jax.nn.sigmoid(h1) * h3).astype(BF16), sw2_ref[...], preferred_element_type=F32)
    tm, rows = h_ref.shape[0], stage_ref.shape[0]
    q8, g8 = q_ref[0], g_ref[0]
    pos = lax.broadcasted_iota(jnp.int32, (tm, rows), 1)
    gate_mat = jnp.zeros((tm, rows), F32)
    for j in range(TOP_K):
        gate_mat = jnp.where(pos == q8[:, j:j + 1], g8[:, j:j + 1], gate_mat)

    def drain(c, carry):
        copy(0, 0).wait()
        return carry

    lax.fori_loop(0, total, drain, 0)
    y = y + jnp.dot(gate_mat.astype(BF16), stage_ref[...].astype(BF16), preferred_element_type=F32)
    out_ref[...] = _layer_norm_rows(DN_ALPHA * x1_ref[...] + mod_ref[0, 0, 5:6, :] * y, ln_ref[0:1, :], ln_ref[1:2, :])


def _moe_combine(tokens, q_cols, g_cols, dest_base, nchunk, loff, ys, sw1, sw3, sw2, x1, modtab, ln, tiles_per_batch,
                 first_tile):
    n, d = tokens.shape
    tm = TOKEN_TILE
    ff = sw1.shape[-1]
    whole = lambda shape: pl.BlockSpec(shape, lambda i, *_: (0,) * len(shape))
    rows = pl.BlockSpec((tm, d), lambda i, *_: (i, 0))
    slots = pl.BlockSpec((1, tm, MOE_SLOTS), lambda i, *_: (i, 0, 0))
    return pl.pallas_call(
        _combine_body,
        out_shape=jax.ShapeDtypeStruct((n, d), F32),
        grid_spec=pltpu.PrefetchScalarGridSpec(
            num_scalar_prefetch=3,
            grid=(n // tm,),
            in_specs=[rows, slots, slots, pl.BlockSpec(memory_space=pl.ANY),
                      whole((d, ff)), whole((d, ff)), whole((ff, d)),
                      rows, _mod_spec(tiles_per_batch, first_tile, True), whole(ln.shape)],
            out_specs=rows,
            scratch_shapes=[pltpu.VMEM((MOE_STAGE_ROWS, d), F32), pltpu.SemaphoreType.DMA]),
        compiler_params=_params("arbitrary"),
        name="moe_combine",
    )(dest_base, nchunk, loff, tokens, q_cols, g_cols, ys, sw1, sw3, sw2, x1, modtab, ln)


def _moe_block(h2, x1, modtab, ln, first_tile, router_w, router_b, w1, w3, w2, sw1, sw3, sw2):
    B, Tp, d = h2.shape
    tokens = h2.reshape(B * Tp, d)
    n = B * Tp
    tm, br = TOKEN_TILE, MOE_ROW_BLOCK
    q_rows, g_rows, cnt = _moe_route(tokens, router_w, router_b)
    cnt = cnt[:, :, 0].astype(jnp.int32)
    nchunk = (cnt + MOE_CHUNK - 1) // MOE_CHUNK
    run = nchunk * MOE_CHUNK
    total = jnp.sum(run, axis=0)
    padded = (total + br - 1) // br * br
    p_end = jnp.cumsum(padded)
    offs = p_end - padded
    dest_base = (offs[None, :] + jnp.cumsum(run, axis=0) - run).reshape(-1).astype(jnp.int32)
    loff = jnp.cumsum(run, axis=1) - run
    nchunk, loff = nchunk.reshape(-1).astype(jnp.int32), loff.reshape(-1).astype(jnp.int32)
    n_blocks = -(-(n * TOP_K + (n // tm) * N_EXPERTS * (MOE_CHUNK - 1) + N_EXPERTS * (br - 1)) // br)
    blk_start = jnp.arange(n_blocks, dtype=jnp.int32) * br
    block_expert = jnp.minimum(jnp.sum(blk_start[:, None] >= p_end[None, :], axis=1), N_EXPERTS - 1).astype(jnp.int32)
    block_valid = jnp.clip(total[block_expert] - (blk_start - offs[block_expert]), 0, br).astype(jnp.int32)
    xs = _moe_dispatch(tokens, q_rows, dest_base, nchunk, loff, block_valid)
    ys = _moe_experts(xs, block_expert, block_valid, w1.astype(BF16), w3.astype(BF16), w2.astype(BF16))
    q_cols, g_cols = jnp.swapaxes(q_rows, 1, 2), jnp.swapaxes(g_rows, 1, 2)
    out = _moe_combine(tokens, q_cols, g_cols, dest_base, nchunk, loff, ys,
                       sw1.astype(BF16), sw3.astype(BF16), sw2.astype(BF16),
                       x1.reshape(n, d), modtab, ln, Tp // tm, first_tile)
    return out.reshape(B, Tp, d)


def _low_half():
    return lax.broadcasted_iota(jnp.int32, (1, LANES), 1) < LANES // 2


def _pair_attention(q, parts):
    low = _low_half()
    outs = []
    for use_low in (True, False):
        qm = jnp.where(low == use_low, q, jnp.zeros_like(q))
        scores = []
        for k, _, b_lo, b_hi in parts:
            s = lax.dot_general(qm, k, (((1,), (1,)), ((), ())), preferred_element_type=F32) * HEAD_DIM ** -0.5
            b = b_lo if use_low else b_hi
            scores.append(s if b is None else s + b)
        m = functools.reduce(jnp.maximum, [jnp.max(s, axis=-1, keepdims=True) for s in scores])
        den = 0.0
        num = 0.0
        for s, (_, v, _, _) in zip(scores, parts):
            p = jnp.exp(s - m)
            den = den + jnp.sum(p, axis=-1, keepdims=True)
            num = num + jnp.dot(p.astype(BF16), v, preferred_element_type=F32)
        outs.append(num / den)
    return jnp.where(low, outs[0], outs[1])


def _na_body(q_ref, k_ref, v_ref, bias_ref, o_ref, *, n_ctx, rows):
    s = pl.program_id(1)
    ctx_blocks = n_ctx // GRID_W
    n_tiles = NA_DIM // LANES
    tile = lambda t: slice(t * LANES, (t + 1) * LANES)

    @pl.when(s < ctx_blocks)
    def _():
        for t in range(n_tiles):
            part = (k_ref[0, 0:n_ctx, tile(t)], v_ref[0, 0:n_ctx, tile(t)], None, None)
            o_ref[0, :, tile(t)] = _pair_attention(q_ref[0, :, tile(t)], [part]).astype(o_ref.dtype)

    @pl.when(s >= ctx_blocks)
    def _():
        r = s - ctx_blocks
        start = jnp.clip(r - NA_WIN_ROWS // 2, 0, rows - NA_WIN_ROWS)
        off = pl.multiple_of(n_ctx + start * GRID_W, GRID_W)
        band = pl.ds(off, NA_WIN_ROWS * GRID_W)
        for t in range(n_tiles):
            parts = [(k_ref[0, band, tile(t)], v_ref[0, band, tile(t)], bias_ref[0, 2 * t], bias_ref[0, 2 * t + 1]),
                     (k_ref[0, 0:n_ctx, tile(t)], v_ref[0, 0:n_ctx, tile(t)], None, None)]
            o_ref[0, :, tile(t)] = _pair_attention(q_ref[0, :, tile(t)], parts).astype(o_ref.dtype)


def _na_bias_table(rpb):
    kc = NA_WIN_COLS
    cidx = jnp.arange(GRID_W)
    col_start = jnp.clip(cidx - kc // 2, 0, GRID_W - kc)
    col_in = (cidx[None, :] >= col_start[:, None]) & (cidx[None, :] < col_start[:, None] + kc)
    d_col = jnp.clip(cidx[None, :] - cidx[:, None], -(kc - 1), kc - 1) + kc - 1
    tab = jnp.where(col_in, rpb[:, :, d_col], -jnp.inf)
    d_rows = jnp.arange(NA_WIN_ROWS)[:, None] + jnp.arange(NA_WIN_ROWS)[None, :]
    tab = tab[:, d_rows]
    return jnp.transpose(tab, (1, 0, 3, 2, 4)).reshape(NA_WIN_ROWS, NA_HEADS, GRID_W, NA_WIN_ROWS * GRID_W)


def _na_attention(pa, rpb, n_ctx):
    B, T, _ = pa.shape
    rows = (T - n_ctx) // GRID_W
    ctx_blocks = n_ctx // GRID_W

    def bias_idx(b, s):
        r = jnp.maximum(s - ctx_blocks, 0)
        return (jnp.clip(r - NA_WIN_ROWS // 2, 0, rows - NA_WIN_ROWS) - r + NA_WIN_ROWS - 1, 0, 0, 0)

    return pl.pallas_call(
        functools.partial(_na_body, n_ctx=n_ctx, rows=rows),
        out_shape=jax.ShapeDtypeStruct((B, T, NA_DIM), BF16),
        grid=(B, T // GRID_W),
        in_specs=[pl.BlockSpec((1, GRID_W, NA_DIM), lambda b, s: (b, s, 0)),
                  pl.BlockSpec((1, T, NA_DIM), lambda b, s: (b, 0, 1)),
                  pl.BlockSpec((1, T, NA_DIM), lambda b, s: (b, 0, 2)),
                  pl.BlockSpec((1, NA_HEADS, GRID_W, NA_WIN_ROWS * GRID_W), bias_idx)],
        out_specs=pl.BlockSpec((1, GRID_W, NA_DIM), lambda b, s: (b, s, 0)),
        compiler_params=_params("arbitrary", "arbitrary"),
        name="na_attention",
    )(pa, pa, pa, _na_bias_table(rpb))


def _rms_pair(x, gain):
    low = _low_half()
    sq = x * x
    s_lo = jnp.sum(jnp.where(low, sq, 0.0), axis=-1, keepdims=True)
    s_hi = jnp.sum(jnp.where(low, 0.0, sq), axis=-1, keepdims=True)
    ms = jnp.where(low, s_lo, s_hi) * (1.0 / HEAD_DIM)
    return x * lax.rsqrt(ms + NORM_EPS) * gain


def _rope_pair(x, cos, sin_signed):
    even = lax.broadcasted_iota(jnp.int32, (1, LANES), 1) % 2 == 0
    partner = jnp.where(even, pltpu.roll(x, LANES - 1, axis=1), pltpu.roll(x, 1, axis=1))
    return x * cos + partner * sin_signed


def _gqa_body(q_ref, k_ref, v_ref, cos_q, sin_q, cos_k, sin_k, gain_ref, o_ref, kn_ref, vn_ref, *, n_ctx):
    s = pl.program_id(1)
    tq = q_ref.shape[1]

    @pl.when(s == 0)
    def _():
        kn_ref[...] = _rope_pair(_rms_pair(k_ref[0], gain_ref[1:2, :]), cos_k[...], sin_k[...]).astype(BF16)
        vn_ref[...] = v_ref[0].astype(BF16)

    def run(n_keys):
        k, v = kn_ref[0:n_keys, :], vn_ref[0:n_keys, :]
        for t in range(GQA_Q_DIM // LANES):
            q = q_ref[0, :, t * LANES:(t + 1) * LANES]
            qn = _rope_pair(_rms_pair(q, gain_ref[0:1, :]), cos_q[...], sin_q[...]).astype(BF16)
            o_ref[0, :, t * LANES:(t + 1) * LANES] = _pair_attention(qn, [(k, v, None, None)]).astype(o_ref.dtype)

    @pl.when(s < n_ctx // tq)
    def _():
        run(n_ctx)

    @pl.when(s >= n_ctx // tq)
    def _():
        run(kn_ref.shape[0])


GQA_HEAD_ORDER = (0, 4, 1, 5, 2, 6, 3, 7)


def _axial_rope(n_tokens):
    t = jnp.arange(n_tokens)
    row = (t // GRID_W).astype(F32)
    col = (t % GRID_W).astype(F32)
    inv = ROPE_THETA ** (-jnp.arange(0, ROPE_AXIS_DIM, 2, dtype=F32) / ROPE_AXIS_DIM)
    ang = jnp.concatenate([row[:, None] * inv, col[:, None] * inv], -1)
    return jnp.cos(ang), jnp.sin(ang)


def _gqa_rope_tables(T, n_ctx):
    cos, sin = _axial_rope(T - n_ctx)
    cos = jnp.concatenate([jnp.ones((n_ctx, ROPE_AXIS_DIM), F32), cos], 0)
    sin = jnp.concatenate([jnp.zeros((n_ctx, ROPE_AXIS_DIM), F32), sin], 0)
    cos = jnp.tile(jnp.repeat(cos, 2, axis=-1), (1, 2))
    sign = jnp.tile(jnp.array([-1.0, 1.0], F32), LANES // 2)
    sin = jnp.tile(jnp.repeat(sin, 2, axis=-1), (1, 2)) * sign
    return cos, sin


def _gqa_attention(pb, qk_gain, n_ctx):
    B, T, _ = pb.shape
    tq = GQA_Q_TILE
    cos, sin = _gqa_rope_tables(T, n_ctx)
    gain = jnp.tile(qk_gain, (1, 2))
    kv_blk = GQA_Q_DIM // GQA_KV_DIM
    return pl.pallas_call(
        functools.partial(_gqa_body, n_ctx=n_ctx),
        out_shape=jax.ShapeDtypeStruct((B, T, GQA_Q_DIM), BF16),
        grid=(B, T // tq),
        in_specs=[pl.BlockSpec((1, tq, GQA_Q_DIM), lambda b, s: (b, s, 0)),
                  pl.BlockSpec((1, T, GQA_KV_DIM), lambda b, s: (b, 0, kv_blk)),
                  pl.BlockSpec((1, T, GQA_KV_DIM), lambda b, s: (b, 0, kv_blk + 1)),
                  pl.BlockSpec((tq, LANES), lambda b, s: (s, 0)),
                  pl.BlockSpec((tq, LANES), lambda b, s: (s, 0)),
                  pl.BlockSpec((T, LANES), lambda b, s: (0, 0)),
                  pl.BlockSpec((T, LANES), lambda b, s: (0, 0)),
                  pl.BlockSpec((2, LANES), lambda b, s: (0, 0))],
        out_specs=pl.BlockSpec((1, tq, GQA_Q_DIM), lambda b, s: (b, s, 0)),
        scratch_shapes=[pltpu.VMEM((T, GQA_KV_DIM), BF16), pltpu.VMEM((T, GQA_KV_DIM), BF16)],
        compiler_params=_params("arbitrary", "arbitrary"),
        name="gqa_attention",
    )(pb, pb, pb, cos, sin, cos, sin, gain)


def _attention_mixers(x, modtab, w_in, rpb, qk_gain, n_ctx):
    order = jnp.array(GQA_HEAD_ORDER)
    qb_cols = 3 * NA_DIM + (order[:, None] * HEAD_DIM + jnp.arange(HEAD_DIM)[None, :]).reshape(-1)
    cols = jnp.concatenate([jnp.arange(3 * NA_DIM), qb_cols, jnp.arange(3 * NA_DIM + GQA_Q_DIM, w_in.shape[1])])
    w = w_in[:, cols].astype(BF16)
    pa, pb = _in_proj(x, modtab, w, ((0, 3 * NA_DIM), (3 * NA_DIM, w.shape[1])), (BF16, F32))
    return jnp.concatenate([_na_attention(pa, rpb, n_ctx), _gqa_attention(pb, qk_gain, n_ctx)], -1)


def _attention_w_out(w_out):
    order = jnp.array(GQA_HEAD_ORDER)
    rows = NA_DIM + (order[:, None] * HEAD_DIM + jnp.arange(HEAD_DIM)[None, :]).reshape(-1)
    return jnp.concatenate([w_out[:NA_DIM], w_out[rows]], 0)


def _log_sigmoid(x):
    return jnp.minimum(x, 0.0) - jnp.log(1.0 + jnp.exp(-jnp.abs(x)))


def _mlstm_body(qf, kf, vf, gcf, grf, qb, kb, vb, gcb, grb, bias_c, bias_r, hf_ref, hb_ref, c_ref, n_ref, m_ref):
    @pl.when(pl.program_id(1) == 0)
    def _():
        c_ref[...] = jnp.zeros_like(c_ref)
        n_ref[...] = jnp.zeros_like(n_ref)
        m_ref[...] = jnp.zeros_like(m_ref)

    L = MLSTM_CHUNK
    row = lax.broadcasted_iota(jnp.int32, (L, L), 0)
    col = lax.broadcasted_iota(jnp.int32, (L, L), 1)
    hi = lax.Precision.HIGHEST
    for d, (q_ref, k_ref, v_ref, gc_ref, gr_ref, h_ref) in enumerate(((qf, kf, vf, gcf, grf, hf_ref),
                                                                     (qb, kb, vb, gcb, grb, hb_ref))):
        seen = (col <= row) if d == 0 else (col >= row)
        seen_f = jnp.where(seen, 1.0, 0.0)
        g_col = gc_ref[0] + bias_c[...]
        g_row = gr_ref[0, 0] + bias_r[...]
        b_col = jnp.dot(seen_f, _log_sigmoid(g_col), preferred_element_type=F32, precision=hi)
        lf_row = _log_sigmoid(g_row)
        b_row = lax.dot_general(lf_row, seen_f, (((1,), (1,)), ((), ())), preferred_element_type=F32, precision=hi)
        b_end = jnp.sum(lf_row, axis=1, keepdims=True)
        for h in range(MLSTM_HEADS):
            gi, gf = d * 2 * MLSTM_HEADS + h, d * 2 * MLSTM_HEADS + MLSTM_HEADS + h
            sl = slice(h * MLSTM_HEAD, (h + 1) * MLSTM_HEAD)
            s_idx = d * MLSTM_HEADS + h
            c_state, n_state, m_state = c_ref[s_idx], n_ref[s_idx:s_idx + 1, :], m_ref[s_idx:s_idx + 1, 0:1]
            bc, br, ig_c, ig_r = b_col[:, gf:gf + 1], b_row[gf:gf + 1, :], g_col[:, gi:gi + 1], g_row[gi:gi + 1, :]
            be = b_end[gf:gf + 1, :]
            d_intra = jnp.where(seen, bc - br + ig_r, -jnp.inf)
            d_inter = bc + m_state
            m_t = jnp.maximum(d_inter, jnp.max(d_intra, axis=1, keepdims=True))
            q = (q_ref[0, :, sl] * MLSTM_HEAD ** -0.5).astype(BF16)
            k, v = k_ref[0, :, sl], v_ref[0, :, sl]
            kb16 = k.astype(BF16)
            s = lax.dot_general(q, kb16, (((1,), (1,)), ((), ())), preferred_element_type=F32) * jnp.exp(d_intra - m_t)
            w_inter = jnp.exp(d_inter - m_t)
            qc = lax.dot_general(q, c_state.astype(BF16), (((1,), (1,)), ((), ())), preferred_element_type=F32)
            num = jnp.dot(s.astype(BF16), v.astype(BF16), preferred_element_type=F32) + w_inter * qc
            qn = jnp.sum(q.astype(F32) * n_state, axis=1, keepdims=True)
            den = jnp.sum(s, axis=1, keepdims=True) + w_inter * qn
            h_ref[0, :, sl] = num / jnp.maximum(jnp.abs(den), jnp.exp(-m_t))
            d_state = be - bc + ig_c
            m_new = jnp.maximum(be + m_state, jnp.max(d_state, axis=0, keepdims=True))
            w_s = jnp.exp(d_state - m_new)
            w_c = jnp.exp(be + m_state - m_new)
            vk = lax.dot_general((v * w_s).astype(BF16), kb16, (((0,), (0,)), ((), ())), preferred_element_type=F32)
            c_ref[s_idx] = w_c * c_state + vk
            n_ref[s_idx:s_idx + 1, :] = w_c * n_state + jnp.sum(w_s * k, axis=0, keepdims=True)
            m_ref[s_idx:s_idx + 1, :] = jnp.broadcast_to(m_new, (1, LANES))


def _mlstm(pm, gates, gate_b, n_ctx):
    B, T, _ = pm.shape
    L = MLSTM_CHUNK
    nc, ncc = T // L, n_ctx // L
    ng = 4 * MLSTM_HEADS
    g_cols = jnp.pad(gates, ((0, 0), (0, 0), (0, LANES - ng)))
    g_rows = jnp.swapaxes(gates.reshape(B, nc, L, ng), 2, 3)
    bias = gate_b.reshape(ng)
    bias_c = jnp.pad(bias, (0, LANES - ng)).reshape(1, LANES)
    bias_r = bias.reshape(ng, 1)
    fwd = lambda c: c
    bwd = lambda c: jnp.where(c < ncc, ncc - 1 - c, nc - 1 - (c - ncc))
    blk = (1, L, MLSTM_DIM)
    seq = lambda order, j: pl.BlockSpec(blk, lambda b, c: (b, order(c), j))
    gcol = lambda order: pl.BlockSpec((1, L, LANES), lambda b, c: (b, order(c), 0))
    grow = lambda order: pl.BlockSpec((1, 1, ng, L), lambda b, c: (b, order(c), 0, 0))
    n_state = 2 * MLSTM_HEADS
    out_sds = jax.ShapeDtypeStruct((B, T, MLSTM_DIM), F32)
    return pl.pallas_call(
        _mlstm_body,
        out_shape=(out_sds, out_sds),
        grid=(B, nc),
        in_specs=[seq(fwd, 0), seq(fwd, 1), seq(fwd, 2), gcol(fwd), grow(fwd),
                  seq(bwd, 0), seq(bwd, 1), seq(bwd, 2), gcol(bwd), grow(bwd),
                  pl.BlockSpec((1, LANES), lambda b, c: (0, 0)), pl.BlockSpec((ng, 1), lambda b, c: (0, 0))],
        out_specs=(pl.BlockSpec(blk, lambda b, c: (b, fwd(c), 0)), pl.BlockSpec(blk, lambda b, c: (b, bwd(c), 0))),
        scratch_shapes=[pltpu.VMEM((n_state, MLSTM_HEAD, MLSTM_HEAD), F32), pltpu.VMEM((n_state, MLSTM_HEAD), F32),
                        pltpu.VMEM((n_state, LANES), F32)],
        compiler_params=_params("arbitrary", "arbitrary"),
        name="mlstm",
    )(pm, pm, pm, g_cols, g_rows, pm, pm, pm, g_cols, g_rows, bias_c, bias_r)


def _rms_norm(x, g):
    return x * lax.rsqrt(jnp.mean(jnp.square(x), -1, keepdims=True) + NORM_EPS) * g


def _segment_shift(p, n_ctx):
    def shift(x):
        xp = jnp.pad(x, ((0, 0), (1, 1), (0, 0)))
        return 0.5 * (xp[:, :-2] + xp[:, 2:])
    return jnp.concatenate([shift(p[:, :n_ctx]), shift(p[:, n_ctx:])], axis=1)


def _to_state_lanes(x):
    nd, B, T, _ = x.shape
    y = jnp.transpose(x.reshape(nd, B, T, RWKV_HEADS, RWKV_HEAD), (2, 4, 0, 1, 3)).reshape(T, RWKV_HEAD, nd * B * RWKV_HEADS)
    return jnp.concatenate([y, y], -1) if nd == 1 else y


def _rwkv7_bidirectional(p, n_ctx, mu, w0, w1, w2, a0, a1, a2, g1, g2, kvec, r_k, gn):
    d = _segment_shift(p, n_ctx) - p
    (r, k, v, z), (dr, dk, dv, dz) = jnp.split(p, 4, -1), jnp.split(d, 4, -1)
    r, k, v, z_w, z_a, z_g = r + dr * mu[0], k + dk * mu[1], v + dv * mu[2], z + dz * mu[3], z + dz * mu[4], z + dz * mu[5]
    B, T, C = r.shape
    assert 2 * B * RWKV_HEADS == LANES
    heads = lambda t: t.reshape(t.shape[:-1] + (RWKV_HEADS, RWKV_HEAD))
    w_pre = w0[:, None, None] + jnp.einsum('dbtr,drc->dbtc', jnp.tanh(jnp.einsum('btc,dcr->dbtr', z_w, w1)), w2)
    decay = jnp.exp(-jnp.exp(-jax.nn.softplus(-w_pre) - 0.5))
    iclr = jax.nn.sigmoid(a0[:, None, None] + jnp.einsum('dbtr,drc->dbtc', jnp.einsum('btc,dcr->dbtr', z_a, a1), a2))
    gate = jax.nn.sigmoid(z_g @ g1) @ g2
    kk = heads(k * kvec[0])
    kk = (kk * lax.rsqrt(jnp.maximum(jnp.sum(jnp.square(kk), -1, keepdims=True), 1e-24))).reshape(B, T, C)
    k_eff = k[None] * (1.0 + (iclr - 1.0) * kvec[1])
    xs = (_to_state_lanes(r[None]), _to_state_lanes(decay), _to_state_lanes(k_eff), _to_state_lanes(v[None]),
          _to_state_lanes(-kk[None]), _to_state_lanes(kk[None] * iclr))
    yf, yb = _rwkv_scan(xs, n_ctx)
    half = LANES // 2
    y = yf[:, :, :half] + yb[:, :, half:]
    y = jnp.transpose(y.reshape(T, RWKV_HEAD, B, RWKV_HEADS), (2, 0, 3, 1))
    mu_y = jnp.mean(y, -1, keepdims=True)
    var_y = jnp.mean(jnp.square(y - mu_y), -1, keepdims=True)
    y = (y - mu_y) * lax.rsqrt(var_y + RWKV_GN_EPS) * heads(gn[0]) + heads(gn[1])
    bonus = jnp.sum(heads(r)[None] * heads(k_eff) * r_k, axis=-1, keepdims=True).sum(0) * heads(v)
    return (y + bonus).reshape(B, T, C) * gate


def _recurrent_mixers(x, modtab, w_in, n_ctx, mu, w0, w1, w2, a0, a1, a2, g1, g2, kvec, r_k, gn, gate_b, norm_g):
    B, T, _ = x.shape
    n_main = RWKV_IN + 4 * MLSTM_DIM
    w = jnp.pad(w_in, ((0, 0), (0, n_main + LANES - w_in.shape[1]))).astype(BF16)
    pr, pm, pg = _in_proj(x, modtab, w, ((0, RWKV_IN), (RWKV_IN, n_main), (n_main, n_main + LANES)), (F32, F32, F32))
    out_r = _rwkv7_bidirectional(pr, n_ctx, mu, w0, w1, w2, a0, a1, a2, g1, g2, kvec, r_k, gn)
    h_f, h_b = _mlstm(pm, pg[..., :4 * MLSTM_HEADS], gate_b, n_ctx)
    h = _rms_norm((h_f + h_b).reshape(B, T, MLSTM_HEADS, MLSTM_HEAD), norm_g.reshape(MLSTM_HEADS, MLSTM_HEAD))
    out_m = h.reshape(B, T, MLSTM_DIM) * jax.nn.sigmoid(pm[..., 3 * MLSTM_DIM:])
    return jnp.concatenate([out_r, out_m], -1).astype(BF16)


def kernel(x, c, ctx, c_ctx, ada_w, ada_b, ln_g, ln_b, mix_w_out, att_w_in, na_rpb, qk_gain, rec_w_in, rwkv_mu, rwkv_w0, rwkv_w1, rwkv_w2, rwkv_a0, rwkv_a1, rwkv_a2, rwkv_g1, rwkv_g2, rwkv_kvec, rwkv_rk, rwkv_gn, mlstm_gate_b, mlstm_norm, moe_router, moe_bias, moe_w1, moe_w3, moe_w2, shared_w1, shared_w3, shared_w2):
    B, S, D = x.shape
    n_ctx = ctx.shape[1]
    assert D == D_MODEL and n_ctx % TOKEN_TILE == 0 and S % TOKEN_TILE == 0
    xs = jnp.concatenate([ctx, x], axis=1)
    for i in range(DEPTH):
        last = i == DEPTH - 1
        j = i // 2
        mod = (jax.nn.silu(c) @ ada_w[i] + ada_b[i]).reshape(B, 6, D)
        mod_c = jnp.broadcast_to((jax.nn.silu(c_ctx) @ ada_w[i] + ada_b[i]).reshape(1, 6, D), (B, 6, D))
        modtab = jnp.pad(jnp.stack([mod_c, mod], axis=1), ((0, 0), (0, 0), (0, MOD_ROWS - 6), (0, 0)))
        if i % 2 == 0:
            m = _attention_mixers(xs, modtab, att_w_in[j], na_rpb[j], qk_gain[j], n_ctx)
            w_out = _attention_w_out(mix_w_out[i])
        else:
            m = _recurrent_mixers(xs, modtab, rec_w_in[j], n_ctx, rwkv_mu[j], rwkv_w0[j], rwkv_w1[j], rwkv_w2[j],
                                  rwkv_a0[j], rwkv_a1[j], rwkv_a2[j], rwkv_g1[j], rwkv_g2[j], rwkv_kvec[j],
                                  rwkv_rk[j], rwkv_gn[j], mlstm_gate_b[j], mlstm_norm[j])
            w_out = mix_w_out[i]
        first_tile = n_ctx // TOKEN_TILE if last else 0
        x1, h2 = _out_proj(m, xs, modtab, w_out.astype(BF16), jnp.stack([ln_g[i, 0], ln_b[i, 0]]), first_tile)
        xs = _moe_block(h2, x1, modtab, jnp.stack([ln_g[i, 1], ln_b[i, 1]]), first_tile, moe_router[i], moe_bias[i],
                        moe_w1[i], moe_w3[i], moe_w2[i], shared_w1[i], shared_w3[i], shared_w2[i])
    return xs
```

```python
import functools

import jax
import jax.numpy as jnp
from jax import lax
from jax.experimental import pallas as pl
from jax.experimental.pallas import tpu as pltpu

D_MODEL = 1024
DEPTH = 4
GRID_W = 64
HEAD_DIM = 64
NA_HEADS = 8
NA_WIN_ROWS = 8
NA_WIN_COLS = 16
GQA_Q_HEADS = 8
GQA_KV_HEADS = 2
ROPE_THETA = 10000.0
ROPE_AXIS_DIM = HEAD_DIM // 2
NA_DIM = NA_HEADS * HEAD_DIM
GQA_Q_DIM = GQA_Q_HEADS * HEAD_DIM
GQA_KV_DIM = GQA_KV_HEADS * HEAD_DIM
RWKV_HEADS = 8
RWKV_HEAD = 64
RWKV_DIM = RWKV_HEADS * RWKV_HEAD
RWKV_GN_EPS = 64e-5
RWKV_IN = 4 * RWKV_DIM
MLSTM_HEADS = 4
MLSTM_HEAD = 128
MLSTM_DIM = MLSTM_HEADS * MLSTM_HEAD
MLSTM_CHUNK = 64
N_EXPERTS = 64
TOP_K = 6
ROUTED_SCALE = 2.5
DN_ALPHA = (2 * DEPTH) ** 0.25
LN_EPS = 1e-5
NORM_EPS = 1e-6
F32 = jnp.float32
BF16 = jnp.bfloat16

LANES = 128
SUBLANES = 8
VMEM_LIMIT = 48 * 1024 * 1024
TOKEN_TILE = 256
RWKV_TIME_BLOCK = 16
GQA_Q_TILE = 128
MOE_ROW_BLOCK = 256
MOE_SLOTS = 8
MOE_CHUNK = SUBLANES
MOE_STAGE_ROWS = TOKEN_TILE * TOP_K + N_EXPERTS * MOE_CHUNK
MOD_ROWS = 8


def _params(*semantics):
    return pltpu.CompilerParams(dimension_semantics=semantics, vmem_limit_bytes=VMEM_LIMIT)


def _mod_spec(tiles_per_batch, first_tile, flat):
    if flat:
        idx = lambda i, *_: (i // tiles_per_batch, jnp.minimum(i % tiles_per_batch + first_tile, 1), 0, 0)
    else:
        idx = lambda b, s, *_: (b, jnp.minimum(s + first_tile, 1), 0, 0)
    return pl.BlockSpec((1, 1, MOD_ROWS, D_MODEL), idx)


def _layer_norm_rows(z, g, b):
    mu = jnp.mean(z, axis=-1, keepdims=True)
    zc = z - mu
    var = jnp.mean(zc * zc, axis=-1, keepdims=True)
    return zc * lax.rsqrt(var + LN_EPS) * g + b


def _in_proj_body(x_ref, mod_ref, w_ref, *out_refs, splits):
    h = (x_ref[0] * (1.0 + mod_ref[0, 0, 1:2, :]) + mod_ref[0, 0, 0:1, :]).astype(BF16)
    for o_ref, (c0, c1) in zip(out_refs, splits):
        o_ref[0] = jnp.dot(h, w_ref[:, c0:c1], preferred_element_type=F32).astype(o_ref.dtype)


def _in_proj(x, modtab, w, splits, dtypes):
    B, T, D = x.shape
    tm = TOKEN_TILE
    outs = tuple(jax.ShapeDtypeStruct((B, T, c1 - c0), dt) for (c0, c1), dt in zip(splits, dtypes))
    return pl.pallas_call(
        functools.partial(_in_proj_body, splits=splits),
        out_shape=outs,
        grid=(B, T // tm),
        in_specs=[pl.BlockSpec((1, tm, D), lambda b, s: (b, s, 0)), _mod_spec(T // tm, 0, False),
                  pl.BlockSpec(w.shape, lambda b, s: (0, 0))],
        out_specs=tuple(pl.BlockSpec((1, tm, c1 - c0), lambda b, s: (b, s, 0)) for c0, c1 in splits),
        compiler_params=_params("arbitrary", "arbitrary"),
        name="in_proj",
    )(x, modtab, w)


def _out_proj_body(m_ref, x_ref, mod_ref, w_ref, ln_ref, x1_ref, h2_ref):
    y = jnp.dot(m_ref[0], w_ref[...], preferred_element_type=F32)
    mod = mod_ref[0, 0]
    x1 = _layer_norm_rows(DN_ALPHA * x_ref[0] + mod[2:3, :] * y, ln_ref[0:1, :], ln_ref[1:2, :])
    x1_ref[0] = x1
    h2_ref[0] = x1 * (1.0 + mod[4:5, :]) + mod[3:4, :]


def _out_proj(m, x, modtab, w, ln, first_tile):
    B, T, D = x.shape
    tm = TOKEN_TILE
    n_tiles = T // tm - first_tile
    rows = lambda b, s: (b, s + first_tile, 0)
    out_sds = jax.ShapeDtypeStruct((B, n_tiles * tm, D), F32)
    return pl.pallas_call(
        _out_proj_body,
        out_shape=(out_sds, out_sds),
        grid=(B, n_tiles),
        in_specs=[pl.BlockSpec((1, tm, m.shape[-1]), rows), pl.BlockSpec((1, tm, D), rows),
                  _mod_spec(T // tm, first_tile, False),
                  pl.BlockSpec(w.shape, lambda b, s: (0, 0)), pl.BlockSpec(ln.shape, lambda b, s: (0, 0))],
        out_specs=(pl.BlockSpec((1, tm, D), lambda b, s: (b, s, 0)),) * 2,
        compiler_params=_params("arbitrary", "arbitrary"),
        name="out_proj",
    )(m, x, modtab, w, ln)


def _rwkv_scan_body(rf, wf, kf, vf, nf, bf, rb, wb, kb, vb, nb, bb, yf_ref, yb_ref, state_ref, *, tc):
    @pl.when(pl.program_id(0) == 0)
    def _():
        state_ref[...] = jnp.zeros_like(state_ref)

    fwd_lane = lax.broadcasted_iota(jnp.int32, (RWKV_HEAD, LANES), 1) < LANES // 2

    def step(j, carry):
        jb = tc - 1 - j

        def sel(f, b):
            return jnp.where(fwd_lane, f[j], b[jb])

        r, w, k, v, kkn, bv = sel(rf, rb), sel(wf, wb), sel(kf, kb), sel(vf, vb), sel(nf, nb), sel(bf, bb)
        for vi in range(RWKV_HEAD):
            s = state_ref[vi]
            sa = jnp.sum(s * kkn, axis=0, keepdims=True)
            s2 = s * w + sa * bv + v[vi:vi + 1, :] * k
            state_ref[vi] = s2
            yrow = jnp.sum(s2 * r, axis=0, keepdims=True)
            yf_ref[j, pl.ds(vi, 1), :] = yrow
            yb_ref[jb, pl.ds(vi, 1), :] = yrow
        return carry

    lax.fori_loop(0, tc, step, 0)


def _rwkv_scan(xs, n_ctx):
    T = xs[0].shape[0]
    tc = RWKV_TIME_BLOCK
    assert T % tc == 0 and n_ctx % tc == 0
    nc, ncc = T // tc, n_ctx // tc
    blk = (tc, RWKV_HEAD, LANES)
    fwd = lambda c: (c, 0, 0)
    bwd = lambda c: (jnp.where(c < ncc, ncc - 1 - c, nc - 1 - (c - ncc)), 0, 0)
    out_sds = jax.ShapeDtypeStruct((T, RWKV_HEAD, LANES), F32)
    return pl.pallas_call(
        functools.partial(_rwkv_scan_body, tc=tc),
        out_shape=(out_sds, out_sds),
        grid=(nc,),
        in_specs=[pl.BlockSpec(blk, fwd)] * 6 + [pl.BlockSpec(blk, bwd)] * 6,
        out_specs=(pl.BlockSpec(blk, fwd), pl.BlockSpec(blk, bwd)),
        scratch_shapes=[pltpu.VMEM((RWKV_HEAD, RWKV_HEAD, LANES), F32)],
        compiler_params=_params("arbitrary"),
        name="rwkv_scan",
    )(*xs, *xs)


def _route_body(x_ref, rwt_ref, rb_ref, upper_ref, lower_ref, q_ref, g_ref, cnt_ref):
    tm = x_ref.shape[0]
    logits = lax.dot_general(rwt_ref[...], x_ref[...], (((1,), (1,)), ((), ())), preferred_element_type=F32,
                             precision=lax.Precision.HIGHEST)
    scores = jax.nn.sigmoid(logits)
    sel = scores + rb_ref[...]
    eidx = lax.broadcasted_iota(jnp.int32, (N_EXPERTS, tm), 0)
    slot = lax.broadcasted_iota(jnp.int32, (MOE_SLOTS, tm), 0)
    onehots = []
    for _ in range(TOP_K):
        m = jnp.max(sel, axis=0, keepdims=True)
        ij = jnp.min(jnp.where(sel == m, eidx, N_EXPERTS), axis=0, keepdims=True)
        oh = eidx == ij
        onehots.append(oh)
        sel = jnp.where(oh, -jnp.inf, sel)
    mask = functools.reduce(jnp.logical_or, onehots)
    maskf = jnp.where(mask, 1.0, 0.0)
    gsum = jnp.sum(jnp.where(mask, scores, 0.0), axis=0, keepdims=True)
    gates = scores / gsum * ROUTED_SCALE
    cnt = jnp.sum(maskf, axis=1, keepdims=True)
    cnt_pad = jnp.ceil(cnt * (1.0 / MOE_CHUNK)) * MOE_CHUNK
    lrank = jnp.dot(maskf.astype(BF16), upper_ref[...], preferred_element_type=F32)
    loff = jnp.dot(lower_ref[...], jnp.broadcast_to(cnt_pad, (N_EXPERTS, LANES)).astype(BF16),
                   preferred_element_type=F32)[:, 0:1]
    q = loff + lrank
    q8 = jnp.full((MOE_SLOTS, tm), -1.0, F32)
    g8 = jnp.zeros((MOE_SLOTS, tm), F32)
    for j, oh in enumerate(onehots):
        q8 = jnp.where(slot == j, jnp.sum(jnp.where(oh, q, 0.0), axis=0, keepdims=True), q8)
        g8 = jnp.where(slot == j, jnp.sum(jnp.where(oh, gates, 0.0), axis=0, keepdims=True), g8)
    q_ref[0] = q8.astype(jnp.int32)
    g_ref[0] = g8
    cnt_ref[0] = jnp.broadcast_to(cnt, (N_EXPERTS, LANES))


def _moe_route(tokens, router_w, router_b):
    n, d = tokens.shape
    tm = TOKEN_TILE
    nt = n // tm
    ar = jnp.arange(tm)
    upper = (ar[:, None] < ar[None, :]).astype(BF16)
    ae = jnp.arange(N_EXPERTS)
    lower = (ae[:, None] > ae[None, :]).astype(BF16)
    const = lambda shape: pl.BlockSpec(shape, lambda i: (0,) * len(shape))
    return pl.pallas_call(
        _route_body,
        out_shape=(jax.ShapeDtypeStruct((nt, MOE_SLOTS, tm), jnp.int32), jax.ShapeDtypeStruct((nt, MOE_SLOTS, tm), F32),
                   jax.ShapeDtypeStruct((nt, N_EXPERTS, LANES), F32)),
        grid=(nt,),
        in_specs=[pl.BlockSpec((tm, d), lambda i: (i, 0)), const((N_EXPERTS, d)), const((N_EXPERTS, 1)),
                  const((tm, tm)), const((N_EXPERTS, N_EXPERTS))],
        out_specs=(pl.BlockSpec((1, MOE_SLOTS, tm), lambda i: (i, 0, 0)),
                   pl.BlockSpec((1, MOE_SLOTS, tm), lambda i: (i, 0, 0)),
                   pl.BlockSpec((1, N_EXPERTS, LANES), lambda i: (i, 0, 0))),
        compiler_params=_params("arbitrary"),
        name="moe_router",
    )(tokens, router_w.T, router_b.reshape(N_EXPERTS, 1), upper, lower)


def _chunk_loops(i, base_ref, nchunk_ref, loff_ref, copy):
    def per_expert(e, total):
        k = i * N_EXPERTS + e
        n, base, lo = nchunk_ref[k], base_ref[k], loff_ref[k]

        def piece(c, carry):
            copy(pl.multiple_of(lo + c * MOE_CHUNK, MOE_CHUNK), pl.multiple_of(base + c * MOE_CHUNK, MOE_CHUNK)).start()
            return carry

        lax.fori_loop(0, n, piece, 0)
        return total + n

    return lax.fori_loop(0, N_EXPERTS, per_expert, 0)


def _dispatch_body(base_ref, nchunk_ref, loff_ref, bv_ref, x_ref, q_ref, xs_hbm, stage_ref, zero_ref, sem_z, sem):
    i = pl.program_id(0)
    br = zero_ref.shape[0]
    n_blocks = xs_hbm.shape[0] // br

    @pl.when(i == 0)
    def _():
        zero_ref[...] = jnp.zeros_like(zero_ref)

        def fill_copy(b):
            return pltpu.make_async_copy(zero_ref, xs_hbm.at[pl.ds(b * br, br)], sem_z)

        def fill(b, carry):
            @pl.when(bv_ref[b] < br)
            def _():
                fill_copy(b).start()
            return carry

        def fill_wait(b, carry):
            @pl.when(bv_ref[b] < br)
            def _():
                fill_copy(b).wait()
            return carry

        lax.fori_loop(0, n_blocks, fill, 0)
        lax.fori_loop(0, n_blocks, fill_wait, 0)

    rows, tm = stage_ref.shape[0], x_ref.shape[0]
    q8 = q_ref[0]
    pos = lax.broadcasted_iota(jnp.int32, (rows, tm), 0)
    hit = functools.reduce(jnp.logical_or, [pos == q8[j:j + 1, :] for j in range(TOP_K)])
    perm = jnp.where(hit, 1.0, 0.0).astype(BF16)
    stage_ref[...] = jnp.dot(perm, x_ref[...].astype(BF16), preferred_element_type=F32)

    def copy(local_row, global_row):
        return pltpu.make_async_copy(stage_ref.at[pl.ds(local_row, MOE_CHUNK)], xs_hbm.at[pl.ds(global_row, MOE_CHUNK)], sem)

    total = _chunk_loops(i, base_ref, nchunk_ref, loff_ref, copy)

    def drain(c, carry):
        copy(0, 0).wait()
        return carry

    lax.fori_loop(0, total, drain, 0)


def _moe_dispatch(tokens, q_rows, dest_base, nchunk, loff, block_valid):
    n, d = tokens.shape
    tm, br = TOKEN_TILE, MOE_ROW_BLOCK
    n_rows = block_valid.shape[0] * br
    return pl.pallas_call(
        _dispatch_body,
        out_shape=jax.ShapeDtypeStruct((n_rows, d), F32),
        grid_spec=pltpu.PrefetchScalarGridSpec(
            num_scalar_prefetch=4,
            grid=(n // tm,),
            in_specs=[pl.BlockSpec((tm, d), lambda i, *_: (i, 0)),
                      pl.BlockSpec((1, MOE_SLOTS, tm), lambda i, *_: (i, 0, 0))],
            out_specs=pl.BlockSpec(memory_space=pl.ANY),
            scratch_shapes=[pltpu.VMEM((MOE_STAGE_ROWS, d), F32), pltpu.VMEM((br, d), F32),
                            pltpu.SemaphoreType.DMA, pltpu.SemaphoreType.DMA]),
        compiler_params=_params("arbitrary"),
        name="moe_dispatch",
    )(dest_base, nchunk, loff, block_valid, tokens, q_rows)


def _expert_body(be_ref, bv_ref, x_ref, w1_ref, w3_ref, w2_ref, y_ref, w1b, w3b, w2b):
    i = pl.program_id(0)
    valid = bv_ref[i]

    @pl.when(jnp.logical_or(i == 0, be_ref[i] != be_ref[jnp.maximum(i - 1, 0)]))
    def _():
        w1b[...] = w1_ref[0].astype(BF16)
        w3b[...] = w3_ref[0].astype(BF16)
        w2b[...] = w2_ref[0].astype(BF16)

    @pl.when(valid > 0)
    def _():
        row = lax.broadcasted_iota(jnp.int32, (x_ref.shape[0], 1), 0)
        x = jnp.where(row < valid, x_ref[...], 0.0).astype(BF16)
        h1 = jnp.dot(x, w1b[...], preferred_element_type=F32)
        h3 = jnp.dot(x, w3b[...], preferred_element_type=F32)
        a = (h1 * jax.nn.sigmoid(h1) * h3).astype(BF16)
        y_ref[...] = jnp.dot(a, w2b[...], preferred_element_type=F32)

    @pl.when(valid <= 0)
    def _():
        y_ref[...] = jnp.zeros_like(y_ref)


def _moe_experts(xs, block_expert, block_valid, w1, w3, w2):
    n_rows, d = xs.shape
    br = MOE_ROW_BLOCK
    ff = w1.shape[-1]
    return pl.pallas_call(
        _expert_body,
        out_shape=jax.ShapeDtypeStruct((n_rows, d), F32),
        grid_spec=pltpu.PrefetchScalarGridSpec(
            num_scalar_prefetch=2,
            grid=(n_rows // br,),
            in_specs=[pl.BlockSpec((br, d), lambda i, be, bv: (i, 0)),
                      pl.BlockSpec((1, d, ff), lambda i, be, bv: (be[i], 0, 0)),
                      pl.BlockSpec((1, d, ff), lambda i, be, bv: (be[i], 0, 0)),
                      pl.BlockSpec((1, ff, d), lambda i, be, bv: (be[i], 0, 0))],
            out_specs=pl.BlockSpec((br, d), lambda i, be, bv: (i, 0)),
            scratch_shapes=[pltpu.VMEM((d, ff), BF16), pltpu.VMEM((d, ff), BF16), pltpu.VMEM((ff, d), BF16)]),
        compiler_params=_params("arbitrary"),
        name="moe_experts",
    )(block_expert, block_valid, xs, w1, w3, w2)


def _combine_body(base_ref, nchunk_ref, loff_ref, h_ref, q_ref, g_ref, ys_hbm, sw1_ref, sw3_ref, sw2_ref,
                  x1_ref, mod_ref, ln_ref, out_ref, stage_ref, sem):
    i = pl.program_id(0)

    @pl.when(i == 0)
    def _():
        stage_ref[...] = jnp.zeros_like(stage_ref)

    def copy(local_row, global_row):
        return pltpu.make_async_copy(ys_hbm.at[pl.ds(global_row, MOE_CHUNK)], stage_ref.at[pl.ds(local_row, MOE_CHUNK)], sem)

    total = _chunk_loops(i, base_ref, nchunk_ref, loff_ref, copy)

    h = h_ref[...].astype(BF16)
    h1 = jnp.dot(h, sw1_ref[...], preferred_element_type=F32)
    h3 = jnp.dot(h, sw3_ref[...], preferred_element_type=F32)
    y = jnp.dot((h1 * jax.nn.sigmoid(h1) * h3).astype(BF16), sw2_ref[...], preferred_element_type=F32)
    tm, rows = h_ref.shape[0], stage_ref.shape[0]
    q8, g8 = q_ref[0], g_ref[0]
    pos = lax.broadcasted_iota(jnp.int32, (tm, rows), 1)
    gate_mat = jnp.zeros((tm, rows), F32)
    for j in range(TOP_K):
        gate_mat = jnp.where(pos == q8[:, j:j + 1], g8[:, j:j + 1], gate_mat)

    def drain(c, carry):
        copy(0, 0).wait()
        return carry

    lax.fori_loop(0, total, drain, 0)
    y = y + jnp.dot(gate_mat.astype(BF16), stage_ref[...].astype(BF16), preferred_element_type=F32)
    out_ref[...] = _layer_norm_rows(DN_ALPHA * x1_ref[...] + mod_ref[0, 0, 5:6, :] * y, ln_ref[0:1, :], ln_ref[1:2, :])


def _moe_combine(tokens, q_cols, g_cols, dest_base, nchunk, loff, ys, sw1, sw3, sw2, x1, modtab, ln, tiles_per_batch,
                 first_tile):
    n, d = tokens.shape
    tm = TOKEN_TILE
    ff = sw1.shape[-1]
    whole = lambda shape: pl.BlockSpec(shape, lambda i, *_: (0,) * len(shape))
    rows = pl.BlockSpec((tm, d), lambda i, *_: (i, 0))
    slots = pl.BlockSpec((1, tm, MOE_SLOTS), lambda i, *_: (i, 0, 0))
    return pl.pallas_call(
        _combine_body,
        out_shape=jax.ShapeDtypeStruct((n, d), F32),
        grid_spec=pltpu.PrefetchScalarGridSpec(
            num_scalar_prefetch=3,
            grid=(n // tm,),
            in_specs=[rows, slots, slots, pl.BlockSpec(memory_space=pl.ANY),
                      whole((d, ff)), whole((d, ff)), whole((ff, d)),
                      rows, _mod_spec(tiles_per_batch, first_tile, True), whole(ln.shape)],
            out_specs=rows,
            scratch_shapes=[pltpu.VMEM((MOE_STAGE_ROWS, d), F32), pltpu.SemaphoreType.DMA]),
        compiler_params=_params("arbitrary"),
        name="moe_combine",
    )(dest_base, nchunk, loff, tokens, q_cols, g_cols, ys, sw1, sw3, sw2, x1, modtab, ln)


def _moe_block(h2, x1, modtab, ln, first_tile, router_w, router_b, w1, w3, w2, sw1, sw3, sw2):
    B, Tp, d = h2.shape
    tokens = h2.reshape(B * Tp, d)
    n = B * Tp
    tm, br = TOKEN_TILE, MOE_ROW_BLOCK
    q_rows, g_rows, cnt = _moe_route(tokens, router_w, router_b)
    cnt = cnt[:, :, 0].astype(jnp.int32)
    nchunk = (cnt + MOE_CHUNK - 1) // MOE_CHUNK
    run = nchunk * MOE_CHUNK
    total = jnp.sum(run, axis=0)
    padded = (total + br - 1) // br * br
    p_end = jnp.cumsum(padded)
    offs = p_end - padded
    dest_base = (offs[None, :] + jnp.cumsum(run, axis=0) - run).reshape(-1).astype(jnp.int32)
    loff = jnp.cumsum(run, axis=1) - run
    nchunk, loff = nchunk.reshape(-1).astype(jnp.int32), loff.reshape(-1).astype(jnp.int32)
    n_blocks = -(-(n * TOP_K + (n // tm) * N_EXPERTS * (MOE_CHUNK - 1) + N_EXPERTS * (br - 1)) // br)
    blk_start = jnp.arange(n_blocks, dtype=jnp.int32) * br
    block_expert = jnp.minimum(jnp.sum(blk_start[:, None] >= p_end[None, :], axis=1), N_EXPERTS - 1).astype(jnp.int32)
    block_valid = jnp.clip(total[block_expert] - (blk_start - offs[block_expert]), 0, br).astype(jnp.int32)
    xs = _moe_dispatch(tokens, q_rows, dest_base, nchunk, loff, block_valid)
    ys = _moe_experts(xs, block_expert, block_valid, w1, w3, w2)
    q_cols, g_cols = jnp.swapaxes(q_rows, 1, 2), jnp.swapaxes(g_rows, 1, 2)
    out = _moe_combine(tokens, q_cols, g_cols, dest_base, nchunk, loff, ys,
                       sw1.astype(BF16), sw3.astype(BF16), sw2.astype(BF16),
                       x1.reshape(n, d), modtab, ln, Tp // tm, first_tile)
    return out.reshape(B, Tp, d)


def _low_half():
    return lax.broadcasted_iota(jnp.int32, (1, LANES), 1) < LANES // 2


def _pair_attention(q, parts):
    low = _low_half()
    outs = []
    for use_low in (True, False):
        qm = jnp.where(low == use_low, q, jnp.zeros_like(q))
        scores = []
        for k, _, b_lo, b_hi in parts:
            s = lax.dot_general(qm, k, (((1,), (1,)), ((), ())), preferred_element_type=F32) * HEAD_DIM ** -0.5
            b = b_lo if use_low else b_hi
            scores.append(s if b is None else s + b)
        m = functools.reduce(jnp.maximum, [jnp.max(s, axis=-1, keepdims=True) for s in scores])
        den = 0.0
        num = 0.0
        for s, (_, v, _, _) in zip(scores, parts):
            p = jnp.exp(s - m)
            den = den + jnp.sum(p, axis=-1, keepdims=True)
            num = num + jnp.dot(p.astype(BF16), v, preferred_element_type=F32)
        outs.append(num / den)
    return jnp.where(low, outs[0], outs[1])


def _na_body(q_ref, k_ref, v_ref, bias_ref, o_ref, *, n_ctx, rows):
    s = pl.program_id(1)
    ctx_blocks = n_ctx // GRID_W
    n_tiles = NA_DIM // LANES
    tile = lambda t: slice(t * LANES, (t + 1) * LANES)

    @pl.when(s < ctx_blocks)
    def _():
        for t in range(n_tiles):
            part = (k_ref[0, 0:n_ctx, tile(t)], v_ref[0, 0:n_ctx, tile(t)], None, None)
            o_ref[0, :, tile(t)] = _pair_attention(q_ref[0, :, tile(t)], [part]).astype(o_ref.dtype)

    @pl.when(s >= ctx_blocks)
    def _():
        r = s - ctx_blocks
        start = jnp.clip(r - NA_WIN_ROWS // 2, 0, rows - NA_WIN_ROWS)
        off = pl.multiple_of(n_ctx + start * GRID_W, GRID_W)
        band = pl.ds(off, NA_WIN_ROWS * GRID_W)
        for t in range(n_tiles):
            parts = [(k_ref[0, band, tile(t)], v_ref[0, band, tile(t)], bias_ref[0, 2 * t], bias_ref[0, 2 * t + 1]),
                     (k_ref[0, 0:n_ctx, tile(t)], v_ref[0, 0:n_ctx, tile(t)], None, None)]
            o_ref[0, :, tile(t)] = _pair_attention(q_ref[0, :, tile(t)], parts).astype(o_ref.dtype)


def _na_bias_table(rpb):
    kc = NA_WIN_COLS
    cidx = jnp.arange(GRID_W)
    col_start = jnp.clip(cidx - kc // 2, 0, GRID_W - kc)
    col_in = (cidx[None, :] >= col_start[:, None]) & (cidx[None, :] < col_start[:, None] + kc)
    d_col = jnp.clip(cidx[None, :] - cidx[:, None], -(kc - 1), kc - 1) + kc - 1
    tab = jnp.where(col_in, rpb[:, :, d_col], -jnp.inf)
    d_rows = jnp.arange(NA_WIN_ROWS)[:, None] + jnp.arange(NA_WIN_ROWS)[None, :]
    tab = tab[:, d_rows]
    return jnp.transpose(tab, (1, 0, 3, 2, 4)).reshape(NA_WIN_ROWS, NA_HEADS, GRID_W, NA_WIN_ROWS * GRID_W)


def _na_attention(pa, rpb, n_ctx):
    B, T, _ = pa.shape
    rows = (T - n_ctx) // GRID_W
    ctx_blocks = n_ctx // GRID_W

    def bias_idx(b, s):
        r = jnp.maximum(s - ctx_blocks, 0)
        return (jnp.clip(r - NA_WIN_ROWS // 2, 0, rows - NA_WIN_ROWS) - r + NA_WIN_ROWS - 1, 0, 0, 0)

    return pl.pallas_call(
        functools.partial(_na_body, n_ctx=n_ctx, rows=rows),
        out_shape=jax.ShapeDtypeStruct((B, T, NA_DIM), BF16),
        grid=(B, T // GRID_W),
        in_specs=[pl.BlockSpec((1, GRID_W, NA_DIM), lambda b, s: (b, s, 0)),
                  pl.BlockSpec((1, T, NA_DIM), lambda b, s: (b, 0, 1)),
                  pl.BlockSpec((1, T, NA_DIM), lambda b, s: (b, 0, 2)),
                  pl.BlockSpec((1, NA_HEADS, GRID_W, NA_WIN_ROWS * GRID_W), bias_idx)],
        out_specs=pl.BlockSpec((1, GRID_W, NA_DIM), lambda b, s: (b, s, 0)),
        compiler_params=_params("arbitrary", "arbitrary"),
        name="na_attention",
    )(pa, pa, pa, _na_bias_table(rpb))


def _rms_pair(x, gain):
    low = _low_half()
    sq = x * x
    s_lo = jnp.sum(jnp.where(low, sq, 0.0), axis=-1, keepdims=True)
    s_hi = jnp.sum(jnp.where(low, 0.0, sq), axis=-1, keepdims=True)
    ms = jnp.where(low, s_lo, s_hi) * (1.0 / HEAD_DIM)
    return x * lax.rsqrt(ms + NORM_EPS) * gain


def _rope_pair(x, cos, sin_signed):
    even = lax.broadcasted_iota(jnp.int32, (1, LANES), 1) % 2 == 0
    partner = jnp.where(even, pltpu.roll(x, LANES - 1, axis=1), pltpu.roll(x, 1, axis=1))
    return x * cos + partner * sin_signed


def _gqa_body(q_ref, k_ref, v_ref, cos_q, sin_q, cos_k, sin_k, gain_ref, o_ref, kn_ref, vn_ref, *, n_ctx):
    s = pl.program_id(1)
    tq = q_ref.shape[1]

    @pl.when(s == 0)
    def _():
        kn_ref[...] = _rope_pair(_rms_pair(k_ref[0], gain_ref[1:2, :]), cos_k[...], sin_k[...]).astype(BF16)
        vn_ref[...] = v_ref[0].astype(BF16)

    def run(n_keys):
        k, v = kn_ref[0:n_keys, :], vn_ref[0:n_keys, :]
        for t in range(GQA_Q_DIM // LANES):
            q = q_ref[0, :, t * LANES:(t + 1) * LANES]
            qn = _rope_pair(_rms_pair(q, gain_ref[0:1, :]), cos_q[...], sin_q[...]).astype(BF16)
            o_ref[0, :, t * LANES:(t + 1) * LANES] = _pair_attention(qn, [(k, v, None, None)]).astype(o_ref.dtype)

    @pl.when(s < n_ctx // tq)
    def _():
        run(n_ctx)

    @pl.when(s >= n_ctx // tq)
    def _():
        run(kn_ref.shape[0])


GQA_HEAD_ORDER = (0, 4, 1, 5, 2, 6, 3, 7)


def _axial_rope(n_tokens):
    t = jnp.arange(n_tokens)
    row = (t // GRID_W).astype(F32)
    col = (t % GRID_W).astype(F32)
    inv = ROPE_THETA ** (-jnp.arange(0, ROPE_AXIS_DIM, 2, dtype=F32) / ROPE_AXIS_DIM)
    ang = jnp.concatenate([row[:, None] * inv, col[:, None] * inv], -1)
    return jnp.cos(ang), jnp.sin(ang)


def _gqa_rope_tables(T, n_ctx):
    cos, sin = _axial_rope(T - n_ctx)
    cos = jnp.concatenate([jnp.ones((n_ctx, ROPE_AXIS_DIM), F32), cos], 0)
    sin = jnp.concatenate([jnp.zeros((n_ctx, ROPE_AXIS_DIM), F32), sin], 0)
    cos = jnp.tile(jnp.repeat(cos, 2, axis=-1), (1, 2))
    sign = jnp.tile(jnp.array([-1.0, 1.0], F32), LANES // 2)
    sin = jnp.tile(jnp.repeat(sin, 2, axis=-1), (1, 2)) * sign
    return cos, sin


def _gqa_attention(pb, qk_gain, n_ctx):
    B, T, _ = pb.shape
    tq = GQA_Q_TILE
    cos, sin = _gqa_rope_tables(T, n_ctx)
    gain = jnp.tile(qk_gain, (1, 2))
    kv_blk = GQA_Q_DIM // GQA_KV_DIM
    return pl.pallas_call(
        functools.partial(_gqa_body, n_ctx=n_ctx),
        out_shape=jax.ShapeDtypeStruct((B, T, GQA_Q_DIM), BF16),
        grid=(B, T // tq),
        in_specs=[pl.BlockSpec((1, tq, GQA_Q_DIM), lambda b, s: (b, s, 0)),
                  pl.BlockSpec((1, T, GQA_KV_DIM), lambda b, s: (b, 0, kv_blk)),
                  pl.BlockSpec((1, T, GQA_KV_DIM), lambda b, s: (b, 0, kv_blk + 1)),
                  pl.BlockSpec((tq, LANES), lambda b, s: (s, 0)),
                  pl.BlockSpec((tq, LANES), lambda b, s: (s, 0)),
                  pl.BlockSpec((T, LANES), lambda b, s: (0, 0)),
                  pl.BlockSpec((T, LANES), lambda b, s: (0, 0)),
                  pl.BlockSpec((2, LANES), lambda b, s: (0, 0))],
        out_specs=pl.BlockSpec((1, tq, GQA_Q_DIM), lambda b, s: (b, s, 0)),
        scratch_shapes=[pltpu.VMEM((T, GQA_KV_DIM), BF16), pltpu.VMEM((T, GQA_KV_DIM), BF16)],
        compiler_params=_params("arbitrary", "arbitrary"),
        name="gqa_attention",
    )(pb, pb, pb, cos, sin, cos, sin, gain)


def _attention_mixers(x, modtab, w_in, rpb, qk_gain, n_ctx):
    order = jnp.array(GQA_HEAD_ORDER)
    qb_cols = 3 * NA_DIM + (order[:, None] * HEAD_DIM + jnp.arange(HEAD_DIM)[None, :]).reshape(-1)
    cols = jnp.concatenate([jnp.arange(3 * NA_DIM), qb_cols, jnp.arange(3 * NA_DIM + GQA_Q_DIM, w_in.shape[1])])
    w = w_in[:, cols].astype(BF16)
    pa, pb = _in_proj(x, modtab, w, ((0, 3 * NA_DIM), (3 * NA_DIM, w.shape[1])), (BF16, F32))
    return jnp.concatenate([_na_attention(pa, rpb, n_ctx), _gqa_attention(pb, qk_gain, n_ctx)], -1)


def _attention_w_out(w_out):
    order = jnp.array(GQA_HEAD_ORDER)
    rows = NA_DIM + (order[:, None] * HEAD_DIM + jnp.arange(HEAD_DIM)[None, :]).reshape(-1)
    return jnp.concatenate([w_out[:NA_DIM], w_out[rows]], 0)


def _log_sigmoid(x):
    return jnp.minimum(x, 0.0) - jnp.log(1.0 + jnp.exp(-jnp.abs(x)))


def _mlstm_body(qf, kf, vf, gcf, grf, qb, kb, vb, gcb, grb, bias_c, bias_r, hf_ref, hb_ref, c_ref, n_ref, m_ref):
    @pl.when(pl.program_id(1) == 0)
    def _():
        c_ref[...] = jnp.zeros_like(c_ref)
        n_ref[...] = jnp.zeros_like(n_ref)
        m_ref[...] = jnp.zeros_like(m_ref)

    L = MLSTM_CHUNK
    row = lax.broadcasted_iota(jnp.int32, (L, L), 0)
    col = lax.broadcasted_iota(jnp.int32, (L, L), 1)
    hi = lax.Precision.HIGHEST
    for d, (q_ref, k_ref, v_ref, gc_ref, gr_ref, h_ref) in enumerate(((qf, kf, vf, gcf, grf, hf_ref),
                                                                     (qb, kb, vb, gcb, grb, hb_ref))):
        seen = (col <= row) if d == 0 else (col >= row)
        seen_f = jnp.where(seen, 1.0, 0.0)
        g_col = gc_ref[0] + bias_c[...]
        g_row = gr_ref[0, 0] + bias_r[...]
        b_col = jnp.dot(seen_f, _log_sigmoid(g_col), preferred_element_type=F32, precision=hi)
        lf_row = _log_sigmoid(g_row)
        b_row = lax.dot_general(lf_row, seen_f, (((1,), (1,)), ((), ())), preferred_element_type=F32, precision=hi)
        b_end = jnp.sum(lf_row, axis=1, keepdims=True)
        for h in range(MLSTM_HEADS):
            gi, gf = d * 2 * MLSTM_HEADS + h, d * 2 * MLSTM_HEADS + MLSTM_HEADS + h
            sl = slice(h * MLSTM_HEAD, (h + 1) * MLSTM_HEAD)
            s_idx = d * MLSTM_HEADS + h
            c_state, n_state, m_state = c_ref[s_idx], n_ref[s_idx:s_idx + 1, :], m_ref[s_idx:s_idx + 1, 0:1]
            bc, br, ig_c, ig_r = b_col[:, gf:gf + 1], b_row[gf:gf + 1, :], g_col[:, gi:gi + 1], g_row[gi:gi + 1, :]
            be = b_end[gf:gf + 1, :]
            d_intra = jnp.where(seen, bc - br + ig_r, -jnp.inf)
            d_inter = bc + m_state
            m_t = jnp.maximum(d_inter, jnp.max(d_intra, axis=1, keepdims=True))
            q = (q_ref[0, :, sl] * MLSTM_HEAD ** -0.5).astype(BF16)
            k, v = k_ref[0, :, sl], v_ref[0, :, sl]
            kb16 = k.astype(BF16)
            s = lax.dot_general(q, kb16, (((1,), (1,)), ((), ())), preferred_element_type=F32) * jnp.exp(d_intra - m_t)
            w_inter = jnp.exp(d_inter - m_t)
            qc = lax.dot_general(q, c_state.astype(BF16), (((1,), (1,)), ((), ())), preferred_element_type=F32)
            num = jnp.dot(s.astype(BF16), v.astype(BF16), preferred_element_type=F32) + w_inter * qc
            qn = jnp.sum(q.astype(F32) * n_state, axis=1, keepdims=True)
            den = jnp.sum(s, axis=1, keepdims=True) + w_inter * qn
            h_ref[0, :, sl] = num / jnp.maximum(jnp.abs(den), jnp.exp(-m_t))
            d_state = be - bc + ig_c
            m_new = jnp.maximum(be + m_state, jnp.max(d_state, axis=0, keepdims=True))
            w_s = jnp.exp(d_state - m_new)
            w_c = jnp.exp(be + m_state - m_new)
            vk = lax.dot_general((v * w_s).astype(BF16), kb16, (((0,), (0,)), ((), ())), preferred_element_type=F32)
            c_ref[s_idx] = w_c * c_state + vk
            n_ref[s_idx:s_idx + 1, :] = w_c * n_state + jnp.sum(w_s * k, axis=0, keepdims=True)
            m_ref[s_idx:s_idx + 1, :] = jnp.broadcast_to(m_new, (1, LANES))


def _mlstm(pm, gates, gate_b, n_ctx):
    B, T, _ = pm.shape
    L = MLSTM_CHUNK
    nc, ncc = T // L, n_ctx // L
    ng = 4 * MLSTM_HEADS
    g_cols = jnp.pad(gates, ((0, 0), (0, 0), (0, LANES - ng)))
    g_rows = jnp.swapaxes(gates.reshape(B, nc, L, ng), 2, 3)
    bias = gate_b.reshape(ng)
    bias_c = jnp.pad(bias, (0, LANES - ng)).reshape(1, LANES)
    bias_r = bias.reshape(ng, 1)
    fwd = lambda c: c
    bwd = lambda c: jnp.where(c < ncc, ncc - 1 - c, nc - 1 - (c - ncc))
    blk = (1, L, MLSTM_DIM)
    seq = lambda order, j: pl.BlockSpec(blk, lambda b, c: (b, order(c), j))
    gcol = lambda order: pl.BlockSpec((1, L, LANES), lambda b, c: (b, order(c), 0))
    grow = lambda order: pl.BlockSpec((1, 1, ng, L), lambda b, c: (b, order(c), 0, 0))
    n_state = 2 * MLSTM_HEADS
    out_sds = jax.ShapeDtypeStruct((B, T, MLSTM_DIM), F32)
    return pl.pallas_call(
        _mlstm_body,
        out_shape=(out_sds, out_sds),
        grid=(B, nc),
        in_specs=[seq(fwd, 0), seq(fwd, 1), seq(fwd, 2), gcol(fwd), grow(fwd),
                  seq(bwd, 0), seq(bwd, 1), seq(bwd, 2), gcol(bwd), grow(bwd),
                  pl.BlockSpec((1, LANES), lambda b, c: (0, 0)), pl.BlockSpec((ng, 1), lambda b, c: (0, 0))],
        out_specs=(pl.BlockSpec(blk, lambda b, c: (b, fwd(c), 0)), pl.BlockSpec(blk, lambda b, c: (b, bwd(c), 0))),
        scratch_shapes=[pltpu.VMEM((n_state, MLSTM_HEAD, MLSTM_HEAD), F32), pltpu.VMEM((n_state, MLSTM_HEAD), F32),
                        pltpu.VMEM((n_state, LANES), F32)],
        compiler_params=_params("arbitrary", "arbitrary"),
        name="mlstm",
    )(pm, pm, pm, g_cols, g_rows, pm, pm, pm, g_cols, g_rows, bias_c, bias_r)


def _pair_sums(x):
    low = _low_half()
    s_lo = jnp.sum(jnp.where(low, x, 0.0), axis=-1, keepdims=True)
    s_hi = jnp.sum(jnp.where(low, 0.0, x), axis=-1, keepdims=True)
    return jnp.where(low, s_lo, s_hi)


def _rwkv_prep_body(p_ref, prev_ref, next_ref, tab_ref, tab2_ref, w1_ref, w2_ref, a1_ref, a2_ref, g1_ref, g2_ref,
                    r_ref, v_ref, n_ref, w_ref, k_ref, b_ref, gate_ref, *, n_ctx):
    s = pl.program_id(1)
    tm = p_ref.shape[1]
    C = RWKV_DIM
    ctx_tiles = n_ctx // tm
    x = p_ref[0]
    has_prev = jnp.logical_and(s != 0, s != ctx_tiles)
    has_next = jnp.logical_and(s != ctx_tiles - 1, s != pl.num_programs(1) - 1)
    prev_row = jnp.where(has_prev, prev_ref[0, SUBLANES - 1:SUBLANES, :], 0.0)
    next_row = jnp.where(has_next, next_ref[0, 0:1, :], 0.0)
    rowid = lax.broadcasted_iota(jnp.int32, (tm, 1), 0)
    up = jnp.where(rowid == 0, prev_row, pltpu.roll(x, 1, axis=0))
    dn = jnp.where(rowid == tm - 1, next_row, pltpu.roll(x, tm - 1, axis=0))
    d = 0.5 * (up + dn) - x
    part = lambda a, i: a[:, i * C:(i + 1) * C]
    mu = lambda i: tab_ref[i:i + 1, :]
    r = part(x, 0) + part(d, 0) * mu(0)
    k = part(x, 1) + part(d, 1) * mu(1)
    v = part(x, 2) + part(d, 2) * mu(2)
    z, dz = part(x, 3), part(d, 3)
    z_w, z_a, z_g = (z + dz * mu(3)).astype(BF16), (z + dz * mu(4)).astype(BF16), (z + dz * mu(5)).astype(BF16)
    lora = lambda t, w: jnp.dot(t.astype(BF16), w[...], preferred_element_type=F32)
    w_pre = tab2_ref[0:1, :] + lora(jnp.tanh(lora(z_w, w1_ref)), w2_ref)
    neg = -w_pre
    softplus = jnp.maximum(neg, 0.0) + jnp.log(1.0 + jnp.exp(-jnp.abs(neg)))
    decay = jnp.exp(-jnp.exp(-softplus - 0.5))
    iclr = jax.nn.sigmoid(tab2_ref[1:2, :] + lora(lora(z_a, a1_ref), a2_ref))
    gate_ref[0] = lora(jax.nn.sigmoid(lora(z_g, g1_ref)), g2_ref)
    kk = k * tab_ref[6:7, :]
    kk = jnp.concatenate(
        [kk[:, t * LANES:(t + 1) * LANES]
         * lax.rsqrt(jnp.maximum(_pair_sums(jnp.square(kk[:, t * LANES:(t + 1) * LANES])), 1e-24))
         for t in range(C // LANES)], axis=1)
    k2, kk2 = jnp.concatenate([k, k], axis=1), jnp.concatenate([kk, kk], axis=1)
    r_ref[0] = r
    v_ref[0] = v
    n_ref[0] = -kk
    w_ref[0] = decay
    k_ref[0] = k2 * (1.0 + (iclr - 1.0) * tab2_ref[2:3, :])
    b_ref[0] = kk2 * iclr


def _rwkv_prep(pr, n_ctx, mu, w0, w1, w2, a0, a1, a2, g1, g2, kvec):
    B, T, _ = pr.shape
    tm, C = TOKEN_TILE, RWKV_DIM
    per_tile = tm // SUBLANES
    tab = jnp.concatenate([mu, kvec[0:1], jnp.zeros((1, C), F32)], 0)
    cat = lambda a: jnp.concatenate([a[0], a[1]], -1)
    tab2 = jnp.concatenate([cat(w0)[None], cat(a0)[None], jnp.tile(kvec[1], 2)[None], jnp.zeros((5, 2 * C), F32)], 0)
    blockdiag = lambda a: jnp.concatenate([jnp.pad(a[0], ((0, 0), (0, C))), jnp.pad(a[1], ((0, 0), (C, 0)))], 0)
    gl = g1.shape[1]
    consts = (tab, tab2, cat(w1).astype(BF16), blockdiag(w2).astype(BF16), cat(a1).astype(BF16),
              blockdiag(a2).astype(BF16), jnp.pad(g1, ((0, 0), (0, LANES - gl))).astype(BF16),
              jnp.pad(g2, ((0, LANES - gl), (0, 0))).astype(BF16))
    const = lambda a: pl.BlockSpec(a.shape, lambda b, s: (0, 0))
    one = jax.ShapeDtypeStruct((B, T, C), F32)
    two = jax.ShapeDtypeStruct((B, T, 2 * C), F32)
    spec1 = pl.BlockSpec((1, tm, C), lambda b, s: (b, s, 0))
    spec2 = pl.BlockSpec((1, tm, 2 * C), lambda b, s: (b, s, 0))
    return pl.pallas_call(
        functools.partial(_rwkv_prep_body, n_ctx=n_ctx),
        out_shape=(one, one, one, two, two, two, one),
        grid=(B, T // tm),
        in_specs=[pl.BlockSpec((1, tm, 4 * C), lambda b, s: (b, s, 0)),
                  pl.BlockSpec((1, SUBLANES, 4 * C), lambda b, s: (b, jnp.maximum(s * per_tile - 1, 0), 0)),
                  pl.BlockSpec((1, SUBLANES, 4 * C), lambda b, s: (b, jnp.minimum((s + 1) * per_tile, T // SUBLANES - 1), 0))]
        + [const(a) for a in consts],
        out_specs=(spec1, spec1, spec1, spec2, spec2, spec2, spec1),
        compiler_params=_params("arbitrary", "arbitrary"),
        name="rwkv_prep",
    )(pr, pr, pr, *consts)


def _rec_post_body(y_ref, r_ref, v_ref, k_ref, gate_ref, hf_ref, hb_ref, o_ref, tab_ref, out_ref):
    C = RWKV_DIM
    for t in range(C // LANES):
        sl = slice(t * LANES, (t + 1) * LANES)
        y = y_ref[0, :, sl]
        yc = y - _pair_sums(y) * (1.0 / RWKV_HEAD)
        var = _pair_sums(yc * yc) * (1.0 / RWKV_HEAD)
        yn = yc * lax.rsqrt(var + RWKV_GN_EPS) * tab_ref[0:1, sl] + tab_ref[1:2, sl]
        k_sum = k_ref[0, :, sl] + k_ref[0, :, C + t * LANES:C + (t + 1) * LANES]
        bonus = _pair_sums(r_ref[0, :, sl] * k_sum * tab_ref[2:3, sl]) * v_ref[0, :, sl]
        out_ref[0, :, sl] = ((yn + bonus) * gate_ref[0, :, sl]).astype(out_ref.dtype)
    for t in range(MLSTM_HEADS):
        sl = slice(t * MLSTM_HEAD, (t + 1) * MLSTM_HEAD)
        h = hf_ref[0, :, sl] + hb_ref[0, :, sl]
        hn = h * lax.rsqrt(jnp.mean(h * h, axis=-1, keepdims=True) + NORM_EPS) * tab_ref[3:4, sl]
        out_ref[0, :, C + t * MLSTM_HEAD:C + (t + 1) * MLSTM_HEAD] = (hn * jax.nn.sigmoid(o_ref[0, :, sl])).astype(out_ref.dtype)


def _rec_post(y, r, v, k_eff, gate, h_f, h_b, pm, gn, r_k, norm_g):
    B, T, C = y.shape
    tm = TOKEN_TILE
    tab = jnp.concatenate([gn, r_k.reshape(1, C), norm_g.reshape(1, C), jnp.zeros((4, C), F32)], 0)
    spec1 = pl.BlockSpec((1, tm, C), lambda b, s: (b, s, 0))
    return pl.pallas_call(
        _rec_post_body,
        out_shape=jax.ShapeDtypeStruct((B, T, C + MLSTM_DIM), BF16),
        grid=(B, T // tm),
        in_specs=[spec1, spec1, spec1, pl.BlockSpec((1, tm, 2 * C), lambda b, s: (b, s, 0)), spec1, spec1, spec1,
                  pl.BlockSpec((1, tm, MLSTM_DIM), lambda b, s: (b, s, 3)), pl.BlockSpec(tab.shape, lambda b, s: (0, 0))],
        out_specs=pl.BlockSpec((1, tm, C + MLSTM_DIM), lambda b, s: (b, s, 0)),
        compiler_params=_params("arbitrary", "arbitrary"),
        name="rec_post",
    )(y, r, v, k_eff, gate, h_f, h_b, pm, tab)


def _to_state_lanes(x, nd):
    B, T, _ = x.shape
    y = jnp.transpose(x.reshape(B, T, nd, RWKV_HEADS, RWKV_HEAD), (1, 4, 2, 0, 3)).reshape(T, RWKV_HEAD, nd * B * RWKV_HEADS)
    return jnp.concatenate([y, y], -1) if nd == 1 else y


def _recurrent_mixers(x, modtab, w_in, n_ctx, mu, w0, w1, w2, a0, a1, a2, g1, g2, kvec, r_k, gn, gate_b, norm_g):
    B, T, _ = x.shape
    assert 2 * B * RWKV_HEADS == LANES
    n_main = RWKV_IN + 4 * MLSTM_DIM
    w = jnp.pad(w_in, ((0, 0), (0, n_main + LANES - w_in.shape[1]))).astype(BF16)
    pr, pm, pg = _in_proj(x, modtab, w, ((0, RWKV_IN), (RWKV_IN, n_main), (n_main, n_main + LANES)), (F32, F32, F32))
    r, v, kkn, decay, k_eff, kka, gate = _rwkv_prep(pr, n_ctx, mu, w0, w1, w2, a0, a1, a2, g1, g2, kvec)
    scan_in = (_to_state_lanes(r, 1), _to_state_lanes(decay, 2), _to_state_lanes(k_eff, 2), _to_state_lanes(v, 1),
               _to_state_lanes(kkn, 1), _to_state_lanes(kka, 2))
    yf, yb = _rwkv_scan(scan_in, n_ctx)
    half = LANES // 2
    y = yf[:, :, :half] + yb[:, :, half:]
    y = jnp.transpose(y.reshape(T, RWKV_HEAD, B, RWKV_HEADS), (2, 0, 3, 1)).reshape(B, T, RWKV_DIM)
    h_f, h_b = _mlstm(pm, pg[..., :4 * MLSTM_HEADS], gate_b, n_ctx)
    return _rec_post(y, r, v, k_eff, gate, h_f, h_b, pm, gn, r_k, norm_g)


def kernel(x, c, ctx, c_ctx, ada_w, ada_b, ln_g, ln_b, mix_w_out, att_w_in, na_rpb, qk_gain, rec_w_in, rwkv_mu, rwkv_w0, rwkv_w1, rwkv_w2, rwkv_a0, rwkv_a1, rwkv_a2, rwkv_g1, rwkv_g2, rwkv_kvec, rwkv_rk, rwkv_gn, mlstm_gate_b, mlstm_norm, moe_router, moe_bias, moe_w1, moe_w3, moe_w2, shared_w1, shared_w3, shared_w2):
    B, S, D = x.shape
    n_ctx = ctx.shape[1]
    assert D == D_MODEL and n_ctx % TOKEN_TILE == 0 and S % TOKEN_TILE == 0
    xs = jnp.concatenate([ctx, x], axis=1)
    for i in range(DEPTH):
        last = i == DEPTH - 1
        j = i // 2
        mod = (jax.nn.silu(c) @ ada_w[i] + ada_b[i]).reshape(B, 6, D)
        mod_c = jnp.broadcast_to((jax.nn.silu(c_ctx) @ ada_w[i] + ada_b[i]).reshape(1, 6, D), (B, 6, D))
        modtab = jnp.pad(jnp.stack([mod_c, mod], axis=1), ((0, 0), (0, 0), (0, MOD_ROWS - 6), (0, 0)))
        if i % 2 == 0:
            m = _attention_mixers(xs, modtab, att_w_in[j], na_rpb[j], qk_gain[j], n_ctx)
            w_out = _attention_w_out(mix_w_out[i])
        else:
            m = _recurrent_mixers(xs, modtab, rec_w_in[j], n_ctx, rwkv_mu[j], rwkv_w0[j], rwkv_w1[j], rwkv_w2[j],
                                  rwkv_a0[j], rwkv_a1[j], rwkv_a2[j], rwkv_g1[j], rwkv_g2[j], rwkv_kvec[j],
                                  rwkv_rk[j], rwkv_gn[j], mlstm_gate_b[j], mlstm_norm[j])
            w_out = mix_w_out[i]
        first_tile = n_ctx // TOKEN_TILE if last else 0
        x1, h2 = _out_proj(m, xs, modtab, w_out.astype(BF16), jnp.stack([ln_g[i, 0], ln_b[i, 0]]), first_tile)
        xs = _moe_block(h2, x1, modtab, jnp.stack([ln_g[i, 1], ln_b[i, 1]]), first_tile, moe_router[i], moe_bias[i],
                        moe_w1[i], moe_w3[i], moe_w2[i], shared_w1[i], shared_w3[i], shared_w2[i])
    return xs
```

```python
import functools

import jax
import jax.numpy as jnp
from jax import lax
from jax.experimental import pallas as pl
from jax.experimental.pallas import tpu as pltpu

D_MODEL = 1024
DEPTH = 4
GRID_W = 64
HEAD_DIM = 64
NA_HEADS = 8
NA_WIN_ROWS = 8
NA_WIN_COLS = 16
GQA_Q_HEADS = 8
GQA_KV_HEADS = 2
ROPE_THETA = 10000.0
ROPE_AXIS_DIM = HEAD_DIM // 2
NA_DIM = NA_HEADS * HEAD_DIM
GQA_Q_DIM = GQA_Q_HEADS * HEAD_DIM
GQA_KV_DIM = GQA_KV_HEADS * HEAD_DIM
RWKV_HEADS = 8
RWKV_HEAD = 64
RWKV_DIM = RWKV_HEADS * RWKV_HEAD
RWKV_GN_EPS = 64e-5
RWKV_IN = 4 * RWKV_DIM
MLSTM_HEADS = 4
MLSTM_HEAD = 128
MLSTM_DIM = MLSTM_HEADS * MLSTM_HEAD
MLSTM_CHUNK = 64
N_EXPERTS = 64
TOP_K = 6
ROUTED_SCALE = 2.5
DN_ALPHA = (2 * DEPTH) ** 0.25
LN_EPS = 1e-5
NORM_EPS = 1e-6
F32 = jnp.float32
BF16 = jnp.bfloat16

LANES = 128
SUBLANES = 8
VMEM_LIMIT = 48 * 1024 * 1024
TOKEN_TILE = 256
RWKV_TIME_BLOCK = 16
GQA_Q_TILE = 256
MOE_ROW_BLOCK = 256
MOE_SLOTS = 8
MOE_CHUNK = SUBLANES
MOE_STAGE_ROWS = TOKEN_TILE * TOP_K + N_EXPERTS * MOE_CHUNK
MOD_ROWS = 8


def _params(*semantics):
    return pltpu.CompilerParams(dimension_semantics=semantics, vmem_limit_bytes=VMEM_LIMIT)


def _mod_spec(tiles_per_batch, first_tile, flat):
    if flat:
        idx = lambda i, *_: (i // tiles_per_batch, jnp.minimum(i % tiles_per_batch + first_tile, 1), 0, 0)
    else:
        idx = lambda b, s, *_: (b, jnp.minimum(s + first_tile, 1), 0, 0)
    return pl.BlockSpec((1, 1, MOD_ROWS, D_MODEL), idx)


def _layer_norm_rows(z, g, b):
    mu = jnp.mean(z, axis=-1, keepdims=True)
    zc = z - mu
    var = jnp.mean(zc * zc, axis=-1, keepdims=True)
    return zc * lax.rsqrt(var + LN_EPS) * g + b


ADA_ROWS = 16


def _ada_body(c_ref, w_ref, b_ref, o_ref):
    x = c_ref[...]
    a = (x * jax.nn.sigmoid(x)).astype(BF16)
    o_ref[0] = jnp.dot(a, w_ref[0].astype(BF16), preferred_element_type=F32) + b_ref[0]


def _ada_modulation(c, c_ctx, ada_w, ada_b):
    B, D = c.shape
    depth, _, n = ada_w.shape
    cond = jnp.concatenate([c, c_ctx[None], jnp.zeros((ADA_ROWS - B - 1, D), F32)], 0)
    return pl.pallas_call(
        _ada_body,
        out_shape=jax.ShapeDtypeStruct((depth, ADA_ROWS, n), F32),
        grid=(depth, n // D),
        in_specs=[pl.BlockSpec((ADA_ROWS, D), lambda l, j: (0, 0)),
                  pl.BlockSpec((1, D, D), lambda l, j: (l, 0, j)),
                  pl.BlockSpec((1, 1, D), lambda l, j: (l, 0, j))],
        out_specs=pl.BlockSpec((1, ADA_ROWS, D), lambda l, j: (l, 0, j)),
        compiler_params=_params("arbitrary", "arbitrary"),
        name="ada_modulation",
    )(cond, ada_w, ada_b.reshape(depth, 1, n))
def _in_proj_body(x_ref, mod_ref, w_ref, *out_refs, splits):
    h = (x_ref[0] * (1.0 + mod_ref[0, 0, 1:2, :]) + mod_ref[0, 0, 0:1, :]).astype(BF16)
    for o_ref, (c0, c1) in zip(out_refs, splits):
        o_ref[0] = jnp.dot(h, w_ref[:, c0:c1], preferred_element_type=F32).astype(o_ref.dtype)


def _in_proj(x, modtab, w, splits, dtypes):
    B, T, D = x.shape
    tm = TOKEN_TILE
    outs = tuple(jax.ShapeDtypeStruct((B, T, c1 - c0), dt) for (c0, c1), dt in zip(splits, dtypes))
    return pl.pallas_call(
        functools.partial(_in_proj_body, splits=splits),
        out_shape=outs,
        grid=(B, T // tm),
        in_specs=[pl.BlockSpec((1, tm, D), lambda b, s: (b, s, 0)), _mod_spec(T // tm, 0, False),
                  pl.BlockSpec(w.shape, lambda b, s: (0, 0))],
        out_specs=tuple(pl.BlockSpec((1, tm, c1 - c0), lambda b, s: (b, s, 0)) for c0, c1 in splits),
        compiler_params=_params("arbitrary", "arbitrary"),
        name="in_proj",
    )(x, modtab, w)


def _out_proj_body(m_ref, x_ref, mod_ref, w_ref, ln_ref, x1_ref, h2_ref):
    y = jnp.dot(m_ref[0], w_ref[...], preferred_element_type=F32)
    mod = mod_ref[0, 0]
    x1 = _layer_norm_rows(DN_ALPHA * x_ref[0] + mod[2:3, :] * y, ln_ref[0:1, :], ln_ref[1:2, :])
    x1_ref[0] = x1
    h2_ref[0] = x1 * (1.0 + mod[4:5, :]) + mod[3:4, :]


def _out_proj(m, x, modtab, w, ln, first_tile):
    B, T, D = x.shape
    tm = TOKEN_TILE
    n_tiles = T // tm - first_tile
    rows = lambda b, s: (b, s + first_tile, 0)
    out_sds = jax.ShapeDtypeStruct((B, n_tiles * tm, D), F32)
    return pl.pallas_call(
        _out_proj_body,
        out_shape=(out_sds, out_sds),
        grid=(B, n_tiles),
        in_specs=[pl.BlockSpec((1, tm, m.shape[-1]), rows), pl.BlockSpec((1, tm, D), rows),
                  _mod_spec(T // tm, first_tile, False),
                  pl.BlockSpec(w.shape, lambda b, s: (0, 0)), pl.BlockSpec(ln.shape, lambda b, s: (0, 0))],
        out_specs=(pl.BlockSpec((1, tm, D), lambda b, s: (b, s, 0)),) * 2,
        compiler_params=_params("arbitrary", "arbitrary"),
        name="out_proj",
    )(m, x, modtab, w, ln)


def _rwkv_scan_body(rf, wf, kf, vf, nf, bf, rb, wb, kb, vb, nb, bb, yf_ref, yb_ref, state_ref, *, tc):
    @pl.when(pl.program_id(0) == 0)
    def _():
        state_ref[...] = jnp.zeros_like(state_ref)

    fwd_lane = lax.broadcasted_iota(jnp.int32, (RWKV_HEAD, LANES), 1) < LANES // 2

    def step(j, carry):
        jb = tc - 1 - j

        def sel(f, b):
            return jnp.where(fwd_lane, f[j], b[jb])

        r, w, k, v, kkn, bv = sel(rf, rb), sel(wf, wb), sel(kf, kb), sel(vf, vb), sel(nf, nb), sel(bf, bb)
        for vi in range(RWKV_HEAD):
            s = state_ref[vi]
            sa = jnp.sum(s * kkn, axis=0, keepdims=True)
            s2 = s * w + sa * bv + v[vi:vi + 1, :] * k
            state_ref[vi] = s2
            yrow = jnp.sum(s2 * r, axis=0, keepdims=True)
            yf_ref[j, pl.ds(vi, 1), :] = yrow
            yb_ref[jb, pl.ds(vi, 1), :] = yrow
        return carry

    lax.fori_loop(0, tc, step, 0)


def _rwkv_scan(xs, n_ctx):
    T = xs[0].shape[0]
    tc = RWKV_TIME_BLOCK
    assert T % tc == 0 and n_ctx % tc == 0
    nc, ncc = T // tc, n_ctx // tc
    blk = (tc, RWKV_HEAD, LANES)
    fwd = lambda c: (c, 0, 0)
    bwd = lambda c: (jnp.where(c < ncc, ncc - 1 - c, nc - 1 - (c - ncc)), 0, 0)
    out_sds = jax.ShapeDtypeStruct((T, RWKV_HEAD, LANES), F32)
    return pl.pallas_call(
        functools.partial(_rwkv_scan_body, tc=tc),
        out_shape=(out_sds, out_sds),
        grid=(nc,),
        in_specs=[pl.BlockSpec(blk, fwd)] * 6 + [pl.BlockSpec(blk, bwd)] * 6,
        out_specs=(pl.BlockSpec(blk, fwd), pl.BlockSpec(blk, bwd)),
        scratch_shapes=[pltpu.VMEM((RWKV_HEAD, RWKV_HEAD, LANES), F32)],
        compiler_params=_params("arbitrary"),
        name="rwkv_scan",
    )(*xs, *xs)


def _route_body(x_ref, rwt_ref, rb_ref, upper_ref, lower_ref, q_ref, g_ref, cnt_ref):
    tm = x_ref.shape[0]
    logits = lax.dot_general(rwt_ref[...], x_ref[...], (((1,), (1,)), ((), ())), preferred_element_type=F32,
                             precision=lax.Precision.HIGHEST)
    scores = jax.nn.sigmoid(logits)
    sel = scores + rb_ref[...]
    eidx = lax.broadcasted_iota(jnp.int32, (N_EXPERTS, tm), 0)
    slot = lax.broadcasted_iota(jnp.int32, (MOE_SLOTS, tm), 0)
    onehots = []
    for _ in range(TOP_K):
        m = jnp.max(sel, axis=0, keepdims=True)
        ij = jnp.min(jnp.where(sel == m, eidx, N_EXPERTS), axis=0, keepdims=True)
        oh = eidx == ij
        onehots.append(oh)
        sel = jnp.where(oh, -jnp.inf, sel)
    mask = functools.reduce(jnp.logical_or, onehots)
    maskf = jnp.where(mask, 1.0, 0.0)
    gsum = jnp.sum(jnp.where(mask, scores, 0.0), axis=0, keepdims=True)
    gates = scores / gsum * ROUTED_SCALE
    cnt = jnp.sum(maskf, axis=1, keepdims=True)
    cnt_pad = jnp.ceil(cnt * (1.0 / MOE_CHUNK)) * MOE_CHUNK
    lrank = jnp.dot(maskf.astype(BF16), upper_ref[...], preferred_element_type=F32)
    loff = jnp.dot(lower_ref[...], jnp.broadcast_to(cnt_pad, (N_EXPERTS, LANES)).astype(BF16),
                   preferred_element_type=F32)[:, 0:1]
    q = loff + lrank
    q8 = jnp.full((MOE_SLOTS, tm), -1.0, F32)
    g8 = jnp.zeros((MOE_SLOTS, tm), F32)
    for j, oh in enumerate(onehots):
        q8 = jnp.where(slot == j, jnp.sum(jnp.where(oh, q, 0.0), axis=0, keepdims=True), q8)
        g8 = jnp.where(slot == j, jnp.sum(jnp.where(oh, gates, 0.0), axis=0, keepdims=True), g8)
    q_ref[0] = q8.astype(jnp.int32)
    g_ref[0] = g8
    cnt_ref[0] = jnp.broadcast_to(cnt, (N_EXPERTS, LANES))


def _moe_route(tokens, router_w, router_b):
    n, d = tokens.shape
    tm = TOKEN_TILE
    nt = n // tm
    ar = jnp.arange(tm)
    upper = (ar[:, None] < ar[None, :]).astype(BF16)
    ae = jnp.arange(N_EXPERTS)
    lower = (ae[:, None] > ae[None, :]).astype(BF16)
    const = lambda shape: pl.BlockSpec(shape, lambda i: (0,) * len(shape))
    return pl.pallas_call(
        _route_body,
        out_shape=(jax.ShapeDtypeStruct((nt, MOE_SLOTS, tm), jnp.int32), jax.ShapeDtypeStruct((nt, MOE_SLOTS, tm), F32),
                   jax.ShapeDtypeStruct((nt, N_EXPERTS, LANES), F32)),
        grid=(nt,),
        in_specs=[pl.BlockSpec((tm, d), lambda i: (i, 0)), const((N_EXPERTS, d)), const((N_EXPERTS, 1)),
                  const((tm, tm)), const((N_EXPERTS, N_EXPERTS))],
        out_specs=(pl.BlockSpec((1, MOE_SLOTS, tm), lambda i: (i, 0, 0)),
                   pl.BlockSpec((1, MOE_SLOTS, tm), lambda i: (i, 0, 0)),
                   pl.BlockSpec((1, N_EXPERTS, LANES), lambda i: (i, 0, 0))),
        compiler_params=_params("arbitrary"),
        name="moe_router",
    )(tokens, router_w.T, router_b.reshape(N_EXPERTS, 1), upper, lower)


def _pack_bf16_pairs(x):
    m = x.shape[1] // 2
    bits = lambda a: lax.bitcast_convert_type(a.astype(BF16).astype(F32), jnp.uint32)
    return bits(x[:, :m]) | (bits(x[:, m:]) >> 16)


def _unpack_bf16_pairs(u):
    hi = lax.bitcast_convert_type(u & jnp.uint32(0xFFFF0000), F32)
    lo = lax.bitcast_convert_type(u << 16, F32)
    return jnp.concatenate([hi.astype(BF16), lo.astype(BF16)], axis=1)


def _chunk_loops(i, base_ref, nchunk_ref, loff_ref, copy):
    def per_expert(e, total):
        k = i * N_EXPERTS + e
        n, base, lo = nchunk_ref[k], base_ref[k], loff_ref[k]

        def piece(c, carry):
            copy(pl.multiple_of(lo + c * MOE_CHUNK, MOE_CHUNK), pl.multiple_of(base + c * MOE_CHUNK, MOE_CHUNK)).start()
            return carry

        lax.fori_loop(0, n, piece, 0)
        return total + n

    return lax.fori_loop(0, N_EXPERTS, per_expert, 0)


def _dispatch_body(base_ref, nchunk_ref, loff_ref, bv_ref, x_ref, q_ref, xs_hbm, stage_ref, zero_ref, sem_z, sem):
    i = pl.program_id(0)
    br = zero_ref.shape[0]
    n_blocks = xs_hbm.shape[0] // br

    @pl.when(i == 0)
    def _():
        zero_ref[...] = jnp.zeros_like(zero_ref)

        def fill_copy(b):
            return pltpu.make_async_copy(zero_ref, xs_hbm.at[pl.ds(b * br, br)], sem_z)

        def fill(b, carry):
            @pl.when(bv_ref[b] < br)
            def _():
                fill_copy(b).start()
            return carry

        def fill_wait(b, carry):
            @pl.when(bv_ref[b] < br)
            def _():
                fill_copy(b).wait()
            return carry

        lax.fori_loop(0, n_blocks, fill, 0)
        lax.fori_loop(0, n_blocks, fill_wait, 0)

    rows, tm = stage_ref.shape[0], x_ref.shape[0]
    q8 = q_ref[0]
    pos = lax.broadcasted_iota(jnp.int32, (rows, tm), 0)
    hit = functools.reduce(jnp.logical_or, [pos == q8[j:j + 1, :] for j in range(TOP_K)])
    perm = jnp.where(hit, 1.0, 0.0).astype(BF16)
    stage_ref[...] = _pack_bf16_pairs(jnp.dot(perm, x_ref[...].astype(BF16), preferred_element_type=F32))

    def copy(local_row, global_row):
        return pltpu.make_async_copy(stage_ref.at[pl.ds(local_row, MOE_CHUNK)], xs_hbm.at[pl.ds(global_row, MOE_CHUNK)], sem)

    total = _chunk_loops(i, base_ref, nchunk_ref, loff_ref, copy)

    def drain(c, carry):
        copy(0, 0).wait()
        return carry

    lax.fori_loop(0, total, drain, 0)


def _moe_dispatch(tokens, q_rows, dest_base, nchunk, loff, block_valid):
    n, d = tokens.shape
    tm, br = TOKEN_TILE, MOE_ROW_BLOCK
    n_rows = block_valid.shape[0] * br
    return pl.pallas_call(
        _dispatch_body,
        out_shape=jax.ShapeDtypeStruct((n_rows, d // 2), jnp.uint32),
        grid_spec=pltpu.PrefetchScalarGridSpec(
            num_scalar_prefetch=4,
            grid=(n // tm,),
            in_specs=[pl.BlockSpec((tm, d), lambda i, *_: (i, 0)),
                      pl.BlockSpec((1, MOE_SLOTS, tm), lambda i, *_: (i, 0, 0))],
            out_specs=pl.BlockSpec(memory_space=pl.ANY),
            scratch_shapes=[pltpu.VMEM((MOE_STAGE_ROWS, d // 2), jnp.uint32), pltpu.VMEM((br, d // 2), jnp.uint32),
                            pltpu.SemaphoreType.DMA, pltpu.SemaphoreType.DMA]),
        compiler_params=_params("arbitrary"),
        name="moe_dispatch",
    )(dest_base, nchunk, loff, block_valid, tokens, q_rows)


def _expert_body(be_ref, bv_ref, x_ref, w1_ref, w3_ref, w2_ref, y_ref, w1b, w3b, w2b):
    i = pl.program_id(0)
    valid = bv_ref[i]

    @pl.when(jnp.logical_or(i == 0, be_ref[i] != be_ref[jnp.maximum(i - 1, 0)]))
    def _():
        w1b[...] = w1_ref[0].astype(BF16)
        w3b[...] = w3_ref[0].astype(BF16)
        w2b[...] = w2_ref[0].astype(BF16)

    @pl.when(valid > 0)
    def _():
        x = _unpack_bf16_pairs(x_ref[...])
        h1 = jnp.dot(x, w1b[...], preferred_element_type=F32)
        h3 = jnp.dot(x, w3b[...], preferred_element_type=F32)
        a = (h1 * jax.nn.sigmoid(h1) * h3).astype(BF16)
        y_ref[...] = _pack_bf16_pairs(jnp.dot(a, w2b[...], preferred_element_type=F32))

    @pl.when(valid <= 0)
    def _():
        y_ref[...] = jnp.zeros_like(y_ref)


def _moe_experts(xs, block_expert, block_valid, w1, w3, w2):
    n_rows, dp = xs.shape
    br = MOE_ROW_BLOCK
    d, ff = w1.shape[-2:]
    return pl.pallas_call(
        _expert_body,
        out_shape=jax.ShapeDtypeStruct((n_rows, dp), jnp.uint32),
        grid_spec=pltpu.PrefetchScalarGridSpec(
            num_scalar_prefetch=2,
            grid=(n_rows // br,),
            in_specs=[pl.BlockSpec((br, dp), lambda i, be, bv: (i, 0)),
                      pl.BlockSpec((1, d, ff), lambda i, be, bv: (be[i], 0, 0)),
                      pl.BlockSpec((1, d, ff), lambda i, be, bv: (be[i], 0, 0)),
                      pl.BlockSpec((1, ff, d), lambda i, be, bv: (be[i], 0, 0))],
            out_specs=pl.BlockSpec((br, dp), lambda i, be, bv: (i, 0)),
            scratch_shapes=[pltpu.VMEM((d, ff), BF16), pltpu.VMEM((d, ff), BF16), pltpu.VMEM((ff, d), BF16)]),
        compiler_params=_params("arbitrary"),
        name="moe_experts",
    )(block_expert, block_valid, xs, w1, w3, w2)


def _combine_body(base_ref, nchunk_ref, loff_ref, h_ref, q_ref, g_ref, ys_hbm, sw1_ref, sw3_ref, sw2_ref,
                  x1_ref, mod_ref, ln_ref, out_ref, stage_ref, sem):
    i = pl.program_id(0)

    @pl.when(i == 0)
    def _():
        stage_ref[...] = jnp.zeros_like(stage_ref)

    def copy(local_row, global_row):
        return pltpu.make_async_copy(ys_hbm.at[pl.ds(global_row, MOE_CHUNK)], stage_ref.at[pl.ds(local_row, MOE_CHUNK)], sem)

    total = _chunk_loops(i, base_ref, nchunk_ref, loff_ref, copy)

    h = h_ref[...].astype(BF16)
    h1 = jnp.dot(h, sw1_ref[...], preferred_element_type=F32)
    h3 = jnp.dot(h, sw3_ref[...], preferred_element_type=F32)
    y = jnp.dot((h1 * jax.nn.sigmoid(h1) * h3).astype(BF16), sw2_ref[...], preferred_element_type=F32)
    tm, rows = h_ref.shape[0], stage_ref.shape[0]
    q8, g8 = q_ref[0], g_ref[0]
    pos = lax.broadcasted_iota(jnp.int32, (tm, rows), 1)
    gate_mat = jnp.zeros((tm, rows), F32)
    for j in range(TOP_K):
        gate_mat = jnp.where(pos == q8[:, j:j + 1], g8[:, j:j + 1], gate_mat)

    def drain(c, carry):
        copy(0, 0).wait()
        return carry

    lax.fori_loop(0, total, drain, 0)
    y = y + jnp.dot(gate_mat.astype(BF16), _unpack_bf16_pairs(stage_ref[...]), preferred_element_type=F32)
    out_ref[...] = _layer_norm_rows(DN_ALPHA * x1_ref[...] + mod_ref[0, 0, 5:6, :] * y, ln_ref[0:1, :], ln_ref[1:2, :])


def _moe_combine(tokens, q_cols, g_cols, dest_base, nchunk, loff, ys, sw1, sw3, sw2, x1, modtab, ln, tiles_per_batch,
                 first_tile):
    n, d = tokens.shape
    tm = TOKEN_TILE
    ff = sw1.shape[-1]
    whole = lambda shape: pl.BlockSpec(shape, lambda i, *_: (0,) * len(shape))
    rows = pl.BlockSpec((tm, d), lambda i, *_: (i, 0))
    slots = pl.BlockSpec((1, tm, MOE_SLOTS), lambda i, *_: (i, 0, 0))
    return pl.pallas_call(
        _combine_body,
        out_shape=jax.ShapeDtypeStruct((n, d), F32),
        grid_spec=pltpu.PrefetchScalarGridSpec(
            num_scalar_prefetch=3,
            grid=(n // tm,),
            in_specs=[rows, slots, slots, pl.BlockSpec(memory_space=pl.ANY),
                      whole((d, ff)), whole((d, ff)), whole((ff, d)),
                      rows, _mod_spec(tiles_per_batch, first_tile, True), whole(ln.shape)],
            out_specs=rows,
            scratch_shapes=[pltpu.VMEM((MOE_STAGE_ROWS, d // 2), jnp.uint32), pltpu.SemaphoreType.DMA]),
        compiler_params=_params("arbitrary"),
        name="moe_combine",
    )(dest_base, nchunk, loff, tokens, q_cols, g_cols, ys, sw1, sw3, sw2, x1, modtab, ln)


def _moe_block(h2, x1, modtab, ln, first_tile, router_w, router_b, w1, w3, w2, sw1, sw3, sw2):
    B, Tp, d = h2.shape
    tokens = h2.reshape(B * Tp, d)
    n = B * Tp
    tm, br = TOKEN_TILE, MOE_ROW_BLOCK
    q_rows, g_rows, cnt = _moe_route(tokens, router_w, router_b)
    cnt = cnt[:, :, 0].astype(jnp.int32)
    nchunk = (cnt + MOE_CHUNK - 1) // MOE_CHUNK
    run = nchunk * MOE_CHUNK
    total = jnp.sum(run, axis=0)
    padded = (total + br - 1) // br * br
    p_end = jnp.cumsum(padded)
    offs = p_end - padded
    dest_base = (offs[None, :] + jnp.cumsum(run, axis=0) - run).reshape(-1).astype(jnp.int32)
    loff = jnp.cumsum(run, axis=1) - run
    nchunk, loff = nchunk.reshape(-1).astype(jnp.int32), loff.reshape(-1).astype(jnp.int32)
    n_blocks = -(-(n * TOP_K + (n // tm) * N_EXPERTS * (MOE_CHUNK - 1) + N_EXPERTS * (br - 1)) // br)
    blk_start = jnp.arange(n_blocks, dtype=jnp.int32) * br
    block_expert = jnp.minimum(jnp.sum(blk_start[:, None] >= p_end[None, :], axis=1), N_EXPERTS - 1).astype(jnp.int32)
    block_valid = jnp.clip(total[block_expert] - (blk_start - offs[block_expert]), 0, br).astype(jnp.int32)
    xs = _moe_dispatch(tokens, q_rows, dest_base, nchunk, loff, block_valid)
    ys = _moe_experts(xs, block_expert, block_valid, w1, w3, w2)
    q_cols, g_cols = jnp.swapaxes(q_rows, 1, 2), jnp.swapaxes(g_rows, 1, 2)
    out = _moe_combine(tokens, q_cols, g_cols, dest_base, nchunk, loff, ys,
                       sw1.astype(BF16), sw3.astype(BF16), sw2.astype(BF16),
                       x1.reshape(n, d), modtab, ln, Tp // tm, first_tile)
    return out.reshape(B, Tp, d)


def _low_half():
    return lax.broadcasted_iota(jnp.int32, (1, LANES), 1) < LANES // 2


def _pair_attention(q, parts):
    low = _low_half()
    outs = []
    for use_low in (True, False):
        qm = jnp.where(low == use_low, q, jnp.zeros_like(q))
        scores = []
        for k, _, b_lo, b_hi in parts:
            s = lax.dot_general(qm, k, (((1,), (1,)), ((), ())), preferred_element_type=F32) * HEAD_DIM ** -0.5
            b = b_lo if use_low else b_hi
            scores.append(s if b is None else s + b)
        m = functools.reduce(jnp.maximum, [jnp.max(s, axis=-1, keepdims=True) for s in scores])
        den = 0.0
        num = 0.0
        for s, (_, v, _, _) in zip(scores, parts):
            p = jnp.exp(s - m)
            den = den + jnp.sum(p, axis=-1, keepdims=True)
            num = num + jnp.dot(p.astype(BF16), v, preferred_element_type=F32)
        outs.append(num / den)
    return jnp.where(low, outs[0], outs[1])


def _na_body(q_ref, k_ref, v_ref, bias_ref, o_ref, *, n_ctx, rows):
    s = pl.program_id(1)
    ctx_blocks = n_ctx // GRID_W
    n_tiles = NA_DIM // LANES
    tile = lambda t: slice(t * LANES, (t + 1) * LANES)

    @pl.when(s < ctx_blocks)
    def _():
        for t in range(n_tiles):
            part = (k_ref[0, 0:n_ctx, tile(t)], v_ref[0, 0:n_ctx, tile(t)], None, None)
            o_ref[0, :, tile(t)] = _pair_attention(q_ref[0, :, tile(t)], [part]).astype(o_ref.dtype)

    @pl.when(s >= ctx_blocks)
    def _():
        r = s - ctx_blocks
        start = jnp.clip(r - NA_WIN_ROWS // 2, 0, rows - NA_WIN_ROWS)
        off = pl.multiple_of(n_ctx + start * GRID_W, GRID_W)
        band = pl.ds(off, NA_WIN_ROWS * GRID_W)
        for t in range(n_tiles):
            parts = [(k_ref[0, band, tile(t)], v_ref[0, band, tile(t)], bias_ref[0, 2 * t], bias_ref[0, 2 * t + 1]),
                     (k_ref[0, 0:n_ctx, tile(t)], v_ref[0, 0:n_ctx, tile(t)], None, None)]
            o_ref[0, :, tile(t)] = _pair_attention(q_ref[0, :, tile(t)], parts).astype(o_ref.dtype)


def _na_bias_table(rpb):
    kc = NA_WIN_COLS
    cidx = jnp.arange(GRID_W)
    col_start = jnp.clip(cidx - kc // 2, 0, GRID_W - kc)
    col_in = (cidx[None, :] >= col_start[:, None]) & (cidx[None, :] < col_start[:, None] + kc)
    d_col = jnp.clip(cidx[None, :] - cidx[:, None], -(kc - 1), kc - 1) + kc - 1
    tab = jnp.where(col_in, rpb[:, :, d_col], -jnp.inf)
    d_rows = jnp.arange(NA_WIN_ROWS)[:, None] + jnp.arange(NA_WIN_ROWS)[None, :]
    tab = tab[:, d_rows]
    return jnp.transpose(tab, (1, 0, 3, 2, 4)).reshape(NA_WIN_ROWS, NA_HEADS, GRID_W, NA_WIN_ROWS * GRID_W)


def _na_attention(pa, rpb, n_ctx):
    B, T, _ = pa.shape
    rows = (T - n_ctx) // GRID_W
    ctx_blocks = n_ctx // GRID_W

    def bias_idx(b, s):
        r = jnp.maximum(s - ctx_blocks, 0)
        return (jnp.clip(r - NA_WIN_ROWS // 2, 0, rows - NA_WIN_ROWS) - r + NA_WIN_ROWS - 1, 0, 0, 0)

    return pl.pallas_call(
        functools.partial(_na_body, n_ctx=n_ctx, rows=rows),
        out_shape=jax.ShapeDtypeStruct((B, T, NA_DIM), BF16),
        grid=(B, T // GRID_W),
        in_specs=[pl.BlockSpec((1, GRID_W, NA_DIM), lambda b, s: (b, s, 0)),
                  pl.BlockSpec((1, T, NA_DIM), lambda b, s: (b, 0, 1)),
                  pl.BlockSpec((1, T, NA_DIM), lambda b, s: (b, 0, 2)),
                  pl.BlockSpec((1, NA_HEADS, GRID_W, NA_WIN_ROWS * GRID_W), bias_idx)],
        out_specs=pl.BlockSpec((1, GRID_W, NA_DIM), lambda b, s: (b, s, 0)),
        compiler_params=_params("arbitrary", "arbitrary"),
        name="na_attention",
    )(pa, pa, pa, _na_bias_table(rpb))


def _rms_pair(x, gain):
    low = _low_half()
    sq = x * x
    s_lo = jnp.sum(jnp.where(low, sq, 0.0), axis=-1, keepdims=True)
    s_hi = jnp.sum(jnp.where(low, 0.0, sq), axis=-1, keepdims=True)
    ms = jnp.where(low, s_lo, s_hi) * (1.0 / HEAD_DIM)
    return x * lax.rsqrt(ms + NORM_EPS) * gain


def _rope_pair(x, cos, sin_signed):
    even = lax.broadcasted_iota(jnp.int32, (1, LANES), 1) % 2 == 0
    partner = jnp.where(even, pltpu.roll(x, LANES - 1, axis=1), pltpu.roll(x, 1, axis=1))
    return x * cos + partner * sin_signed


def _gqa_body(q_ref, k_ref, v_ref, cos_q, sin_q, cos_k, sin_k, gain_ref, o_ref, kn_ref, vn_ref, *, n_ctx):
    s = pl.program_id(1)
    tq = q_ref.shape[1]

    @pl.when(s == 0)
    def _():
        kn_ref[...] = _rope_pair(_rms_pair(k_ref[0], gain_ref[1:2, :]), cos_k[...], sin_k[...]).astype(BF16)
        vn_ref[...] = v_ref[0].astype(BF16)

    def run(n_keys):
        k, v = kn_ref[0:n_keys, :], vn_ref[0:n_keys, :]
        for t in range(GQA_Q_DIM // LANES):
            q = q_ref[0, :, t * LANES:(t + 1) * LANES]
            qn = _rope_pair(_rms_pair(q, gain_ref[0:1, :]), cos_q[...], sin_q[...]).astype(BF16)
            o_ref[0, :, t * LANES:(t + 1) * LANES] = _pair_attention(qn, [(k, v, None, None)]).astype(o_ref.dtype)

    @pl.when(s < n_ctx // tq)
    def _():
        run(n_ctx)

    @pl.when(s >= n_ctx // tq)
    def _():
        run(kn_ref.shape[0])


GQA_HEAD_ORDER = (0, 4, 1, 5, 2, 6, 3, 7)


def _axial_rope(n_tokens):
    t = jnp.arange(n_tokens)
    row = (t // GRID_W).astype(F32)
    col = (t % GRID_W).astype(F32)
    inv = ROPE_THETA ** (-jnp.arange(0, ROPE_AXIS_DIM, 2, dtype=F32) / ROPE_AXIS_DIM)
    ang = jnp.concatenate([row[:, None] * inv, col[:, None] * inv], -1)
    return jnp.cos(ang), jnp.sin(ang)


def _gqa_rope_tables(T, n_ctx):
    cos, sin = _axial_rope(T - n_ctx)
    cos = jnp.concatenate([jnp.ones((n_ctx, ROPE_AXIS_DIM), F32), cos], 0)
    sin = jnp.concatenate([jnp.zeros((n_ctx, ROPE_AXIS_DIM), F32), sin], 0)
    cos = jnp.tile(jnp.repeat(cos, 2, axis=-1), (1, 2))
    sign = jnp.tile(jnp.array([-1.0, 1.0], F32), LANES // 2)
    sin = jnp.tile(jnp.repeat(sin, 2, axis=-1), (1, 2)) * sign
    return cos, sin


def _gqa_attention(pb, qk_gain, n_ctx):
    B, T, _ = pb.shape
    tq = GQA_Q_TILE
    cos, sin = _gqa_rope_tables(T, n_ctx)
    gain = jnp.tile(qk_gain, (1, 2))
    kv_blk = GQA_Q_DIM // GQA_KV_DIM
    return pl.pallas_call(
        functools.partial(_gqa_body, n_ctx=n_ctx),
        out_shape=jax.ShapeDtypeStruct((B, T, GQA_Q_DIM), BF16),
        grid=(B, T // tq),
        in_specs=[pl.BlockSpec((1, tq, GQA_Q_DIM), lambda b, s: (b, s, 0)),
                  pl.BlockSpec((1, T, GQA_KV_DIM), lambda b, s: (b, 0, kv_blk)),
                  pl.BlockSpec((1, T, GQA_KV_DIM), lambda b, s: (b, 0, kv_blk + 1)),
                  pl.BlockSpec((tq, LANES), lambda b, s: (s, 0)),
                  pl.BlockSpec((tq, LANES), lambda b, s: (s, 0)),
                  pl.BlockSpec((T, LANES), lambda b, s: (0, 0)),
                  pl.BlockSpec((T, LANES), lambda b, s: (0, 0)),
                  pl.BlockSpec((2, LANES), lambda b, s: (0, 0))],
        out_specs=pl.BlockSpec((1, tq, GQA_Q_DIM), lambda b, s: (b, s, 0)),
        scratch_shapes=[pltpu.VMEM((T, GQA_KV_DIM), BF16), pltpu.VMEM((T, GQA_KV_DIM), BF16)],
        compiler_params=_params("arbitrary", "arbitrary"),
        name="gqa_attention",
    )(pb, pb, pb, cos, sin, cos, sin, gain)


def _attention_mixers(x, modtab, w_in, rpb, qk_gain, n_ctx):
    order = jnp.array(GQA_HEAD_ORDER)
    qb_cols = 3 * NA_DIM + (order[:, None] * HEAD_DIM + jnp.arange(HEAD_DIM)[None, :]).reshape(-1)
    cols = jnp.concatenate([jnp.arange(3 * NA_DIM), qb_cols, jnp.arange(3 * NA_DIM + GQA_Q_DIM, w_in.shape[1])])
    w = w_in[:, cols].astype(BF16)
    pa, pb = _in_proj(x, modtab, w, ((0, 3 * NA_DIM), (3 * NA_DIM, w.shape[1])), (BF16, F32))
    return jnp.concatenate([_na_attention(pa, rpb, n_ctx), _gqa_attention(pb, qk_gain, n_ctx)], -1)


def _attention_w_out(w_out):
    order = jnp.array(GQA_HEAD_ORDER)
    rows = NA_DIM + (order[:, None] * HEAD_DIM + jnp.arange(HEAD_DIM)[None, :]).reshape(-1)
    return jnp.concatenate([w_out[:NA_DIM], w_out[rows]], 0)


def _log_sigmoid(x):
    return jnp.minimum(x, 0.0) - jnp.log(1.0 + jnp.exp(-jnp.abs(x)))


def _mlstm_body(qf, kf, vf, gcf, grf, qb, kb, vb, gcb, grb, bias_c, bias_r, hf_ref, hb_ref, c_ref, n_ref, m_ref):
    @pl.when(pl.program_id(1) == 0)
    def _():
        c_ref[...] = jnp.zeros_like(c_ref)
        n_ref[...] = jnp.zeros_like(n_ref)
        m_ref[...] = jnp.zeros_like(m_ref)

    L = MLSTM_CHUNK
    row = lax.broadcasted_iota(jnp.int32, (L, L), 0)
    col = lax.broadcasted_iota(jnp.int32, (L, L), 1)
    hi = lax.Precision.HIGHEST
    writes = []
    for d, (q_ref, k_ref, v_ref, gc_ref, gr_ref, h_ref) in enumerate(((qf, kf, vf, gcf, grf, hf_ref),
                                                                     (qb, kb, vb, gcb, grb, hb_ref))):
        seen = (col <= row) if d == 0 else (col >= row)
        seen_f = jnp.where(seen, 1.0, 0.0)
        g_col = gc_ref[0] + bias_c[...]
        g_row = gr_ref[0, 0] + bias_r[...]
        b_col = jnp.dot(seen_f, _log_sigmoid(g_col), preferred_element_type=F32, precision=hi)
        lf_row = _log_sigmoid(g_row)
        b_row = lax.dot_general(lf_row, seen_f, (((1,), (1,)), ((), ())), preferred_element_type=F32, precision=hi)
        b_end = jnp.sum(lf_row, axis=1, keepdims=True)
        for h in range(MLSTM_HEADS):
            gi, gf = d * 2 * MLSTM_HEADS + h, d * 2 * MLSTM_HEADS + MLSTM_HEADS + h
            sl = slice(h * MLSTM_HEAD, (h + 1) * MLSTM_HEAD)
            s_idx = d * MLSTM_HEADS + h
            c_state, n_state, m_state = c_ref[s_idx], n_ref[s_idx:s_idx + 1, :], m_ref[s_idx:s_idx + 1, 0:1]
            bc, br, ig_c, ig_r = b_col[:, gf:gf + 1], b_row[gf:gf + 1, :], g_col[:, gi:gi + 1], g_row[gi:gi + 1, :]
            be = b_end[gf:gf + 1, :]
            d_intra = jnp.where(seen, bc - br + ig_r, -jnp.inf)
            d_inter = bc + m_state
            m_t = jnp.maximum(d_inter, jnp.max(d_intra, axis=1, keepdims=True))
            q = (q_ref[0, :, sl] * MLSTM_HEAD ** -0.5).astype(BF16)
            k, v = k_ref[0, :, sl], v_ref[0, :, sl]
            kb16 = k.astype(BF16)
            s = lax.dot_general(q, kb16, (((1,), (1,)), ((), ())), preferred_element_type=F32) * jnp.exp(d_intra - m_t)
            w_inter = jnp.exp(d_inter - m_t)
            qc = lax.dot_general(q, c_state.astype(BF16), (((1,), (1,)), ((), ())), preferred_element_type=F32)
            num = jnp.dot(s.astype(BF16), v.astype(BF16), preferred_element_type=F32) + w_inter * qc
            qn = jnp.sum(q.astype(F32) * n_state, axis=1, keepdims=True)
            den = jnp.sum(s, axis=1, keepdims=True) + w_inter * qn
            writes.append((h_ref.at[0, :, sl], num / jnp.maximum(jnp.abs(den), jnp.exp(-m_t))))
            d_state = be - bc + ig_c
            m_new = jnp.maximum(be + m_state, jnp.max(d_state, axis=0, keepdims=True))
            w_s = jnp.exp(d_state - m_new)
            w_c = jnp.exp(be + m_state - m_new)
            vk = lax.dot_general((v * w_s).astype(BF16), kb16, (((0,), (0,)), ((), ())), preferred_element_type=F32)
            writes.append((c_ref.at[s_idx], w_c * c_state + vk))
            writes.append((n_ref.at[s_idx:s_idx + 1, :], w_c * n_state + jnp.sum(w_s * k, axis=0, keepdims=True)))
            writes.append((m_ref.at[s_idx:s_idx + 1, :], jnp.broadcast_to(m_new, (1, LANES))))
    for ref, value in writes:
        ref[...] = value


def _mlstm(pm, gates, gate_b, n_ctx):
    B, T, _ = pm.shape
    L = MLSTM_CHUNK
    nc, ncc = T // L, n_ctx // L
    ng = 4 * MLSTM_HEADS
    g_cols = jnp.pad(gates, ((0, 0), (0, 0), (0, LANES - ng)))
    g_rows = jnp.swapaxes(gates.reshape(B, nc, L, ng), 2, 3)
    bias = gate_b.reshape(ng)
    bias_c = jnp.pad(bias, (0, LANES - ng)).reshape(1, LANES)
    bias_r = bias.reshape(ng, 1)
    fwd = lambda c: c
    bwd = lambda c: jnp.where(c < ncc, ncc - 1 - c, nc - 1 - (c - ncc))
    blk = (1, L, MLSTM_DIM)
    seq = lambda order, j: pl.BlockSpec(blk, lambda b, c: (b, order(c), j))
    gcol = lambda order: pl.BlockSpec((1, L, LANES), lambda b, c: (b, order(c), 0))
    grow = lambda order: pl.BlockSpec((1, 1, ng, L), lambda b, c: (b, order(c), 0, 0))
    n_state = 2 * MLSTM_HEADS
    out_sds = jax.ShapeDtypeStruct((B, T, MLSTM_DIM), F32)
    return pl.pallas_call(
        _mlstm_body,
        out_shape=(out_sds, out_sds),
        grid=(B, nc),
        in_specs=[seq(fwd, 0), seq(fwd, 1), seq(fwd, 2), gcol(fwd), grow(fwd),
                  seq(bwd, 0), seq(bwd, 1), seq(bwd, 2), gcol(bwd), grow(bwd),
                  pl.BlockSpec((1, LANES), lambda b, c: (0, 0)), pl.BlockSpec((ng, 1), lambda b, c: (0, 0))],
        out_specs=(pl.BlockSpec(blk, lambda b, c: (b, fwd(c), 0)), pl.BlockSpec(blk, lambda b, c: (b, bwd(c), 0))),
        scratch_shapes=[pltpu.VMEM((n_state, MLSTM_HEAD, MLSTM_HEAD), F32), pltpu.VMEM((n_state, MLSTM_HEAD), F32),
                        pltpu.VMEM((n_state, LANES), F32)],
        compiler_params=_params("arbitrary", "arbitrary"),
        name="mlstm",
    )(pm, pm, pm, g_cols, g_rows, pm, pm, pm, g_cols, g_rows, bias_c, bias_r)


def _pair_sums(x):
    low = _low_half()
    s_lo = jnp.sum(jnp.where(low, x, 0.0), axis=-1, keepdims=True)
    s_hi = jnp.sum(jnp.where(low, 0.0, x), axis=-1, keepdims=True)
    return jnp.where(low, s_lo, s_hi)


def _rwkv_prep_body(p_ref, prev_ref, next_ref, tab_ref, tab2_ref, w1_ref, w2_ref, a1_ref, a2_ref, g1_ref, g2_ref,
                    r_ref, v_ref, n_ref, w_ref, k_ref, b_ref, gate_ref, *, n_ctx):
    s = pl.program_id(1)
    tm = p_ref.shape[1]
    C = RWKV_DIM
    ctx_tiles = n_ctx // tm
    x = p_ref[0]
    has_prev = jnp.logical_and(s != 0, s != ctx_tiles)
    has_next = jnp.logical_and(s != ctx_tiles - 1, s != pl.num_programs(1) - 1)
    prev_row = jnp.where(has_prev, prev_ref[0, SUBLANES - 1:SUBLANES, :], 0.0)
    next_row = jnp.where(has_next, next_ref[0, 0:1, :], 0.0)
    rowid = lax.broadcasted_iota(jnp.int32, (tm, 1), 0)
    up = jnp.where(rowid == 0, prev_row, pltpu.roll(x, 1, axis=0))
    dn = jnp.where(rowid == tm - 1, next_row, pltpu.roll(x, tm - 1, axis=0))
    d = 0.5 * (up + dn) - x
    part = lambda a, i: a[:, i * C:(i + 1) * C]
    mu = lambda i: tab_ref[i:i + 1, :]
    r = part(x, 0) + part(d, 0) * mu(0)
    k = part(x, 1) + part(d, 1) * mu(1)
    v = part(x, 2) + part(d, 2) * mu(2)
    z, dz = part(x, 3), part(d, 3)
    z_w, z_a, z_g = (z + dz * mu(3)).astype(BF16), (z + dz * mu(4)).astype(BF16), (z + dz * mu(5)).astype(BF16)
    lora = lambda t, w: jnp.dot(t.astype(BF16), w[...], preferred_element_type=F32)
    w_pre = tab2_ref[0:1, :] + lora(jnp.tanh(lora(z_w, w1_ref)), w2_ref)
    neg = -w_pre
    softplus = jnp.maximum(neg, 0.0) + jnp.log(1.0 + jnp.exp(-jnp.abs(neg)))
    decay = jnp.exp(-jnp.exp(-softplus - 0.5))
    iclr = jax.nn.sigmoid(tab2_ref[1:2, :] + lora(lora(z_a, a1_ref), a2_ref))
    gate_ref[0] = lora(jax.nn.sigmoid(lora(z_g, g1_ref)), g2_ref)
    kk = k * tab_ref[6:7, :]
    kk = jnp.concatenate(
        [kk[:, t * LANES:(t + 1) * LANES]
         * lax.rsqrt(jnp.maximum(_pair_sums(jnp.square(kk[:, t * LANES:(t + 1) * LANES])), 1e-24))
         for t in range(C // LANES)], axis=1)
    k2, kk2 = jnp.concatenate([k, k], axis=1), jnp.concatenate([kk, kk], axis=1)
    r_ref[0] = r
    v_ref[0] = v
    n_ref[0] = -kk
    w_ref[0] = decay
    k_ref[0] = k2 * (1.0 + (iclr - 1.0) * tab2_ref[2:3, :])
    b_ref[0] = kk2 * iclr


def _rwkv_prep(pr, n_ctx, mu, w0, w1, w2, a0, a1, a2, g1, g2, kvec):
    B, T, _ = pr.shape
    tm, C = TOKEN_TILE, RWKV_DIM
    per_tile = tm // SUBLANES
    tab = jnp.concatenate([mu, kvec[0:1], jnp.zeros((1, C), F32)], 0)
    cat = lambda a: jnp.concatenate([a[0], a[1]], -1)
    tab2 = jnp.concatenate([cat(w0)[None], cat(a0)[None], jnp.tile(kvec[1], 2)[None], jnp.zeros((5, 2 * C), F32)], 0)
    blockdiag = lambda a: jnp.concatenate([jnp.pad(a[0], ((0, 0), (0, C))), jnp.pad(a[1], ((0, 0), (C, 0)))], 0)
    gl = g1.shape[1]
    consts = (tab, tab2, cat(w1).astype(BF16), blockdiag(w2).astype(BF16), cat(a1).astype(BF16),
              blockdiag(a2).astype(BF16), jnp.pad(g1, ((0, 0), (0, LANES - gl))).astype(BF16),
              jnp.pad(g2, ((0, LANES - gl), (0, 0))).astype(BF16))
    const = lambda a: pl.BlockSpec(a.shape, lambda b, s: (0, 0))
    one = jax.ShapeDtypeStruct((B, T, C), F32)
    two = jax.ShapeDtypeStruct((B, T, 2 * C), F32)
    spec1 = pl.BlockSpec((1, tm, C), lambda b, s: (b, s, 0))
    spec2 = pl.BlockSpec((1, tm, 2 * C), lambda b, s: (b, s, 0))
    return pl.pallas_call(
        functools.partial(_rwkv_prep_body, n_ctx=n_ctx),
        out_shape=(one, one, one, two, two, two, one),
        grid=(B, T // tm),
        in_specs=[pl.BlockSpec((1, tm, 4 * C), lambda b, s: (b, s, 0)),
                  pl.BlockSpec((1, SUBLANES, 4 * C), lambda b, s: (b, jnp.maximum(s * per_tile - 1, 0), 0)),
                  pl.BlockSpec((1, SUBLANES, 4 * C), lambda b, s: (b, jnp.minimum((s + 1) * per_tile, T // SUBLANES - 1), 0))]
        + [const(a) for a in consts],
        out_specs=(spec1, spec1, spec1, spec2, spec2, spec2, spec1),
        compiler_params=_params("arbitrary", "arbitrary"),
        name="rwkv_prep",
    )(pr, pr, pr, *consts)


def _rec_post_body(y_ref, r_ref, v_ref, k_ref, gate_ref, hf_ref, hb_ref, o_ref, tab_ref, out_ref):
    C = RWKV_DIM
    for t in range(C // LANES):
        sl = slice(t * LANES, (t + 1) * LANES)
        y = y_ref[0, :, sl]
        yc = y - _pair_sums(y) * (1.0 / RWKV_HEAD)
        var = _pair_sums(yc * yc) * (1.0 / RWKV_HEAD)
        yn = yc * lax.rsqrt(var + RWKV_GN_EPS) * tab_ref[0:1, sl] + tab_ref[1:2, sl]
        k_sum = k_ref[0, :, sl] + k_ref[0, :, C + t * LANES:C + (t + 1) * LANES]
        bonus = _pair_sums(r_ref[0, :, sl] * k_sum * tab_ref[2:3, sl]) * v_ref[0, :, sl]
        out_ref[0, :, sl] = ((yn + bonus) * gate_ref[0, :, sl]).astype(out_ref.dtype)
    for t in range(MLSTM_HEADS):
        sl = slice(t * MLSTM_HEAD, (t + 1) * MLSTM_HEAD)
        h = hf_ref[0, :, sl] + hb_ref[0, :, sl]
        hn = h * lax.rsqrt(jnp.mean(h * h, axis=-1, keepdims=True) + NORM_EPS) * tab_ref[3:4, sl]
        out_ref[0, :, C + t * MLSTM_HEAD:C + (t + 1) * MLSTM_HEAD] = (hn * jax.nn.sigmoid(o_ref[0, :, sl])).astype(out_ref.dtype)


def _rec_post(y, r, v, k_eff, gate, h_f, h_b, pm, gn, r_k, norm_g):
    B, T, C = y.shape
    tm = TOKEN_TILE
    tab = jnp.concatenate([gn, r_k.reshape(1, C), norm_g.reshape(1, C), jnp.zeros((4, C), F32)], 0)
    spec1 = pl.BlockSpec((1, tm, C), lambda b, s: (b, s, 0))
    return pl.pallas_call(
        _rec_post_body,
        out_shape=jax.ShapeDtypeStruct((B, T, C + MLSTM_DIM), BF16),
        grid=(B, T // tm),
        in_specs=[spec1, spec1, spec1, pl.BlockSpec((1, tm, 2 * C), lambda b, s: (b, s, 0)), spec1, spec1, spec1,
                  pl.BlockSpec((1, tm, MLSTM_DIM), lambda b, s: (b, s, 3)), pl.BlockSpec(tab.shape, lambda b, s: (0, 0))],
        out_specs=pl.BlockSpec((1, tm, C + MLSTM_DIM), lambda b, s: (b, s, 0)),
        compiler_params=_params("arbitrary", "arbitrary"),
        name="rec_post",
    )(y, r, v, k_eff, gate, h_f, h_b, pm, tab)


def _to_state_lanes(x, nd):
    B, T, _ = x.shape
    y = jnp.transpose(x.reshape(B, T, nd, RWKV_HEADS, RWKV_HEAD), (1, 4, 2, 0, 3)).reshape(T, RWKV_HEAD, nd * B * RWKV_HEADS)
    return jnp.concatenate([y, y], -1) if nd == 1 else y


def _recurrent_mixers(x, modtab, w_in, n_ctx, mu, w0, w1, w2, a0, a1, a2, g1, g2, kvec, r_k, gn, gate_b, norm_g):
    B, T, _ = x.shape
    assert 2 * B * RWKV_HEADS == LANES
    n_main = RWKV_IN + 4 * MLSTM_DIM
    w = jnp.pad(w_in, ((0, 0), (0, n_main + LANES - w_in.shape[1]))).astype(BF16)
    pr, pm, pg = _in_proj(x, modtab, w, ((0, RWKV_IN), (RWKV_IN, n_main), (n_main, n_main + LANES)), (F32, F32, F32))
    r, v, kkn, decay, k_eff, kka, gate = _rwkv_prep(pr, n_ctx, mu, w0, w1, w2, a0, a1, a2, g1, g2, kvec)
    scan_in = (_to_state_lanes(r, 1), _to_state_lanes(decay, 2), _to_state_lanes(k_eff, 2), _to_state_lanes(v, 1),
               _to_state_lanes(kkn, 1), _to_state_lanes(kka, 2))
    yf, yb = _rwkv_scan(scan_in, n_ctx)
    half = LANES // 2
    y = yf[:, :, :half] + yb[:, :, half:]
    y = jnp.transpose(y.reshape(T, RWKV_HEAD, B, RWKV_HEADS), (2, 0, 3, 1)).reshape(B, T, RWKV_DIM)
    h_f, h_b = _mlstm(pm, pg[..., :4 * MLSTM_HEADS], gate_b, n_ctx)
    return _rec_post(y, r, v, k_eff, gate, h_f, h_b, pm, gn, r_k, norm_g)


def kernel(x, c, ctx, c_ctx, ada_w, ada_b, ln_g, ln_b, mix_w_out, att_w_in, na_rpb, qk_gain, rec_w_in, rwkv_mu, rwkv_w0, rwkv_w1, rwkv_w2, rwkv_a0, rwkv_a1, rwkv_a2, rwkv_g1, rwkv_g2, rwkv_kvec, rwkv_rk, rwkv_gn, mlstm_gate_b, mlstm_norm, moe_router, moe_bias, moe_w1, moe_w3, moe_w2, shared_w1, shared_w3, shared_w2):
    B, S, D = x.shape
    n_ctx = ctx.shape[1]
    assert D == D_MODEL and n_ctx % TOKEN_TILE == 0 and S % TOKEN_TILE == 0
    xs = jnp.concatenate([ctx, x], axis=1)
    mods = _ada_modulation(c, c_ctx, ada_w, ada_b)
    for i in range(DEPTH):
        last = i == DEPTH - 1
        j = i // 2
        mod = mods[i, :B].reshape(B, 6, D)
        mod_c = jnp.broadcast_to(mods[i, B].reshape(1, 6, D), (B, 6, D))
        modtab = jnp.pad(jnp.stack([mod_c, mod], axis=1), ((0, 0), (0, 0), (0, MOD_ROWS - 6), (0, 0)))
        if i % 2 == 0:
            m = _attention_mixers(xs, modtab, att_w_in[j], na_rpb[j], qk_gain[j], n_ctx)
            w_out = _attention_w_out(mix_w_out[i])
        else:
            m = _recurrent_mixers(xs, modtab, rec_w_in[j], n_ctx, rwkv_mu[j], rwkv_w0[j], rwkv_w1[j], rwkv_w2[j],
                                  rwkv_a0[j], rwkv_a1[j], rwkv_a2[j], rwkv_g1[j], rwkv_g2[j], rwkv_kvec[j],
                                  rwkv_rk[j], rwkv_gn[j], mlstm_gate_b[j], mlstm_norm[j])
            w_out = mix_w_out[i]
        first_tile = n_ctx // TOKEN_TILE if last else 0
        x1, h2 = _out_proj(m, xs, modtab, w_out.astype(BF16), jnp.stack([ln_g[i, 0], ln_b[i, 0]]), first_tile)
        xs = _moe_block(h2, x1, modtab, jnp.stack([ln_g[i, 1], ln_b[i, 1]]), first_tile, moe_router[i], moe_bias[i],
                        moe_w1[i], moe_w3[i], moe_w2[i], shared_w1[i], shared_w3[i], shared_w2[i])
    return xs
```

```python
import functools

import jax
import jax.numpy as jnp
from jax import lax
from jax.experimental import pallas as pl
from jax.experimental.pallas import tpu as pltpu

D_MODEL = 1024
DEPTH = 4
GRID_W = 64
HEAD_DIM = 64
NA_HEADS = 8
NA_WIN_ROWS = 8
NA_WIN_COLS = 16
GQA_Q_HEADS = 8
GQA_KV_HEADS = 2
ROPE_THETA = 10000.0
ROPE_AXIS_DIM = HEAD_DIM // 2
NA_DIM = NA_HEADS * HEAD_DIM
GQA_Q_DIM = GQA_Q_HEADS * HEAD_DIM
GQA_KV_DIM = GQA_KV_HEADS * HEAD_DIM
RWKV_HEADS = 8
RWKV_HEAD = 64
RWKV_DIM = RWKV_HEADS * RWKV_HEAD
RWKV_GN_EPS = 64e-5
RWKV_IN = 4 * RWKV_DIM
MLSTM_HEADS = 4
MLSTM_HEAD = 128
MLSTM_DIM = MLSTM_HEADS * MLSTM_HEAD
MLSTM_CHUNK = 64
N_EXPERTS = 64
TOP_K = 6
ROUTED_SCALE = 2.5
DN_ALPHA = (2 * DEPTH) ** 0.25
LN_EPS = 1e-5
NORM_EPS = 1e-6
F32 = jnp.float32
BF16 = jnp.bfloat16

LANES = 128
SUBLANES = 8
VMEM_LIMIT = 48 * 1024 * 1024
TOKEN_TILE = 256
RWKV_TIME_BLOCK = 16
GQA_Q_TILE = 256
MOE_ROW_BLOCK = 512
MOE_SLOTS = 8
MOE_CHUNK = SUBLANES
MOE_STAGE_ROWS = TOKEN_TILE * TOP_K + N_EXPERTS * MOE_CHUNK
MOD_ROWS = 8


def _params(*semantics):
    return pltpu.CompilerParams(dimension_semantics=semantics, vmem_limit_bytes=VMEM_LIMIT)


def _mod_spec(tiles_per_batch, first_tile, flat):
    if flat:
        idx = lambda i, *_: (i // tiles_per_batch, jnp.minimum(i % tiles_per_batch + first_tile, 1), 0, 0)
    else:
        idx = lambda b, s, *_: (b, jnp.minimum(s + first_tile, 1), 0, 0)
    return pl.BlockSpec((1, 1, MOD_ROWS, D_MODEL), idx)


def _layer_norm_rows(z, g, b):
    mu = jnp.mean(z, axis=-1, keepdims=True)
    zc = z - mu
    var = jnp.mean(zc * zc, axis=-1, keepdims=True)
    return zc * lax.rsqrt(var + LN_EPS) * g + b


ADA_ROWS = 16


def _ada_body(c_ref, w_ref, b_ref, o_ref):
    x = c_ref[...]
    a = (x * jax.nn.sigmoid(x)).astype(BF16)
    o_ref[0] = jnp.dot(a, w_ref[0].astype(BF16), preferred_element_type=F32) + b_ref[0]


def _ada_modulation(c, c_ctx, ada_w, ada_b):
    B, D = c.shape
    depth, _, n = ada_w.shape
    cond = jnp.concatenate([c, c_ctx[None], jnp.zeros((ADA_ROWS - B - 1, D), F32)], 0)
    return pl.pallas_call(
        _ada_body,
        out_shape=jax.ShapeDtypeStruct((depth, ADA_ROWS, n), F32),
        grid=(depth, n // D),
        in_specs=[pl.BlockSpec((ADA_ROWS, D), lambda l, j: (0, 0)),
                  pl.BlockSpec((1, D, D), lambda l, j: (l, 0, j)),
                  pl.BlockSpec((1, 1, D), lambda l, j: (l, 0, j))],
        out_specs=pl.BlockSpec((1, ADA_ROWS, D), lambda l, j: (l, 0, j)),
        compiler_params=_params("arbitrary", "arbitrary"),
        name="ada_modulation",
    )(cond, ada_w, ada_b.reshape(depth, 1, n))
def _in_proj_body(x_ref, mod_ref, w_ref, *out_refs, splits):
    h = (x_ref[0] * (1.0 + mod_ref[0, 0, 1:2, :]) + mod_ref[0, 0, 0:1, :]).astype(BF16)
    for o_ref, (c0, c1) in zip(out_refs, splits):
        o_ref[0] = jnp.dot(h, w_ref[:, c0:c1], preferred_element_type=F32).astype(o_ref.dtype)


def _in_proj(x, modtab, w, splits, dtypes):
    B, T, D = x.shape
    tm = TOKEN_TILE
    outs = tuple(jax.ShapeDtypeStruct((B, T, c1 - c0), dt) for (c0, c1), dt in zip(splits, dtypes))
    return pl.pallas_call(
        functools.partial(_in_proj_body, splits=splits),
        out_shape=outs,
        grid=(B, T // tm),
        in_specs=[pl.BlockSpec((1, tm, D), lambda b, s: (b, s, 0)), _mod_spec(T // tm, 0, False),
                  pl.BlockSpec(w.shape, lambda b, s: (0, 0))],
        out_specs=tuple(pl.BlockSpec((1, tm, c1 - c0), lambda b, s: (b, s, 0)) for c0, c1 in splits),
        compiler_params=_params("arbitrary", "arbitrary"),
        name="in_proj",
    )(x, modtab, w)


def _out_proj_body(m_ref, x_ref, mod_ref, w_ref, ln_ref, x1_ref, h2_ref):
    y = jnp.dot(m_ref[0], w_ref[...], preferred_element_type=F32)
    mod = mod_ref[0, 0]
    x1 = _layer_norm_rows(DN_ALPHA * x_ref[0] + mod[2:3, :] * y, ln_ref[0:1, :], ln_ref[1:2, :])
    x1_ref[0] = x1
    h2_ref[0] = x1 * (1.0 + mod[4:5, :]) + mod[3:4, :]


def _out_proj(m, x, modtab, w, ln, first_tile):
    B, T, D = x.shape
    tm = TOKEN_TILE
    n_tiles = T // tm - first_tile
    rows = lambda b, s: (b, s + first_tile, 0)
    out_sds = jax.ShapeDtypeStruct((B, n_tiles * tm, D), F32)
    return pl.pallas_call(
        _out_proj_body,
        out_shape=(out_sds, out_sds),
        grid=(B, n_tiles),
        in_specs=[pl.BlockSpec((1, tm, m.shape[-1]), rows), pl.BlockSpec((1, tm, D), rows),
                  _mod_spec(T // tm, first_tile, False),
                  pl.BlockSpec(w.shape, lambda b, s: (0, 0)), pl.BlockSpec(ln.shape, lambda b, s: (0, 0))],
        out_specs=(pl.BlockSpec((1, tm, D), lambda b, s: (b, s, 0)),) * 2,
        compiler_params=_params("arbitrary", "arbitrary"),
        name="out_proj",
    )(m, x, modtab, w, ln)


def _rwkv_scan_body(rf, wf, kf, vf, nf, bf, rb, wb, kb, vb, nb, bb, yf_ref, yb_ref, state_ref, *, tc):
    @pl.when(pl.program_id(0) == 0)
    def _():
        state_ref[...] = jnp.zeros_like(state_ref)

    fwd_lane = lax.broadcasted_iota(jnp.int32, (RWKV_HEAD, LANES), 1) < LANES // 2

    def step(j, carry):
        jb = tc - 1 - j

        def sel(f, b):
            return jnp.where(fwd_lane, f[j], b[jb])

        r, w, k, v, kkn, bv = sel(rf, rb), sel(wf, wb), sel(kf, kb), sel(vf, vb), sel(nf, nb), sel(bf, bb)
        for vi in range(RWKV_HEAD):
            s = state_ref[vi]
            sa = jnp.sum(s * kkn, axis=0, keepdims=True)
            s2 = s * w + sa * bv + v[vi:vi + 1, :] * k
            state_ref[vi] = s2
            yrow = jnp.sum(s2 * r, axis=0, keepdims=True)
            yf_ref[j, pl.ds(vi, 1), :] = yrow
            yb_ref[jb, pl.ds(vi, 1), :] = yrow
        return carry

    lax.fori_loop(0, tc, step, 0)


def _rwkv_scan(xs, n_ctx):
    T = xs[0].shape[0]
    tc = RWKV_TIME_BLOCK
    assert T % tc == 0 and n_ctx % tc == 0
    nc, ncc = T // tc, n_ctx // tc
    blk = (tc, RWKV_HEAD, LANES)
    fwd = lambda c: (c, 0, 0)
    bwd = lambda c: (jnp.where(c < ncc, ncc - 1 - c, nc - 1 - (c - ncc)), 0, 0)
    out_sds = jax.ShapeDtypeStruct((T, RWKV_HEAD, LANES), F32)
    return pl.pallas_call(
        functools.partial(_rwkv_scan_body, tc=tc),
        out_shape=(out_sds, out_sds),
        grid=(nc,),
        in_specs=[pl.BlockSpec(blk, fwd)] * 6 + [pl.BlockSpec(blk, bwd)] * 6,
        out_specs=(pl.BlockSpec(blk, fwd), pl.BlockSpec(blk, bwd)),
        scratch_shapes=[pltpu.VMEM((RWKV_HEAD, RWKV_HEAD, LANES), F32)],
        compiler_params=_params("arbitrary"),
        name="rwkv_scan",
    )(*xs, *xs)


def _route_body(x_ref, rwt_ref, rb_ref, upper_ref, lower_ref, q_ref, g_ref, cnt_ref):
    tm = x_ref.shape[0]
    logits = lax.dot_general(rwt_ref[...], x_ref[...], (((1,), (1,)), ((), ())), preferred_element_type=F32,
                             precision=lax.Precision.HIGHEST)
    scores = jax.nn.sigmoid(logits)
    sel = scores + rb_ref[...]
    eidx = lax.broadcasted_iota(jnp.int32, (N_EXPERTS, tm), 0)
    slot = lax.broadcasted_iota(jnp.int32, (MOE_SLOTS, tm), 0)
    onehots = []
    for _ in range(TOP_K):
        m = jnp.max(sel, axis=0, keepdims=True)
        ij = jnp.min(jnp.where(sel == m, eidx, N_EXPERTS), axis=0, keepdims=True)
        oh = eidx == ij
        onehots.append(oh)
        sel = jnp.where(oh, -jnp.inf, sel)
    mask = functools.reduce(jnp.logical_or, onehots)
    maskf = jnp.where(mask, 1.0, 0.0)
    gsum = jnp.sum(jnp.where(mask, scores, 0.0), axis=0, keepdims=True)
    gates = scores / gsum * ROUTED_SCALE
    cnt = jnp.sum(maskf, axis=1, keepdims=True)
    cnt_pad = jnp.ceil(cnt * (1.0 / MOE_CHUNK)) * MOE_CHUNK
    lrank = jnp.dot(maskf.astype(BF16), upper_ref[...], preferred_element_type=F32)
    loff = jnp.dot(lower_ref[...], jnp.broadcast_to(cnt_pad, (N_EXPERTS, LANES)).astype(BF16),
                   preferred_element_type=F32)[:, 0:1]
    q = loff + lrank
    q8 = jnp.full((MOE_SLOTS, tm), -1.0, F32)
    g8 = jnp.zeros((MOE_SLOTS, tm), F32)
    for j, oh in enumerate(onehots):
        q8 = jnp.where(slot == j, jnp.sum(jnp.where(oh, q, 0.0), axis=0, keepdims=True), q8)
        g8 = jnp.where(slot == j, jnp.sum(jnp.where(oh, gates, 0.0), axis=0, keepdims=True), g8)
    q_ref[0] = q8.astype(jnp.int32)
    g_ref[0] = g8
    cnt_ref[0] = jnp.broadcast_to(cnt, (N_EXPERTS, LANES))


def _moe_route(tokens, router_w, router_b):
    n, d = tokens.shape
    tm = TOKEN_TILE
    nt = n // tm
    ar = jnp.arange(tm)
    upper = (ar[:, None] < ar[None, :]).astype(BF16)
    ae = jnp.arange(N_EXPERTS)
    lower = (ae[:, None] > ae[None, :]).astype(BF16)
    const = lambda shape: pl.BlockSpec(shape, lambda i: (0,) * len(shape))
    return pl.pallas_call(
        _route_body,
        out_shape=(jax.ShapeDtypeStruct((nt, MOE_SLOTS, tm), jnp.int32), jax.ShapeDtypeStruct((nt, MOE_SLOTS, tm), F32),
                   jax.ShapeDtypeStruct((nt, N_EXPERTS, LANES), F32)),
        grid=(nt,),
        in_specs=[pl.BlockSpec((tm, d), lambda i: (i, 0)), const((N_EXPERTS, d)), const((N_EXPERTS, 1)),
                  const((tm, tm)), const((N_EXPERTS, N_EXPERTS))],
        out_specs=(pl.BlockSpec((1, MOE_SLOTS, tm), lambda i: (i, 0, 0)),
                   pl.BlockSpec((1, MOE_SLOTS, tm), lambda i: (i, 0, 0)),
                   pl.BlockSpec((1, N_EXPERTS, LANES), lambda i: (i, 0, 0))),
        compiler_params=_params("arbitrary"),
        name="moe_router",
    )(tokens, router_w.T, router_b.reshape(N_EXPERTS, 1), upper, lower)


def _pack_bf16_pairs(x):
    m = x.shape[1] // 2
    bits = lambda a: lax.bitcast_convert_type(a.astype(BF16).astype(F32), jnp.uint32)
    return bits(x[:, :m]) | (bits(x[:, m:]) >> 16)


def _unpack_bf16_pairs(u):
    hi = lax.bitcast_convert_type(u & jnp.uint32(0xFFFF0000), F32)
    lo = lax.bitcast_convert_type(u << 16, F32)
    return jnp.concatenate([hi.astype(BF16), lo.astype(BF16)], axis=1)


def _chunk_loops(i, base_ref, nchunk_ref, loff_ref, copy):
    def per_expert(e, total):
        k = i * N_EXPERTS + e
        n, base, lo = nchunk_ref[k], base_ref[k], loff_ref[k]

        def piece(c, carry):
            copy(pl.multiple_of(lo + c * MOE_CHUNK, MOE_CHUNK), pl.multiple_of(base + c * MOE_CHUNK, MOE_CHUNK)).start()
            return carry

        lax.fori_loop(0, n, piece, 0)
        return total + n

    return lax.fori_loop(0, N_EXPERTS, per_expert, 0)


def _dispatch_body(base_ref, nchunk_ref, loff_ref, bv_ref, x_ref, q_ref, xs_hbm, stage_ref, zero_ref, sem_z, sem):
    i = pl.program_id(0)
    br = zero_ref.shape[0]
    n_blocks = xs_hbm.shape[0] // br

    @pl.when(i == 0)
    def _():
        zero_ref[...] = jnp.zeros_like(zero_ref)

        def fill_copy(b):
            return pltpu.make_async_copy(zero_ref, xs_hbm.at[pl.ds(b * br, br)], sem_z)

        def fill(b, carry):
            @pl.when(bv_ref[b] < br)
            def _():
                fill_copy(b).start()
            return carry

        def fill_wait(b, carry):
            @pl.when(bv_ref[b] < br)
            def _():
                fill_copy(b).wait()
            return carry

        lax.fori_loop(0, n_blocks, fill, 0)
        lax.fori_loop(0, n_blocks, fill_wait, 0)

    rows, tm = stage_ref.shape[0], x_ref.shape[0]
    q8 = q_ref[0]
    pos = lax.broadcasted_iota(jnp.int32, (rows, tm), 0)
    hit = functools.reduce(jnp.logical_or, [pos == q8[j:j + 1, :] for j in range(TOP_K)])
    perm = jnp.where(hit, 1.0, 0.0).astype(BF16)
    stage_ref[...] = _pack_bf16_pairs(jnp.dot(perm, x_ref[...].astype(BF16), preferred_element_type=F32))

    def copy(local_row, global_row):
        return pltpu.make_async_copy(stage_ref.at[pl.ds(local_row, MOE_CHUNK)], xs_hbm.at[pl.ds(global_row, MOE_CHUNK)], sem)

    total = _chunk_loops(i, base_ref, nchunk_ref, loff_ref, copy)

    def drain(c, carry):
        copy(0, 0).wait()
        return carry

    lax.fori_loop(0, total, drain, 0)


def _moe_dispatch(tokens, q_rows, dest_base, nchunk, loff, block_valid):
    n, d = tokens.shape
    tm, br = TOKEN_TILE, MOE_ROW_BLOCK
    n_rows = block_valid.shape[0] * br
    return pl.pallas_call(
        _dispatch_body,
        out_shape=jax.ShapeDtypeStruct((n_rows, d // 2), jnp.uint32),
        grid_spec=pltpu.PrefetchScalarGridSpec(
            num_scalar_prefetch=4,
            grid=(n // tm,),
            in_specs=[pl.BlockSpec((tm, d), lambda i, *_: (i, 0)),
                      pl.BlockSpec((1, MOE_SLOTS, tm), lambda i, *_: (i, 0, 0))],
            out_specs=pl.BlockSpec(memory_space=pl.ANY),
            scratch_shapes=[pltpu.VMEM((MOE_STAGE_ROWS, d // 2), jnp.uint32), pltpu.VMEM((br, d // 2), jnp.uint32),
                            pltpu.SemaphoreType.DMA, pltpu.SemaphoreType.DMA]),
        compiler_params=_params("arbitrary"),
        name="moe_dispatch",
    )(dest_base, nchunk, loff, block_valid, tokens, q_rows)


def _expert_body(be_ref, bv_ref, x_ref, w1_ref, w3_ref, w2_ref, y_ref, w1b, w3b, w2b):
    i = pl.program_id(0)
    valid = bv_ref[i]

    @pl.when(jnp.logical_or(i == 0, be_ref[i] != be_ref[jnp.maximum(i - 1, 0)]))
    def _():
        w1b[...] = w1_ref[0].astype(BF16)
        w3b[...] = w3_ref[0].astype(BF16)
        w2b[...] = w2_ref[0].astype(BF16)

    @pl.when(valid > 0)
    def _():
        x = _unpack_bf16_pairs(x_ref[...])
        h1 = jnp.dot(x, w1b[...], preferred_element_type=F32)
        h3 = jnp.dot(x, w3b[...], preferred_element_type=F32)
        a = (h1 * jax.nn.sigmoid(h1) * h3).astype(BF16)
        y_ref[...] = _pack_bf16_pairs(jnp.dot(a, w2b[...], preferred_element_type=F32))

    @pl.when(valid <= 0)
    def _():
        y_ref[...] = jnp.zeros_like(y_ref)


def _moe_experts(xs, block_expert, block_valid, w1, w3, w2):
    n_rows, dp = xs.shape
    br = MOE_ROW_BLOCK
    d, ff = w1.shape[-2:]
    return pl.pallas_call(
        _expert_body,
        out_shape=jax.ShapeDtypeStruct((n_rows, dp), jnp.uint32),
        grid_spec=pltpu.PrefetchScalarGridSpec(
            num_scalar_prefetch=2,
            grid=(n_rows // br,),
            in_specs=[pl.BlockSpec((br, dp), lambda i, be, bv: (i, 0)),
                      pl.BlockSpec((1, d, ff), lambda i, be, bv: (be[i], 0, 0)),
                      pl.BlockSpec((1, d, ff), lambda i, be, bv: (be[i], 0, 0)),
                      pl.BlockSpec((1, ff, d), lambda i, be, bv: (be[i], 0, 0))],
            out_specs=pl.BlockSpec((br, dp), lambda i, be, bv: (i, 0)),
            scratch_shapes=[pltpu.VMEM((d, ff), BF16), pltpu.VMEM((d, ff), BF16), pltpu.VMEM((ff, d), BF16)]),
        compiler_params=_params("arbitrary"),
        name="moe_experts",
    )(block_expert, block_valid, xs, w1, w3, w2)


def _combine_body(base_ref, nchunk_ref, loff_ref, h_ref, q_ref, g_ref, ys_hbm, sw1_ref, sw3_ref, sw2_ref,
                  x1_ref, mod_ref, ln_ref, out_ref, stage_ref, sem):
    i = pl.program_id(0)

    @pl.when(i == 0)
    def _():
        stage_ref[...] = jnp.zeros_like(stage_ref)

    def copy(local_row, global_row):
        return pltpu.make_async_copy(ys_hbm.at[pl.ds(global_row, MOE_CHUNK)], stage_ref.at[pl.ds(local_row, MOE_CHUNK)], sem)

    total = _chunk_loops(i, base_ref, nchunk_ref, loff_ref, copy)

    h = h_ref[...].astype(BF16)
    h1 = jnp.dot(h, sw1_ref[...], preferred_element_type=F32)
    h3 = jnp.dot(h, sw3_ref[...], preferred_element_type=F32)
    y = jnp.dot((h1 * jax.nn.sigmoid(h1) * h3).astype(BF16), sw2_ref[...], preferred_element_type=F32)
    tm, rows = h_ref.shape[0], stage_ref.shape[0]
    q8, g8 = q_ref[0], g_ref[0]
    pos = lax.broadcasted_iota(jnp.int32, (tm, rows), 1)
    gate_mat = jnp.zeros((tm, rows), F32)
    for j in range(TOP_K):
        gate_mat = jnp.where(pos == q8[:, j:j + 1], g8[:, j:j + 1], gate_mat)

    def drain(c, carry):
        copy(0, 0).wait()
        return carry

    lax.fori_loop(0, total, drain, 0)
    y = y + jnp.dot(gate_mat.astype(BF16), _unpack_bf16_pairs(stage_ref[...]), preferred_element_type=F32)
    out_ref[...] = _layer_norm_rows(DN_ALPHA * x1_ref[...] + mod_ref[0, 0, 5:6, :] * y, ln_ref[0:1, :], ln_ref[1:2, :])


def _moe_combine(tokens, q_cols, g_cols, dest_base, nchunk, loff, ys, sw1, sw3, sw2, x1, modtab, ln, tiles_per_batch,
                 first_tile):
    n, d = tokens.shape
    tm = TOKEN_TILE
    ff = sw1.shape[-1]
    whole = lambda shape: pl.BlockSpec(shape, lambda i, *_: (0,) * len(shape))
    rows = pl.BlockSpec((tm, d), lambda i, *_: (i, 0))
    slots = pl.BlockSpec((1, tm, MOE_SLOTS), lambda i, *_: (i, 0, 0))
    return pl.pallas_call(
        _combine_body,
        out_shape=jax.ShapeDtypeStruct((n, d), F32),
        grid_spec=pltpu.PrefetchScalarGridSpec(
            num_scalar_prefetch=3,
            grid=(n // tm,),
            in_specs=[rows, slots, slots, pl.BlockSpec(memory_space=pl.ANY),
                      whole((d, ff)), whole((d, ff)), whole((ff, d)),
                      rows, _mod_spec(tiles_per_batch, first_tile, True), whole(ln.shape)],
            out_specs=rows,
            scratch_shapes=[pltpu.VMEM((MOE_STAGE_ROWS, d // 2), jnp.uint32), pltpu.SemaphoreType.DMA]),
        compiler_params=_params("arbitrary"),
        name="moe_combine",
    )(dest_base, nchunk, loff, tokens, q_cols, g_cols, ys, sw1, sw3, sw2, x1, modtab, ln)


def _moe_block(h2, x1, modtab, ln, first_tile, router_w, router_b, w1, w3, w2, sw1, sw3, sw2):
    B, Tp, d = h2.shape
    tokens = h2.reshape(B * Tp, d)
    n = B * Tp
    tm, br = TOKEN_TILE, MOE_ROW_BLOCK
    q_rows, g_rows, cnt = _moe_route(tokens, router_w, router_b)
    cnt = cnt[:, :, 0].astype(jnp.int32)
    nchunk = (cnt + MOE_CHUNK - 1) // MOE_CHUNK
    run = nchunk * MOE_CHUNK
    total = jnp.sum(run, axis=0)
    padded = (total + br - 1) // br * br
    p_end = jnp.cumsum(padded)
    offs = p_end - padded
    dest_base = (offs[None, :] + jnp.cumsum(run, axis=0) - run).reshape(-1).astype(jnp.int32)
    loff = jnp.cumsum(run, axis=1) - run
    nchunk, loff = nchunk.reshape(-1).astype(jnp.int32), loff.reshape(-1).astype(jnp.int32)
    n_blocks = -(-(n * TOP_K + (n // tm) * N_EXPERTS * (MOE_CHUNK - 1) + N_EXPERTS * (br - 1)) // br)
    blk_start = jnp.arange(n_blocks, dtype=jnp.int32) * br
    block_expert = jnp.minimum(jnp.sum(blk_start[:, None] >= p_end[None, :], axis=1), N_EXPERTS - 1).astype(jnp.int32)
    block_valid = jnp.clip(total[block_expert] - (blk_start - offs[block_expert]), 0, br).astype(jnp.int32)
    xs = _moe_dispatch(tokens, q_rows, dest_base, nchunk, loff, block_valid)
    ys = _moe_experts(xs, block_expert, block_valid, w1, w3, w2)
    q_cols, g_cols = jnp.swapaxes(q_rows, 1, 2), jnp.swapaxes(g_rows, 1, 2)
    out = _moe_combine(tokens, q_cols, g_cols, dest_base, nchunk, loff, ys,
                       sw1.astype(BF16), sw3.astype(BF16), sw2.astype(BF16),
                       x1.reshape(n, d), modtab, ln, Tp // tm, first_tile)
    return out.reshape(B, Tp, d)


def _low_half():
    return lax.broadcasted_iota(jnp.int32, (1, LANES), 1) < LANES // 2


def _pair_attention(q, parts):
    low = _low_half()
    outs = []
    for use_low in (True, False):
        qm = jnp.where(low == use_low, q, jnp.zeros_like(q))
        scores = []
        for k, _, b_lo, b_hi in parts:
            s = lax.dot_general(qm, k, (((1,), (1,)), ((), ())), preferred_element_type=F32) * HEAD_DIM ** -0.5
            b = b_lo if use_low else b_hi
            scores.append(s if b is None else s + b)
        m = functools.reduce(jnp.maximum, [jnp.max(s, axis=-1, keepdims=True) for s in scores])
        den = 0.0
        num = 0.0
        for s, (_, v, _, _) in zip(scores, parts):
            p = jnp.exp(s - m)
            den = den + jnp.sum(p, axis=-1, keepdims=True)
            num = num + jnp.dot(p.astype(BF16), v, preferred_element_type=F32)
        outs.append(num / den)
    return jnp.where(low, outs[0], outs[1])


NA_Q_ROWS = 2
NA_BAND_ROWS = NA_WIN_ROWS + NA_Q_ROWS - 1


def _na_body(cls_ref, q_ref, k_ref, v_ref, bias_ref, o_ref, *, n_ctx, rows):
    s = pl.program_id(1)
    tq = q_ref.shape[1]
    ctx_steps = n_ctx // tq
    n_tiles = NA_DIM // LANES
    tile = lambda t: slice(t * LANES, (t + 1) * LANES)

    @pl.when(s < ctx_steps)
    def _():
        for t in range(n_tiles):
            part = (k_ref[0, 0:n_ctx, tile(t)], v_ref[0, 0:n_ctx, tile(t)], None, None)
            o_ref[0, :, tile(t)] = _pair_attention(q_ref[0, :, tile(t)], [part]).astype(o_ref.dtype)

    @pl.when(s >= ctx_steps)
    def _():
        first = (s - ctx_steps) * NA_Q_ROWS
        start = jnp.clip(first - NA_WIN_ROWS // 2, 0, rows - NA_BAND_ROWS)
        off = pl.multiple_of(n_ctx + start * GRID_W, GRID_W)
        band = pl.ds(off, NA_BAND_ROWS * GRID_W)
        for t in range(n_tiles):
            parts = [(k_ref[0, band, tile(t)], v_ref[0, band, tile(t)], bias_ref[0, 2 * t], bias_ref[0, 2 * t + 1]),
                     (k_ref[0, 0:n_ctx, tile(t)], v_ref[0, 0:n_ctx, tile(t)], None, None)]
            o_ref[0, :, tile(t)] = _pair_attention(q_ref[0, :, tile(t)], parts).astype(o_ref.dtype)


def _na_bias_table(rpb, rows):
    import numpy as np
    kc, W = NA_WIN_COLS, GRID_W
    cidx = np.arange(W)
    col_start = np.clip(cidx - kc // 2, 0, W - kc)
    col_in = (cidx[None, :] >= col_start[:, None]) & (cidx[None, :] < col_start[:, None] + kc)
    d_col = np.clip(cidx[None, :] - cidx[:, None], -(kc - 1), kc - 1) + kc - 1
    classes, class_of, d_rows, allowed = {}, [], [], []
    for step in range(rows // NA_Q_ROWS):
        first = step * NA_Q_ROWS
        band0 = min(max(first - NA_WIN_ROWS // 2, 0), rows - NA_BAND_ROWS)
        d_row = np.zeros((NA_Q_ROWS, NA_BAND_ROWS), np.int32)
        ok = np.zeros((NA_Q_ROWS, NA_BAND_ROWS), bool)
        for p in range(NA_Q_ROWS):
            win0 = min(max(first + p - NA_WIN_ROWS // 2, 0), rows - NA_WIN_ROWS)
            for j in range(NA_BAND_ROWS):
                ok[p, j] = win0 <= band0 + j < win0 + NA_WIN_ROWS
                d_row[p, j] = min(max(band0 + j - (first + p) + NA_WIN_ROWS - 1, 0), 2 * NA_WIN_ROWS - 2)
        key = (d_row.tobytes(), ok.tobytes())
        if key not in classes:
            classes[key] = len(classes)
            d_rows.append(d_row)
            allowed.append(ok)
        class_of.append(classes[key])
    d_rows, allowed = np.stack(d_rows), np.stack(allowed)
    tab = rpb[:, d_rows][..., d_col]
    mask = allowed[None, :, :, :, None, None] & col_in[None, None, None, None]
    tab = jnp.where(mask, tab, -jnp.inf)
    tab = jnp.transpose(tab, (1, 0, 2, 4, 3, 5))
    return tab.reshape(len(classes), NA_HEADS, NA_Q_ROWS * W, NA_BAND_ROWS * W), class_of


def _na_attention(pa, rpb, n_ctx):
    B, T, _ = pa.shape
    rows = (T - n_ctx) // GRID_W
    tq = NA_Q_ROWS * GRID_W
    assert rows % NA_Q_ROWS == 0 and rows >= NA_BAND_ROWS and n_ctx % tq == 0
    table, class_of = _na_bias_table(rpb, rows)
    step_class = jnp.array([0] * (n_ctx // tq) + class_of, jnp.int32)
    return pl.pallas_call(
        functools.partial(_na_body, n_ctx=n_ctx, rows=rows),
        out_shape=jax.ShapeDtypeStruct((B, T, NA_DIM), BF16),
        grid_spec=pltpu.PrefetchScalarGridSpec(
            num_scalar_prefetch=1,
            grid=(B, T // tq),
            in_specs=[pl.BlockSpec((1, tq, NA_DIM), lambda b, s, cls: (b, s, 0)),
                      pl.BlockSpec((1, T, NA_DIM), lambda b, s, cls: (b, 0, 1)),
                      pl.BlockSpec((1, T, NA_DIM), lambda b, s, cls: (b, 0, 2)),
                      pl.BlockSpec((1,) + table.shape[1:], lambda b, s, cls: (cls[s], 0, 0, 0))],
            out_specs=pl.BlockSpec((1, tq, NA_DIM), lambda b, s, cls: (b, s, 0))),
        compiler_params=_params("arbitrary", "arbitrary"),
        name="na_attention",
    )(step_class, pa, pa, pa, table)


def _rms_pair(x, gain):
    low = _low_half()
    sq = x * x
    s_lo = jnp.sum(jnp.where(low, sq, 0.0), axis=-1, keepdims=True)
    s_hi = jnp.sum(jnp.where(low, 0.0, sq), axis=-1, keepdims=True)
    ms = jnp.where(low, s_lo, s_hi) * (1.0 / HEAD_DIM)
    return x * lax.rsqrt(ms + NORM_EPS) * gain


def _rope_pair(x, cos, sin_signed):
    even = lax.broadcasted_iota(jnp.int32, (1, LANES), 1) % 2 == 0
    partner = jnp.where(even, pltpu.roll(x, LANES - 1, axis=1), pltpu.roll(x, 1, axis=1))
    return x * cos + partner * sin_signed


def _gqa_body(q_ref, k_ref, v_ref, cos_q, sin_q, cos_k, sin_k, gain_ref, o_ref, kn_ref, vn_ref, *, n_ctx):
    s = pl.program_id(1)
    tq = q_ref.shape[1]

    @pl.when(s == 0)
    def _():
        kn_ref[...] = _rope_pair(_rms_pair(k_ref[0], gain_ref[1:2, :]), cos_k[...], sin_k[...]).astype(BF16)
        vn_ref[...] = v_ref[0].astype(BF16)

    def run(n_keys):
        k, v = kn_ref[0:n_keys, :], vn_ref[0:n_keys, :]
        for t in range(GQA_Q_DIM // LANES):
            q = q_ref[0, :, t * LANES:(t + 1) * LANES]
            qn = _rope_pair(_rms_pair(q, gain_ref[0:1, :]), cos_q[...], sin_q[...]).astype(BF16)
            o_ref[0, :, t * LANES:(t + 1) * LANES] = _pair_attention(qn, [(k, v, None, None)]).astype(o_ref.dtype)

    @pl.when(s < n_ctx // tq)
    def _():
        run(n_ctx)

    @pl.when(s >= n_ctx // tq)
    def _():
        run(kn_ref.shape[0])


GQA_HEAD_ORDER = (0, 4, 1, 5, 2, 6, 3, 7)


def _axial_rope(n_tokens):
    t = jnp.arange(n_tokens)
    row = (t // GRID_W).astype(F32)
    col = (t % GRID_W).astype(F32)
    inv = ROPE_THETA ** (-jnp.arange(0, ROPE_AXIS_DIM, 2, dtype=F32) / ROPE_AXIS_DIM)
    ang = jnp.concatenate([row[:, None] * inv, col[:, None] * inv], -1)
    return jnp.cos(ang), jnp.sin(ang)


def _gqa_rope_tables(T, n_ctx):
    cos, sin = _axial_rope(T - n_ctx)
    cos = jnp.concatenate([jnp.ones((n_ctx, ROPE_AXIS_DIM), F32), cos], 0)
    sin = jnp.concatenate([jnp.zeros((n_ctx, ROPE_AXIS_DIM), F32), sin], 0)
    cos = jnp.tile(jnp.repeat(cos, 2, axis=-1), (1, 2))
    sign = jnp.tile(jnp.array([-1.0, 1.0], F32), LANES // 2)
    sin = jnp.tile(jnp.repeat(sin, 2, axis=-1), (1, 2)) * sign
    return cos, sin


def _gqa_attention(pb, qk_gain, n_ctx):
    B, T, _ = pb.shape
    tq = GQA_Q_TILE
    cos, sin = _gqa_rope_tables(T, n_ctx)
    gain = jnp.tile(qk_gain, (1, 2))
    kv_blk = GQA_Q_DIM // GQA_KV_DIM
    return pl.pallas_call(
        functools.partial(_gqa_body, n_ctx=n_ctx),
        out_shape=jax.ShapeDtypeStruct((B, T, GQA_Q_DIM), BF16),
        grid=(B, T // tq),
        in_specs=[pl.BlockSpec((1, tq, GQA_Q_DIM), lambda b, s: (b, s, 0)),
                  pl.BlockSpec((1, T, GQA_KV_DIM), lambda b, s: (b, 0, kv_blk)),
                  pl.BlockSpec((1, T, GQA_KV_DIM), lambda b, s: (b, 0, kv_blk + 1)),
                  pl.BlockSpec((tq, LANES), lambda b, s: (s, 0)),
                  pl.BlockSpec((tq, LANES), lambda b, s: (s, 0)),
                  pl.BlockSpec((T, LANES), lambda b, s: (0, 0)),
                  pl.BlockSpec((T, LANES), lambda b, s: (0, 0)),
                  pl.BlockSpec((2, LANES), lambda b, s: (0, 0))],
        out_specs=pl.BlockSpec((1, tq, GQA_Q_DIM), lambda b, s: (b, s, 0)),
        scratch_shapes=[pltpu.VMEM((T, GQA_KV_DIM), BF16), pltpu.VMEM((T, GQA_KV_DIM), BF16)],
        compiler_params=_params("arbitrary", "arbitrary"),
        name="gqa_attention",
    )(pb, pb, pb, cos, sin, cos, sin, gain)


def _attention_mixers(x, modtab, w_in, rpb, qk_gain, n_ctx):
    order = jnp.array(GQA_HEAD_ORDER)
    qb_cols = 3 * NA_DIM + (order[:, None] * HEAD_DIM + jnp.arange(HEAD_DIM)[None, :]).reshape(-1)
    cols = jnp.concatenate([jnp.arange(3 * NA_DIM), qb_cols, jnp.arange(3 * NA_DIM + GQA_Q_DIM, w_in.shape[1])])
    w = w_in[:, cols].astype(BF16)
    pa, pb = _in_proj(x, modtab, w, ((0, 3 * NA_DIM), (3 * NA_DIM, w.shape[1])), (BF16, F32))
    return jnp.concatenate([_na_attention(pa, rpb, n_ctx), _gqa_attention(pb, qk_gain, n_ctx)], -1)


def _attention_w_out(w_out):
    order = jnp.array(GQA_HEAD_ORDER)
    rows = NA_DIM + (order[:, None] * HEAD_DIM + jnp.arange(HEAD_DIM)[None, :]).reshape(-1)
    return jnp.concatenate([w_out[:NA_DIM], w_out[rows]], 0)


def _log_sigmoid(x):
    return jnp.minimum(x, 0.0) - jnp.log(1.0 + jnp.exp(-jnp.abs(x)))


def _mlstm_body(qf, kf, vf, gcf, grf, qb, kb, vb, gcb, grb, bias_c, bias_r, hf_ref, hb_ref, c_ref, n_ref, m_ref):
    @pl.when(pl.program_id(1) == 0)
    def _():
        c_ref[...] = jnp.zeros_like(c_ref)
        n_ref[...] = jnp.zeros_like(n_ref)
        m_ref[...] = jnp.zeros_like(m_ref)

    L = MLSTM_CHUNK
    row = lax.broadcasted_iota(jnp.int32, (L, L), 0)
    col = lax.broadcasted_iota(jnp.int32, (L, L), 1)
    hi = lax.Precision.HIGHEST
    writes = []
    for d, (q_ref, k_ref, v_ref, gc_ref, gr_ref, h_ref) in enumerate(((qf, kf, vf, gcf, grf, hf_ref),
                                                                     (qb, kb, vb, gcb, grb, hb_ref))):
        seen = (col <= row) if d == 0 else (col >= row)
        seen_f = jnp.where(seen, 1.0, 0.0)
        g_col = gc_ref[0] + bias_c[...]
        g_row = gr_ref[0, 0] + bias_r[...]
        b_col = jnp.dot(seen_f, _log_sigmoid(g_col), preferred_element_type=F32, precision=hi)
        lf_row = _log_sigmoid(g_row)
        b_row = lax.dot_general(lf_row, seen_f, (((1,), (1,)), ((), ())), preferred_element_type=F32, precision=hi)
        b_end = jnp.sum(lf_row, axis=1, keepdims=True)
        for h in range(MLSTM_HEADS):
            gi, gf = d * 2 * MLSTM_HEADS + h, d * 2 * MLSTM_HEADS + MLSTM_HEADS + h
            sl = slice(h * MLSTM_HEAD, (h + 1) * MLSTM_HEAD)
            s_idx = d * MLSTM_HEADS + h
            c_state, n_state, m_state = c_ref[s_idx], n_ref[s_idx:s_idx + 1, :], m_ref[s_idx:s_idx + 1, 0:1]
            bc, br, ig_c, ig_r = b_col[:, gf:gf + 1], b_row[gf:gf + 1, :], g_col[:, gi:gi + 1], g_row[gi:gi + 1, :]
            be = b_end[gf:gf + 1, :]
            d_intra = jnp.where(seen, bc - br + ig_r, -jnp.inf)
            d_inter = bc + m_state
            m_t = jnp.maximum(d_inter, jnp.max(d_intra, axis=1, keepdims=True))
            q = (q_ref[0, :, sl] * MLSTM_HEAD ** -0.5).astype(BF16)
            k, v = k_ref[0, :, sl], v_ref[0, :, sl]
            kb16 = k.astype(BF16)
            s = lax.dot_general(q, kb16, (((1,), (1,)), ((), ())), preferred_element_type=F32) * jnp.exp(d_intra - m_t)
            w_inter = jnp.exp(d_inter - m_t)
            qc = lax.dot_general(q, c_state.astype(BF16), (((1,), (1,)), ((), ())), preferred_element_type=F32)
            num = jnp.dot(s.astype(BF16), v.astype(BF16), preferred_element_type=F32) + w_inter * qc
            qn = jnp.sum(q.astype(F32) * n_state, axis=1, keepdims=True)
            den = jnp.sum(s, axis=1, keepdims=True) + w_inter * qn
            writes.append((h_ref.at[0, :, sl], num / jnp.maximum(jnp.abs(den), jnp.exp(-m_t))))
            d_state = be - bc + ig_c
            m_new = jnp.maximum(be + m_state, jnp.max(d_state, axis=0, keepdims=True))
            w_s = jnp.exp(d_state - m_new)
            w_c = jnp.exp(be + m_state - m_new)
            vk = lax.dot_general((v * w_s).astype(BF16), kb16, (((0,), (0,)), ((), ())), preferred_element_type=F32)
            writes.append((c_ref.at[s_idx], w_c * c_state + vk))
            writes.append((n_ref.at[s_idx:s_idx + 1, :], w_c * n_state + jnp.sum(w_s * k, axis=0, keepdims=True)))
            writes.append((m_ref.at[s_idx:s_idx + 1, :], jnp.broadcast_to(m_new, (1, LANES))))
    for ref, value in writes:
        ref[...] = value


def _mlstm(pm, gates, gate_b, n_ctx):
    B, T, _ = pm.shape
    L = MLSTM_CHUNK
    nc, ncc = T // L, n_ctx // L
    ng = 4 * MLSTM_HEADS
    g_cols = jnp.pad(gates, ((0, 0), (0, 0), (0, LANES - ng)))
    g_rows = jnp.swapaxes(gates.reshape(B, nc, L, ng), 2, 3)
    bias = gate_b.reshape(ng)
    bias_c = jnp.pad(bias, (0, LANES - ng)).reshape(1, LANES)
    bias_r = bias.reshape(ng, 1)
    fwd = lambda c: c
    bwd = lambda c: jnp.where(c < ncc, ncc - 1 - c, nc - 1 - (c - ncc))
    blk = (1, L, MLSTM_DIM)
    seq = lambda order, j: pl.BlockSpec(blk, lambda b, c: (b, order(c), j))
    gcol = lambda order: pl.BlockSpec((1, L, LANES), lambda b, c: (b, order(c), 0))
    grow = lambda order: pl.BlockSpec((1, 1, ng, L), lambda b, c: (b, order(c), 0, 0))
    n_state = 2 * MLSTM_HEADS
    out_sds = jax.ShapeDtypeStruct((B, T, MLSTM_DIM), F32)
    return pl.pallas_call(
        _mlstm_body,
        out_shape=(out_sds, out_sds),
        grid=(B, nc),
        in_specs=[seq(fwd, 0), seq(fwd, 1), seq(fwd, 2), gcol(fwd), grow(fwd),
                  seq(bwd, 0), seq(bwd, 1), seq(bwd, 2), gcol(bwd), grow(bwd),
                  pl.BlockSpec((1, LANES), lambda b, c: (0, 0)), pl.BlockSpec((ng, 1), lambda b, c: (0, 0))],
        out_specs=(pl.BlockSpec(blk, lambda b, c: (b, fwd(c), 0)), pl.BlockSpec(blk, lambda b, c: (b, bwd(c), 0))),
        scratch_shapes=[pltpu.VMEM((n_state, MLSTM_HEAD, MLSTM_HEAD), F32), pltpu.VMEM((n_state, MLSTM_HEAD), F32),
                        pltpu.VMEM((n_state, LANES), F32)],
        compiler_params=_params("arbitrary", "arbitrary"),
        name="mlstm",
    )(pm, pm, pm, g_cols, g_rows, pm, pm, pm, g_cols, g_rows, bias_c, bias_r)


def _pair_sums(x):
    low = _low_half()
    s_lo = jnp.sum(jnp.where(low, x, 0.0), axis=-1, keepdims=True)
    s_hi = jnp.sum(jnp.where(low, 0.0, x), axis=-1, keepdims=True)
    return jnp.where(low, s_lo, s_hi)


def _rwkv_prep_body(p_ref, prev_ref, next_ref, tab_ref, tab2_ref, w1_ref, w2_ref, a1_ref, a2_ref, g1_ref, g2_ref,
                    r_ref, v_ref, n_ref, w_ref, k_ref, b_ref, gate_ref, *, n_ctx):
    s = pl.program_id(1)
    tm = p_ref.shape[1]
    C = RWKV_DIM
    ctx_tiles = n_ctx // tm
    x = p_ref[0]
    has_prev = jnp.logical_and(s != 0, s != ctx_tiles)
    has_next = jnp.logical_and(s != ctx_tiles - 1, s != pl.num_programs(1) - 1)
    prev_row = jnp.where(has_prev, prev_ref[0, SUBLANES - 1:SUBLANES, :], 0.0)
    next_row = jnp.where(has_next, next_ref[0, 0:1, :], 0.0)
    rowid = lax.broadcasted_iota(jnp.int32, (tm, 1), 0)
    up = jnp.where(rowid == 0, prev_row, pltpu.roll(x, 1, axis=0))
    dn = jnp.where(rowid == tm - 1, next_row, pltpu.roll(x, tm - 1, axis=0))
    d = 0.5 * (up + dn) - x
    part = lambda a, i: a[:, i * C:(i + 1) * C]
    mu = lambda i: tab_ref[i:i + 1, :]
    r = part(x, 0) + part(d, 0) * mu(0)
    k = part(x, 1) + part(d, 1) * mu(1)
    v = part(x, 2) + part(d, 2) * mu(2)
    z, dz = part(x, 3), part(d, 3)
    z_w, z_a, z_g = (z + dz * mu(3)).astype(BF16), (z + dz * mu(4)).astype(BF16), (z + dz * mu(5)).astype(BF16)
    lora = lambda t, w: jnp.dot(t.astype(BF16), w[...], preferred_element_type=F32)
    w_pre = tab2_ref[0:1, :] + lora(jnp.tanh(lora(z_w, w1_ref)), w2_ref)
    neg = -w_pre
    softplus = jnp.maximum(neg, 0.0) + jnp.log(1.0 + jnp.exp(-jnp.abs(neg)))
    decay = jnp.exp(-jnp.exp(-softplus - 0.5))
    iclr = jax.nn.sigmoid(tab2_ref[1:2, :] + lora(lora(z_a, a1_ref), a2_ref))
    gate_ref[0] = lora(jax.nn.sigmoid(lora(z_g, g1_ref)), g2_ref)
    kk = k * tab_ref[6:7, :]
    kk = jnp.concatenate(
        [kk[:, t * LANES:(t + 1) * LANES]
         * lax.rsqrt(jnp.maximum(_pair_sums(jnp.square(kk[:, t * LANES:(t + 1) * LANES])), 1e-24))
         for t in range(C // LANES)], axis=1)
    k2, kk2 = jnp.concatenate([k, k], axis=1), jnp.concatenate([kk, kk], axis=1)
    r_ref[0] = r
    v_ref[0] = v
    n_ref[0] = -kk
    w_ref[0] = decay
    k_ref[0] = k2 * (1.0 + (iclr - 1.0) * tab2_ref[2:3, :])
    b_ref[0] = kk2 * iclr


def _rwkv_prep(pr, n_ctx, mu, w0, w1, w2, a0, a1, a2, g1, g2, kvec):
    B, T, _ = pr.shape
    tm, C = TOKEN_TILE, RWKV_DIM
    per_tile = tm // SUBLANES
    tab = jnp.concatenate([mu, kvec[0:1], jnp.zeros((1, C), F32)], 0)
    cat = lambda a: jnp.concatenate([a[0], a[1]], -1)
    tab2 = jnp.concatenate([cat(w0)[None], cat(a0)[None], jnp.tile(kvec[1], 2)[None], jnp.zeros((5, 2 * C), F32)], 0)
    blockdiag = lambda a: jnp.concatenate([jnp.pad(a[0], ((0, 0), (0, C))), jnp.pad(a[1], ((0, 0), (C, 0)))], 0)
    gl = g1.shape[1]
    consts = (tab, tab2, cat(w1).astype(BF16), blockdiag(w2).astype(BF16), cat(a1).astype(BF16),
              blockdiag(a2).astype(BF16), jnp.pad(g1, ((0, 0), (0, LANES - gl))).astype(BF16),
              jnp.pad(g2, ((0, LANES - gl), (0, 0))).astype(BF16))
    const = lambda a: pl.BlockSpec(a.shape, lambda b, s: (0, 0))
    one = jax.ShapeDtypeStruct((B, T, C), F32)
    two = jax.ShapeDtypeStruct((B, T, 2 * C), F32)
    spec1 = pl.BlockSpec((1, tm, C), lambda b, s: (b, s, 0))
    spec2 = pl.BlockSpec((1, tm, 2 * C), lambda b, s: (b, s, 0))
    return pl.pallas_call(
        functools.partial(_rwkv_prep_body, n_ctx=n_ctx),
        out_shape=(one, one, one, two, two, two, one),
        grid=(B, T // tm),
        in_specs=[pl.BlockSpec((1, tm, 4 * C), lambda b, s: (b, s, 0)),
                  pl.BlockSpec((1, SUBLANES, 4 * C), lambda b, s: (b, jnp.maximum(s * per_tile - 1, 0), 0)),
                  pl.BlockSpec((1, SUBLANES, 4 * C), lambda b, s: (b, jnp.minimum((s + 1) * per_tile, T // SUBLANES - 1), 0))]
        + [const(a) for a in consts],
        out_specs=(spec1, spec1, spec1, spec2, spec2, spec2, spec1),
        compiler_params=_params("arbitrary", "arbitrary"),
        name="rwkv_prep",
    )(pr, pr, pr, *consts)


def _rec_post_body(y_ref, r_ref, v_ref, k_ref, gate_ref, hf_ref, hb_ref, o_ref, tab_ref, out_ref):
    C = RWKV_DIM
    for t in range(C // LANES):
        sl = slice(t * LANES, (t + 1) * LANES)
        y = y_ref[0, :, sl]
        yc = y - _pair_sums(y) * (1.0 / RWKV_HEAD)
        var = _pair_sums(yc * yc) * (1.0 / RWKV_HEAD)
        yn = yc * lax.rsqrt(var + RWKV_GN_EPS) * tab_ref[0:1, sl] + tab_ref[1:2, sl]
        k_sum = k_ref[0, :, sl] + k_ref[0, :, C + t * LANES:C + (t + 1) * LANES]
        bonus = _pair_sums(r_ref[0, :, sl] * k_sum * tab_ref[2:3, sl]) * v_ref[0, :, sl]
        out_ref[0, :, sl] = ((yn + bonus) * gate_ref[0, :, sl]).astype(out_ref.dtype)
    for t in range(MLSTM_HEADS):
        sl = slice(t * MLSTM_HEAD, (t + 1) * MLSTM_HEAD)
        h = hf_ref[0, :, sl] + hb_ref[0, :, sl]
        hn = h * lax.rsqrt(jnp.mean(h * h, axis=-1, keepdims=True) + NORM_EPS) * tab_ref[3:4, sl]
        out_ref[0, :, C + t * MLSTM_HEAD:C + (t + 1) * MLSTM_HEAD] = (hn * jax.nn.sigmoid(o_ref[0, :, sl])).astype(out_ref.dtype)


def _rec_post(y, r, v, k_eff, gate, h_f, h_b, pm, gn, r_k, norm_g):
    B, T, C = y.shape
    tm = TOKEN_TILE
    tab = jnp.concatenate([gn, r_k.reshape(1, C), norm_g.reshape(1, C), jnp.zeros((4, C), F32)], 0)
    spec1 = pl.BlockSpec((1, tm, C), lambda b, s: (b, s, 0))
    return pl.pallas_call(
        _rec_post_body,
        out_shape=jax.ShapeDtypeStruct((B, T, C + MLSTM_DIM), BF16),
        grid=(B, T // tm),
        in_specs=[spec1, spec1, spec1, pl.BlockSpec((1, tm, 2 * C), lambda b, s: (b, s, 0)), spec1, spec1, spec1,
                  pl.BlockSpec((1, tm, MLSTM_DIM), lambda b, s: (b, s, 3)), pl.BlockSpec(tab.shape, lambda b, s: (0, 0))],
        out_specs=pl.BlockSpec((1, tm, C + MLSTM_DIM), lambda b, s: (b, s, 0)),
        compiler_params=_params("arbitrary", "arbitrary"),
        name="rec_post",
    )(y, r, v, k_eff, gate, h_f, h_b, pm, tab)


def _to_state_lanes(x, nd):
    B, T, _ = x.shape
    y = jnp.transpose(x.reshape(B, T, nd, RWKV_HEADS, RWKV_HEAD), (1, 4, 2, 0, 3)).reshape(T, RWKV_HEAD, nd * B * RWKV_HEADS)
    return jnp.concatenate([y, y], -1) if nd == 1 else y


def _recurrent_mixers(x, modtab, w_in, n_ctx, mu, w0, w1, w2, a0, a1, a2, g1, g2, kvec, r_k, gn, gate_b, norm_g):
    B, T, _ = x.shape
    assert 2 * B * RWKV_HEADS == LANES
    n_main = RWKV_IN + 4 * MLSTM_DIM
    w = jnp.pad(w_in, ((0, 0), (0, n_main + LANES - w_in.shape[1]))).astype(BF16)
    pr, pm, pg = _in_proj(x, modtab, w, ((0, RWKV_IN), (RWKV_IN, n_main), (n_main, n_main + LANES)), (F32, F32, F32))
    r, v, kkn, decay, k_eff, kka, gate = _rwkv_prep(pr, n_ctx, mu, w0, w1, w2, a0, a1, a2, g1, g2, kvec)
    scan_in = (_to_state_lanes(r, 1), _to_state_lanes(decay, 2), _to_state_lanes(k_eff, 2), _to_state_lanes(v, 1),
               _to_state_lanes(kkn, 1), _to_state_lanes(kka, 2))
    h_f, h_b = _mlstm(pm, pg[..., :4 * MLSTM_HEADS], gate_b, n_ctx)
    scan_in, h_f, h_b = lax.optimization_barrier((scan_in, h_f, h_b))
    yf, yb = _rwkv_scan(scan_in, n_ctx)
    half = LANES // 2
    y = yf[:, :, :half] + yb[:, :, half:]
    y = jnp.transpose(y.reshape(T, RWKV_HEAD, B, RWKV_HEADS), (2, 0, 3, 1)).reshape(B, T, RWKV_DIM)
    return _rec_post(y, r, v, k_eff, gate, h_f, h_b, pm, gn, r_k, norm_g)


def kernel(x, c, ctx, c_ctx, ada_w, ada_b, ln_g, ln_b, mix_w_out, att_w_in, na_rpb, qk_gain, rec_w_in, rwkv_mu, rwkv_w0, rwkv_w1, rwkv_w2, rwkv_a0, rwkv_a1, rwkv_a2, rwkv_g1, rwkv_g2, rwkv_kvec, rwkv_rk, rwkv_gn, mlstm_gate_b, mlstm_norm, moe_router, moe_bias, moe_w1, moe_w3, moe_w2, shared_w1, shared_w3, shared_w2):
    B, S, D = x.shape
    n_ctx = ctx.shape[1]
    assert D == D_MODEL and n_ctx % TOKEN_TILE == 0 and S % TOKEN_TILE == 0
    xs = jnp.concatenate([ctx, x], axis=1)
    mods = _ada_modulation(c, c_ctx, ada_w, ada_b)
    for i in range(DEPTH):
        last = i == DEPTH - 1
        j = i // 2
        mod = mods[i, :B].reshape(B, 6, D)
        mod_c = jnp.broadcast_to(mods[i, B].reshape(1, 6, D), (B, 6, D))
        modtab = jnp.pad(jnp.stack([mod_c, mod], axis=1), ((0, 0), (0, 0), (0, MOD_ROWS - 6), (0, 0)))
        if i % 2 == 0:
            m = _attention_mixers(xs, modtab, att_w_in[j], na_rpb[j], qk_gain[j], n_ctx)
            w_out = _attention_w_out(mix_w_out[i])
        else:
            m = _recurrent_mixers(xs, modtab, rec_w_in[j], n_ctx, rwkv_mu[j], rwkv_w0[j], rwkv_w1[j], rwkv_w2[j],
                                  rwkv_a0[j], rwkv_a1[j], rwkv_a2[j], rwkv_g1[j], rwkv_g2[j], rwkv_kvec[j],
                                  rwkv_rk[j], rwkv_gn[j], mlstm_gate_b[j], mlstm_norm[j])
            w_out = mix_w_out[i]
        first_tile = n_ctx // TOKEN_TILE if last else 0
        x1, h2 = _out_proj(m, xs, modtab, w_out.astype(BF16), jnp.stack([ln_g[i, 0], ln_b[i, 0]]), first_tile)
        xs = _moe_block(h2, x1, modtab, jnp.stack([ln_g[i, 1], ln_b[i, 1]]), first_tile, moe_router[i], moe_bias[i],
                        moe_w1[i], moe_w3[i], moe_w2[i], shared_w1[i], shared_w3[i], shared_w2[i])
    return xs
```

```python
import functools

import jax
import jax.numpy as jnp
from jax import lax
from jax.experimental import pallas as pl
from jax.experimental.pallas import tpu as pltpu

D_MODEL = 1024
DEPTH = 4
GRID_W = 64
HEAD_DIM = 64
NA_HEADS = 8
NA_WIN_ROWS = 8
NA_WIN_COLS = 16
GQA_Q_HEADS = 8
GQA_KV_HEADS = 2
ROPE_THETA = 10000.0
ROPE_AXIS_DIM = HEAD_DIM // 2
NA_DIM = NA_HEADS * HEAD_DIM
GQA_Q_DIM = GQA_Q_HEADS * HEAD_DIM
GQA_KV_DIM = GQA_KV_HEADS * HEAD_DIM
RWKV_HEADS = 8
RWKV_HEAD = 64
RWKV_DIM = RWKV_HEADS * RWKV_HEAD
RWKV_GN_EPS = 64e-5
RWKV_IN = 4 * RWKV_DIM
MLSTM_HEADS = 4
MLSTM_HEAD = 128
MLSTM_DIM = MLSTM_HEADS * MLSTM_HEAD
MLSTM_CHUNK = 64
N_EXPERTS = 64
TOP_K = 6
ROUTED_SCALE = 2.5
DN_ALPHA = (2 * DEPTH) ** 0.25
LN_EPS = 1e-5
NORM_EPS = 1e-6
F32 = jnp.float32
BF16 = jnp.bfloat16

LANES = 128
SUBLANES = 8
VMEM_LIMIT = 48 * 1024 * 1024
TOKEN_TILE = 256
RWKV_TIME_BLOCK = 16
GQA_Q_TILE = 256
MOE_ROW_BLOCK = 512
MOE_SLOTS = 8
MOE_CHUNK = SUBLANES
MOE_STAGE_ROWS = TOKEN_TILE * TOP_K + N_EXPERTS * MOE_CHUNK
MOD_ROWS = 8


def _params(*semantics):
    return pltpu.CompilerParams(dimension_semantics=semantics, vmem_limit_bytes=VMEM_LIMIT)


def _mod_spec(tiles_per_batch, first_tile, flat):
    if flat:
        idx = lambda i, *_: (i // tiles_per_batch, jnp.minimum(i % tiles_per_batch + first_tile, 1), 0, 0)
    else:
        idx = lambda b, s, *_: (b, jnp.minimum(s + first_tile, 1), 0, 0)
    return pl.BlockSpec((1, 1, MOD_ROWS, D_MODEL), idx)


def _layer_norm_rows(z, g, b):
    mu = jnp.mean(z, axis=-1, keepdims=True)
    zc = z - mu
    var = jnp.mean(zc * zc, axis=-1, keepdims=True)
    return zc * lax.rsqrt(var + LN_EPS) * g + b


ADA_ROWS = 16


def _ada_body(c_ref, w_ref, b_ref, o_ref):
    x = c_ref[...]
    a = (x * jax.nn.sigmoid(x)).astype(BF16)
    o_ref[0] = jnp.dot(a, w_ref[0].astype(BF16), preferred_element_type=F32) + b_ref[0]


def _ada_modulation(c, c_ctx, ada_w, ada_b):
    B, D = c.shape
    depth, _, n = ada_w.shape
    cond = jnp.concatenate([c, c_ctx[None], jnp.zeros((ADA_ROWS - B - 1, D), F32)], 0)
    return pl.pallas_call(
        _ada_body,
        out_shape=jax.ShapeDtypeStruct((depth, ADA_ROWS, n), F32),
        grid=(depth, n // D),
        in_specs=[pl.BlockSpec((ADA_ROWS, D), lambda l, j: (0, 0)),
                  pl.BlockSpec((1, D, D), lambda l, j: (l, 0, j)),
                  pl.BlockSpec((1, 1, D), lambda l, j: (l, 0, j))],
        out_specs=pl.BlockSpec((1, ADA_ROWS, D), lambda l, j: (l, 0, j)),
        compiler_params=_params("arbitrary", "arbitrary"),
        name="ada_modulation",
    )(cond, ada_w, ada_b.reshape(depth, 1, n))
def _in_proj_body(x_ref, mod_ref, w_ref, *out_refs, splits):
    h = (x_ref[0] * (1.0 + mod_ref[0, 0, 1:2, :]) + mod_ref[0, 0, 0:1, :]).astype(BF16)
    for o_ref, (c0, c1) in zip(out_refs, splits):
        o_ref[0] = jnp.dot(h, w_ref[:, c0:c1], preferred_element_type=F32).astype(o_ref.dtype)


def _in_proj(x, modtab, w, splits, dtypes):
    B, T, D = x.shape
    tm = TOKEN_TILE
    outs = tuple(jax.ShapeDtypeStruct((B, T, c1 - c0), dt) for (c0, c1), dt in zip(splits, dtypes))
    return pl.pallas_call(
        functools.partial(_in_proj_body, splits=splits),
        out_shape=outs,
        grid=(B, T // tm),
        in_specs=[pl.BlockSpec((1, tm, D), lambda b, s: (b, s, 0)), _mod_spec(T // tm, 0, False),
                  pl.BlockSpec(w.shape, lambda b, s: (0, 0))],
        out_specs=tuple(pl.BlockSpec((1, tm, c1 - c0), lambda b, s: (b, s, 0)) for c0, c1 in splits),
        compiler_params=_params("arbitrary", "arbitrary"),
        name="in_proj",
    )(x, modtab, w)


def _out_proj_body(m_ref, x_ref, mod_ref, w_ref, ln_ref, x1_ref, h2_ref):
    y = jnp.dot(m_ref[0], w_ref[...], preferred_element_type=F32)
    mod = mod_ref[0, 0]
    x1 = _layer_norm_rows(DN_ALPHA * x_ref[0] + mod[2:3, :] * y, ln_ref[0:1, :], ln_ref[1:2, :])
    x1_ref[0] = x1
    h2_ref[0] = x1 * (1.0 + mod[4:5, :]) + mod[3:4, :]


def _out_proj(m, x, modtab, w, ln, first_tile):
    B, T, D = x.shape
    tm = TOKEN_TILE
    n_tiles = T // tm - first_tile
    rows = lambda b, s: (b, s + first_tile, 0)
    out_sds = jax.ShapeDtypeStruct((B, n_tiles * tm, D), F32)
    return pl.pallas_call(
        _out_proj_body,
        out_shape=(out_sds, out_sds),
        grid=(B, n_tiles),
        in_specs=[pl.BlockSpec((1, tm, m.shape[-1]), rows), pl.BlockSpec((1, tm, D), rows),
                  _mod_spec(T // tm, first_tile, False),
                  pl.BlockSpec(w.shape, lambda b, s: (0, 0)), pl.BlockSpec(ln.shape, lambda b, s: (0, 0))],
        out_specs=(pl.BlockSpec((1, tm, D), lambda b, s: (b, s, 0)),) * 2,
        compiler_params=_params("arbitrary", "arbitrary"),
        name="out_proj",
    )(m, x, modtab, w, ln)


def _rwkv_scan_body(rvf, wf, kf, nf, bf, rvb, wb, kb, nb, bb, yf_ref, yb_ref, state_ref, *, tc):
    @pl.when(pl.program_id(0) == 0)
    def _():
        state_ref[...] = jnp.zeros_like(state_ref)

    half = LANES // 2
    fwd_lane = lax.broadcasted_iota(jnp.int32, (RWKV_HEAD, LANES), 1) < half

    def step(j, carry):
        jb = tc - 1 - j

        def sel(f, b):
            return jnp.where(fwd_lane, f[j], b[jb])

        w, k, kkn, bv = sel(wf, wb), sel(kf, kb), sel(nf, nb), sel(bf, bb)
        rv_f, rv_b = rvf[j], rvb[jb]
        r = jnp.where(fwd_lane, rv_f, pltpu.roll(rv_b, half, axis=1))
        v = jnp.where(fwd_lane, pltpu.roll(rv_f, half, axis=1), rv_b)
        for vi in range(RWKV_HEAD):
            s = state_ref[vi]
            sa = jnp.sum(s * kkn, axis=0, keepdims=True)
            s2 = s * w + sa * bv + v[vi:vi + 1, :] * k
            state_ref[vi] = s2
            yrow = jnp.sum(s2 * r, axis=0, keepdims=True)
            yf_ref[j, pl.ds(vi, 1), :] = yrow
            yb_ref[jb, pl.ds(vi, 1), :] = yrow
        return carry

    lax.fori_loop(0, tc, step, 0)


def _rwkv_scan(xs, n_ctx):
    T = xs[0].shape[0]
    tc = RWKV_TIME_BLOCK
    assert T % tc == 0 and n_ctx % tc == 0
    nc, ncc = T // tc, n_ctx // tc
    blk = (tc, RWKV_HEAD, LANES)
    fwd = lambda c: (c, 0, 0)
    bwd = lambda c: (jnp.where(c < ncc, ncc - 1 - c, nc - 1 - (c - ncc)), 0, 0)
    out_sds = jax.ShapeDtypeStruct((T, RWKV_HEAD, LANES), F32)
    return pl.pallas_call(
        functools.partial(_rwkv_scan_body, tc=tc),
        out_shape=(out_sds, out_sds),
        grid=(nc,),
        in_specs=[pl.BlockSpec(blk, fwd)] * len(xs) + [pl.BlockSpec(blk, bwd)] * len(xs),
        out_specs=(pl.BlockSpec(blk, fwd), pl.BlockSpec(blk, bwd)),
        scratch_shapes=[pltpu.VMEM((RWKV_HEAD, RWKV_HEAD, LANES), F32)],
        compiler_params=_params("arbitrary"),
        name="rwkv_scan",
    )(*xs, *xs)


def _route_body(x_ref, rwt_ref, rb_ref, upper_ref, lower_ref, q_ref, g_ref, cnt_ref):
    tm = x_ref.shape[0]
    logits = lax.dot_general(rwt_ref[...], x_ref[...], (((1,), (1,)), ((), ())), preferred_element_type=F32,
                             precision=lax.Precision.HIGHEST)
    scores = jax.nn.sigmoid(logits)
    sel = scores + rb_ref[...]
    eidx = lax.broadcasted_iota(jnp.int32, (N_EXPERTS, tm), 0)
    slot = lax.broadcasted_iota(jnp.int32, (MOE_SLOTS, tm), 0)
    onehots = []
    for _ in range(TOP_K):
        m = jnp.max(sel, axis=0, keepdims=True)
        ij = jnp.min(jnp.where(sel == m, eidx, N_EXPERTS), axis=0, keepdims=True)
        oh = eidx == ij
        onehots.append(oh)
        sel = jnp.where(oh, -jnp.inf, sel)
    mask = functools.reduce(jnp.logical_or, onehots)
    maskf = jnp.where(mask, 1.0, 0.0)
    gsum = jnp.sum(jnp.where(mask, scores, 0.0), axis=0, keepdims=True)
    gates = scores / gsum * ROUTED_SCALE
    cnt = jnp.sum(maskf, axis=1, keepdims=True)
    cnt_pad = jnp.ceil(cnt * (1.0 / MOE_CHUNK)) * MOE_CHUNK
    lrank = jnp.dot(maskf.astype(BF16), upper_ref[...], preferred_element_type=F32)
    loff = jnp.dot(lower_ref[...], jnp.broadcast_to(cnt_pad, (N_EXPERTS, LANES)).astype(BF16),
                   preferred_element_type=F32)[:, 0:1]
    q = loff + lrank
    q8 = jnp.full((MOE_SLOTS, tm), -1.0, F32)
    g8 = jnp.zeros((MOE_SLOTS, tm), F32)
    for j, oh in enumerate(onehots):
        q8 = jnp.where(slot == j, jnp.sum(jnp.where(oh, q, 0.0), axis=0, keepdims=True), q8)
        g8 = jnp.where(slot == j, jnp.sum(jnp.where(oh, gates, 0.0), axis=0, keepdims=True), g8)
    q_ref[0] = q8.astype(jnp.int32)
    g_ref[0] = g8
    cnt_ref[0] = jnp.broadcast_to(cnt, (N_EXPERTS, LANES))


def _moe_route(tokens, router_w, router_b):
    n, d = tokens.shape
    tm = TOKEN_TILE
    nt = n // tm
    ar = jnp.arange(tm)
    upper = (ar[:, None] < ar[None, :]).astype(BF16)
    ae = jnp.arange(N_EXPERTS)
    lower = (ae[:, None] > ae[None, :]).astype(BF16)
    const = lambda shape: pl.BlockSpec(shape, lambda i: (0,) * len(shape))
    return pl.pallas_call(
        _route_body,
        out_shape=(jax.ShapeDtypeStruct((nt, MOE_SLOTS, tm), jnp.int32), jax.ShapeDtypeStruct((nt, MOE_SLOTS, tm), F32),
                   jax.ShapeDtypeStruct((nt, N_EXPERTS, LANES), F32)),
        grid=(nt,),
        in_specs=[pl.BlockSpec((tm, d), lambda i: (i, 0)), const((N_EXPERTS, d)), const((N_EXPERTS, 1)),
                  const((tm, tm)), const((N_EXPERTS, N_EXPERTS))],
        out_specs=(pl.BlockSpec((1, MOE_SLOTS, tm), lambda i: (i, 0, 0)),
                   pl.BlockSpec((1, MOE_SLOTS, tm), lambda i: (i, 0, 0)),
                   pl.BlockSpec((1, N_EXPERTS, LANES), lambda i: (i, 0, 0))),
        compiler_params=_params("arbitrary"),
        name="moe_router",
    )(tokens, router_w.T, router_b.reshape(N_EXPERTS, 1), upper, lower)


def _pack_bf16_pairs(x):
    m = x.shape[1] // 2
    bits = lambda a: lax.bitcast_convert_type(a.astype(BF16).astype(F32), jnp.uint32)
    return bits(x[:, :m]) | (bits(x[:, m:]) >> 16)


def _unpack_bf16_pairs(u):
    hi = lax.bitcast_convert_type(u & jnp.uint32(0xFFFF0000), F32)
    lo = lax.bitcast_convert_type(u << 16, F32)
    return jnp.concatenate([hi.astype(BF16), lo.astype(BF16)], axis=1)


def _chunk_loops(i, base_ref, nchunk_ref, loff_ref, copy):
    def per_expert(e, total):
        k = i * N_EXPERTS + e
        n, base, lo = nchunk_ref[k], base_ref[k], loff_ref[k]

        def piece(c, carry):
            copy(pl.multiple_of(lo + c * MOE_CHUNK, MOE_CHUNK), pl.multiple_of(base + c * MOE_CHUNK, MOE_CHUNK)).start()
            return carry

        lax.fori_loop(0, n, piece, 0)
        return total + n

    return lax.fori_loop(0, N_EXPERTS, per_expert, 0)


def _dispatch_body(base_ref, nchunk_ref, loff_ref, bv_ref, x_ref, q_ref, xs_hbm, stage_ref, zero_ref, sem_z, sem):
    i = pl.program_id(0)
    br = zero_ref.shape[0]
    n_blocks = xs_hbm.shape[0] // br

    @pl.when(i == 0)
    def _():
        zero_ref[...] = jnp.zeros_like(zero_ref)

        def fill_copy(b):
            return pltpu.make_async_copy(zero_ref, xs_hbm.at[pl.ds(b * br, br)], sem_z)

        def fill(b, carry):
            @pl.when(bv_ref[b] < br)
            def _():
                fill_copy(b).start()
            return carry

        def fill_wait(b, carry):
            @pl.when(bv_ref[b] < br)
            def _():
                fill_copy(b).wait()
            return carry

        lax.fori_loop(0, n_blocks, fill, 0)
        lax.fori_loop(0, n_blocks, fill_wait, 0)

    rows, tm = stage_ref.shape[0], x_ref.shape[0]
    q8 = q_ref[0]
    pos = lax.broadcasted_iota(jnp.int32, (rows, tm), 0)
    hit = functools.reduce(jnp.logical_or, [pos == q8[j:j + 1, :] for j in range(TOP_K)])
    perm = jnp.where(hit, 1.0, 0.0).astype(BF16)
    stage_ref[...] = _pack_bf16_pairs(jnp.dot(perm, x_ref[...].astype(BF16), preferred_element_type=F32))

    def copy(local_row, global_row):
        return pltpu.make_async_copy(stage_ref.at[pl.ds(local_row, MOE_CHUNK)], xs_hbm.at[pl.ds(global_row, MOE_CHUNK)], sem)

    total = _chunk_loops(i, base_ref, nchunk_ref, loff_ref, copy)

    def drain(c, carry):
        copy(0, 0).wait()
        return carry

    lax.fori_loop(0, total, drain, 0)


def _moe_dispatch(tokens, q_rows, dest_base, nchunk, loff, block_valid):
    n, d = tokens.shape
    tm, br = TOKEN_TILE, MOE_ROW_BLOCK
    n_rows = block_valid.shape[0] * br
    return pl.pallas_call(
        _dispatch_body,
        out_shape=jax.ShapeDtypeStruct((n_rows, d // 2), jnp.uint32),
        grid_spec=pltpu.PrefetchScalarGridSpec(
            num_scalar_prefetch=4,
            grid=(n // tm,),
            in_specs=[pl.BlockSpec((tm, d), lambda i, *_: (i, 0)),
                      pl.BlockSpec((1, MOE_SLOTS, tm), lambda i, *_: (i, 0, 0))],
            out_specs=pl.BlockSpec(memory_space=pl.ANY),
            scratch_shapes=[pltpu.VMEM((MOE_STAGE_ROWS, d // 2), jnp.uint32), pltpu.VMEM((br, d // 2), jnp.uint32),
                            pltpu.SemaphoreType.DMA, pltpu.SemaphoreType.DMA]),
        compiler_params=_params("arbitrary"),
        name="moe_dispatch",
    )(dest_base, nchunk, loff, block_valid, tokens, q_rows)


def _expert_body(be_ref, bv_ref, x_ref, w1_ref, w3_ref, w2_ref, y_ref, w1b, w3b, w2b):
    i = pl.program_id(0)
    valid = bv_ref[i]

    @pl.when(jnp.logical_or(i == 0, be_ref[i] != be_ref[jnp.maximum(i - 1, 0)]))
    def _():
        w1b[...] = w1_ref[0, 0].astype(BF16)
        w3b[...] = w3_ref[0, 0].astype(BF16)
        w2b[...] = w2_ref[0, 0].astype(BF16)

    @pl.when(valid > 0)
    def _():
        x = _unpack_bf16_pairs(x_ref[...])
        h1 = jnp.dot(x, w1b[...], preferred_element_type=F32)
        h3 = jnp.dot(x, w3b[...], preferred_element_type=F32)
        a = (h1 * jax.nn.sigmoid(h1) * h3).astype(BF16)
        y_ref[...] = _pack_bf16_pairs(jnp.dot(a, w2b[...], preferred_element_type=F32))

    @pl.when(valid <= 0)
    def _():
        y_ref[...] = jnp.zeros_like(y_ref)


def _moe_experts(xs, block_expert, block_valid, layer, w1, w3, w2):
    n_rows, dp = xs.shape
    br = MOE_ROW_BLOCK
    d, ff = w1.shape[-2:]
    return pl.pallas_call(
        _expert_body,
        out_shape=jax.ShapeDtypeStruct((n_rows, dp), jnp.uint32),
        grid_spec=pltpu.PrefetchScalarGridSpec(
            num_scalar_prefetch=2,
            grid=(n_rows // br,),
            in_specs=[pl.BlockSpec((br, dp), lambda i, be, bv: (i, 0)),
                      pl.BlockSpec((1, 1, d, ff), lambda i, be, bv: (layer, be[i], 0, 0)),
                      pl.BlockSpec((1, 1, d, ff), lambda i, be, bv: (layer, be[i], 0, 0)),
                      pl.BlockSpec((1, 1, ff, d), lambda i, be, bv: (layer, be[i], 0, 0))],
            out_specs=pl.BlockSpec((br, dp), lambda i, be, bv: (i, 0)),
            scratch_shapes=[pltpu.VMEM((d, ff), BF16), pltpu.VMEM((d, ff), BF16), pltpu.VMEM((ff, d), BF16)]),
        compiler_params=_params("arbitrary"),
        name="moe_experts",
    )(block_expert, block_valid, xs, w1, w3, w2)


def _combine_body(base_ref, nchunk_ref, loff_ref, h_ref, q_ref, g_ref, ys_hbm, sw1_ref, sw3_ref, sw2_ref,
                  x1_ref, mod_ref, ln_ref, out_ref, stage_ref, sem):
    i = pl.program_id(0)

    @pl.when(i == 0)
    def _():
        stage_ref[...] = jnp.zeros_like(stage_ref)

    def copy(local_row, global_row):
        return pltpu.make_async_copy(ys_hbm.at[pl.ds(global_row, MOE_CHUNK)], stage_ref.at[pl.ds(local_row, MOE_CHUNK)], sem)

    total = _chunk_loops(i, base_ref, nchunk_ref, loff_ref, copy)

    h = h_ref[...].astype(BF16)
    h1 = jnp.dot(h, sw1_ref[...], preferred_element_type=F32)
    h3 = jnp.dot(h, sw3_ref[...], preferred_element_type=F32)
    y = jnp.dot((h1 * jax.nn.sigmoid(h1) * h3).astype(BF16), sw2_ref[...], preferred_element_type=F32)
    tm, rows = h_ref.shape[0], stage_ref.shape[0]
    q8, g8 = q_ref[0], g_ref[0]
    pos = lax.broadcasted_iota(jnp.int32, (tm, rows), 1)
    gate_mat = jnp.zeros((tm, rows), F32)
    for j in range(TOP_K):
        gate_mat = jnp.where(pos == q8[:, j:j + 1], g8[:, j:j + 1], gate_mat)

    def drain(c, carry):
        copy(0, 0).wait()
        return carry

    lax.fori_loop(0, total, drain, 0)
    y = y + jnp.dot(gate_mat.astype(BF16), _unpack_bf16_pairs(stage_ref[...]), preferred_element_type=F32)
    out_ref[...] = _layer_norm_rows(DN_ALPHA * x1_ref[...] + mod_ref[0, 0, 5:6, :] * y, ln_ref[0:1, :], ln_ref[1:2, :])


def _moe_combine(tokens, q_cols, g_cols, dest_base, nchunk, loff, ys, sw1, sw3, sw2, x1, modtab, ln, tiles_per_batch,
                 first_tile):
    n, d = tokens.shape
    tm = TOKEN_TILE
    ff = sw1.shape[-1]
    whole = lambda shape: pl.BlockSpec(shape, lambda i, *_: (0,) * len(shape))
    rows = pl.BlockSpec((tm, d), lambda i, *_: (i, 0))
    slots = pl.BlockSpec((1, tm, MOE_SLOTS), lambda i, *_: (i, 0, 0))
    return pl.pallas_call(
        _combine_body,
        out_shape=jax.ShapeDtypeStruct((n, d), F32),
        grid_spec=pltpu.PrefetchScalarGridSpec(
            num_scalar_prefetch=3,
            grid=(n // tm,),
            in_specs=[rows, slots, slots, pl.BlockSpec(memory_space=pl.ANY),
                      whole((d, ff)), whole((d, ff)), whole((ff, d)),
                      rows, _mod_spec(tiles_per_batch, first_tile, True), whole(ln.shape)],
            out_specs=rows,
            scratch_shapes=[pltpu.VMEM((MOE_STAGE_ROWS, d // 2), jnp.uint32), pltpu.SemaphoreType.DMA]),
        compiler_params=_params("arbitrary"),
        name="moe_combine",
    )(dest_base, nchunk, loff, tokens, q_cols, g_cols, ys, sw1, sw3, sw2, x1, modtab, ln)


def _moe_block(h2, x1, modtab, ln, first_tile, layer, router_w, router_b, w1, w3, w2, sw1, sw3, sw2):
    B, Tp, d = h2.shape
    tokens = h2.reshape(B * Tp, d)
    n = B * Tp
    tm, br = TOKEN_TILE, MOE_ROW_BLOCK
    q_rows, g_rows, cnt = _moe_route(tokens, router_w, router_b)
    cnt = cnt[:, :, 0].astype(jnp.int32)
    nchunk = (cnt + MOE_CHUNK - 1) // MOE_CHUNK
    run = nchunk * MOE_CHUNK
    total = jnp.sum(run, axis=0)
    padded = (total + br - 1) // br * br
    p_end = jnp.cumsum(padded)
    offs = p_end - padded
    dest_base = (offs[None, :] + jnp.cumsum(run, axis=0) - run).reshape(-1).astype(jnp.int32)
    loff = jnp.cumsum(run, axis=1) - run
    nchunk, loff = nchunk.reshape(-1).astype(jnp.int32), loff.reshape(-1).astype(jnp.int32)
    n_blocks = -(-(n * TOP_K + (n // tm) * N_EXPERTS * (MOE_CHUNK - 1) + N_EXPERTS * (br - 1)) // br)
    blk_start = jnp.arange(n_blocks, dtype=jnp.int32) * br
    block_expert = jnp.minimum(jnp.sum(blk_start[:, None] >= p_end[None, :], axis=1), N_EXPERTS - 1).astype(jnp.int32)
    block_valid = jnp.clip(total[block_expert] - (blk_start - offs[block_expert]), 0, br).astype(jnp.int32)
    xs = _moe_dispatch(tokens, q_rows, dest_base, nchunk, loff, block_valid)
    ys = _moe_experts(xs, block_expert, block_valid, layer, w1, w3, w2)
    q_cols, g_cols = jnp.swapaxes(q_rows, 1, 2), jnp.swapaxes(g_rows, 1, 2)
    out = _moe_combine(tokens, q_cols, g_cols, dest_base, nchunk, loff, ys,
                       sw1.astype(BF16), sw3.astype(BF16), sw2.astype(BF16),
                       x1.reshape(n, d), modtab, ln, Tp // tm, first_tile)
    return out.reshape(B, Tp, d)


def _low_half():
    return lax.broadcasted_iota(jnp.int32, (1, LANES), 1) < LANES // 2


def _pair_attention(q, parts):
    low = _low_half()
    outs = []
    for use_low in (True, False):
        qm = jnp.where(low == use_low, q, jnp.zeros_like(q))
        scores = []
        for k, _, b_lo, b_hi in parts:
            s = lax.dot_general(qm, k, (((1,), (1,)), ((), ())), preferred_element_type=F32) * HEAD_DIM ** -0.5
            b = b_lo if use_low else b_hi
            scores.append(s if b is None else s + b)
        m = functools.reduce(jnp.maximum, [jnp.max(s, axis=-1, keepdims=True) for s in scores])
        den = 0.0
        num = 0.0
        for s, (_, v, _, _) in zip(scores, parts):
            p = jnp.exp(s - m)
            den = den + jnp.sum(p, axis=-1, keepdims=True)
            num = num + jnp.dot(p.astype(BF16), v, preferred_element_type=F32)
        outs.append(num / den)
    return jnp.where(low, outs[0], outs[1])


NA_Q_ROWS = 2
NA_BAND_ROWS = NA_WIN_ROWS + NA_Q_ROWS - 1


def _na_body(cls_ref, q_ref, k_ref, v_ref, bias_ref, o_ref, *, n_ctx, rows):
    s = pl.program_id(1)
    tq = q_ref.shape[1]
    ctx_steps = n_ctx // tq
    n_tiles = NA_DIM // LANES
    tile = lambda t: slice(t * LANES, (t + 1) * LANES)

    @pl.when(s < ctx_steps)
    def _():
        for t in range(n_tiles):
            part = (k_ref[0, 0:n_ctx, tile(t)], v_ref[0, 0:n_ctx, tile(t)], None, None)
            o_ref[0, :, tile(t)] = _pair_attention(q_ref[0, :, tile(t)], [part]).astype(o_ref.dtype)

    @pl.when(s >= ctx_steps)
    def _():
        first = (s - ctx_steps) * NA_Q_ROWS
        start = jnp.clip(first - NA_WIN_ROWS // 2, 0, rows - NA_BAND_ROWS)
        off = pl.multiple_of(n_ctx + start * GRID_W, GRID_W)
        band = pl.ds(off, NA_BAND_ROWS * GRID_W)
        for t in range(n_tiles):
            parts = [(k_ref[0, band, tile(t)], v_ref[0, band, tile(t)], bias_ref[0, 2 * t], bias_ref[0, 2 * t + 1]),
                     (k_ref[0, 0:n_ctx, tile(t)], v_ref[0, 0:n_ctx, tile(t)], None, None)]
            o_ref[0, :, tile(t)] = _pair_attention(q_ref[0, :, tile(t)], parts).astype(o_ref.dtype)


def _na_bias_table(rpb, rows):
    import numpy as np
    kc, W = NA_WIN_COLS, GRID_W
    cidx = np.arange(W)
    col_start = np.clip(cidx - kc // 2, 0, W - kc)
    col_in = (cidx[None, :] >= col_start[:, None]) & (cidx[None, :] < col_start[:, None] + kc)
    d_col = np.clip(cidx[None, :] - cidx[:, None], -(kc - 1), kc - 1) + kc - 1
    classes, class_of, d_rows, allowed = {}, [], [], []
    for step in range(rows // NA_Q_ROWS):
        first = step * NA_Q_ROWS
        band0 = min(max(first - NA_WIN_ROWS // 2, 0), rows - NA_BAND_ROWS)
        d_row = np.zeros((NA_Q_ROWS, NA_BAND_ROWS), np.int32)
        ok = np.zeros((NA_Q_ROWS, NA_BAND_ROWS), bool)
        for p in range(NA_Q_ROWS):
            win0 = min(max(first + p - NA_WIN_ROWS // 2, 0), rows - NA_WIN_ROWS)
            for j in range(NA_BAND_ROWS):
                ok[p, j] = win0 <= band0 + j < win0 + NA_WIN_ROWS
                d_row[p, j] = min(max(band0 + j - (first + p) + NA_WIN_ROWS - 1, 0), 2 * NA_WIN_ROWS - 2)
        key = (d_row.tobytes(), ok.tobytes())
        if key not in classes:
            classes[key] = len(classes)
            d_rows.append(d_row)
            allowed.append(ok)
        class_of.append(classes[key])
    d_rows, allowed = np.stack(d_rows), np.stack(allowed)
    tab = rpb[:, d_rows][..., d_col]
    mask = allowed[None, :, :, :, None, None] & col_in[None, None, None, None]
    tab = jnp.where(mask, tab, -jnp.inf)
    tab = jnp.transpose(tab, (1, 0, 2, 4, 3, 5))
    return tab.reshape(len(classes), NA_HEADS, NA_Q_ROWS * W, NA_BAND_ROWS * W), class_of


def _na_attention(pa, rpb, n_ctx):
    B, T, _ = pa.shape
    rows = (T - n_ctx) // GRID_W
    tq = NA_Q_ROWS * GRID_W
    assert rows % NA_Q_ROWS == 0 and rows >= NA_BAND_ROWS and n_ctx % tq == 0
    table, class_of = _na_bias_table(rpb, rows)
    step_class = jnp.array([0] * (n_ctx // tq) + class_of, jnp.int32)
    return pl.pallas_call(
        functools.partial(_na_body, n_ctx=n_ctx, rows=rows),
        out_shape=jax.ShapeDtypeStruct((B, T, NA_DIM), BF16),
        grid_spec=pltpu.PrefetchScalarGridSpec(
            num_scalar_prefetch=1,
            grid=(B, T // tq),
            in_specs=[pl.BlockSpec((1, tq, NA_DIM), lambda b, s, cls: (b, s, 0)),
                      pl.BlockSpec((1, T, NA_DIM), lambda b, s, cls: (b, 0, 1)),
                      pl.BlockSpec((1, T, NA_DIM), lambda b, s, cls: (b, 0, 2)),
                      pl.BlockSpec((1,) + table.shape[1:], lambda b, s, cls: (cls[s], 0, 0, 0))],
            out_specs=pl.BlockSpec((1, tq, NA_DIM), lambda b, s, cls: (b, s, 0))),
        compiler_params=_params("arbitrary", "arbitrary"),
        name="na_attention",
    )(step_class, pa, pa, pa, table)


def _rms_pair(x, gain):
    low = _low_half()
    sq = x * x
    s_lo = jnp.sum(jnp.where(low, sq, 0.0), axis=-1, keepdims=True)
    s_hi = jnp.sum(jnp.where(low, 0.0, sq), axis=-1, keepdims=True)
    ms = jnp.where(low, s_lo, s_hi) * (1.0 / HEAD_DIM)
    return x * lax.rsqrt(ms + NORM_EPS) * gain


def _rope_pair(x, cos, sin_signed):
    even = lax.broadcasted_iota(jnp.int32, (1, LANES), 1) % 2 == 0
    partner = jnp.where(even, pltpu.roll(x, LANES - 1, axis=1), pltpu.roll(x, 1, axis=1))
    return x * cos + partner * sin_signed


def _gqa_body(q_ref, k_ref, v_ref, cos_q, sin_q, cos_k, sin_k, gain_ref, o_ref, kn_ref, vn_ref, *, n_ctx):
    s = pl.program_id(1)
    tq = q_ref.shape[1]

    @pl.when(s == 0)
    def _():
        kn_ref[...] = _rope_pair(_rms_pair(k_ref[0], gain_ref[1:2, :]), cos_k[...], sin_k[...]).astype(BF16)
        vn_ref[...] = v_ref[0].astype(BF16)

    def run(n_keys):
        k, v = kn_ref[0:n_keys, :], vn_ref[0:n_keys, :]
        for t in range(GQA_Q_DIM // LANES):
            q = q_ref[0, :, t * LANES:(t + 1) * LANES]
            qn = _rope_pair(_rms_pair(q, gain_ref[0:1, :]), cos_q[...], sin_q[...]).astype(BF16)
            o_ref[0, :, t * LANES:(t + 1) * LANES] = _pair_attention(qn, [(k, v, None, None)]).astype(o_ref.dtype)

    @pl.when(s < n_ctx // tq)
    def _():
        run(n_ctx)

    @pl.when(s >= n_ctx // tq)
    def _():
        run(kn_ref.shape[0])


GQA_HEAD_ORDER = (0, 4, 1, 5, 2, 6, 3, 7)


def _axial_rope(n_tokens):
    t = jnp.arange(n_tokens)
    row = (t // GRID_W).astype(F32)
    col = (t % GRID_W).astype(F32)
    inv = ROPE_THETA ** (-jnp.arange(0, ROPE_AXIS_DIM, 2, dtype=F32) / ROPE_AXIS_DIM)
    ang = jnp.concatenate([row[:, None] * inv, col[:, None] * inv], -1)
    return jnp.cos(ang), jnp.sin(ang)


def _gqa_rope_tables(T, n_ctx):
    cos, sin = _axial_rope(T - n_ctx)
    cos = jnp.concatenate([jnp.ones((n_ctx, ROPE_AXIS_DIM), F32), cos], 0)
    sin = jnp.concatenate([jnp.zeros((n_ctx, ROPE_AXIS_DIM), F32), sin], 0)
    cos = jnp.tile(jnp.repeat(cos, 2, axis=-1), (1, 2))
    sign = jnp.tile(jnp.array([-1.0, 1.0], F32), LANES // 2)
    sin = jnp.tile(jnp.repeat(sin, 2, axis=-1), (1, 2)) * sign
    return cos, sin


def _gqa_attention(pb, qk_gain, n_ctx):
    B, T, _ = pb.shape
    tq = GQA_Q_TILE
    cos, sin = _gqa_rope_tables(T, n_ctx)
    gain = jnp.tile(qk_gain, (1, 2))
    kv_blk = GQA_Q_DIM // GQA_KV_DIM
    return pl.pallas_call(
        functools.partial(_gqa_body, n_ctx=n_ctx),
        out_shape=jax.ShapeDtypeStruct((B, T, GQA_Q_DIM), BF16),
        grid=(B, T // tq),
        in_specs=[pl.BlockSpec((1, tq, GQA_Q_DIM), lambda b, s: (b, s, 0)),
                  pl.BlockSpec((1, T, GQA_KV_DIM), lambda b, s: (b, 0, kv_blk)),
                  pl.BlockSpec((1, T, GQA_KV_DIM), lambda b, s: (b, 0, kv_blk + 1)),
                  pl.BlockSpec((tq, LANES), lambda b, s: (s, 0)),
                  pl.BlockSpec((tq, LANES), lambda b, s: (s, 0)),
                  pl.BlockSpec((T, LANES), lambda b, s: (0, 0)),
                  pl.BlockSpec((T, LANES), lambda b, s: (0, 0)),
                  pl.BlockSpec((2, LANES), lambda b, s: (0, 0))],
        out_specs=pl.BlockSpec((1, tq, GQA_Q_DIM), lambda b, s: (b, s, 0)),
        scratch_shapes=[pltpu.VMEM((T, GQA_KV_DIM), BF16), pltpu.VMEM((T, GQA_KV_DIM), BF16)],
        compiler_params=_params("arbitrary", "arbitrary"),
        name="gqa_attention",
    )(pb, pb, pb, cos, sin, cos, sin, gain)


def _attention_mixers(x, modtab, w_in, rpb, qk_gain, n_ctx):
    order = jnp.array(GQA_HEAD_ORDER)
    qb_cols = 3 * NA_DIM + (order[:, None] * HEAD_DIM + jnp.arange(HEAD_DIM)[None, :]).reshape(-1)
    cols = jnp.concatenate([jnp.arange(3 * NA_DIM), qb_cols, jnp.arange(3 * NA_DIM + GQA_Q_DIM, w_in.shape[1])])
    w = w_in[:, cols].astype(BF16)
    pa, pb = _in_proj(x, modtab, w, ((0, 3 * NA_DIM), (3 * NA_DIM, w.shape[1])), (BF16, F32))
    return jnp.concatenate([_na_attention(pa, rpb, n_ctx), _gqa_attention(pb, qk_gain, n_ctx)], -1)


def _attention_w_out(w_out):
    order = jnp.array(GQA_HEAD_ORDER)
    rows = NA_DIM + (order[:, None] * HEAD_DIM + jnp.arange(HEAD_DIM)[None, :]).reshape(-1)
    return jnp.concatenate([w_out[:NA_DIM], w_out[rows]], 0)


def _log_sigmoid(x):
    return jnp.minimum(x, 0.0) - jnp.log(1.0 + jnp.exp(-jnp.abs(x)))


def _mlstm_body(qf, kf, vf, gcf, grf, qb, kb, vb, gcb, grb, bias_c, bias_r, hf_ref, hb_ref, c_ref, n_ref, m_ref):
    @pl.when(pl.program_id(1) == 0)
    def _():
        c_ref[...] = jnp.zeros_like(c_ref)
        n_ref[...] = jnp.zeros_like(n_ref)
        m_ref[...] = jnp.zeros_like(m_ref)

    L = MLSTM_CHUNK
    row = lax.broadcasted_iota(jnp.int32, (L, L), 0)
    col = lax.broadcasted_iota(jnp.int32, (L, L), 1)
    hi = lax.Precision.HIGHEST
    writes = []
    for d, (q_ref, k_ref, v_ref, gc_ref, gr_ref, h_ref) in enumerate(((qf, kf, vf, gcf, grf, hf_ref),
                                                                     (qb, kb, vb, gcb, grb, hb_ref))):
        seen = (col <= row) if d == 0 else (col >= row)
        seen_f = jnp.where(seen, 1.0, 0.0)
        g_col = gc_ref[0] + bias_c[...]
        g_row = gr_ref[0, 0] + bias_r[...]
        b_col = jnp.dot(seen_f, _log_sigmoid(g_col), preferred_element_type=F32, precision=hi)
        lf_row = _log_sigmoid(g_row)
        b_row = lax.dot_general(lf_row, seen_f, (((1,), (1,)), ((), ())), preferred_element_type=F32, precision=hi)
        b_end = jnp.sum(lf_row, axis=1, keepdims=True)
        for h in range(MLSTM_HEADS):
            gi, gf = d * 2 * MLSTM_HEADS + h, d * 2 * MLSTM_HEADS + MLSTM_HEADS + h
            sl = slice(h * MLSTM_HEAD, (h + 1) * MLSTM_HEAD)
            s_idx = d * MLSTM_HEADS + h
            c_state, n_state, m_state = c_ref[s_idx], n_ref[s_idx:s_idx + 1, :], m_ref[s_idx:s_idx + 1, 0:1]
            bc, br, ig_c, ig_r = b_col[:, gf:gf + 1], b_row[gf:gf + 1, :], g_col[:, gi:gi + 1], g_row[gi:gi + 1, :]
            be = b_end[gf:gf + 1, :]
            d_intra = jnp.where(seen, bc - br + ig_r, -jnp.inf)
            d_inter = bc + m_state
            m_t = jnp.maximum(d_inter, jnp.max(d_intra, axis=1, keepdims=True))
            q = (q_ref[0, :, sl] * MLSTM_HEAD ** -0.5).astype(BF16)
            k, v = k_ref[0, :, sl], v_ref[0, :, sl]
            kb16 = k.astype(BF16)
            s = lax.dot_general(q, kb16, (((1,), (1,)), ((), ())), preferred_element_type=F32) * jnp.exp(d_intra - m_t)
            w_inter = jnp.exp(d_inter - m_t)
            qc = lax.dot_general(q, c_state.astype(BF16), (((1,), (1,)), ((), ())), preferred_element_type=F32)
            num = jnp.dot(s.astype(BF16), v.astype(BF16), preferred_element_type=F32) + w_inter * qc
            qn = jnp.sum(q.astype(F32) * n_state, axis=1, keepdims=True)
            den = jnp.sum(s, axis=1, keepdims=True) + w_inter * qn
            writes.append((h_ref.at[0, :, sl], num / jnp.maximum(jnp.abs(den), jnp.exp(-m_t))))
            d_state = be - bc + ig_c
            m_new = jnp.maximum(be + m_state, jnp.max(d_state, axis=0, keepdims=True))
            w_s = jnp.exp(d_state - m_new)
            w_c = jnp.exp(be + m_state - m_new)
            vk = lax.dot_general((v * w_s).astype(BF16), kb16, (((0,), (0,)), ((), ())), preferred_element_type=F32)
            writes.append((c_ref.at[s_idx], w_c * c_state + vk))
            writes.append((n_ref.at[s_idx:s_idx + 1, :], w_c * n_state + jnp.sum(w_s * k, axis=0, keepdims=True)))
            writes.append((m_ref.at[s_idx:s_idx + 1, :], jnp.broadcast_to(m_new, (1, LANES))))
    for ref, value in writes:
        ref[...] = value


def _mlstm(pm, gates, gate_b, n_ctx):
    B, T, _ = pm.shape
    L = MLSTM_CHUNK
    nc, ncc = T // L, n_ctx // L
    ng = 4 * MLSTM_HEADS
    g_cols = jnp.pad(gates, ((0, 0), (0, 0), (0, LANES - ng)))
    g_rows = jnp.swapaxes(gates.reshape(B, nc, L, ng), 2, 3)
    bias = gate_b.reshape(ng)
    bias_c = jnp.pad(bias, (0, LANES - ng)).reshape(1, LANES)
    bias_r = bias.reshape(ng, 1)
    fwd = lambda c: c
    bwd = lambda c: jnp.where(c < ncc, ncc - 1 - c, nc - 1 - (c - ncc))
    blk = (1, L, MLSTM_DIM)
    seq = lambda order, j: pl.BlockSpec(blk, lambda b, c: (b, order(c), j))
    gcol = lambda order: pl.BlockSpec((1, L, LANES), lambda b, c: (b, order(c), 0))
    grow = lambda order: pl.BlockSpec((1, 1, ng, L), lambda b, c: (b, order(c), 0, 0))
    n_state = 2 * MLSTM_HEADS
    out_sds = jax.ShapeDtypeStruct((B, T, MLSTM_DIM), F32)
    return pl.pallas_call(
        _mlstm_body,
        out_shape=(out_sds, out_sds),
        grid=(B, nc),
        in_specs=[seq(fwd, 0), seq(fwd, 1), seq(fwd, 2), gcol(fwd), grow(fwd),
                  seq(bwd, 0), seq(bwd, 1), seq(bwd, 2), gcol(bwd), grow(bwd),
                  pl.BlockSpec((1, LANES), lambda b, c: (0, 0)), pl.BlockSpec((ng, 1), lambda b, c: (0, 0))],
        out_specs=(pl.BlockSpec(blk, lambda b, c: (b, fwd(c), 0)), pl.BlockSpec(blk, lambda b, c: (b, bwd(c), 0))),
        scratch_shapes=[pltpu.VMEM((n_state, MLSTM_HEAD, MLSTM_HEAD), F32), pltpu.VMEM((n_state, MLSTM_HEAD), F32),
                        pltpu.VMEM((n_state, LANES), F32)],
        compiler_params=_params("arbitrary", "arbitrary"),
        name="mlstm",
    )(pm, pm, pm, g_cols, g_rows, pm, pm, pm, g_cols, g_rows, bias_c, bias_r)


def _pair_sums(x):
    low = _low_half()
    s_lo = jnp.sum(jnp.where(low, x, 0.0), axis=-1, keepdims=True)
    s_hi = jnp.sum(jnp.where(low, 0.0, x), axis=-1, keepdims=True)
    return jnp.where(low, s_lo, s_hi)


def _rwkv_prep_body(p_ref, prev_ref, next_ref, tab_ref, tab2_ref, w1_ref, w2_ref, a1_ref, a2_ref, g1_ref, g2_ref,
                    rv_ref, n_ref, w_ref, k_ref, b_ref, gate_ref, *, n_ctx):
    s = pl.program_id(1)
    tm = p_ref.shape[1]
    C = RWKV_DIM
    ctx_tiles = n_ctx // tm
    x = p_ref[0]
    has_prev = jnp.logical_and(s != 0, s != ctx_tiles)
    has_next = jnp.logical_and(s != ctx_tiles - 1, s != pl.num_programs(1) - 1)
    prev_row = jnp.where(has_prev, prev_ref[0, SUBLANES - 1:SUBLANES, :], 0.0)
    next_row = jnp.where(has_next, next_ref[0, 0:1, :], 0.0)
    rowid = lax.broadcasted_iota(jnp.int32, (tm, 1), 0)
    up = jnp.where(rowid == 0, prev_row, pltpu.roll(x, 1, axis=0))
    dn = jnp.where(rowid == tm - 1, next_row, pltpu.roll(x, tm - 1, axis=0))
    d = 0.5 * (up + dn) - x
    part = lambda a, i: a[:, i * C:(i + 1) * C]
    mu = lambda i: tab_ref[i:i + 1, :]
    r = part(x, 0) + part(d, 0) * mu(0)
    k = part(x, 1) + part(d, 1) * mu(1)
    v = part(x, 2) + part(d, 2) * mu(2)
    z, dz = part(x, 3), part(d, 3)
    z_w, z_a, z_g = (z + dz * mu(3)).astype(BF16), (z + dz * mu(4)).astype(BF16), (z + dz * mu(5)).astype(BF16)
    lora = lambda t, w: jnp.dot(t.astype(BF16), w[...], preferred_element_type=F32)
    w_pre = tab2_ref[0:1, :] + lora(jnp.tanh(lora(z_w, w1_ref)), w2_ref)
    neg = -w_pre
    softplus = jnp.maximum(neg, 0.0) + jnp.log(1.0 + jnp.exp(-jnp.abs(neg)))
    decay = jnp.exp(-jnp.exp(-softplus - 0.5))
    iclr = jax.nn.sigmoid(tab2_ref[1:2, :] + lora(lora(z_a, a1_ref), a2_ref))
    gate_ref[0] = lora(jax.nn.sigmoid(lora(z_g, g1_ref)), g2_ref)
    kk = k * tab_ref[6:7, :]
    kk = jnp.concatenate(
        [kk[:, t * LANES:(t + 1) * LANES]
         * lax.rsqrt(jnp.maximum(_pair_sums(jnp.square(kk[:, t * LANES:(t + 1) * LANES])), 1e-24))
         for t in range(C // LANES)], axis=1)
    k2, kk2 = jnp.concatenate([k, k], axis=1), jnp.concatenate([kk, kk], axis=1)
    rv_ref[0] = jnp.concatenate([r, v], axis=1)
    n_ref[0] = -kk2
    w_ref[0] = decay
    k_ref[0] = k2 * (1.0 + (iclr - 1.0) * tab2_ref[2:3, :])
    b_ref[0] = kk2 * iclr


def _rwkv_prep(pr, n_ctx, mu, w0, w1, w2, a0, a1, a2, g1, g2, kvec):
    B, T, _ = pr.shape
    tm, C = TOKEN_TILE, RWKV_DIM
    per_tile = tm // SUBLANES
    tab = jnp.concatenate([mu, kvec[0:1], jnp.zeros((1, C), F32)], 0)
    cat = lambda a: jnp.concatenate([a[0], a[1]], -1)
    tab2 = jnp.concatenate([cat(w0)[None], cat(a0)[None], jnp.tile(kvec[1], 2)[None], jnp.zeros((5, 2 * C), F32)], 0)
    blockdiag = lambda a: jnp.concatenate([jnp.pad(a[0], ((0, 0), (0, C))), jnp.pad(a[1], ((0, 0), (C, 0)))], 0)
    gl = g1.shape[1]
    consts = (tab, tab2, cat(w1).astype(BF16), blockdiag(w2).astype(BF16), cat(a1).astype(BF16),
              blockdiag(a2).astype(BF16), jnp.pad(g1, ((0, 0), (0, LANES - gl))).astype(BF16),
              jnp.pad(g2, ((0, LANES - gl), (0, 0))).astype(BF16))
    const = lambda a: pl.BlockSpec(a.shape, lambda b, s: (0, 0))
    one = jax.ShapeDtypeStruct((B, T, C), F32)
    two = jax.ShapeDtypeStruct((B, T, 2 * C), F32)
    spec1 = pl.BlockSpec((1, tm, C), lambda b, s: (b, s, 0))
    spec2 = pl.BlockSpec((1, tm, 2 * C), lambda b, s: (b, s, 0))
    return pl.pallas_call(
        functools.partial(_rwkv_prep_body, n_ctx=n_ctx),
        out_shape=(two, two, two, two, two, one),
        grid=(B, T // tm),
        in_specs=[pl.BlockSpec((1, tm, 4 * C), lambda b, s: (b, s, 0)),
                  pl.BlockSpec((1, SUBLANES, 4 * C), lambda b, s: (b, jnp.maximum(s * per_tile - 1, 0), 0)),
                  pl.BlockSpec((1, SUBLANES, 4 * C), lambda b, s: (b, jnp.minimum((s + 1) * per_tile, T // SUBLANES - 1), 0))]
        + [const(a) for a in consts],
        out_specs=(spec2, spec2, spec2, spec2, spec2, spec1),
        compiler_params=_params("arbitrary", "arbitrary"),
        name="rwkv_prep",
    )(pr, pr, pr, *consts)


def _rec_post_body(y_ref, r_ref, v_ref, k_ref, gate_ref, hf_ref, hb_ref, o_ref, tab_ref, out_ref):
    C = RWKV_DIM
    for t in range(C // LANES):
        sl = slice(t * LANES, (t + 1) * LANES)
        y = y_ref[0, :, sl]
        yc = y - _pair_sums(y) * (1.0 / RWKV_HEAD)
        var = _pair_sums(yc * yc) * (1.0 / RWKV_HEAD)
        yn = yc * lax.rsqrt(var + RWKV_GN_EPS) * tab_ref[0:1, sl] + tab_ref[1:2, sl]
        k_sum = k_ref[0, :, sl] + k_ref[0, :, C + t * LANES:C + (t + 1) * LANES]
        bonus = _pair_sums(r_ref[0, :, sl] * k_sum * tab_ref[2:3, sl]) * v_ref[0, :, sl]
        out_ref[0, :, sl] = ((yn + bonus) * gate_ref[0, :, sl]).astype(out_ref.dtype)
    for t in range(MLSTM_HEADS):
        sl = slice(t * MLSTM_HEAD, (t + 1) * MLSTM_HEAD)
        h = hf_ref[0, :, sl] + hb_ref[0, :, sl]
        hn = h * lax.rsqrt(jnp.mean(h * h, axis=-1, keepdims=True) + NORM_EPS) * tab_ref[3:4, sl]
        out_ref[0, :, C + t * MLSTM_HEAD:C + (t + 1) * MLSTM_HEAD] = (hn * jax.nn.sigmoid(o_ref[0, :, sl])).astype(out_ref.dtype)


def _rec_post(y, rv, k_eff, gate, h_f, h_b, pm, gn, r_k, norm_g):
    B, T, C = y.shape
    tm = TOKEN_TILE
    tab = jnp.concatenate([gn, r_k.reshape(1, C), norm_g.reshape(1, C), jnp.zeros((4, C), F32)], 0)
    spec1 = pl.BlockSpec((1, tm, C), lambda b, s: (b, s, 0))
    return pl.pallas_call(
        _rec_post_body,
        out_shape=jax.ShapeDtypeStruct((B, T, C + MLSTM_DIM), BF16),
        grid=(B, T // tm),
        in_specs=[spec1, spec1, pl.BlockSpec((1, tm, C), lambda b, s: (b, s, 1)),
                  pl.BlockSpec((1, tm, 2 * C), lambda b, s: (b, s, 0)), spec1, spec1, spec1,
                  pl.BlockSpec((1, tm, MLSTM_DIM), lambda b, s: (b, s, 3)), pl.BlockSpec(tab.shape, lambda b, s: (0, 0))],
        out_specs=pl.BlockSpec((1, tm, C + MLSTM_DIM), lambda b, s: (b, s, 0)),
        compiler_params=_params("arbitrary", "arbitrary"),
        name="rec_post",
    )(y, rv, rv, k_eff, gate, h_f, h_b, pm, tab)


def _to_state_lanes(x):
    B, T, _ = x.shape
    return jnp.transpose(x.reshape(B, T, 2, RWKV_HEADS, RWKV_HEAD), (1, 4, 2, 0, 3)).reshape(T, RWKV_HEAD, LANES)


def _recurrent_mixers(x, modtab, w_in, n_ctx, mu, w0, w1, w2, a0, a1, a2, g1, g2, kvec, r_k, gn, gate_b, norm_g):
    B, T, _ = x.shape
    assert 2 * B * RWKV_HEADS == LANES
    n_main = RWKV_IN + 4 * MLSTM_DIM
    w = jnp.pad(w_in, ((0, 0), (0, n_main + LANES - w_in.shape[1]))).astype(BF16)
    pr, pm, pg = _in_proj(x, modtab, w, ((0, RWKV_IN), (RWKV_IN, n_main), (n_main, n_main + LANES)), (F32, F32, F32))
    rv, kkn, decay, k_eff, kka, gate = _rwkv_prep(pr, n_ctx, mu, w0, w1, w2, a0, a1, a2, g1, g2, kvec)
    scan_in = tuple(_to_state_lanes(a) for a in (rv, decay, k_eff, kkn, kka))
    h_f, h_b = _mlstm(pm, pg[..., :4 * MLSTM_HEADS], gate_b, n_ctx)
    scan_in, h_f, h_b = lax.optimization_barrier((scan_in, h_f, h_b))
    yf, yb = _rwkv_scan(scan_in, n_ctx)
    half = LANES // 2
    y = yf[:, :, :half] + yb[:, :, half:]
    y = jnp.transpose(y.reshape(T, RWKV_HEAD, B, RWKV_HEADS), (2, 0, 3, 1)).reshape(B, T, RWKV_DIM)
    return _rec_post(y, rv, k_eff, gate, h_f, h_b, pm, gn, r_k, norm_g)


def kernel(x, c, ctx, c_ctx, ada_w, ada_b, ln_g, ln_b, mix_w_out, att_w_in, na_rpb, qk_gain, rec_w_in, rwkv_mu, rwkv_w0, rwkv_w1, rwkv_w2, rwkv_a0, rwkv_a1, rwkv_a2, rwkv_g1, rwkv_g2, rwkv_kvec, rwkv_rk, rwkv_gn, mlstm_gate_b, mlstm_norm, moe_router, moe_bias, moe_w1, moe_w3, moe_w2, shared_w1, shared_w3, shared_w2):
    B, S, D = x.shape
    n_ctx = ctx.shape[1]
    assert D == D_MODEL and n_ctx % TOKEN_TILE == 0 and S % TOKEN_TILE == 0
    xs = jnp.concatenate([ctx, x], axis=1)
    mods = _ada_modulation(c, c_ctx, ada_w, ada_b)
    for i in range(DEPTH):
        last = i == DEPTH - 1
        j = i // 2
        mod = mods[i, :B].reshape(B, 6, D)
        mod_c = jnp.broadcast_to(mods[i, B].reshape(1, 6, D), (B, 6, D))
        modtab = jnp.pad(jnp.stack([mod_c, mod], axis=1), ((0, 0), (0, 0), (0, MOD_ROWS - 6), (0, 0)))
        if i % 2 == 0:
            m = _attention_mixers(xs, modtab, att_w_in[j], na_rpb[j], qk_gain[j], n_ctx)
            w_out = _attention_w_out(mix_w_out[i])
        else:
            m = _recurrent_mixers(xs, modtab, rec_w_in[j], n_ctx, rwkv_mu[j], rwkv_w0[j], rwkv_w1[j], rwkv_w2[j],
                                  rwkv_a0[j], rwkv_a1[j], rwkv_a2[j], rwkv_g1[j], rwkv_g2[j], rwkv_kvec[j],
                                  rwkv_rk[j], rwkv_gn[j], mlstm_gate_b[j], mlstm_norm[j])
            w_out = mix_w_out[i]
        first_tile = n_ctx // TOKEN_TILE if last else 0
        x1, h2 = _out_proj(m, xs, modtab, w_out.astype(BF16), jnp.stack([ln_g[i, 0], ln_b[i, 0]]), first_tile)
        xs = _moe_block(h2, x1, modtab, jnp.stack([ln_g[i, 1], ln_b[i, 1]]), first_tile, i, moe_router[i], moe_bias[i],
                        moe_w1, moe_w3, moe_w2, shared_w1[i], shared_w3[i], shared_w2[i])
    return xs
```

```python
import functools

import jax
import jax.numpy as jnp
from jax import lax
from jax.experimental import pallas as pl
from jax.experimental.pallas import tpu as pltpu

D_MODEL = 1024
DEPTH = 4
GRID_W = 64
HEAD_DIM = 64
NA_HEADS = 8
NA_WIN_ROWS = 8
NA_WIN_COLS = 16
GQA_Q_HEADS = 8
GQA_KV_HEADS = 2
ROPE_THETA = 10000.0
ROPE_AXIS_DIM = HEAD_DIM // 2
NA_DIM = NA_HEADS * HEAD_DIM
GQA_Q_DIM = GQA_Q_HEADS * HEAD_DIM
GQA_KV_DIM = GQA_KV_HEADS * HEAD_DIM
RWKV_HEADS = 8
RWKV_HEAD = 64
RWKV_DIM = RWKV_HEADS * RWKV_HEAD
RWKV_GN_EPS = 64e-5
RWKV_IN = 4 * RWKV_DIM
MLSTM_HEADS = 4
MLSTM_HEAD = 128
MLSTM_DIM = MLSTM_HEADS * MLSTM_HEAD
MLSTM_CHUNK = 64
N_EXPERTS = 64
TOP_K = 6
ROUTED_SCALE = 2.5
DN_ALPHA = (2 * DEPTH) ** 0.25
LN_EPS = 1e-5
NORM_EPS = 1e-6
F32 = jnp.float32
BF16 = jnp.bfloat16

LANES = 128
SUBLANES = 8
VMEM_LIMIT = 48 * 1024 * 1024
TOKEN_TILE = 256
RWKV_TIME_BLOCK = 16
GQA_Q_TILE = 256
MOE_ROW_BLOCK = 512
MOE_SLOTS = 8
MOE_CHUNK = SUBLANES
MOE_STAGE_ROWS = TOKEN_TILE * TOP_K + N_EXPERTS * MOE_CHUNK
MOD_ROWS = 8


def _params(*semantics):
    return pltpu.CompilerParams(dimension_semantics=semantics, vmem_limit_bytes=VMEM_LIMIT)


def _mod_spec(tiles_per_batch, first_tile, flat):
    if flat:
        idx = lambda i, *_: (i // tiles_per_batch, jnp.minimum(i % tiles_per_batch + first_tile, 1), 0, 0)
    else:
        idx = lambda b, s, *_: (b, jnp.minimum(s + first_tile, 1), 0, 0)
    return pl.BlockSpec((1, 1, MOD_ROWS, D_MODEL), idx)


def _layer_norm_rows(z, g, b):
    mu = jnp.mean(z, axis=-1, keepdims=True)
    zc = z - mu
    var = jnp.mean(zc * zc, axis=-1, keepdims=True)
    return zc * lax.rsqrt(var + LN_EPS) * g + b


ADA_ROWS = 16


def _ada_body(c_ref, w_ref, b_ref, o_ref):
    x = c_ref[...]
    a = (x * jax.nn.sigmoid(x)).astype(BF16)
    o_ref[0] = jnp.dot(a, w_ref[0].astype(BF16), preferred_element_type=F32) + b_ref[0]


def _ada_modulation(c, c_ctx, ada_w, ada_b):
    B, D = c.shape
    depth, _, n = ada_w.shape
    cond = jnp.concatenate([c, c_ctx[None], jnp.zeros((ADA_ROWS - B - 1, D), F32)], 0)
    return pl.pallas_call(
        _ada_body,
        out_shape=jax.ShapeDtypeStruct((depth, ADA_ROWS, n), F32),
        grid=(depth, n // D),
        in_specs=[pl.BlockSpec((ADA_ROWS, D), lambda l, j: (0, 0)),
                  pl.BlockSpec((1, D, D), lambda l, j: (l, 0, j)),
                  pl.BlockSpec((1, 1, D), lambda l, j: (l, 0, j))],
        out_specs=pl.BlockSpec((1, ADA_ROWS, D), lambda l, j: (l, 0, j)),
        compiler_params=_params("arbitrary", "arbitrary"),
        name="ada_modulation",
    )(cond, ada_w, ada_b.reshape(depth, 1, n))
def _in_proj_body(x_ref, mod_ref, w_ref, *out_refs, splits):
    h = (x_ref[0] * (1.0 + mod_ref[0, 0, 1:2, :]) + mod_ref[0, 0, 0:1, :]).astype(BF16)
    for o_ref, (c0, c1) in zip(out_refs, splits):
        o_ref[0] = jnp.dot(h, w_ref[:, c0:c1], preferred_element_type=F32).astype(o_ref.dtype)


def _in_proj(x, modtab, w, splits, dtypes):
    B, T, D = x.shape
    tm = TOKEN_TILE
    outs = tuple(jax.ShapeDtypeStruct((B, T, c1 - c0), dt) for (c0, c1), dt in zip(splits, dtypes))
    return pl.pallas_call(
        functools.partial(_in_proj_body, splits=splits),
        out_shape=outs,
        grid=(B, T // tm),
        in_specs=[pl.BlockSpec((1, tm, D), lambda b, s: (b, s, 0)), _mod_spec(T // tm, 0, False),
                  pl.BlockSpec(w.shape, lambda b, s: (0, 0))],
        out_specs=tuple(pl.BlockSpec((1, tm, c1 - c0), lambda b, s: (b, s, 0)) for c0, c1 in splits),
        compiler_params=_params("arbitrary", "arbitrary"),
        name="in_proj",
    )(x, modtab, w)


def _out_proj_body(m_ref, x_ref, mod_ref, w_ref, ln_ref, x1_ref, h2_ref):
    y = jnp.dot(m_ref[0], w_ref[...], preferred_element_type=F32)
    mod = mod_ref[0, 0]
    x1 = _layer_norm_rows(DN_ALPHA * x_ref[0] + mod[2:3, :] * y, ln_ref[0:1, :], ln_ref[1:2, :])
    x1_ref[0] = x1
    h2_ref[0] = x1 * (1.0 + mod[4:5, :]) + mod[3:4, :]


def _out_proj(m, x, modtab, w, ln, first_tile):
    B, T, D = x.shape
    tm = TOKEN_TILE
    n_tiles = T // tm - first_tile
    rows = lambda b, s: (b, s + first_tile, 0)
    out_sds = jax.ShapeDtypeStruct((B, n_tiles * tm, D), F32)
    return pl.pallas_call(
        _out_proj_body,
        out_shape=(out_sds, out_sds),
        grid=(B, n_tiles),
        in_specs=[pl.BlockSpec((1, tm, m.shape[-1]), rows), pl.BlockSpec((1, tm, D), rows),
                  _mod_spec(T // tm, first_tile, False),
                  pl.BlockSpec(w.shape, lambda b, s: (0, 0)), pl.BlockSpec(ln.shape, lambda b, s: (0, 0))],
        out_specs=(pl.BlockSpec((1, tm, D), lambda b, s: (b, s, 0)),) * 2,
        compiler_params=_params("arbitrary", "arbitrary"),
        name="out_proj",
    )(m, x, modtab, w, ln)


def _rwkv_scan_body(rvf, wf, kf, nf, bf, rvb, wb, kb, nb, bb, yf_ref, yb_ref, state_ref, *, tc):
    @pl.when(pl.program_id(0) == 0)
    def _():
        state_ref[...] = jnp.zeros_like(state_ref)

    half = LANES // 2
    fwd_lane = lax.broadcasted_iota(jnp.int32, (RWKV_HEAD, LANES), 1) < half

    def step(j, carry):
        jb = tc - 1 - j

        def sel(f, b):
            return jnp.where(fwd_lane, f[j], b[jb])

        w, k, kkn, bv = sel(wf, wb), sel(kf, kb), sel(nf, nb), sel(bf, bb)
        rv_f, rv_b = rvf[j], rvb[jb]
        r = jnp.where(fwd_lane, rv_f, pltpu.roll(rv_b, half, axis=1))
        v = jnp.where(fwd_lane, pltpu.roll(rv_f, half, axis=1), rv_b)
        for vi in range(RWKV_HEAD):
            s = state_ref[vi]
            sa = jnp.sum(s * kkn, axis=0, keepdims=True)
            s2 = s * w + sa * bv + v[vi:vi + 1, :] * k
            state_ref[vi] = s2
            yrow = jnp.sum(s2 * r, axis=0, keepdims=True)
            yf_ref[j, pl.ds(vi, 1), :] = yrow
            yb_ref[jb, pl.ds(vi, 1), :] = yrow
        return carry

    lax.fori_loop(0, tc, step, 0)


def _rwkv_scan(xs, n_ctx):
    T = xs[0].shape[0]
    tc = RWKV_TIME_BLOCK
    assert T % tc == 0 and n_ctx % tc == 0
    nc, ncc = T // tc, n_ctx // tc
    blk = (tc, RWKV_HEAD, LANES)
    fwd = lambda c: (c, 0, 0)
    bwd = lambda c: (jnp.where(c < ncc, ncc - 1 - c, nc - 1 - (c - ncc)), 0, 0)
    out_sds = jax.ShapeDtypeStruct((T, RWKV_HEAD, LANES), F32)
    return pl.pallas_call(
        functools.partial(_rwkv_scan_body, tc=tc),
        out_shape=(out_sds, out_sds),
        grid=(nc,),
        in_specs=[pl.BlockSpec(blk, fwd)] * len(xs) + [pl.BlockSpec(blk, bwd)] * len(xs),
        out_specs=(pl.BlockSpec(blk, fwd), pl.BlockSpec(blk, bwd)),
        scratch_shapes=[pltpu.VMEM((RWKV_HEAD, RWKV_HEAD, LANES), F32)],
        compiler_params=_params("arbitrary"),
        name="rwkv_scan",
    )(*xs, *xs)


def _route_body(x_ref, rwt_ref, rb_ref, upper_ref, lower_ref, q_ref, g_ref, cnt_ref):
    tm = x_ref.shape[0]
    logits = lax.dot_general(rwt_ref[...], x_ref[...], (((1,), (1,)), ((), ())), preferred_element_type=F32,
                             precision=lax.Precision.HIGHEST)
    scores = jax.nn.sigmoid(logits)
    sel = scores + rb_ref[...]
    eidx = lax.broadcasted_iota(jnp.int32, (N_EXPERTS, tm), 0)
    slot = lax.broadcasted_iota(jnp.int32, (MOE_SLOTS, tm), 0)
    onehots = []
    for _ in range(TOP_K):
        m = jnp.max(sel, axis=0, keepdims=True)
        ij = jnp.min(jnp.where(sel == m, eidx, N_EXPERTS), axis=0, keepdims=True)
        oh = eidx == ij
        onehots.append(oh)
        sel = jnp.where(oh, -jnp.inf, sel)
    mask = functools.reduce(jnp.logical_or, onehots)
    maskf = jnp.where(mask, 1.0, 0.0)
    gsum = jnp.sum(jnp.where(mask, scores, 0.0), axis=0, keepdims=True)
    gates = scores / gsum * ROUTED_SCALE
    cnt = jnp.sum(maskf, axis=1, keepdims=True)
    cnt_pad = jnp.ceil(cnt * (1.0 / MOE_CHUNK)) * MOE_CHUNK
    lrank = jnp.dot(maskf.astype(BF16), upper_ref[...], preferred_element_type=F32)
    loff = jnp.dot(lower_ref[...], jnp.broadcast_to(cnt_pad, (N_EXPERTS, LANES)).astype(BF16),
                   preferred_element_type=F32)[:, 0:1]
    q = loff + lrank
    q8 = jnp.full((MOE_SLOTS, tm), -1.0, F32)
    g8 = jnp.zeros((MOE_SLOTS, tm), F32)
    for j, oh in enumerate(onehots):
        q8 = jnp.where(slot == j, jnp.sum(jnp.where(oh, q, 0.0), axis=0, keepdims=True), q8)
        g8 = jnp.where(slot == j, jnp.sum(jnp.where(oh, gates, 0.0), axis=0, keepdims=True), g8)
    q_ref[0] = q8.astype(jnp.int32)
    g_ref[0] = g8
    cnt_ref[0] = jnp.broadcast_to(cnt, (N_EXPERTS, LANES))


def _moe_route(tokens, router_w, router_b):
    n, d = tokens.shape
    tm = TOKEN_TILE
    nt = n // tm
    ar = jnp.arange(tm)
    upper = (ar[:, None] < ar[None, :]).astype(BF16)
    ae = jnp.arange(N_EXPERTS)
    lower = (ae[:, None] > ae[None, :]).astype(BF16)
    const = lambda shape: pl.BlockSpec(shape, lambda i: (0,) * len(shape))
    return pl.pallas_call(
        _route_body,
        out_shape=(jax.ShapeDtypeStruct((nt, MOE_SLOTS, tm), jnp.int32), jax.ShapeDtypeStruct((nt, MOE_SLOTS, tm), F32),
                   jax.ShapeDtypeStruct((nt, N_EXPERTS, LANES), F32)),
        grid=(nt,),
        in_specs=[pl.BlockSpec((tm, d), lambda i: (i, 0)), const((N_EXPERTS, d)), const((N_EXPERTS, 1)),
                  const((tm, tm)), const((N_EXPERTS, N_EXPERTS))],
        out_specs=(pl.BlockSpec((1, MOE_SLOTS, tm), lambda i: (i, 0, 0)),
                   pl.BlockSpec((1, MOE_SLOTS, tm), lambda i: (i, 0, 0)),
                   pl.BlockSpec((1, N_EXPERTS, LANES), lambda i: (i, 0, 0))),
        compiler_params=_params("arbitrary"),
        name="moe_router",
    )(tokens, router_w.T, router_b.reshape(N_EXPERTS, 1), upper, lower)


def _pack_bf16_pairs(x):
    m = x.shape[1] // 2
    bits = lambda a: lax.bitcast_convert_type(a.astype(BF16).astype(F32), jnp.uint32)
    return bits(x[:, :m]) | (bits(x[:, m:]) >> 16)


def _unpack_bf16_pairs(u):
    hi = lax.bitcast_convert_type(u & jnp.uint32(0xFFFF0000), F32)
    lo = lax.bitcast_convert_type(u << 16, F32)
    return jnp.concatenate([hi.astype(BF16), lo.astype(BF16)], axis=1)


def _chunk_loops(i, base_ref, nchunk_ref, loff_ref, copy):
    def per_expert(e, total):
        k = i * N_EXPERTS + e
        n, base, lo = nchunk_ref[k], base_ref[k], loff_ref[k]

        def piece(c, carry):
            copy(pl.multiple_of(lo + c * MOE_CHUNK, MOE_CHUNK), pl.multiple_of(base + c * MOE_CHUNK, MOE_CHUNK)).start()
            return carry

        lax.fori_loop(0, n, piece, 0)
        return total + n

    return lax.fori_loop(0, N_EXPERTS, per_expert, 0)


def _dispatch_body(base_ref, nchunk_ref, loff_ref, bv_ref, x_ref, q_ref, xs_hbm, stage_ref, zero_ref, sem_z, sem):
    i = pl.program_id(0)
    br = zero_ref.shape[0]
    n_blocks = xs_hbm.shape[0] // br

    @pl.when(i == 0)
    def _():
        zero_ref[...] = jnp.zeros_like(zero_ref)

        def fill_copy(b):
            return pltpu.make_async_copy(zero_ref, xs_hbm.at[pl.ds(b * br, br)], sem_z)

        def fill(b, carry):
            @pl.when(bv_ref[b] < br)
            def _():
                fill_copy(b).start()
            return carry

        def fill_wait(b, carry):
            @pl.when(bv_ref[b] < br)
            def _():
                fill_copy(b).wait()
            return carry

        lax.fori_loop(0, n_blocks, fill, 0)
        lax.fori_loop(0, n_blocks, fill_wait, 0)

    rows, tm = stage_ref.shape[0], x_ref.shape[0]
    q8 = q_ref[0]
    pos = lax.broadcasted_iota(jnp.int32, (rows, tm), 0)
    hit = functools.reduce(jnp.logical_or, [pos == q8[j:j + 1, :] for j in range(TOP_K)])
    perm = jnp.where(hit, 1.0, 0.0).astype(BF16)
    stage_ref[...] = _pack_bf16_pairs(jnp.dot(perm, x_ref[...].astype(BF16), preferred_element_type=F32))

    def copy(local_row, global_row):
        return pltpu.make_async_copy(stage_ref.at[pl.ds(local_row, MOE_CHUNK)], xs_hbm.at[pl.ds(global_row, MOE_CHUNK)], sem)

    total = _chunk_loops(i, base_ref, nchunk_ref, loff_ref, copy)

    def drain(c, carry):
        copy(0, 0).wait()
        return carry

    lax.fori_loop(0, total, drain, 0)


def _moe_dispatch(tokens, q_rows, dest_base, nchunk, loff, block_valid):
    n, d = tokens.shape
    tm, br = TOKEN_TILE, MOE_ROW_BLOCK
    n_rows = block_valid.shape[0] * br
    return pl.pallas_call(
        _dispatch_body,
        out_shape=jax.ShapeDtypeStruct((n_rows, d // 2), jnp.uint32),
        grid_spec=pltpu.PrefetchScalarGridSpec(
            num_scalar_prefetch=4,
            grid=(n // tm,),
            in_specs=[pl.BlockSpec((tm, d), lambda i, *_: (i, 0)),
                      pl.BlockSpec((1, MOE_SLOTS, tm), lambda i, *_: (i, 0, 0))],
            out_specs=pl.BlockSpec(memory_space=pl.ANY),
            scratch_shapes=[pltpu.VMEM((MOE_STAGE_ROWS, d // 2), jnp.uint32), pltpu.VMEM((br, d // 2), jnp.uint32),
                            pltpu.SemaphoreType.DMA, pltpu.SemaphoreType.DMA]),
        compiler_params=_params("arbitrary"),
        name="moe_dispatch",
    )(dest_base, nchunk, loff, block_valid, tokens, q_rows)


def _expert_body(be_ref, bv_ref, x_ref, w1_ref, w3_ref, w2_ref, y_ref, w1b, w3b, w2b):
    i = pl.program_id(0)
    valid = bv_ref[i]

    @pl.when(jnp.logical_or(i == 0, be_ref[i] != be_ref[jnp.maximum(i - 1, 0)]))
    def _():
        w1b[...] = w1_ref[0, 0].astype(BF16)
        w3b[...] = w3_ref[0, 0].astype(BF16)
        w2b[...] = w2_ref[0, 0].astype(BF16)

    @pl.when(valid > 0)
    def _():
        x = _unpack_bf16_pairs(x_ref[...])
        h1 = jnp.dot(x, w1b[...], preferred_element_type=F32)
        h3 = jnp.dot(x, w3b[...], preferred_element_type=F32)
        a = (h1 * jax.nn.sigmoid(h1) * h3).astype(BF16)
        y_ref[...] = _pack_bf16_pairs(jnp.dot(a, w2b[...], preferred_element_type=F32))

    @pl.when(valid <= 0)
    def _():
        y_ref[...] = jnp.zeros_like(y_ref)


def _moe_experts(xs, block_expert, block_valid, layer, w1, w3, w2):
    n_rows, dp = xs.shape
    br = MOE_ROW_BLOCK
    d, ff = w1.shape[-2:]
    return pl.pallas_call(
        _expert_body,
        out_shape=jax.ShapeDtypeStruct((n_rows, dp), jnp.uint32),
        grid_spec=pltpu.PrefetchScalarGridSpec(
            num_scalar_prefetch=2,
            grid=(n_rows // br,),
            in_specs=[pl.BlockSpec((br, dp), lambda i, be, bv: (i, 0)),
                      pl.BlockSpec((1, 1, d, ff), lambda i, be, bv: (layer, be[i], 0, 0)),
                      pl.BlockSpec((1, 1, d, ff), lambda i, be, bv: (layer, be[i], 0, 0)),
                      pl.BlockSpec((1, 1, ff, d), lambda i, be, bv: (layer, be[i], 0, 0))],
            out_specs=pl.BlockSpec((br, dp), lambda i, be, bv: (i, 0)),
            scratch_shapes=[pltpu.VMEM((d, ff), BF16), pltpu.VMEM((d, ff), BF16), pltpu.VMEM((ff, d), BF16)]),
        compiler_params=_params("arbitrary"),
        name="moe_experts",
    )(block_expert, block_valid, xs, w1, w3, w2)


def _combine_body(base_ref, nchunk_ref, loff_ref, h_ref, q_ref, g_ref, ys_hbm, sw1_ref, sw3_ref, sw2_ref,
                  x1_ref, mod_ref, ln_ref, out_ref, stage_ref, sem):
    i = pl.program_id(0)

    @pl.when(i == 0)
    def _():
        stage_ref[...] = jnp.zeros_like(stage_ref)

    def copy(local_row, global_row):
        return pltpu.make_async_copy(ys_hbm.at[pl.ds(global_row, MOE_CHUNK)], stage_ref.at[pl.ds(local_row, MOE_CHUNK)], sem)

    total = _chunk_loops(i, base_ref, nchunk_ref, loff_ref, copy)

    h = h_ref[...].astype(BF16)
    h1 = jnp.dot(h, sw1_ref[...], preferred_element_type=F32)
    h3 = jnp.dot(h, sw3_ref[...], preferred_element_type=F32)
    y = jnp.dot((h1 * jax.nn.sigmoid(h1) * h3).astype(BF16), sw2_ref[...], preferred_element_type=F32)
    tm, rows = h_ref.shape[0], stage_ref.shape[0]
    q8, g8 = q_ref[0], g_ref[0]
    pos = lax.broadcasted_iota(jnp.int32, (tm, rows), 1)
    gate_mat = jnp.zeros((tm, rows), F32)
    for j in range(TOP_K):
        gate_mat = jnp.where(pos == q8[:, j:j + 1], g8[:, j:j + 1], gate_mat)

    def drain(c, carry):
        copy(0, 0).wait()
        return carry

    lax.fori_loop(0, total, drain, 0)
    y = y + jnp.dot(gate_mat.astype(BF16), _unpack_bf16_pairs(stage_ref[...]), preferred_element_type=F32)
    out_ref[...] = _layer_norm_rows(DN_ALPHA * x1_ref[...] + mod_ref[0, 0, 5:6, :] * y, ln_ref[0:1, :], ln_ref[1:2, :])


def _moe_combine(tokens, q_cols, g_cols, dest_base, nchunk, loff, ys, sw1, sw3, sw2, x1, modtab, ln, tiles_per_batch,
                 first_tile):
    n, d = tokens.shape
    tm = TOKEN_TILE
    ff = sw1.shape[-1]
    whole = lambda shape: pl.BlockSpec(shape, lambda i, *_: (0,) * len(shape))
    rows = pl.BlockSpec((tm, d), lambda i, *_: (i, 0))
    slots = pl.BlockSpec((1, tm, MOE_SLOTS), lambda i, *_: (i, 0, 0))
    return pl.pallas_call(
        _combine_body,
        out_shape=jax.ShapeDtypeStruct((n, d), F32),
        grid_spec=pltpu.PrefetchScalarGridSpec(
            num_scalar_prefetch=3,
            grid=(n // tm,),
            in_specs=[rows, slots, slots, pl.BlockSpec(memory_space=pl.ANY),
                      whole((d, ff)), whole((d, ff)), whole((ff, d)),
                      rows, _mod_spec(tiles_per_batch, first_tile, True), whole(ln.shape)],
            out_specs=rows,
            scratch_shapes=[pltpu.VMEM((MOE_STAGE_ROWS, d // 2), jnp.uint32), pltpu.SemaphoreType.DMA]),
        compiler_params=_params("arbitrary"),
        name="moe_combine",
    )(dest_base, nchunk, loff, tokens, q_cols, g_cols, ys, sw1, sw3, sw2, x1, modtab, ln)


def _moe_block(h2, x1, modtab, ln, first_tile, layer, router_w, router_b, w1, w3, w2, sw1, sw3, sw2):
    B, Tp, d = h2.shape
    tokens = h2.reshape(B * Tp, d)
    n = B * Tp
    tm, br = TOKEN_TILE, MOE_ROW_BLOCK
    q_rows, g_rows, cnt = _moe_route(tokens, router_w, router_b)
    cnt = cnt[:, :, 0].astype(jnp.int32)
    nchunk = (cnt + MOE_CHUNK - 1) // MOE_CHUNK
    run = nchunk * MOE_CHUNK
    total = jnp.sum(run, axis=0)
    padded = (total + br - 1) // br * br
    p_end = jnp.cumsum(padded)
    offs = p_end - padded
    dest_base = (offs[None, :] + jnp.cumsum(run, axis=0) - run).reshape(-1).astype(jnp.int32)
    loff = jnp.cumsum(run, axis=1) - run
    nchunk, loff = nchunk.reshape(-1).astype(jnp.int32), loff.reshape(-1).astype(jnp.int32)
    n_blocks = -(-(n * TOP_K + (n // tm) * N_EXPERTS * (MOE_CHUNK - 1) + N_EXPERTS * (br - 1)) // br)
    blk_start = jnp.arange(n_blocks, dtype=jnp.int32) * br
    block_expert = jnp.minimum(jnp.sum(blk_start[:, None] >= p_end[None, :], axis=1), N_EXPERTS - 1).astype(jnp.int32)
    block_valid = jnp.clip(total[block_expert] - (blk_start - offs[block_expert]), 0, br).astype(jnp.int32)
    xs = _moe_dispatch(tokens, q_rows, dest_base, nchunk, loff, block_valid)
    ys = _moe_experts(xs, block_expert, block_valid, layer, w1, w3, w2)
    q_cols, g_cols = jnp.swapaxes(q_rows, 1, 2), jnp.swapaxes(g_rows, 1, 2)
    out = _moe_combine(tokens, q_cols, g_cols, dest_base, nchunk, loff, ys,
                       sw1.astype(BF16), sw3.astype(BF16), sw2.astype(BF16),
                       x1.reshape(n, d), modtab, ln, Tp // tm, first_tile)
    return out.reshape(B, Tp, d)


def _low_half():
    return lax.broadcasted_iota(jnp.int32, (1, LANES), 1) < LANES // 2


def _pair_attention(q, parts):
    low = _low_half()
    outs = []
    for use_low in (True, False):
        qm = jnp.where(low == use_low, q, jnp.zeros_like(q))
        scores = []
        for k, _, b_lo, b_hi in parts:
            s = lax.dot_general(qm, k, (((1,), (1,)), ((), ())), preferred_element_type=F32) * HEAD_DIM ** -0.5
            b = b_lo if use_low else b_hi
            scores.append(s if b is None else s + b)
        m = functools.reduce(jnp.maximum, [jnp.max(s, axis=-1, keepdims=True) for s in scores])
        den = 0.0
        num = 0.0
        for s, (_, v, _, _) in zip(scores, parts):
            p = jnp.exp(s - m)
            den = den + jnp.sum(p, axis=-1, keepdims=True)
            num = num + jnp.dot(p.astype(BF16), v, preferred_element_type=F32)
        outs.append(num / den)
    return jnp.where(low, outs[0], outs[1])


NA_Q_ROWS = 2
NA_BAND_ROWS = NA_WIN_ROWS + NA_Q_ROWS - 1


def _na_body(cls_ref, q_ref, k_ref, v_ref, bias_ref, o_ref, *, n_ctx, rows):
    s = pl.program_id(1)
    tq = q_ref.shape[1]
    ctx_steps = n_ctx // tq
    n_tiles = NA_DIM // LANES
    tile = lambda t: slice(t * LANES, (t + 1) * LANES)

    @pl.when(s < ctx_steps)
    def _():
        for t in range(n_tiles):
            part = (k_ref[0, 0:n_ctx, tile(t)], v_ref[0, 0:n_ctx, tile(t)], None, None)
            o_ref[0, :, tile(t)] = _pair_attention(q_ref[0, :, tile(t)], [part]).astype(o_ref.dtype)

    @pl.when(s >= ctx_steps)
    def _():
        first = (s - ctx_steps) * NA_Q_ROWS
        start = jnp.clip(first - NA_WIN_ROWS // 2, 0, rows - NA_BAND_ROWS)
        off = pl.multiple_of(n_ctx + start * GRID_W, GRID_W)
        band = pl.ds(off, NA_BAND_ROWS * GRID_W)
        for t in range(n_tiles):
            parts = [(k_ref[0, band, tile(t)], v_ref[0, band, tile(t)], bias_ref[0, 2 * t], bias_ref[0, 2 * t + 1]),
                     (k_ref[0, 0:n_ctx, tile(t)], v_ref[0, 0:n_ctx, tile(t)], None, None)]
            o_ref[0, :, tile(t)] = _pair_attention(q_ref[0, :, tile(t)], parts).astype(o_ref.dtype)


def _na_bias_table(rpb, rows):
    import numpy as np
    kc, W = NA_WIN_COLS, GRID_W
    cidx = np.arange(W)
    col_start = np.clip(cidx - kc // 2, 0, W - kc)
    col_in = (cidx[None, :] >= col_start[:, None]) & (cidx[None, :] < col_start[:, None] + kc)
    d_col = np.clip(cidx[None, :] - cidx[:, None], -(kc - 1), kc - 1) + kc - 1
    classes, class_of, d_rows, allowed = {}, [], [], []
    for step in range(rows // NA_Q_ROWS):
        first = step * NA_Q_ROWS
        band0 = min(max(first - NA_WIN_ROWS // 2, 0), rows - NA_BAND_ROWS)
        d_row = np.zeros((NA_Q_ROWS, NA_BAND_ROWS), np.int32)
        ok = np.zeros((NA_Q_ROWS, NA_BAND_ROWS), bool)
        for p in range(NA_Q_ROWS):
            win0 = min(max(first + p - NA_WIN_ROWS // 2, 0), rows - NA_WIN_ROWS)
            for j in range(NA_BAND_ROWS):
                ok[p, j] = win0 <= band0 + j < win0 + NA_WIN_ROWS
                d_row[p, j] = min(max(band0 + j - (first + p) + NA_WIN_ROWS - 1, 0), 2 * NA_WIN_ROWS - 2)
        key = (d_row.tobytes(), ok.tobytes())
        if key not in classes:
            classes[key] = len(classes)
            d_rows.append(d_row)
            allowed.append(ok)
        class_of.append(classes[key])
    d_rows, allowed = np.stack(d_rows), np.stack(allowed)
    tab = rpb[:, d_rows][..., d_col]
    mask = allowed[None, :, :, :, None, None] & col_in[None, None, None, None]
    tab = jnp.where(mask, tab, -jnp.inf)
    tab = jnp.transpose(tab, (1, 0, 2, 4, 3, 5))
    return tab.reshape(len(classes), NA_HEADS, NA_Q_ROWS * W, NA_BAND_ROWS * W), class_of


def _na_attention(pa, rpb, n_ctx):
    B, T, _ = pa.shape
    rows = (T - n_ctx) // GRID_W
    tq = NA_Q_ROWS * GRID_W
    assert rows % NA_Q_ROWS == 0 and rows >= NA_BAND_ROWS and n_ctx % tq == 0
    table, class_of = _na_bias_table(rpb, rows)
    step_class = jnp.array([0] * (n_ctx // tq) + class_of, jnp.int32)
    return pl.pallas_call(
        functools.partial(_na_body, n_ctx=n_ctx, rows=rows),
        out_shape=jax.ShapeDtypeStruct((B, T, NA_DIM), BF16),
        grid_spec=pltpu.PrefetchScalarGridSpec(
            num_scalar_prefetch=1,
            grid=(B, T // tq),
            in_specs=[pl.BlockSpec((1, tq, NA_DIM), lambda b, s, cls: (b, s, 0)),
                      pl.BlockSpec((1, T, NA_DIM), lambda b, s, cls: (b, 0, 1)),
                      pl.BlockSpec((1, T, NA_DIM), lambda b, s, cls: (b, 0, 2)),
                      pl.BlockSpec((1,) + table.shape[1:], lambda b, s, cls: (cls[s], 0, 0, 0))],
            out_specs=pl.BlockSpec((1, tq, NA_DIM), lambda b, s, cls: (b, s, 0))),
        compiler_params=_params("arbitrary", "arbitrary"),
        name="na_attention",
    )(step_class, pa, pa, pa, table)


def _rms_pair(x, gain):
    low = _low_half()
    sq = x * x
    s_lo = jnp.sum(jnp.where(low, sq, 0.0), axis=-1, keepdims=True)
    s_hi = jnp.sum(jnp.where(low, 0.0, sq), axis=-1, keepdims=True)
    ms = jnp.where(low, s_lo, s_hi) * (1.0 / HEAD_DIM)
    return x * lax.rsqrt(ms + NORM_EPS) * gain


def _rope_pair(x, cos, sin_signed):
    even = lax.broadcasted_iota(jnp.int32, (1, LANES), 1) % 2 == 0
    partner = jnp.where(even, pltpu.roll(x, LANES - 1, axis=1), pltpu.roll(x, 1, axis=1))
    return x * cos + partner * sin_signed


def _gqa_body(q_ref, k_ref, v_ref, cos_q, sin_q, cos_k, sin_k, gain_ref, o_ref, kn_ref, vn_ref, *, n_ctx):
    s = pl.program_id(1)
    tq = q_ref.shape[1]

    @pl.when(s == 0)
    def _():
        kn_ref[...] = _rope_pair(_rms_pair(k_ref[0], gain_ref[1:2, :]), cos_k[...], sin_k[...]).astype(BF16)
        vn_ref[...] = v_ref[0].astype(BF16)

    def run(n_keys):
        k, v = kn_ref[0:n_keys, :], vn_ref[0:n_keys, :]
        for t in range(GQA_Q_DIM // LANES):
            q = q_ref[0, :, t * LANES:(t + 1) * LANES]
            qn = _rope_pair(_rms_pair(q, gain_ref[0:1, :]), cos_q[...], sin_q[...]).astype(BF16)
            o_ref[0, :, t * LANES:(t + 1) * LANES] = _pair_attention(qn, [(k, v, None, None)]).astype(o_ref.dtype)

    @pl.when(s < n_ctx // tq)
    def _():
        run(n_ctx)

    @pl.when(s >= n_ctx // tq)
    def _():
        run(kn_ref.shape[0])


GQA_HEAD_ORDER = (0, 4, 1, 5, 2, 6, 3, 7)


def _axial_rope(n_tokens):
    t = jnp.arange(n_tokens)
    row = (t // GRID_W).astype(F32)
    col = (t % GRID_W).astype(F32)
    inv = ROPE_THETA ** (-jnp.arange(0, ROPE_AXIS_DIM, 2, dtype=F32) / ROPE_AXIS_DIM)
    ang = jnp.concatenate([row[:, None] * inv, col[:, None] * inv], -1)
    return jnp.cos(ang), jnp.sin(ang)


def _gqa_rope_tables(T, n_ctx):
    cos, sin = _axial_rope(T - n_ctx)
    cos = jnp.concatenate([jnp.ones((n_ctx, ROPE_AXIS_DIM), F32), cos], 0)
    sin = jnp.concatenate([jnp.zeros((n_ctx, ROPE_AXIS_DIM), F32), sin], 0)
    cos = jnp.tile(jnp.repeat(cos, 2, axis=-1), (1, 2))
    sign = jnp.tile(jnp.array([-1.0, 1.0], F32), LANES // 2)
    sin = jnp.tile(jnp.repeat(sin, 2, axis=-1), (1, 2)) * sign
    return cos, sin


def _gqa_attention(pb, qk_gain, n_ctx):
    B, T, _ = pb.shape
    tq = GQA_Q_TILE
    cos, sin = _gqa_rope_tables(T, n_ctx)
    gain = jnp.tile(qk_gain, (1, 2))
    kv_blk = GQA_Q_DIM // GQA_KV_DIM
    return pl.pallas_call(
        functools.partial(_gqa_body, n_ctx=n_ctx),
        out_shape=jax.ShapeDtypeStruct((B, T, GQA_Q_DIM), BF16),
        grid=(B, T // tq),
        in_specs=[pl.BlockSpec((1, tq, GQA_Q_DIM), lambda b, s: (b, s, 0)),
                  pl.BlockSpec((1, T, GQA_KV_DIM), lambda b, s: (b, 0, kv_blk)),
                  pl.BlockSpec((1, T, GQA_KV_DIM), lambda b, s: (b, 0, kv_blk + 1)),
                  pl.BlockSpec((tq, LANES), lambda b, s: (s, 0)),
                  pl.BlockSpec((tq, LANES), lambda b, s: (s, 0)),
                  pl.BlockSpec((T, LANES), lambda b, s: (0, 0)),
                  pl.BlockSpec((T, LANES), lambda b, s: (0, 0)),
                  pl.BlockSpec((2, LANES), lambda b, s: (0, 0))],
        out_specs=pl.BlockSpec((1, tq, GQA_Q_DIM), lambda b, s: (b, s, 0)),
        scratch_shapes=[pltpu.VMEM((T, GQA_KV_DIM), BF16), pltpu.VMEM((T, GQA_KV_DIM), BF16)],
        compiler_params=_params("arbitrary", "arbitrary"),
        name="gqa_attention",
    )(pb, pb, pb, cos, sin, cos, sin, gain)


def _attention_mixers(x, modtab, w_in, rpb, qk_gain, n_ctx):
    order = jnp.array(GQA_HEAD_ORDER)
    qb_cols = 3 * NA_DIM + (order[:, None] * HEAD_DIM + jnp.arange(HEAD_DIM)[None, :]).reshape(-1)
    cols = jnp.concatenate([jnp.arange(3 * NA_DIM), qb_cols, jnp.arange(3 * NA_DIM + GQA_Q_DIM, w_in.shape[1])])
    w = w_in[:, cols].astype(BF16)
    pa, pb = _in_proj(x, modtab, w, ((0, 3 * NA_DIM), (3 * NA_DIM, w.shape[1])), (BF16, F32))
    return jnp.concatenate([_na_attention(pa, rpb, n_ctx), _gqa_attention(pb, qk_gain, n_ctx)], -1)


def _attention_w_out(w_out):
    order = jnp.array(GQA_HEAD_ORDER)
    rows = NA_DIM + (order[:, None] * HEAD_DIM + jnp.arange(HEAD_DIM)[None, :]).reshape(-1)
    return jnp.concatenate([w_out[:NA_DIM], w_out[rows]], 0)


def _log_sigmoid(x):
    return jnp.minimum(x, 0.0) - jnp.log(1.0 + jnp.exp(-jnp.abs(x)))


def _mlstm_body(qf, kf, vf, gcf, grf, ktf, qb, kb, vb, gcb, grb, ktb, bias_c, bias_r, hf_ref, hb_ref,
                c_ref, n_ref, m_ref):
    @pl.when(pl.program_id(1) == 0)
    def _():
        c_ref[...] = jnp.zeros_like(c_ref)
        n_ref[...] = jnp.zeros_like(n_ref)
        m_ref[...] = jnp.zeros_like(m_ref)

    L = MLSTM_CHUNK
    row = lax.broadcasted_iota(jnp.int32, (L, L), 0)
    col = lax.broadcasted_iota(jnp.int32, (L, L), 1)
    hi = lax.Precision.HIGHEST
    dirs = ((qf, kf, vf, gcf, grf, hf_ref, ktf), (qb, kb, vb, gcb, grb, hb_ref, ktb))
    combos = [(d, h) for d in range(2) for h in range(MLSTM_HEADS)]
    head = lambda h: slice(h * MLSTM_HEAD, (h + 1) * MLSTM_HEAD)
    gates = []
    for d, (_, _, _, gc_ref, gr_ref, _, _) in enumerate(dirs):
        seen = (col <= row) if d == 0 else (col >= row)
        seen_f = jnp.where(seen, 1.0, 0.0)
        g_col = gc_ref[0] + bias_c[...]
        g_row = gr_ref[0, 0] + bias_r[...]
        b_col = jnp.dot(seen_f, _log_sigmoid(g_col), preferred_element_type=F32, precision=hi)
        lf_row = _log_sigmoid(g_row)
        b_row = lax.dot_general(lf_row, seen_f, (((1,), (1,)), ((), ())), preferred_element_type=F32, precision=hi)
        gates.append((seen, g_col, g_row, b_col, b_row, jnp.sum(lf_row, axis=1, keepdims=True)))
    first = {}
    for d, h in combos:
        q_ref, k_ref = dirs[d][0], dirs[d][1]
        s_idx = d * MLSTM_HEADS + h
        q = (q_ref[0, :, head(h)] * MLSTM_HEAD ** -0.5).astype(BF16)
        k = k_ref[0, :, head(h)]
        kb16 = k.astype(BF16)
        c_state = c_ref[s_idx]
        qk = lax.dot_general(q, kb16, (((1,), (1,)), ((), ())), preferred_element_type=F32)
        qc = jnp.dot(q, c_state.astype(BF16), preferred_element_type=F32)
        first[d, h] = (q, k, dirs[d][6][0, 0, head(h), :].astype(BF16), c_state, qk, qc)
    second = {}
    for d, h in combos:
        seen, g_col, g_row, b_col, b_row, b_end = gates[d]
        q, k, kb16, c_state, qk, qc = first[d, h]
        v = dirs[d][2][0, :, head(h)]
        gi, gf = d * 2 * MLSTM_HEADS + h, d * 2 * MLSTM_HEADS + MLSTM_HEADS + h
        s_idx = d * MLSTM_HEADS + h
        n_state, m_state = n_ref[s_idx:s_idx + 1, :], m_ref[s_idx:s_idx + 1, 0:1]
        bc, br, ig_c, ig_r = b_col[:, gf:gf + 1], b_row[gf:gf + 1, :], g_col[:, gi:gi + 1], g_row[gi:gi + 1, :]
        be = b_end[gf:gf + 1, :]
        d_intra = jnp.where(seen, bc - br + ig_r, -jnp.inf)
        d_inter = bc + m_state
        m_t = jnp.maximum(d_inter, jnp.max(d_intra, axis=1, keepdims=True))
        s = qk * jnp.exp(d_intra - m_t)
        w_inter = jnp.exp(d_inter - m_t)
        den = jnp.sum(s, axis=1, keepdims=True) + w_inter * jnp.sum(q.astype(F32) * n_state, axis=1, keepdims=True)
        d_state = be - bc + ig_c
        m_new = jnp.maximum(be + m_state, jnp.max(d_state, axis=0, keepdims=True))
        w_s = jnp.exp(d_state - m_new)
        w_c = jnp.exp(be + m_state - m_new)
        n_new = w_c * n_state + jnp.sum(w_s * k, axis=0, keepdims=True)
        second[d, h] = (s.astype(BF16), v.astype(BF16), (v * w_s).astype(BF16), w_inter * qc,
                        jnp.maximum(jnp.abs(den), jnp.exp(-m_t)), w_c, n_new, m_new)
    writes = []
    for d, h in combos:
        s16, v16, vw16, inter, den, w_c, n_new, m_new = second[d, h]
        _, _, kt16, c_state, _, _ = first[d, h]
        s_idx = d * MLSTM_HEADS + h
        num = jnp.dot(s16, v16, preferred_element_type=F32) + inter
        vk = jnp.dot(kt16, vw16, preferred_element_type=F32)
        writes.append((dirs[d][5].at[0, :, head(h)], num / den))
        writes.append((c_ref.at[s_idx], w_c * c_state + vk))
        writes.append((n_ref.at[s_idx:s_idx + 1, :], n_new))
        writes.append((m_ref.at[s_idx:s_idx + 1, :], jnp.broadcast_to(m_new, (1, LANES))))
    for ref, value in writes:
        ref[...] = value


def _mlstm(pm, gates, gate_b, n_ctx):
    B, T, _ = pm.shape
    L = MLSTM_CHUNK
    nc, ncc = T // L, n_ctx // L
    ng = 4 * MLSTM_HEADS
    g_cols = jnp.pad(gates, ((0, 0), (0, 0), (0, LANES - ng)))
    g_rows = jnp.swapaxes(gates.reshape(B, nc, L, ng), 2, 3)
    k_t = jnp.swapaxes(pm[..., MLSTM_DIM:2 * MLSTM_DIM].reshape(B, nc, L, MLSTM_DIM), 2, 3)
    ktr = lambda order: pl.BlockSpec((1, 1, MLSTM_DIM, L), lambda b, c: (b, order(c), 0, 0))
    bias = gate_b.reshape(ng)
    bias_c = jnp.pad(bias, (0, LANES - ng)).reshape(1, LANES)
    bias_r = bias.reshape(ng, 1)
    fwd = lambda c: c
    bwd = lambda c: jnp.where(c < ncc, ncc - 1 - c, nc - 1 - (c - ncc))
    blk = (1, L, MLSTM_DIM)
    seq = lambda order, j: pl.BlockSpec(blk, lambda b, c: (b, order(c), j))
    gcol = lambda order: pl.BlockSpec((1, L, LANES), lambda b, c: (b, order(c), 0))
    grow = lambda order: pl.BlockSpec((1, 1, ng, L), lambda b, c: (b, order(c), 0, 0))
    n_state = 2 * MLSTM_HEADS
    out_sds = jax.ShapeDtypeStruct((B, T, MLSTM_DIM), F32)
    return pl.pallas_call(
        _mlstm_body,
        out_shape=(out_sds, out_sds),
        grid=(B, nc),
        in_specs=[seq(fwd, 0), seq(fwd, 1), seq(fwd, 2), gcol(fwd), grow(fwd), ktr(fwd),
                  seq(bwd, 0), seq(bwd, 1), seq(bwd, 2), gcol(bwd), grow(bwd), ktr(bwd),
                  pl.BlockSpec((1, LANES), lambda b, c: (0, 0)), pl.BlockSpec((ng, 1), lambda b, c: (0, 0))],
        out_specs=(pl.BlockSpec(blk, lambda b, c: (b, fwd(c), 0)), pl.BlockSpec(blk, lambda b, c: (b, bwd(c), 0))),
        scratch_shapes=[pltpu.VMEM((n_state, MLSTM_HEAD, MLSTM_HEAD), F32), pltpu.VMEM((n_state, MLSTM_HEAD), F32),
                        pltpu.VMEM((n_state, LANES), F32)],
        compiler_params=_params("arbitrary", "arbitrary"),
        name="mlstm",
    )(pm, pm, pm, g_cols, g_rows, k_t, pm, pm, pm, g_cols, g_rows, k_t, bias_c, bias_r)


def _pair_sums(x):
    low = _low_half()
    s_lo = jnp.sum(jnp.where(low, x, 0.0), axis=-1, keepdims=True)
    s_hi = jnp.sum(jnp.where(low, 0.0, x), axis=-1, keepdims=True)
    return jnp.where(low, s_lo, s_hi)


def _rwkv_prep_body(p_ref, prev_ref, next_ref, tab_ref, tab2_ref, w1_ref, w2_ref, a1_ref, a2_ref, g1_ref, g2_ref,
                    rv_ref, n_ref, w_ref, k_ref, b_ref, gate_ref, *, n_ctx):
    s = pl.program_id(1)
    tm = p_ref.shape[1]
    C = RWKV_DIM
    ctx_tiles = n_ctx // tm
    x = p_ref[0]
    has_prev = jnp.logical_and(s != 0, s != ctx_tiles)
    has_next = jnp.logical_and(s != ctx_tiles - 1, s != pl.num_programs(1) - 1)
    prev_row = jnp.where(has_prev, prev_ref[0, SUBLANES - 1:SUBLANES, :], 0.0)
    next_row = jnp.where(has_next, next_ref[0, 0:1, :], 0.0)
    rowid = lax.broadcasted_iota(jnp.int32, (tm, 1), 0)
    up = jnp.where(rowid == 0, prev_row, pltpu.roll(x, 1, axis=0))
    dn = jnp.where(rowid == tm - 1, next_row, pltpu.roll(x, tm - 1, axis=0))
    d = 0.5 * (up + dn) - x
    part = lambda a, i: a[:, i * C:(i + 1) * C]
    mu = lambda i: tab_ref[i:i + 1, :]
    r = part(x, 0) + part(d, 0) * mu(0)
    k = part(x, 1) + part(d, 1) * mu(1)
    v = part(x, 2) + part(d, 2) * mu(2)
    z, dz = part(x, 3), part(d, 3)
    z_w, z_a, z_g = (z + dz * mu(3)).astype(BF16), (z + dz * mu(4)).astype(BF16), (z + dz * mu(5)).astype(BF16)
    lora = lambda t, w: jnp.dot(t.astype(BF16), w[...], preferred_element_type=F32)
    w_pre = tab2_ref[0:1, :] + lora(jnp.tanh(lora(z_w, w1_ref)), w2_ref)
    neg = -w_pre
    softplus = jnp.maximum(neg, 0.0) + jnp.log(1.0 + jnp.exp(-jnp.abs(neg)))
    decay = jnp.exp(-jnp.exp(-softplus - 0.5))
    iclr = jax.nn.sigmoid(tab2_ref[1:2, :] + lora(lora(z_a, a1_ref), a2_ref))
    gate_ref[0] = lora(jax.nn.sigmoid(lora(z_g, g1_ref)), g2_ref)
    kk = k * tab_ref[6:7, :]
    kk = jnp.concatenate(
        [kk[:, t * LANES:(t + 1) * LANES]
         * lax.rsqrt(jnp.maximum(_pair_sums(jnp.square(kk[:, t * LANES:(t + 1) * LANES])), 1e-24))
         for t in range(C // LANES)], axis=1)
    k2, kk2 = jnp.concatenate([k, k], axis=1), jnp.concatenate([kk, kk], axis=1)
    rv_ref[0] = jnp.concatenate([r, v], axis=1)
    n_ref[0] = -kk2
    w_ref[0] = decay
    k_ref[0] = k2 * (1.0 + (iclr - 1.0) * tab2_ref[2:3, :])
    b_ref[0] = kk2 * iclr


def _rwkv_prep(pr, n_ctx, mu, w0, w1, w2, a0, a1, a2, g1, g2, kvec):
    B, T, _ = pr.shape
    tm, C = TOKEN_TILE, RWKV_DIM
    per_tile = tm // SUBLANES
    tab = jnp.concatenate([mu, kvec[0:1], jnp.zeros((1, C), F32)], 0)
    cat = lambda a: jnp.concatenate([a[0], a[1]], -1)
    tab2 = jnp.concatenate([cat(w0)[None], cat(a0)[None], jnp.tile(kvec[1], 2)[None], jnp.zeros((5, 2 * C), F32)], 0)
    blockdiag = lambda a: jnp.concatenate([jnp.pad(a[0], ((0, 0), (0, C))), jnp.pad(a[1], ((0, 0), (C, 0)))], 0)
    gl = g1.shape[1]
    consts = (tab, tab2, cat(w1).astype(BF16), blockdiag(w2).astype(BF16), cat(a1).astype(BF16),
              blockdiag(a2).astype(BF16), jnp.pad(g1, ((0, 0), (0, LANES - gl))).astype(BF16),
              jnp.pad(g2, ((0, LANES - gl), (0, 0))).astype(BF16))
    const = lambda a: pl.BlockSpec(a.shape, lambda b, s: (0, 0))
    one = jax.ShapeDtypeStruct((B, T, C), F32)
    two = jax.ShapeDtypeStruct((B, T, 2 * C), F32)
    spec1 = pl.BlockSpec((1, tm, C), lambda b, s: (b, s, 0))
    spec2 = pl.BlockSpec((1, tm, 2 * C), lambda b, s: (b, s, 0))
    return pl.pallas_call(
        functools.partial(_rwkv_prep_body, n_ctx=n_ctx),
        out_shape=(two, two, two, two, two, one),
        grid=(B, T // tm),
        in_specs=[pl.BlockSpec((1, tm, 4 * C), lambda b, s: (b, s, 0)),
                  pl.BlockSpec((1, SUBLANES, 4 * C), lambda b, s: (b, jnp.maximum(s * per_tile - 1, 0), 0)),
                  pl.BlockSpec((1, SUBLANES, 4 * C), lambda b, s: (b, jnp.minimum((s + 1) * per_tile, T // SUBLANES - 1), 0))]
        + [const(a) for a in consts],
        out_specs=(spec2, spec2, spec2, spec2, spec2, spec1),
        compiler_params=_params("arbitrary", "arbitrary"),
        name="rwkv_prep",
    )(pr, pr, pr, *consts)


def _rec_post_body(y_ref, r_ref, v_ref, k_ref, gate_ref, hf_ref, hb_ref, o_ref, tab_ref, out_ref):
    C = RWKV_DIM
    for t in range(C // LANES):
        sl = slice(t * LANES, (t + 1) * LANES)
        y = y_ref[0, :, sl]
        yc = y - _pair_sums(y) * (1.0 / RWKV_HEAD)
        var = _pair_sums(yc * yc) * (1.0 / RWKV_HEAD)
        yn = yc * lax.rsqrt(var + RWKV_GN_EPS) * tab_ref[0:1, sl] + tab_ref[1:2, sl]
        k_sum = k_ref[0, :, sl] + k_ref[0, :, C + t * LANES:C + (t + 1) * LANES]
        bonus = _pair_sums(r_ref[0, :, sl] * k_sum * tab_ref[2:3, sl]) * v_ref[0, :, sl]
        out_ref[0, :, sl] = ((yn + bonus) * gate_ref[0, :, sl]).astype(out_ref.dtype)
    for t in range(MLSTM_HEADS):
        sl = slice(t * MLSTM_HEAD, (t + 1) * MLSTM_HEAD)
        h = hf_ref[0, :, sl] + hb_ref[0, :, sl]
        hn = h * lax.rsqrt(jnp.mean(h * h, axis=-1, keepdims=True) + NORM_EPS) * tab_ref[3:4, sl]
        out_ref[0, :, C + t * MLSTM_HEAD:C + (t + 1) * MLSTM_HEAD] = (hn * jax.nn.sigmoid(o_ref[0, :, sl])).astype(out_ref.dtype)


def _rec_post(y, rv, k_eff, gate, h_f, h_b, pm, gn, r_k, norm_g):
    B, T, C = y.shape
    tm = TOKEN_TILE
    tab = jnp.concatenate([gn, r_k.reshape(1, C), norm_g.reshape(1, C), jnp.zeros((4, C), F32)], 0)
    spec1 = pl.BlockSpec((1, tm, C), lambda b, s: (b, s, 0))
    return pl.pallas_call(
        _rec_post_body,
        out_shape=jax.ShapeDtypeStruct((B, T, C + MLSTM_DIM), BF16),
        grid=(B, T // tm),
        in_specs=[spec1, spec1, pl.BlockSpec((1, tm, C), lambda b, s: (b, s, 1)),
                  pl.BlockSpec((1, tm, 2 * C), lambda b, s: (b, s, 0)), spec1, spec1, spec1,
                  pl.BlockSpec((1, tm, MLSTM_DIM), lambda b, s: (b, s, 3)), pl.BlockSpec(tab.shape, lambda b, s: (0, 0))],
        out_specs=pl.BlockSpec((1, tm, C + MLSTM_DIM), lambda b, s: (b, s, 0)),
        compiler_params=_params("arbitrary", "arbitrary"),
        name="rec_post",
    )(y, rv, rv, k_eff, gate, h_f, h_b, pm, tab)


def _to_state_lanes(x):
    B, T, _ = x.shape
    return jnp.transpose(x.reshape(B, T, 2, RWKV_HEADS, RWKV_HEAD), (1, 4, 2, 0, 3)).reshape(T, RWKV_HEAD, LANES)


def _recurrent_mixers(x, modtab, w_in, n_ctx, mu, w0, w1, w2, a0, a1, a2, g1, g2, kvec, r_k, gn, gate_b, norm_g):
    B, T, _ = x.shape
    assert 2 * B * RWKV_HEADS == LANES
    n_main = RWKV_IN + 4 * MLSTM_DIM
    w = jnp.pad(w_in, ((0, 0), (0, n_main + LANES - w_in.shape[1]))).astype(BF16)
    pr, pm, pg = _in_proj(x, modtab, w, ((0, RWKV_IN), (RWKV_IN, n_main), (n_main, n_main + LANES)), (F32, F32, F32))
    rv, kkn, decay, k_eff, kka, gate = _rwkv_prep(pr, n_ctx, mu, w0, w1, w2, a0, a1, a2, g1, g2, kvec)
    scan_in = tuple(_to_state_lanes(a) for a in (rv, decay, k_eff, kkn, kka))
    h_f, h_b = _mlstm(pm, pg[..., :4 * MLSTM_HEADS], gate_b, n_ctx)
    scan_in, h_f, h_b = lax.optimization_barrier((scan_in, h_f, h_b))
    yf, yb = _rwkv_scan(scan_in, n_ctx)
    half = LANES // 2
    y = yf[:, :, :half] + yb[:, :, half:]
    y = jnp.transpose(y.reshape(T, RWKV_HEAD, B, RWKV_HEADS), (2, 0, 3, 1)).reshape(B, T, RWKV_DIM)
    return _rec_post(y, rv, k_eff, gate, h_f, h_b, pm, gn, r_k, norm_g)


def kernel(x, c, ctx, c_ctx, ada_w, ada_b, ln_g, ln_b, mix_w_out, att_w_in, na_rpb, qk_gain, rec_w_in, rwkv_mu, rwkv_w0, rwkv_w1, rwkv_w2, rwkv_a0, rwkv_a1, rwkv_a2, rwkv_g1, rwkv_g2, rwkv_kvec, rwkv_rk, rwkv_gn, mlstm_gate_b, mlstm_norm, moe_router, moe_bias, moe_w1, moe_w3, moe_w2, shared_w1, shared_w3, shared_w2):
    B, S, D = x.shape
    n_ctx = ctx.shape[1]
    assert D == D_MODEL and n_ctx % TOKEN_TILE == 0 and S % TOKEN_TILE == 0
    xs = jnp.concatenate([ctx, x], axis=1)
    mods = _ada_modulation(c, c_ctx, ada_w, ada_b)
    for i in range(DEPTH):
        last = i == DEPTH - 1
        j = i // 2
        mod = mods[i, :B].reshape(B, 6, D)
        mod_c = jnp.broadcast_to(mods[i, B].reshape(1, 6, D), (B, 6, D))
        modtab = jnp.pad(jnp.stack([mod_c, mod], axis=1), ((0, 0), (0, 0), (0, MOD_ROWS - 6), (0, 0)))
        if i % 2 == 0:
            m = _attention_mixers(xs, modtab, att_w_in[j], na_rpb[j], qk_gain[j], n_ctx)
            w_out = _attention_w_out(mix_w_out[i])
        else:
            m = _recurrent_mixers(xs, modtab, rec_w_in[j], n_ctx, rwkv_mu[j], rwkv_w0[j], rwkv_w1[j], rwkv_w2[j],
                                  rwkv_a0[j], rwkv_a1[j], rwkv_a2[j], rwkv_g1[j], rwkv_g2[j], rwkv_kvec[j],
                                  rwkv_rk[j], rwkv_gn[j], mlstm_gate_b[j], mlstm_norm[j])
            w_out = mix_w_out[i]
        first_tile = n_ctx // TOKEN_TILE if last else 0
        x1, h2 = _out_proj(m, xs, modtab, w_out.astype(BF16), jnp.stack([ln_g[i, 0], ln_b[i, 0]]), first_tile)
        xs = _moe_block(h2, x1, modtab, jnp.stack([ln_g[i, 1], ln_b[i, 1]]), first_tile, i, moe_router[i], moe_bias[i],
                        moe_w1, moe_w3, moe_w2, shared_w1[i], shared_w3[i], shared_w2[i])
    return xs
```

```python
import functools

import jax
import jax.numpy as jnp
from jax import lax
from jax.experimental import pallas as pl
from jax.experimental.pallas import tpu as pltpu

D_MODEL = 1024
DEPTH = 4
GRID_W = 64
HEAD_DIM = 64
NA_HEADS = 8
NA_WIN_ROWS = 8
NA_WIN_COLS = 16
GQA_Q_HEADS = 8
GQA_KV_HEADS = 2
ROPE_THETA = 10000.0
ROPE_AXIS_DIM = HEAD_DIM // 2
NA_DIM = NA_HEADS * HEAD_DIM
GQA_Q_DIM = GQA_Q_HEADS * HEAD_DIM
GQA_KV_DIM = GQA_KV_HEADS * HEAD_DIM
RWKV_HEADS = 8
RWKV_HEAD = 64
RWKV_DIM = RWKV_HEADS * RWKV_HEAD
RWKV_GN_EPS = 64e-5
RWKV_IN = 4 * RWKV_DIM
MLSTM_HEADS = 4
MLSTM_HEAD = 128
MLSTM_DIM = MLSTM_HEADS * MLSTM_HEAD
MLSTM_CHUNK = 64
N_EXPERTS = 64
TOP_K = 6
ROUTED_SCALE = 2.5
DN_ALPHA = (2 * DEPTH) ** 0.25
LN_EPS = 1e-5
NORM_EPS = 1e-6
F32 = jnp.float32
BF16 = jnp.bfloat16

LANES = 128
SUBLANES = 8
VMEM_LIMIT = 48 * 1024 * 1024
TOKEN_TILE = 256
RWKV_TIME_BLOCK = 16
GQA_Q_TILE = 256
MOE_ROW_BLOCK = 512
MOE_SLOTS = 8
MOE_CHUNK = SUBLANES
MOE_STAGE_ROWS = TOKEN_TILE * TOP_K + N_EXPERTS * MOE_CHUNK
MOD_ROWS = 8


def _params(*semantics):
    return pltpu.CompilerParams(dimension_semantics=semantics, vmem_limit_bytes=VMEM_LIMIT)


def _mod_spec(tiles_per_batch, first_tile, flat):
    if flat:
        idx = lambda i, *_: (i // tiles_per_batch, jnp.minimum(i % tiles_per_batch + first_tile, 1), 0, 0)
    else:
        idx = lambda b, s, *_: (b, jnp.minimum(s + first_tile, 1), 0, 0)
    return pl.BlockSpec((1, 1, MOD_ROWS, D_MODEL), idx)


def _layer_norm_rows(z, g, b):
    mu = jnp.mean(z, axis=-1, keepdims=True)
    zc = z - mu
    var = jnp.mean(zc * zc, axis=-1, keepdims=True)
    return zc * lax.rsqrt(var + LN_EPS) * g + b


ADA_ROWS = 16


def _ada_body(c_ref, w_ref, b_ref, o_ref):
    x = c_ref[...]
    a = (x * jax.nn.sigmoid(x)).astype(BF16)
    o_ref[0] = jnp.dot(a, w_ref[0].astype(BF16), preferred_element_type=F32) + b_ref[0]


def _ada_modulation(c, c_ctx, ada_w, ada_b):
    B, D = c.shape
    depth, _, n = ada_w.shape
    cond = jnp.concatenate([c, c_ctx[None], jnp.zeros((ADA_ROWS - B - 1, D), F32)], 0)
    return pl.pallas_call(
        _ada_body,
        out_shape=jax.ShapeDtypeStruct((depth, ADA_ROWS, n), F32),
        grid=(depth, n // D),
        in_specs=[pl.BlockSpec((ADA_ROWS, D), lambda l, j: (0, 0)),
                  pl.BlockSpec((1, D, D), lambda l, j: (l, 0, j)),
                  pl.BlockSpec((1, 1, D), lambda l, j: (l, 0, j))],
        out_specs=pl.BlockSpec((1, ADA_ROWS, D), lambda l, j: (l, 0, j)),
        compiler_params=_params("arbitrary", "arbitrary"),
        name="ada_modulation",
    )(cond, ada_w, ada_b.reshape(depth, 1, n))
def _in_proj_body(x_ref, mod_ref, w_ref, *out_refs, splits):
    h = (x_ref[0] * (1.0 + mod_ref[0, 0, 1:2, :]) + mod_ref[0, 0, 0:1, :]).astype(BF16)
    for o_ref, (c0, c1) in zip(out_refs, splits):
        o_ref[0] = jnp.dot(h, w_ref[:, c0:c1], preferred_element_type=F32).astype(o_ref.dtype)


def _in_proj(x, modtab, w, splits, dtypes):
    B, T, D = x.shape
    tm = TOKEN_TILE
    outs = tuple(jax.ShapeDtypeStruct((B, T, c1 - c0), dt) for (c0, c1), dt in zip(splits, dtypes))
    return pl.pallas_call(
        functools.partial(_in_proj_body, splits=splits),
        out_shape=outs,
        grid=(B, T // tm),
        in_specs=[pl.BlockSpec((1, tm, D), lambda b, s: (b, s, 0)), _mod_spec(T // tm, 0, False),
                  pl.BlockSpec(w.shape, lambda b, s: (0, 0))],
        out_specs=tuple(pl.BlockSpec((1, tm, c1 - c0), lambda b, s: (b, s, 0)) for c0, c1 in splits),
        compiler_params=_params("arbitrary", "arbitrary"),
        name="in_proj",
    )(x, modtab, w)


def _out_proj_body(m_ref, x_ref, mod_ref, w_ref, ln_ref, x1_ref, h2_ref):
    y = jnp.dot(m_ref[0], w_ref[...], preferred_element_type=F32)
    mod = mod_ref[0, 0]
    x1 = _layer_norm_rows(DN_ALPHA * x_ref[0] + mod[2:3, :] * y, ln_ref[0:1, :], ln_ref[1:2, :])
    x1_ref[0] = x1
    h2_ref[0] = x1 * (1.0 + mod[4:5, :]) + mod[3:4, :]


def _out_proj(m, x, modtab, w, ln, first_tile):
    B, T, D = x.shape
    tm = TOKEN_TILE
    n_tiles = T // tm - first_tile
    rows = lambda b, s: (b, s + first_tile, 0)
    out_sds = jax.ShapeDtypeStruct((B, n_tiles * tm, D), F32)
    return pl.pallas_call(
        _out_proj_body,
        out_shape=(out_sds, out_sds),
        grid=(B, n_tiles),
        in_specs=[pl.BlockSpec((1, tm, m.shape[-1]), rows), pl.BlockSpec((1, tm, D), rows),
                  _mod_spec(T // tm, first_tile, False),
                  pl.BlockSpec(w.shape, lambda b, s: (0, 0)), pl.BlockSpec(ln.shape, lambda b, s: (0, 0))],
        out_specs=(pl.BlockSpec((1, tm, D), lambda b, s: (b, s, 0)),) * 2,
        compiler_params=_params("arbitrary", "arbitrary"),
        name="out_proj",
    )(m, x, modtab, w, ln)


def _rwkv_scan_body(rvf, wf, kf, nf, bf, rvb, wb, kb, nb, bb, yf_ref, yb_ref, state_ref, *, tc):
    @pl.when(pl.program_id(0) == 0)
    def _():
        state_ref[...] = jnp.zeros_like(state_ref)

    half = LANES // 2
    fwd_lane = lax.broadcasted_iota(jnp.int32, (RWKV_HEAD, LANES), 1) < half

    def step(j, carry):
        jb = tc - 1 - j

        def sel(f, b):
            return jnp.where(fwd_lane, f[j], b[jb])

        w, k, kkn, bv = sel(wf, wb), sel(kf, kb), sel(nf, nb), sel(bf, bb)
        rv_f, rv_b = rvf[j], rvb[jb]
        r = jnp.where(fwd_lane, rv_f, pltpu.roll(rv_b, half, axis=1))
        v = jnp.where(fwd_lane, pltpu.roll(rv_f, half, axis=1), rv_b)
        for vi in range(RWKV_HEAD):
            s = state_ref[vi]
            sa = jnp.sum(s * kkn, axis=0, keepdims=True)
            s2 = s * w + sa * bv + v[vi:vi + 1, :] * k
            state_ref[vi] = s2
            yrow = jnp.sum(s2 * r, axis=0, keepdims=True)
            yf_ref[j, pl.ds(vi, 1), :] = yrow
            yb_ref[jb, pl.ds(vi, 1), :] = yrow
        return carry

    lax.fori_loop(0, tc, step, 0)


def _rwkv_scan(xs, n_ctx):
    T = xs[0].shape[0]
    tc = RWKV_TIME_BLOCK
    assert T % tc == 0 and n_ctx % tc == 0
    nc, ncc = T // tc, n_ctx // tc
    blk = (tc, RWKV_HEAD, LANES)
    fwd = lambda c: (c, 0, 0)
    bwd = lambda c: (jnp.where(c < ncc, ncc - 1 - c, nc - 1 - (c - ncc)), 0, 0)
    out_sds = jax.ShapeDtypeStruct((T, RWKV_HEAD, LANES), F32)
    return pl.pallas_call(
        functools.partial(_rwkv_scan_body, tc=tc),
        out_shape=(out_sds, out_sds),
        grid=(nc,),
        in_specs=[pl.BlockSpec(blk, fwd)] * len(xs) + [pl.BlockSpec(blk, bwd)] * len(xs),
        out_specs=(pl.BlockSpec(blk, fwd), pl.BlockSpec(blk, bwd)),
        scratch_shapes=[pltpu.VMEM((RWKV_HEAD, RWKV_HEAD, LANES), F32)],
        compiler_params=_params("arbitrary"),
        name="rwkv_scan",
    )(*xs, *xs)


def _route_body(x_ref, rwt_ref, rb_ref, upper_ref, lower_ref, q_ref, g_ref, cnt_ref):
    tm = x_ref.shape[0]
    logits = lax.dot_general(rwt_ref[...], x_ref[...], (((1,), (1,)), ((), ())), preferred_element_type=F32,
                             precision=lax.Precision.HIGHEST)
    scores = jax.nn.sigmoid(logits)
    sel = scores + rb_ref[...]
    eidx = lax.broadcasted_iota(jnp.int32, (N_EXPERTS, tm), 0)
    slot = lax.broadcasted_iota(jnp.int32, (MOE_SLOTS, tm), 0)
    onehots = []
    for _ in range(TOP_K):
        m = jnp.max(sel, axis=0, keepdims=True)
        ij = jnp.min(jnp.where(sel == m, eidx, N_EXPERTS), axis=0, keepdims=True)
        oh = eidx == ij
        onehots.append(oh)
        sel = jnp.where(oh, -jnp.inf, sel)
    mask = functools.reduce(jnp.logical_or, onehots)
    maskf = jnp.where(mask, 1.0, 0.0)
    gsum = jnp.sum(jnp.where(mask, scores, 0.0), axis=0, keepdims=True)
    gates = scores / gsum * ROUTED_SCALE
    cnt = jnp.sum(maskf, axis=1, keepdims=True)
    cnt_pad = jnp.ceil(cnt * (1.0 / MOE_CHUNK)) * MOE_CHUNK
    lrank = jnp.dot(maskf.astype(BF16), upper_ref[...], preferred_element_type=F32)
    loff = jnp.dot(lower_ref[...], jnp.broadcast_to(cnt_pad, (N_EXPERTS, LANES)).astype(BF16),
                   preferred_element_type=F32)[:, 0:1]
    q = loff + lrank
    q8 = jnp.full((MOE_SLOTS, tm), -1.0, F32)
    g8 = jnp.zeros((MOE_SLOTS, tm), F32)
    for j, oh in enumerate(onehots):
        q8 = jnp.where(slot == j, jnp.sum(jnp.where(oh, q, 0.0), axis=0, keepdims=True), q8)
        g8 = jnp.where(slot == j, jnp.sum(jnp.where(oh, gates, 0.0), axis=0, keepdims=True), g8)
    q_ref[0] = q8.astype(jnp.int32)
    g_ref[0] = g8
    cnt_ref[0] = jnp.broadcast_to(cnt, (N_EXPERTS, LANES))


def _moe_route(tokens, router_w, router_b):
    n, d = tokens.shape
    tm = TOKEN_TILE
    nt = n // tm
    ar = jnp.arange(tm)
    upper = (ar[:, None] < ar[None, :]).astype(BF16)
    ae = jnp.arange(N_EXPERTS)
    lower = (ae[:, None] > ae[None, :]).astype(BF16)
    const = lambda shape: pl.BlockSpec(shape, lambda i: (0,) * len(shape))
    return pl.pallas_call(
        _route_body,
        out_shape=(jax.ShapeDtypeStruct((nt, MOE_SLOTS, tm), jnp.int32), jax.ShapeDtypeStruct((nt, MOE_SLOTS, tm), F32),
                   jax.ShapeDtypeStruct((nt, N_EXPERTS, LANES), F32)),
        grid=(nt,),
        in_specs=[pl.BlockSpec((tm, d), lambda i: (i, 0)), const((N_EXPERTS, d)), const((N_EXPERTS, 1)),
                  const((tm, tm)), const((N_EXPERTS, N_EXPERTS))],
        out_specs=(pl.BlockSpec((1, MOE_SLOTS, tm), lambda i: (i, 0, 0)),
                   pl.BlockSpec((1, MOE_SLOTS, tm), lambda i: (i, 0, 0)),
                   pl.BlockSpec((1, N_EXPERTS, LANES), lambda i: (i, 0, 0))),
        compiler_params=_params("arbitrary"),
        name="moe_router",
    )(tokens, router_w.T, router_b.reshape(N_EXPERTS, 1), upper, lower)


def _pack_bf16_pairs(x, exact=False):
    m = x.shape[1] // 2
    rounded = (lambda a: a) if exact else (lambda a: a.astype(BF16).astype(F32))
    bits = lambda a: lax.bitcast_convert_type(rounded(a), jnp.uint32)
    return (bits(x[:, :m]) & jnp.uint32(0xFFFF0000)) | (bits(x[:, m:]) >> 16)


def _unpack_bf16_pairs(u):
    hi = lax.bitcast_convert_type(u & jnp.uint32(0xFFFF0000), F32)
    lo = lax.bitcast_convert_type(u << 16, F32)
    return jnp.concatenate([hi.astype(BF16), lo.astype(BF16)], axis=1)


def _chunk_loops(i, base_ref, nchunk_ref, loff_ref, copy):
    def per_expert(e, total):
        k = i * N_EXPERTS + e
        n, base, lo = nchunk_ref[k], base_ref[k], loff_ref[k]

        def piece(c, carry):
            copy(pl.multiple_of(lo + c * MOE_CHUNK, MOE_CHUNK), pl.multiple_of(base + c * MOE_CHUNK, MOE_CHUNK)).start()
            return carry

        lax.fori_loop(0, n, piece, 0)
        return total + n

    return lax.fori_loop(0, N_EXPERTS, per_expert, 0)


def _dispatch_body(base_ref, nchunk_ref, loff_ref, bv_ref, x_ref, q_ref, xs_hbm, stage_ref, zero_ref, sem_z, sem):
    i = pl.program_id(0)
    br = zero_ref.shape[0]
    n_blocks = xs_hbm.shape[0] // br

    @pl.when(i == 0)
    def _():
        zero_ref[...] = jnp.zeros_like(zero_ref)

        def fill_copy(b):
            return pltpu.make_async_copy(zero_ref, xs_hbm.at[pl.ds(b * br, br)], sem_z)

        def fill(b, carry):
            @pl.when(bv_ref[b] < br)
            def _():
                fill_copy(b).start()
            return carry

        def fill_wait(b, carry):
            @pl.when(bv_ref[b] < br)
            def _():
                fill_copy(b).wait()
            return carry

        lax.fori_loop(0, n_blocks, fill, 0)
        lax.fori_loop(0, n_blocks, fill_wait, 0)

    rows, tm = stage_ref.shape[0], x_ref.shape[0]
    q8 = q_ref[0]
    pos = lax.broadcasted_iota(jnp.int32, (rows, tm), 0)
    hit = functools.reduce(jnp.logical_or, [pos == q8[j:j + 1, :] for j in range(TOP_K)])
    perm = jnp.where(hit, 1.0, 0.0).astype(BF16)
    stage_ref[...] = _pack_bf16_pairs(jnp.dot(perm, x_ref[...].astype(BF16), preferred_element_type=F32), exact=True)

    def copy(local_row, global_row):
        return pltpu.make_async_copy(stage_ref.at[pl.ds(local_row, MOE_CHUNK)], xs_hbm.at[pl.ds(global_row, MOE_CHUNK)], sem)

    total = _chunk_loops(i, base_ref, nchunk_ref, loff_ref, copy)

    def drain(c, carry):
        copy(0, 0).wait()
        return carry

    lax.fori_loop(0, total, drain, 0)


def _moe_dispatch(tokens, q_rows, dest_base, nchunk, loff, block_valid):
    n, d = tokens.shape
    tm, br = TOKEN_TILE, MOE_ROW_BLOCK
    n_rows = block_valid.shape[0] * br
    return pl.pallas_call(
        _dispatch_body,
        out_shape=jax.ShapeDtypeStruct((n_rows, d // 2), jnp.uint32),
        grid_spec=pltpu.PrefetchScalarGridSpec(
            num_scalar_prefetch=4,
            grid=(n // tm,),
            in_specs=[pl.BlockSpec((tm, d), lambda i, *_: (i, 0)),
                      pl.BlockSpec((1, MOE_SLOTS, tm), lambda i, *_: (i, 0, 0))],
            out_specs=pl.BlockSpec(memory_space=pl.ANY),
            scratch_shapes=[pltpu.VMEM((MOE_STAGE_ROWS, d // 2), jnp.uint32), pltpu.VMEM((br, d // 2), jnp.uint32),
                            pltpu.SemaphoreType.DMA, pltpu.SemaphoreType.DMA]),
        compiler_params=_params("arbitrary"),
        name="moe_dispatch",
    )(dest_base, nchunk, loff, block_valid, tokens, q_rows)


def _expert_body(be_ref, bv_ref, x_ref, w1_ref, w3_ref, w2_ref, y_ref, w1b, w3b, w2b):
    i = pl.program_id(0)
    valid = bv_ref[i]

    @pl.when(jnp.logical_or(i == 0, be_ref[i] != be_ref[jnp.maximum(i - 1, 0)]))
    def _():
        w1b[...] = w1_ref[0, 0].astype(BF16)
        w3b[...] = w3_ref[0, 0].astype(BF16)
        w2b[...] = w2_ref[0, 0].astype(BF16)

    @pl.when(valid > 0)
    def _():
        x = _unpack_bf16_pairs(x_ref[...])
        h1 = jnp.dot(x, w1b[...], preferred_element_type=F32)
        h3 = jnp.dot(x, w3b[...], preferred_element_type=F32)
        a = (h1 * jax.nn.sigmoid(h1) * h3).astype(BF16)
        y_ref[...] = _pack_bf16_pairs(jnp.dot(a, w2b[...], preferred_element_type=F32))

    @pl.when(valid <= 0)
    def _():
        y_ref[...] = jnp.zeros_like(y_ref)


def _moe_experts(xs, block_expert, block_valid, layer, w1, w3, w2):
    n_rows, dp = xs.shape
    br = MOE_ROW_BLOCK
    d, ff = w1.shape[-2:]
    return pl.pallas_call(
        _expert_body,
        out_shape=jax.ShapeDtypeStruct((n_rows, dp), jnp.uint32),
        grid_spec=pltpu.PrefetchScalarGridSpec(
            num_scalar_prefetch=2,
            grid=(n_rows // br,),
            in_specs=[pl.BlockSpec((br, dp), lambda i, be, bv: (i, 0)),
                      pl.BlockSpec((1, 1, d, ff), lambda i, be, bv: (layer, be[i], 0, 0)),
                      pl.BlockSpec((1, 1, d, ff), lambda i, be, bv: (layer, be[i], 0, 0)),
                      pl.BlockSpec((1, 1, ff, d), lambda i, be, bv: (layer, be[i], 0, 0))],
            out_specs=pl.BlockSpec((br, dp), lambda i, be, bv: (i, 0)),
            scratch_shapes=[pltpu.VMEM((d, ff), BF16), pltpu.VMEM((d, ff), BF16), pltpu.VMEM((ff, d), BF16)]),
        compiler_params=_params("arbitrary"),
        name="moe_experts",
    )(block_expert, block_valid, xs, w1, w3, w2)


def _combine_body(base_ref, nchunk_ref, loff_ref, h_ref, q_ref, g_ref, ys_hbm, sw1_ref, sw3_ref, sw2_ref,
                  x1_ref, mod_ref, ln_ref, out_ref, stage_ref, sem):
    i = pl.program_id(0)

    @pl.when(i == 0)
    def _():
        stage_ref[...] = jnp.zeros_like(stage_ref)

    def copy(local_row, global_row):
        return pltpu.make_async_copy(ys_hbm.at[pl.ds(global_row, MOE_CHUNK)], stage_ref.at[pl.ds(local_row, MOE_CHUNK)], sem)

    total = _chunk_loops(i, base_ref, nchunk_ref, loff_ref, copy)

    h = h_ref[...].astype(BF16)
    h1 = jnp.dot(h, sw1_ref[...], preferred_element_type=F32)
    h3 = jnp.dot(h, sw3_ref[...], preferred_element_type=F32)
    y = jnp.dot((h1 * jax.nn.sigmoid(h1) * h3).astype(BF16), sw2_ref[...], preferred_element_type=F32)
    tm, rows = h_ref.shape[0], stage_ref.shape[0]
    q8, g8 = q_ref[0], g_ref[0]
    pos = lax.broadcasted_iota(jnp.int32, (tm, rows), 1)
    gate_mat = jnp.zeros((tm, rows), F32)
    for j in range(TOP_K):
        gate_mat = jnp.where(pos == q8[:, j:j + 1], g8[:, j:j + 1], gate_mat)

    def drain(c, carry):
        copy(0, 0).wait()
        return carry

    lax.fori_loop(0, total, drain, 0)
    y = y + jnp.dot(gate_mat.astype(BF16), _unpack_bf16_pairs(stage_ref[...]), preferred_element_type=F32)
    out_ref[...] = _layer_norm_rows(DN_ALPHA * x1_ref[...] + mod_ref[0, 0, 5:6, :] * y, ln_ref[0:1, :], ln_ref[1:2, :])


def _moe_combine(tokens, q_cols, g_cols, dest_base, nchunk, loff, ys, sw1, sw3, sw2, x1, modtab, ln, tiles_per_batch,
                 first_tile):
    n, d = tokens.shape
    tm = TOKEN_TILE
    ff = sw1.shape[-1]
    whole = lambda shape: pl.BlockSpec(shape, lambda i, *_: (0,) * len(shape))
    rows = pl.BlockSpec((tm, d), lambda i, *_: (i, 0))
    slots = pl.BlockSpec((1, tm, MOE_SLOTS), lambda i, *_: (i, 0, 0))
    return pl.pallas_call(
        _combine_body,
        out_shape=jax.ShapeDtypeStruct((n, d), F32),
        grid_spec=pltpu.PrefetchScalarGridSpec(
            num_scalar_prefetch=3,
            grid=(n // tm,),
            in_specs=[rows, slots, slots, pl.BlockSpec(memory_space=pl.ANY),
                      whole((d, ff)), whole((d, ff)), whole((ff, d)),
                      rows, _mod_spec(tiles_per_batch, first_tile, True), whole(ln.shape)],
            out_specs=rows,
            scratch_shapes=[pltpu.VMEM((MOE_STAGE_ROWS, d // 2), jnp.uint32), pltpu.SemaphoreType.DMA]),
        compiler_params=_params("arbitrary"),
        name="moe_combine",
    )(dest_base, nchunk, loff, tokens, q_cols, g_cols, ys, sw1, sw3, sw2, x1, modtab, ln)


def _moe_block(h2, x1, modtab, ln, first_tile, layer, router_w, router_b, w1, w3, w2, sw1, sw3, sw2):
    B, Tp, d = h2.shape
    tokens = h2.reshape(B * Tp, d)
    n = B * Tp
    tm, br = TOKEN_TILE, MOE_ROW_BLOCK
    q_rows, g_rows, cnt = _moe_route(tokens, router_w, router_b)
    cnt = cnt[:, :, 0].astype(jnp.int32)
    nchunk = (cnt + MOE_CHUNK - 1) // MOE_CHUNK
    run = nchunk * MOE_CHUNK
    total = jnp.sum(run, axis=0)
    padded = (total + br - 1) // br * br
    p_end = jnp.cumsum(padded)
    offs = p_end - padded
    dest_base = (offs[None, :] + jnp.cumsum(run, axis=0) - run).reshape(-1).astype(jnp.int32)
    loff = jnp.cumsum(run, axis=1) - run
    nchunk, loff = nchunk.reshape(-1).astype(jnp.int32), loff.reshape(-1).astype(jnp.int32)
    n_blocks = -(-(n * TOP_K + (n // tm) * N_EXPERTS * (MOE_CHUNK - 1) + N_EXPERTS * (br - 1)) // br)
    blk_start = jnp.arange(n_blocks, dtype=jnp.int32) * br
    block_expert = jnp.minimum(jnp.sum(blk_start[:, None] >= p_end[None, :], axis=1), N_EXPERTS - 1).astype(jnp.int32)
    block_valid = jnp.clip(total[block_expert] - (blk_start - offs[block_expert]), 0, br).astype(jnp.int32)
    xs = _moe_dispatch(tokens, q_rows, dest_base, nchunk, loff, block_valid)
    ys = _moe_experts(xs, block_expert, block_valid, layer, w1, w3, w2)
    q_cols, g_cols = jnp.swapaxes(q_rows, 1, 2), jnp.swapaxes(g_rows, 1, 2)
    out = _moe_combine(tokens, q_cols, g_cols, dest_base, nchunk, loff, ys,
                       sw1.astype(BF16), sw3.astype(BF16), sw2.astype(BF16),
                       x1.reshape(n, d), modtab, ln, Tp // tm, first_tile)
    return out.reshape(B, Tp, d)


def _low_half():
    return lax.broadcasted_iota(jnp.int32, (1, LANES), 1) < LANES // 2


def _pair_attention(q, parts):
    low = _low_half()
    outs = []
    for use_low in (True, False):
        qm = jnp.where(low == use_low, q, jnp.zeros_like(q))
        scores = []
        for k, _, b_lo, b_hi in parts:
            s = lax.dot_general(qm, k, (((1,), (1,)), ((), ())), preferred_element_type=F32) * HEAD_DIM ** -0.5
            b = b_lo if use_low else b_hi
            scores.append(s if b is None else s + b)
        m = functools.reduce(jnp.maximum, [jnp.max(s, axis=-1, keepdims=True) for s in scores])
        den = 0.0
        num = 0.0
        for s, (_, v, _, _) in zip(scores, parts):
            p = jnp.exp(s - m)
            den = den + jnp.sum(p, axis=-1, keepdims=True)
            num = num + jnp.dot(p.astype(BF16), v, preferred_element_type=F32)
        outs.append(num / den)
    return jnp.where(low, outs[0], outs[1])


NA_Q_ROWS = 2
NA_BAND_ROWS = NA_WIN_ROWS + NA_Q_ROWS - 1


def _na_body(cls_ref, q_ref, k_ref, v_ref, bias_ref, o_ref, *, n_ctx, rows):
    s = pl.program_id(1)
    tq = q_ref.shape[1]
    ctx_steps = n_ctx // tq
    n_tiles = NA_DIM // LANES
    tile = lambda t: slice(t * LANES, (t + 1) * LANES)

    @pl.when(s < ctx_steps)
    def _():
        for t in range(n_tiles):
            part = (k_ref[0, 0:n_ctx, tile(t)], v_ref[0, 0:n_ctx, tile(t)], None, None)
            o_ref[0, :, tile(t)] = _pair_attention(q_ref[0, :, tile(t)], [part]).astype(o_ref.dtype)

    @pl.when(s >= ctx_steps)
    def _():
        first = (s - ctx_steps) * NA_Q_ROWS
        start = jnp.clip(first - NA_WIN_ROWS // 2, 0, rows - NA_BAND_ROWS)
        off = pl.multiple_of(n_ctx + start * GRID_W, GRID_W)
        band = pl.ds(off, NA_BAND_ROWS * GRID_W)
        for t in range(n_tiles):
            parts = [(k_ref[0, band, tile(t)], v_ref[0, band, tile(t)], bias_ref[0, 2 * t], bias_ref[0, 2 * t + 1]),
                     (k_ref[0, 0:n_ctx, tile(t)], v_ref[0, 0:n_ctx, tile(t)], None, None)]
            o_ref[0, :, tile(t)] = _pair_attention(q_ref[0, :, tile(t)], parts).astype(o_ref.dtype)


def _na_bias_table(rpb, rows):
    import numpy as np
    kc, W = NA_WIN_COLS, GRID_W
    cidx = np.arange(W)
    col_start = np.clip(cidx - kc // 2, 0, W - kc)
    col_in = (cidx[None, :] >= col_start[:, None]) & (cidx[None, :] < col_start[:, None] + kc)
    d_col = np.clip(cidx[None, :] - cidx[:, None], -(kc - 1), kc - 1) + kc - 1
    classes, class_of, d_rows, allowed = {}, [], [], []
    for step in range(rows // NA_Q_ROWS):
        first = step * NA_Q_ROWS
        band0 = min(max(first - NA_WIN_ROWS // 2, 0), rows - NA_BAND_ROWS)
        d_row = np.zeros((NA_Q_ROWS, NA_BAND_ROWS), np.int32)
        ok = np.zeros((NA_Q_ROWS, NA_BAND_ROWS), bool)
        for p in range(NA_Q_ROWS):
            win0 = min(max(first + p - NA_WIN_ROWS // 2, 0), rows - NA_WIN_ROWS)
            for j in range(NA_BAND_ROWS):
                ok[p, j] = win0 <= band0 + j < win0 + NA_WIN_ROWS
                d_row[p, j] = min(max(band0 + j - (first + p) + NA_WIN_ROWS - 1, 0), 2 * NA_WIN_ROWS - 2)
        key = (d_row.tobytes(), ok.tobytes())
        if key not in classes:
            classes[key] = len(classes)
            d_rows.append(d_row)
            allowed.append(ok)
        class_of.append(classes[key])
    d_rows, allowed = np.stack(d_rows), np.stack(allowed)
    tab = rpb[:, d_rows][..., d_col]
    mask = allowed[None, :, :, :, None, None] & col_in[None, None, None, None]
    tab = jnp.where(mask, tab, -jnp.inf)
    tab = jnp.transpose(tab, (1, 0, 2, 4, 3, 5))
    return tab.reshape(len(classes), NA_HEADS, NA_Q_ROWS * W, NA_BAND_ROWS * W), class_of


def _na_attention(pa, rpb, n_ctx):
    B, T, _ = pa.shape
    rows = (T - n_ctx) // GRID_W
    tq = NA_Q_ROWS * GRID_W
    assert rows % NA_Q_ROWS == 0 and rows >= NA_BAND_ROWS and n_ctx % tq == 0
    table, class_of = _na_bias_table(rpb, rows)
    step_class = jnp.array([0] * (n_ctx // tq) + class_of, jnp.int32)
    return pl.pallas_call(
        functools.partial(_na_body, n_ctx=n_ctx, rows=rows),
        out_shape=jax.ShapeDtypeStruct((B, T, NA_DIM), BF16),
        grid_spec=pltpu.PrefetchScalarGridSpec(
            num_scalar_prefetch=1,
            grid=(B, T // tq),
            in_specs=[pl.BlockSpec((1, tq, NA_DIM), lambda b, s, cls: (b, s, 0)),
                      pl.BlockSpec((1, T, NA_DIM), lambda b, s, cls: (b, 0, 1)),
                      pl.BlockSpec((1, T, NA_DIM), lambda b, s, cls: (b, 0, 2)),
                      pl.BlockSpec((1,) + table.shape[1:], lambda b, s, cls: (cls[s], 0, 0, 0))],
            out_specs=pl.BlockSpec((1, tq, NA_DIM), lambda b, s, cls: (b, s, 0))),
        compiler_params=_params("arbitrary", "arbitrary"),
        name="na_attention",
    )(step_class, pa, pa, pa, table)


def _rms_pair(x, gain):
    low = _low_half()
    sq = x * x
    s_lo = jnp.sum(jnp.where(low, sq, 0.0), axis=-1, keepdims=True)
    s_hi = jnp.sum(jnp.where(low, 0.0, sq), axis=-1, keepdims=True)
    ms = jnp.where(low, s_lo, s_hi) * (1.0 / HEAD_DIM)
    return x * lax.rsqrt(ms + NORM_EPS) * gain


def _rope_pair(x, cos, sin_signed):
    even = lax.broadcasted_iota(jnp.int32, (1, LANES), 1) % 2 == 0
    partner = jnp.where(even, pltpu.roll(x, LANES - 1, axis=1), pltpu.roll(x, 1, axis=1))
    return x * cos + partner * sin_signed


def _gqa_body(q_ref, k_ref, v_ref, cos_q, sin_q, cos_k, sin_k, gain_ref, o_ref, kn_ref, vn_ref, *, n_ctx):
    s = pl.program_id(1)
    tq = q_ref.shape[1]

    @pl.when(s == 0)
    def _():
        kn = _rope_pair(_rms_pair(k_ref[0], gain_ref[1:2, :]), cos_k[...], sin_k[...])
        vn = v_ref[0]
        low = _low_half()
        for src, dst in ((kn, kn_ref), (vn, vn_ref)):
            swapped = pltpu.roll(src, LANES // 2, axis=1)
            dst[0] = jnp.where(low, src, swapped).astype(BF16)
            dst[1] = jnp.where(low, swapped, src).astype(BF16)

    def run(n_keys):
        n_tiles = GQA_Q_DIM // LANES
        for t in range(n_tiles):
            g = t * GQA_KV_HEADS // n_tiles
            k, v = kn_ref[g, 0:n_keys, :], vn_ref[g, 0:n_keys, :]
            q = q_ref[0, :, t * LANES:(t + 1) * LANES]
            qn = _rope_pair(_rms_pair(q, gain_ref[0:1, :]), cos_q[...], sin_q[...]).astype(BF16)
            o_ref[0, :, t * LANES:(t + 1) * LANES] = _pair_attention(qn, [(k, v, None, None)]).astype(o_ref.dtype)

    @pl.when(s < n_ctx // tq)
    def _():
        run(n_ctx)

    @pl.when(s >= n_ctx // tq)
    def _():
        run(kn_ref.shape[1])


def _axial_rope(n_tokens):
    t = jnp.arange(n_tokens)
    row = (t // GRID_W).astype(F32)
    col = (t % GRID_W).astype(F32)
    inv = ROPE_THETA ** (-jnp.arange(0, ROPE_AXIS_DIM, 2, dtype=F32) / ROPE_AXIS_DIM)
    ang = jnp.concatenate([row[:, None] * inv, col[:, None] * inv], -1)
    return jnp.cos(ang), jnp.sin(ang)


def _gqa_rope_tables(T, n_ctx):
    cos, sin = _axial_rope(T - n_ctx)
    cos = jnp.concatenate([jnp.ones((n_ctx, ROPE_AXIS_DIM), F32), cos], 0)
    sin = jnp.concatenate([jnp.zeros((n_ctx, ROPE_AXIS_DIM), F32), sin], 0)
    cos = jnp.tile(jnp.repeat(cos, 2, axis=-1), (1, 2))
    sign = jnp.tile(jnp.array([-1.0, 1.0], F32), LANES // 2)
    sin = jnp.tile(jnp.repeat(sin, 2, axis=-1), (1, 2)) * sign
    return cos, sin


def _gqa_attention(pb, qk_gain, n_ctx):
    B, T, _ = pb.shape
    tq = GQA_Q_TILE
    cos, sin = _gqa_rope_tables(T, n_ctx)
    gain = jnp.tile(qk_gain, (1, 2))
    kv_blk = GQA_Q_DIM // GQA_KV_DIM
    return pl.pallas_call(
        functools.partial(_gqa_body, n_ctx=n_ctx),
        out_shape=jax.ShapeDtypeStruct((B, T, GQA_Q_DIM), BF16),
        grid=(B, T // tq),
        in_specs=[pl.BlockSpec((1, tq, GQA_Q_DIM), lambda b, s: (b, s, 0)),
                  pl.BlockSpec((1, T, GQA_KV_DIM), lambda b, s: (b, 0, kv_blk)),
                  pl.BlockSpec((1, T, GQA_KV_DIM), lambda b, s: (b, 0, kv_blk + 1)),
                  pl.BlockSpec((tq, LANES), lambda b, s: (s, 0)),
                  pl.BlockSpec((tq, LANES), lambda b, s: (s, 0)),
                  pl.BlockSpec((T, LANES), lambda b, s: (0, 0)),
                  pl.BlockSpec((T, LANES), lambda b, s: (0, 0)),
                  pl.BlockSpec((2, LANES), lambda b, s: (0, 0))],
        out_specs=pl.BlockSpec((1, tq, GQA_Q_DIM), lambda b, s: (b, s, 0)),
        scratch_shapes=[pltpu.VMEM((GQA_KV_HEADS, T, GQA_KV_DIM), BF16), pltpu.VMEM((GQA_KV_HEADS, T, GQA_KV_DIM), BF16)],
        compiler_params=_params("arbitrary", "arbitrary"),
        name="gqa_attention",
    )(pb, pb, pb, cos, sin, cos, sin, gain)


def _attention_mixers(x, modtab, w_in, rpb, qk_gain, n_ctx):
    w = w_in.astype(BF16)
    pa, pb = _in_proj(x, modtab, w, ((0, 3 * NA_DIM), (3 * NA_DIM, w.shape[1])), (BF16, F32))
    return jnp.concatenate([_na_attention(pa, rpb, n_ctx), _gqa_attention(pb, qk_gain, n_ctx)], -1)


def _log_sigmoid(x):
    return jnp.minimum(x, 0.0) - jnp.log(1.0 + jnp.exp(-jnp.abs(x)))


def _mlstm_body(qf, kf, vf, gcf, grf, ktf, qb, kb, vb, gcb, grb, ktb, bias_c, bias_r, hf_ref, hb_ref, c_ref, m_ref):
    @pl.when(pl.program_id(1) == 0)
    def _():
        c_ref[...] = jnp.zeros_like(c_ref)
        m_ref[...] = jnp.zeros_like(m_ref)

    L, H, W = MLSTM_CHUNK, MLSTM_HEADS, MLSTM_HEAD
    row = lax.broadcasted_iota(jnp.int32, (L, L), 0)
    col = lax.broadcasted_iota(jnp.int32, (L, L), 1)
    hi = lax.Precision.HIGHEST
    dirs = ((qf, kf, vf, gcf, grf, hf_ref, ktf), (qb, kb, vb, gcb, grb, hb_ref, ktb))
    combos = [(d, h) for d in range(2) for h in range(H)]
    head = lambda h: slice(h * W, (h + 1) * W)
    lanes = lambda column: jnp.broadcast_to(column, (L, LANES))
    gates = []
    for d, (_, _, _, gc_ref, gr_ref, _, _) in enumerate(dirs):
        seen = (col <= row) if d == 0 else (col >= row)
        seen_f = jnp.where(seen, 1.0, 0.0)
        g_col = gc_ref[0] + bias_c[...]
        g_row = gr_ref[0, 0] + bias_r[...]
        b_col = jnp.dot(seen_f, _log_sigmoid(g_col), preferred_element_type=F32, precision=hi)
        b_row = lax.dot_general(_log_sigmoid(g_row), seen_f, (((1,), (1,)), ((), ())), preferred_element_type=F32,
                                precision=hi)
        b_under_i = pltpu.roll(b_col, LANES - H, axis=1)
        run = g_col - b_under_i
        tok = lax.broadcasted_iota(jnp.int32, (L, LANES), 0)
        shift = 1
        while shift < L:
            if d == 0:
                run = jnp.maximum(run, jnp.where(tok >= shift, pltpu.roll(run, shift, axis=0), -jnp.inf))
            else:
                run = jnp.maximum(run, jnp.where(tok < L - shift, pltpu.roll(run, L - shift, axis=0), -jnp.inf))
            shift *= 2
        gates.append((seen, g_col, g_row, b_col, b_row, b_under_i + run))
    first = {}
    for d, h in combos:
        q_ref, k_ref = dirs[d][0], dirs[d][1]
        s_idx = d * H + h
        q = (q_ref[0, :, head(h)] * W ** -0.5).astype(BF16)
        state = c_ref[s_idx]
        qk = lax.dot_general(q, k_ref[0, :, head(h)].astype(BF16), (((1,), (1,)), ((), ())), preferred_element_type=F32)
        q_state = jnp.dot(q, state.astype(BF16), preferred_element_type=F32)
        first[d, h] = (dirs[d][6][0, 0, head(h), :].astype(BF16), state, qk, q_state)
    second = {}
    for d, h in combos:
        seen, g_col, g_row, b_col, b_row, m_intra = gates[d]
        _, _, qk, q_state = first[d, h]
        v = dirs[d][2][0, :, head(h)]
        gi, gf = d * 2 * H + h, d * 2 * H + H + h
        s_idx = d * H + h
        m_state = m_ref[s_idx:s_idx + 1, :]
        b_t, i_t, mi_t = lanes(b_col[:, gf:gf + 1]), lanes(g_col[:, gi:gi + 1]), lanes(m_intra[:, gi:gi + 1])
        b_end = b_t[L - 1:L, :] if d == 0 else b_t[0:1, :]
        d_inter = b_t + m_state
        m_t = jnp.maximum(d_inter, mi_t)
        d_intra = jnp.where(seen, b_t[:, :L] - b_row[gf:gf + 1, :] + g_row[gi:gi + 1, :], -jnp.inf)
        s = qk * jnp.exp(d_intra - m_t[:, :L])
        w_inter = jnp.exp(d_inter - m_t)
        d_state = b_end - b_t + i_t
        m_new = jnp.maximum(b_end + m_state, jnp.max(d_state, axis=0, keepdims=True))
        w_s = jnp.exp(d_state - m_new)
        w_c = jnp.exp(b_end + m_state - m_new)
        ones = jnp.ones((L, LANES), BF16)
        second[d, h] = (s.astype(BF16), jnp.concatenate([v.astype(BF16), ones], axis=1),
                        jnp.concatenate([(v * w_s).astype(BF16), w_s.astype(BF16)], axis=1),
                        w_inter, jnp.exp(-m_t), jnp.concatenate([w_c, w_c], axis=1), m_new)
    writes = []
    for d, h in combos:
        s16, v_ones, vw_ws, w_inter, floor, w_c, m_new = second[d, h]
        kt16, state, _, q_state = first[d, h]
        s_idx = d * H + h
        sv = jnp.dot(s16, v_ones, preferred_element_type=F32)
        num = sv[:, :W] + w_inter * q_state[:, :W]
        den = sv[:, W:] + w_inter * q_state[:, W:]
        writes.append((dirs[d][5].at[0, :, head(h)], num / jnp.maximum(jnp.abs(den), floor)))
        writes.append((c_ref.at[s_idx], w_c * state + jnp.dot(kt16, vw_ws, preferred_element_type=F32)))
        writes.append((m_ref.at[s_idx:s_idx + 1, :], m_new))
    for ref, value in writes:
        ref[...] = value


def _mlstm(pm, gates, gate_b, n_ctx):
    B, T, _ = pm.shape
    L = MLSTM_CHUNK
    nc, ncc = T // L, n_ctx // L
    ng = 4 * MLSTM_HEADS
    g_cols = jnp.pad(gates, ((0, 0), (0, 0), (0, LANES - ng)))
    g_rows = jnp.swapaxes(gates.reshape(B, nc, L, ng), 2, 3)
    k_t = jnp.swapaxes(pm[..., MLSTM_DIM:2 * MLSTM_DIM].reshape(B, nc, L, MLSTM_DIM), 2, 3)
    ktr = lambda order: pl.BlockSpec((1, 1, MLSTM_DIM, L), lambda b, c: (b, order(c), 0, 0))
    bias = gate_b.reshape(ng)
    bias_c = jnp.pad(bias, (0, LANES - ng)).reshape(1, LANES)
    bias_r = bias.reshape(ng, 1)
    fwd = lambda c: c
    bwd = lambda c: jnp.where(c < ncc, ncc - 1 - c, nc - 1 - (c - ncc))
    blk = (1, L, MLSTM_DIM)
    seq = lambda order, j: pl.BlockSpec(blk, lambda b, c: (b, order(c), j))
    gcol = lambda order: pl.BlockSpec((1, L, LANES), lambda b, c: (b, order(c), 0))
    grow = lambda order: pl.BlockSpec((1, 1, ng, L), lambda b, c: (b, order(c), 0, 0))
    n_state = 2 * MLSTM_HEADS
    out_sds = jax.ShapeDtypeStruct((B, T, MLSTM_DIM), F32)
    return pl.pallas_call(
        _mlstm_body,
        out_shape=(out_sds, out_sds),
        grid=(B, nc),
        in_specs=[seq(fwd, 0), seq(fwd, 1), seq(fwd, 2), gcol(fwd), grow(fwd), ktr(fwd),
                  seq(bwd, 0), seq(bwd, 1), seq(bwd, 2), gcol(bwd), grow(bwd), ktr(bwd),
                  pl.BlockSpec((1, LANES), lambda b, c: (0, 0)), pl.BlockSpec((ng, 1), lambda b, c: (0, 0))],
        out_specs=(pl.BlockSpec(blk, lambda b, c: (b, fwd(c), 0)), pl.BlockSpec(blk, lambda b, c: (b, bwd(c), 0))),
        scratch_shapes=[pltpu.VMEM((n_state, MLSTM_HEAD, MLSTM_HEAD + LANES), F32), pltpu.VMEM((n_state, LANES), F32)],
        compiler_params=_params("arbitrary", "arbitrary"),
        name="mlstm",
    )(pm, pm, pm, g_cols, g_rows, k_t, pm, pm, pm, g_cols, g_rows, k_t, bias_c, bias_r)


def _pair_sums(x):
    low = _low_half()
    s_lo = jnp.sum(jnp.where(low, x, 0.0), axis=-1, keepdims=True)
    s_hi = jnp.sum(jnp.where(low, 0.0, x), axis=-1, keepdims=True)
    return jnp.where(low, s_lo, s_hi)


def _rwkv_prep_body(p_ref, prev_ref, next_ref, tab_ref, tab2_ref, w1_ref, w2_ref, a1_ref, a2_ref, g1_ref, g2_ref,
                    rv_ref, n_ref, w_ref, k_ref, b_ref, gate_ref, *, n_ctx):
    s = pl.program_id(1)
    tm = p_ref.shape[1]
    C = RWKV_DIM
    ctx_tiles = n_ctx // tm
    x = p_ref[0]
    has_prev = jnp.logical_and(s != 0, s != ctx_tiles)
    has_next = jnp.logical_and(s != ctx_tiles - 1, s != pl.num_programs(1) - 1)
    prev_row = jnp.where(has_prev, prev_ref[0, SUBLANES - 1:SUBLANES, :], 0.0)
    next_row = jnp.where(has_next, next_ref[0, 0:1, :], 0.0)
    rowid = lax.broadcasted_iota(jnp.int32, (tm, 1), 0)
    up = jnp.where(rowid == 0, prev_row, pltpu.roll(x, 1, axis=0))
    dn = jnp.where(rowid == tm - 1, next_row, pltpu.roll(x, tm - 1, axis=0))
    d = 0.5 * (up + dn) - x
    part = lambda a, i: a[:, i * C:(i + 1) * C]
    mu = lambda i: tab_ref[i:i + 1, :]
    r = part(x, 0) + part(d, 0) * mu(0)
    k = part(x, 1) + part(d, 1) * mu(1)
    v = part(x, 2) + part(d, 2) * mu(2)
    z, dz = part(x, 3), part(d, 3)
    z_w, z_a, z_g = (z + dz * mu(3)).astype(BF16), (z + dz * mu(4)).astype(BF16), (z + dz * mu(5)).astype(BF16)
    lora = lambda t, w: jnp.dot(t.astype(BF16), w[...], preferred_element_type=F32)
    w_pre = tab2_ref[0:1, :] + lora(jnp.tanh(lora(z_w, w1_ref)), w2_ref)
    neg = -w_pre
    softplus = jnp.maximum(neg, 0.0) + jnp.log(1.0 + jnp.exp(-jnp.abs(neg)))
    decay = jnp.exp(-jnp.exp(-softplus - 0.5))
    iclr = jax.nn.sigmoid(tab2_ref[1:2, :] + lora(lora(z_a, a1_ref), a2_ref))
    gate_ref[0] = lora(jax.nn.sigmoid(lora(z_g, g1_ref)), g2_ref)
    kk = k * tab_ref[6:7, :]
    kk = jnp.concatenate(
        [kk[:, t * LANES:(t + 1) * LANES]
         * lax.rsqrt(jnp.maximum(_pair_sums(jnp.square(kk[:, t * LANES:(t + 1) * LANES])), 1e-24))
         for t in range(C // LANES)], axis=1)
    k2, kk2 = jnp.concatenate([k, k], axis=1), jnp.concatenate([kk, kk], axis=1)
    rv_ref[0] = jnp.concatenate([r, v], axis=1)
    n_ref[0] = -kk2
    w_ref[0] = decay
    k_ref[0] = k2 * (1.0 + (iclr - 1.0) * tab2_ref[2:3, :])
    b_ref[0] = kk2 * iclr


def _rwkv_prep(pr, n_ctx, mu, w0, w1, w2, a0, a1, a2, g1, g2, kvec):
    B, T, _ = pr.shape
    tm, C = TOKEN_TILE, RWKV_DIM
    per_tile = tm // SUBLANES
    tab = jnp.concatenate([mu, kvec[0:1], jnp.zeros((1, C), F32)], 0)
    cat = lambda a: jnp.concatenate([a[0], a[1]], -1)
    tab2 = jnp.concatenate([cat(w0)[None], cat(a0)[None], jnp.tile(kvec[1], 2)[None], jnp.zeros((5, 2 * C), F32)], 0)
    blockdiag = lambda a: jnp.concatenate([jnp.pad(a[0], ((0, 0), (0, C))), jnp.pad(a[1], ((0, 0), (C, 0)))], 0)
    gl = g1.shape[1]
    consts = (tab, tab2, cat(w1).astype(BF16), blockdiag(w2).astype(BF16), cat(a1).astype(BF16),
              blockdiag(a2).astype(BF16), jnp.pad(g1, ((0, 0), (0, LANES - gl))).astype(BF16),
              jnp.pad(g2, ((0, LANES - gl), (0, 0))).astype(BF16))
    const = lambda a: pl.BlockSpec(a.shape, lambda b, s: (0, 0))
    one = jax.ShapeDtypeStruct((B, T, C), F32)
    two = jax.ShapeDtypeStruct((B, T, 2 * C), F32)
    spec1 = pl.BlockSpec((1, tm, C), lambda b, s: (b, s, 0))
    spec2 = pl.BlockSpec((1, tm, 2 * C), lambda b, s: (b, s, 0))
    return pl.pallas_call(
        functools.partial(_rwkv_prep_body, n_ctx=n_ctx),
        out_shape=(two, two, two, two, two, one),
        grid=(B, T // tm),
        in_specs=[pl.BlockSpec((1, tm, 4 * C), lambda b, s: (b, s, 0)),
                  pl.BlockSpec((1, SUBLANES, 4 * C), lambda b, s: (b, jnp.maximum(s * per_tile - 1, 0), 0)),
                  pl.BlockSpec((1, SUBLANES, 4 * C), lambda b, s: (b, jnp.minimum((s + 1) * per_tile, T // SUBLANES - 1), 0))]
        + [const(a) for a in consts],
        out_specs=(spec2, spec2, spec2, spec2, spec2, spec1),
        compiler_params=_params("arbitrary", "arbitrary"),
        name="rwkv_prep",
    )(pr, pr, pr, *consts)


def _rec_post_body(y_ref, r_ref, v_ref, k_ref, gate_ref, hf_ref, hb_ref, o_ref, tab_ref, out_ref):
    C = RWKV_DIM
    for t in range(C // LANES):
        sl = slice(t * LANES, (t + 1) * LANES)
        y = y_ref[0, :, sl]
        yc = y - _pair_sums(y) * (1.0 / RWKV_HEAD)
        var = _pair_sums(yc * yc) * (1.0 / RWKV_HEAD)
        yn = yc * lax.rsqrt(var + RWKV_GN_EPS) * tab_ref[0:1, sl] + tab_ref[1:2, sl]
        k_sum = k_ref[0, :, sl] + k_ref[0, :, C + t * LANES:C + (t + 1) * LANES]
        bonus = _pair_sums(r_ref[0, :, sl] * k_sum * tab_ref[2:3, sl]) * v_ref[0, :, sl]
        out_ref[0, :, sl] = ((yn + bonus) * gate_ref[0, :, sl]).astype(out_ref.dtype)
    for t in range(MLSTM_HEADS):
        sl = slice(t * MLSTM_HEAD, (t + 1) * MLSTM_HEAD)
        h = hf_ref[0, :, sl] + hb_ref[0, :, sl]
        hn = h * lax.rsqrt(jnp.mean(h * h, axis=-1, keepdims=True) + NORM_EPS) * tab_ref[3:4, sl]
        out_ref[0, :, C + t * MLSTM_HEAD:C + (t + 1) * MLSTM_HEAD] = (hn * jax.nn.sigmoid(o_ref[0, :, sl])).astype(out_ref.dtype)


def _rec_post(y, rv, k_eff, gate, h_f, h_b, pm, gn, r_k, norm_g):
    B, T, C = y.shape
    tm = TOKEN_TILE
    tab = jnp.concatenate([gn, r_k.reshape(1, C), norm_g.reshape(1, C), jnp.zeros((4, C), F32)], 0)
    spec1 = pl.BlockSpec((1, tm, C), lambda b, s: (b, s, 0))
    return pl.pallas_call(
        _rec_post_body,
        out_shape=jax.ShapeDtypeStruct((B, T, C + MLSTM_DIM), BF16),
        grid=(B, T // tm),
        in_specs=[spec1, spec1, pl.BlockSpec((1, tm, C), lambda b, s: (b, s, 1)),
                  pl.BlockSpec((1, tm, 2 * C), lambda b, s: (b, s, 0)), spec1, spec1, spec1,
                  pl.BlockSpec((1, tm, MLSTM_DIM), lambda b, s: (b, s, 3)), pl.BlockSpec(tab.shape, lambda b, s: (0, 0))],
        out_specs=pl.BlockSpec((1, tm, C + MLSTM_DIM), lambda b, s: (b, s, 0)),
        compiler_params=_params("arbitrary", "arbitrary"),
        name="rec_post",
    )(y, rv, rv, k_eff, gate, h_f, h_b, pm, tab)


def _to_state_lanes(x):
    B, T, _ = x.shape
    return jnp.transpose(x.reshape(B, T, 2, RWKV_HEADS, RWKV_HEAD), (1, 4, 2, 0, 3)).reshape(T, RWKV_HEAD, LANES)


def _recurrent_mixers(x, modtab, w_in, n_ctx, mu, w0, w1, w2, a0, a1, a2, g1, g2, kvec, r_k, gn, gate_b, norm_g):
    B, T, _ = x.shape
    assert 2 * B * RWKV_HEADS == LANES
    n_main = RWKV_IN + 4 * MLSTM_DIM
    w = jnp.pad(w_in, ((0, 0), (0, n_main + LANES - w_in.shape[1]))).astype(BF16)
    pr, pm, pg = _in_proj(x, modtab, w, ((0, RWKV_IN), (RWKV_IN, n_main), (n_main, n_main + LANES)), (F32, F32, F32))
    rv, kkn, decay, k_eff, kka, gate = _rwkv_prep(pr, n_ctx, mu, w0, w1, w2, a0, a1, a2, g1, g2, kvec)
    scan_in = tuple(_to_state_lanes(a) for a in (rv, decay, k_eff, kkn, kka))
    h_f, h_b = _mlstm(pm, pg[..., :4 * MLSTM_HEADS], gate_b, n_ctx)
    scan_in, h_f, h_b = lax.optimization_barrier((scan_in, h_f, h_b))
    yf, yb = _rwkv_scan(scan_in, n_ctx)
    half = LANES // 2
    y = yf[:, :, :half] + yb[:, :, half:]
    y = jnp.transpose(y.reshape(T, RWKV_HEAD, B, RWKV_HEADS), (2, 0, 3, 1)).reshape(B, T, RWKV_DIM)
    return _rec_post(y, rv, k_eff, gate, h_f, h_b, pm, gn, r_k, norm_g)


def kernel(x, c, ctx, c_ctx, ada_w, ada_b, ln_g, ln_b, mix_w_out, att_w_in, na_rpb, qk_gain, rec_w_in, rwkv_mu, rwkv_w0, rwkv_w1, rwkv_w2, rwkv_a0, rwkv_a1, rwkv_a2, rwkv_g1, rwkv_g2, rwkv_kvec, rwkv_rk, rwkv_gn, mlstm_gate_b, mlstm_norm, moe_router, moe_bias, moe_w1, moe_w3, moe_w2, shared_w1, shared_w3, shared_w2):
    B, S, D = x.shape
    n_ctx = ctx.shape[1]
    assert D == D_MODEL and n_ctx % TOKEN_TILE == 0 and S % TOKEN_TILE == 0
    xs = jnp.concatenate([ctx, x], axis=1)
    mods = _ada_modulation(c, c_ctx, ada_w, ada_b)
    for i in range(DEPTH):
        last = i == DEPTH - 1
        j = i // 2
        mod = mods[i, :B].reshape(B, 6, D)
        mod_c = jnp.broadcast_to(mods[i, B].reshape(1, 6, D), (B, 6, D))
        modtab = jnp.pad(jnp.stack([mod_c, mod], axis=1), ((0, 0), (0, 0), (0, MOD_ROWS - 6), (0, 0)))
        if i % 2 == 0:
            m = _attention_mixers(xs, modtab, att_w_in[j], na_rpb[j], qk_gain[j], n_ctx)
        else:
            m = _recurrent_mixers(xs, modtab, rec_w_in[j], n_ctx, rwkv_mu[j], rwkv_w0[j], rwkv_w1[j], rwkv_w2[j],
                                  rwkv_a0[j], rwkv_a1[j], rwkv_a2[j], rwkv_g1[j], rwkv_g2[j], rwkv_kvec[j],
                                  rwkv_rk[j], rwkv_gn[j], mlstm_gate_b[j], mlstm_norm[j])
        first_tile = n_ctx // TOKEN_TILE if last else 0
        x1, h2 = _out_proj(m, xs, modtab, mix_w_out[i].astype(BF16), jnp.stack([ln_g[i, 0], ln_b[i, 0]]), first_tile)
        xs = _moe_block(h2, x1, modtab, jnp.stack([ln_g[i, 1], ln_b[i, 1]]), first_tile, i, moe_router[i], moe_bias[i],
                        moe_w1, moe_w3, moe_w2, shared_w1[i], shared_w3[i], shared_w2[i])
    return xs
```

```python
import functools

import jax
import jax.numpy as jnp
from jax import lax
from jax.experimental import pallas as pl
from jax.experimental.pallas import tpu as pltpu

D_MODEL = 1024
DEPTH = 4
GRID_W = 64
HEAD_DIM = 64
NA_HEADS = 8
NA_WIN_ROWS = 8
NA_WIN_COLS = 16
GQA_Q_HEADS = 8
GQA_KV_HEADS = 2
ROPE_THETA = 10000.0
ROPE_AXIS_DIM = HEAD_DIM // 2
NA_DIM = NA_HEADS * HEAD_DIM
GQA_Q_DIM = GQA_Q_HEADS * HEAD_DIM
GQA_KV_DIM = GQA_KV_HEADS * HEAD_DIM
RWKV_HEADS = 8
RWKV_HEAD = 64
RWKV_DIM = RWKV_HEADS * RWKV_HEAD
RWKV_GN_EPS = 64e-5
RWKV_IN = 4 * RWKV_DIM
MLSTM_HEADS = 4
MLSTM_HEAD = 128
MLSTM_DIM = MLSTM_HEADS * MLSTM_HEAD
MLSTM_CHUNK = 64
N_EXPERTS = 64
TOP_K = 6
ROUTED_SCALE = 2.5
DN_ALPHA = (2 * DEPTH) ** 0.25
LN_EPS = 1e-5
NORM_EPS = 1e-6
F32 = jnp.float32
BF16 = jnp.bfloat16

LANES = 128
SUBLANES = 8
VMEM_LIMIT = 48 * 1024 * 1024
TOKEN_TILE = 256
RWKV_TIME_BLOCK = 32
GQA_Q_TILE = 256
MOE_ROW_BLOCK = 512
MOE_SLOTS = 8
MOE_CHUNK = SUBLANES
MOE_STAGE_ROWS = TOKEN_TILE * TOP_K + N_EXPERTS * MOE_CHUNK
MOD_ROWS = 8


def _params(*semantics):
    return pltpu.CompilerParams(dimension_semantics=semantics, vmem_limit_bytes=VMEM_LIMIT)


def _mod_spec(tiles_per_batch, first_tile, flat):
    if flat:
        idx = lambda i, *_: (i // tiles_per_batch, jnp.minimum(i % tiles_per_batch + first_tile, 1), 0, 0)
    else:
        idx = lambda b, s, *_: (b, jnp.minimum(s + first_tile, 1), 0, 0)
    return pl.BlockSpec((1, 1, MOD_ROWS, D_MODEL), idx)


def _layer_norm_rows(z, g, b):
    mu = jnp.mean(z, axis=-1, keepdims=True)
    zc = z - mu
    var = jnp.mean(zc * zc, axis=-1, keepdims=True)
    return zc * lax.rsqrt(var + LN_EPS) * g + b


ADA_ROWS = 16


def _ada_body(c_ref, w_ref, b_ref, o_ref):
    x = c_ref[...]
    a = (x * jax.nn.sigmoid(x)).astype(BF16)
    o_ref[0] = jnp.dot(a, w_ref[0].astype(BF16), preferred_element_type=F32) + b_ref[0]


def _ada_modulation(c, c_ctx, ada_w, ada_b):
    B, D = c.shape
    depth, _, n = ada_w.shape
    cond = jnp.concatenate([c, c_ctx[None], jnp.zeros((ADA_ROWS - B - 1, D), F32)], 0)
    return pl.pallas_call(
        _ada_body,
        out_shape=jax.ShapeDtypeStruct((depth, ADA_ROWS, n), F32),
        grid=(depth, n // D),
        in_specs=[pl.BlockSpec((ADA_ROWS, D), lambda l, j: (0, 0)),
                  pl.BlockSpec((1, D, D), lambda l, j: (l, 0, j)),
                  pl.BlockSpec((1, 1, D), lambda l, j: (l, 0, j))],
        out_specs=pl.BlockSpec((1, ADA_ROWS, D), lambda l, j: (l, 0, j)),
        compiler_params=_params("arbitrary", "arbitrary"),
        name="ada_modulation",
    )(cond, ada_w, ada_b.reshape(depth, 1, n))
def _in_proj_body(x_ref, mod_ref, w_ref, *out_refs, splits):
    h = (x_ref[0] * (1.0 + mod_ref[0, 0, 1:2, :]) + mod_ref[0, 0, 0:1, :]).astype(BF16)
    for o_ref, (c0, c1) in zip(out_refs, splits):
        o_ref[0] = jnp.dot(h, w_ref[:, c0:c1], preferred_element_type=F32).astype(o_ref.dtype)


def _in_proj(x, modtab, w, splits, dtypes):
    B, T, D = x.shape
    tm = TOKEN_TILE
    outs = tuple(jax.ShapeDtypeStruct((B, T, c1 - c0), dt) for (c0, c1), dt in zip(splits, dtypes))
    return pl.pallas_call(
        functools.partial(_in_proj_body, splits=splits),
        out_shape=outs,
        grid=(B, T // tm),
        in_specs=[pl.BlockSpec((1, tm, D), lambda b, s: (b, s, 0)), _mod_spec(T // tm, 0, False),
                  pl.BlockSpec(w.shape, lambda b, s: (0, 0))],
        out_specs=tuple(pl.BlockSpec((1, tm, c1 - c0), lambda b, s: (b, s, 0)) for c0, c1 in splits),
        compiler_params=_params("arbitrary", "arbitrary"),
        name="in_proj",
    )(x, modtab, w)


def _out_proj_body(m_ref, x_ref, mod_ref, w_ref, ln_ref, x1_ref, h2_ref):
    y = jnp.dot(m_ref[0], w_ref[...], preferred_element_type=F32)
    mod = mod_ref[0, 0]
    x1 = _layer_norm_rows(DN_ALPHA * x_ref[0] + mod[2:3, :] * y, ln_ref[0:1, :], ln_ref[1:2, :])
    x1_ref[0] = x1
    h2_ref[0] = x1 * (1.0 + mod[4:5, :]) + mod[3:4, :]


def _out_proj(m, x, modtab, w, ln, first_tile):
    B, T, D = x.shape
    tm = TOKEN_TILE
    n_tiles = T // tm - first_tile
    rows = lambda b, s: (b, s + first_tile, 0)
    out_sds = jax.ShapeDtypeStruct((B, n_tiles * tm, D), F32)
    return pl.pallas_call(
        _out_proj_body,
        out_shape=(out_sds, out_sds),
        grid=(B, n_tiles),
        in_specs=[pl.BlockSpec((1, tm, m.shape[-1]), rows), pl.BlockSpec((1, tm, D), rows),
                  _mod_spec(T // tm, first_tile, False),
                  pl.BlockSpec(w.shape, lambda b, s: (0, 0)), pl.BlockSpec(ln.shape, lambda b, s: (0, 0))],
        out_specs=(pl.BlockSpec((1, tm, D), lambda b, s: (b, s, 0)),) * 2,
        compiler_params=_params("arbitrary", "arbitrary"),
        name="out_proj",
    )(m, x, modtab, w, ln)


def _rwkv_scan_body(rvf, wf, kf, nf, bf, rvb, wb, kb, nb, bb, yf_ref, yb_ref, state_ref, *, tc):
    @pl.when(pl.program_id(0) == 0)
    def _():
        state_ref[...] = jnp.zeros_like(state_ref)

    half = LANES // 2
    fwd_lane = lax.broadcasted_iota(jnp.int32, (RWKV_HEAD, LANES), 1) < half

    def step(j, carry):
        jb = tc - 1 - j

        def sel(f, b):
            return jnp.where(fwd_lane, f[j], b[jb])

        w, k, kkn, bv = sel(wf, wb), sel(kf, kb), sel(nf, nb), sel(bf, bb)
        rv_f, rv_b = rvf[j], rvb[jb]
        r = jnp.where(fwd_lane, rv_f, pltpu.roll(rv_b, half, axis=1))
        v = jnp.where(fwd_lane, pltpu.roll(rv_f, half, axis=1), rv_b)
        for vi in range(RWKV_HEAD):
            s = state_ref[vi]
            sa = jnp.sum(s * kkn, axis=0, keepdims=True)
            s2 = s * w + sa * bv + v[vi:vi + 1, :] * k
            state_ref[vi] = s2
            yrow = jnp.sum(s2 * r, axis=0, keepdims=True)
            yf_ref[j, pl.ds(vi, 1), :] = yrow
            yb_ref[jb, pl.ds(vi, 1), :] = yrow
        return carry

    lax.fori_loop(0, tc, step, 0)


def _rwkv_scan(xs, n_ctx):
    T = xs[0].shape[0]
    tc = RWKV_TIME_BLOCK
    assert T % tc == 0 and n_ctx % tc == 0
    nc, ncc = T // tc, n_ctx // tc
    blk = (tc, RWKV_HEAD, LANES)
    fwd = lambda c: (c, 0, 0)
    bwd = lambda c: (jnp.where(c < ncc, ncc - 1 - c, nc - 1 - (c - ncc)), 0, 0)
    out_sds = jax.ShapeDtypeStruct((T, RWKV_HEAD, LANES), F32)
    return pl.pallas_call(
        functools.partial(_rwkv_scan_body, tc=tc),
        out_shape=(out_sds, out_sds),
        grid=(nc,),
        in_specs=[pl.BlockSpec(blk, fwd)] * len(xs) + [pl.BlockSpec(blk, bwd)] * len(xs),
        out_specs=(pl.BlockSpec(blk, fwd), pl.BlockSpec(blk, bwd)),
        scratch_shapes=[pltpu.VMEM((RWKV_HEAD, RWKV_HEAD, LANES), F32)],
        compiler_params=_params("arbitrary"),
        name="rwkv_scan",
    )(*xs, *xs)


def _route_body(x_ref, rwt_ref, rb_ref, upper_ref, lower_ref, q_ref, g_ref, cnt_ref):
    tm = x_ref.shape[0]
    logits = lax.dot_general(rwt_ref[...], x_ref[...], (((1,), (1,)), ((), ())), preferred_element_type=F32,
                             precision=lax.Precision.HIGHEST)
    scores = jax.nn.sigmoid(logits)
    sel = scores + rb_ref[...]
    eidx = lax.broadcasted_iota(jnp.int32, (N_EXPERTS, tm), 0)
    slot = lax.broadcasted_iota(jnp.int32, (MOE_SLOTS, tm), 0)
    onehots = []
    for _ in range(TOP_K):
        m = jnp.max(sel, axis=0, keepdims=True)
        ij = jnp.min(jnp.where(sel == m, eidx, N_EXPERTS), axis=0, keepdims=True)
        oh = eidx == ij
        onehots.append(oh)
        sel = jnp.where(oh, -jnp.inf, sel)
    mask = functools.reduce(jnp.logical_or, onehots)
    maskf = jnp.where(mask, 1.0, 0.0)
    gsum = jnp.sum(jnp.where(mask, scores, 0.0), axis=0, keepdims=True)
    gates = scores / gsum * ROUTED_SCALE
    cnt = jnp.sum(maskf, axis=1, keepdims=True)
    cnt_pad = jnp.ceil(cnt * (1.0 / MOE_CHUNK)) * MOE_CHUNK
    lrank = jnp.dot(maskf.astype(BF16), upper_ref[...], preferred_element_type=F32)
    loff = jnp.dot(lower_ref[...], jnp.broadcast_to(cnt_pad, (N_EXPERTS, LANES)).astype(BF16),
                   preferred_element_type=F32)[:, 0:1]
    q = loff + lrank
    q8 = jnp.full((MOE_SLOTS, tm), -1.0, F32)
    g8 = jnp.zeros((MOE_SLOTS, tm), F32)
    for j, oh in enumerate(onehots):
        q8 = jnp.where(slot == j, jnp.sum(jnp.where(oh, q, 0.0), axis=0, keepdims=True), q8)
        g8 = jnp.where(slot == j, jnp.sum(jnp.where(oh, gates, 0.0), axis=0, keepdims=True), g8)
    q_ref[0] = q8.astype(jnp.int32)
    g_ref[0] = g8
    cnt_ref[0] = jnp.broadcast_to(cnt, (N_EXPERTS, LANES))


def _moe_route(tokens, router_w, router_b):
    n, d = tokens.shape
    tm = TOKEN_TILE
    nt = n // tm
    ar = jnp.arange(tm)
    upper = (ar[:, None] < ar[None, :]).astype(BF16)
    ae = jnp.arange(N_EXPERTS)
    lower = (ae[:, None] > ae[None, :]).astype(BF16)
    const = lambda shape: pl.BlockSpec(shape, lambda i: (0,) * len(shape))
    return pl.pallas_call(
        _route_body,
        out_shape=(jax.ShapeDtypeStruct((nt, MOE_SLOTS, tm), jnp.int32), jax.ShapeDtypeStruct((nt, MOE_SLOTS, tm), F32),
                   jax.ShapeDtypeStruct((nt, N_EXPERTS, LANES), F32)),
        grid=(nt,),
        in_specs=[pl.BlockSpec((tm, d), lambda i: (i, 0)), const((N_EXPERTS, d)), const((N_EXPERTS, 1)),
                  const((tm, tm)), const((N_EXPERTS, N_EXPERTS))],
        out_specs=(pl.BlockSpec((1, MOE_SLOTS, tm), lambda i: (i, 0, 0)),
                   pl.BlockSpec((1, MOE_SLOTS, tm), lambda i: (i, 0, 0)),
                   pl.BlockSpec((1, N_EXPERTS, LANES), lambda i: (i, 0, 0))),
        compiler_params=_params("arbitrary"),
        name="moe_router",
    )(tokens, router_w.T, router_b.reshape(N_EXPERTS, 1), upper, lower)


def _pack_bf16_pairs(x, exact=False):
    m = x.shape[1] // 2
    rounded = (lambda a: a) if exact else (lambda a: a.astype(BF16).astype(F32))
    bits = lambda a: lax.bitcast_convert_type(rounded(a), jnp.uint32)
    return (bits(x[:, :m]) & jnp.uint32(0xFFFF0000)) | (bits(x[:, m:]) >> 16)


def _unpack_bf16_pairs(u):
    hi = lax.bitcast_convert_type(u & jnp.uint32(0xFFFF0000), F32)
    lo = lax.bitcast_convert_type(u << 16, F32)
    return jnp.concatenate([hi.astype(BF16), lo.astype(BF16)], axis=1)


def _chunk_loops(i, base_ref, nchunk_ref, loff_ref, copy):
    def per_expert(e, total):
        k = i * N_EXPERTS + e
        n, base, lo = nchunk_ref[k], base_ref[k], loff_ref[k]

        def piece(c, carry):
            copy(pl.multiple_of(lo + c * MOE_CHUNK, MOE_CHUNK), pl.multiple_of(base + c * MOE_CHUNK, MOE_CHUNK)).start()
            return carry

        lax.fori_loop(0, n, piece, 0)
        return total + n

    return lax.fori_loop(0, N_EXPERTS, per_expert, 0)


def _dispatch_body(base_ref, nchunk_ref, loff_ref, bv_ref, x_ref, q_ref, xs_hbm, stage_ref, zero_ref, sem_z, sem):
    i = pl.program_id(0)
    br = zero_ref.shape[0]
    n_blocks = xs_hbm.shape[0] // br

    @pl.when(i == 0)
    def _():
        zero_ref[...] = jnp.zeros_like(zero_ref)

        def fill_copy(b):
            return pltpu.make_async_copy(zero_ref, xs_hbm.at[pl.ds(b * br, br)], sem_z)

        def fill(b, carry):
            @pl.when(bv_ref[b] < br)
            def _():
                fill_copy(b).start()
            return carry

        def fill_wait(b, carry):
            @pl.when(bv_ref[b] < br)
            def _():
                fill_copy(b).wait()
            return carry

        lax.fori_loop(0, n_blocks, fill, 0)
        lax.fori_loop(0, n_blocks, fill_wait, 0)

    rows, tm = stage_ref.shape[0], x_ref.shape[0]
    q8 = q_ref[0]
    pos = lax.broadcasted_iota(jnp.int32, (rows, tm), 0)
    hit = functools.reduce(jnp.logical_or, [pos == q8[j:j + 1, :] for j in range(TOP_K)])
    perm = jnp.where(hit, 1.0, 0.0).astype(BF16)
    stage_ref[...] = _pack_bf16_pairs(jnp.dot(perm, x_ref[...].astype(BF16), preferred_element_type=F32), exact=True)

    def copy(local_row, global_row):
        return pltpu.make_async_copy(stage_ref.at[pl.ds(local_row, MOE_CHUNK)], xs_hbm.at[pl.ds(global_row, MOE_CHUNK)], sem)

    total = _chunk_loops(i, base_ref, nchunk_ref, loff_ref, copy)

    def drain(c, carry):
        copy(0, 0).wait()
        return carry

    lax.fori_loop(0, total, drain, 0)


def _moe_dispatch(tokens, q_rows, dest_base, nchunk, loff, block_valid):
    n, d = tokens.shape
    tm, br = TOKEN_TILE, MOE_ROW_BLOCK
    n_rows = block_valid.shape[0] * br
    return pl.pallas_call(
        _dispatch_body,
        out_shape=jax.ShapeDtypeStruct((n_rows, d // 2), jnp.uint32),
        grid_spec=pltpu.PrefetchScalarGridSpec(
            num_scalar_prefetch=4,
            grid=(n // tm,),
            in_specs=[pl.BlockSpec((tm, d), lambda i, *_: (i, 0)),
                      pl.BlockSpec((1, MOE_SLOTS, tm), lambda i, *_: (i, 0, 0))],
            out_specs=pl.BlockSpec(memory_space=pl.ANY),
            scratch_shapes=[pltpu.VMEM((MOE_STAGE_ROWS, d // 2), jnp.uint32), pltpu.VMEM((br, d // 2), jnp.uint32),
                            pltpu.SemaphoreType.DMA, pltpu.SemaphoreType.DMA]),
        compiler_params=_params("arbitrary"),
        name="moe_dispatch",
    )(dest_base, nchunk, loff, block_valid, tokens, q_rows)


def _expert_body(be_ref, bv_ref, x_ref, w1_ref, w3_ref, w2_ref, y_ref, w1b, w3b, w2b):
    i = pl.program_id(0)
    valid = bv_ref[i]

    @pl.when(jnp.logical_or(i == 0, be_ref[i] != be_ref[jnp.maximum(i - 1, 0)]))
    def _():
        w1b[...] = w1_ref[0, 0].astype(BF16)
        w3b[...] = w3_ref[0, 0].astype(BF16)
        w2b[...] = w2_ref[0, 0].astype(BF16)

    @pl.when(valid > 0)
    def _():
        x = _unpack_bf16_pairs(x_ref[...])
        h1 = jnp.dot(x, w1b[...], preferred_element_type=F32)
        h3 = jnp.dot(x, w3b[...], preferred_element_type=F32)
        a = (h1 * jax.nn.sigmoid(h1) * h3).astype(BF16)
        y_ref[...] = _pack_bf16_pairs(jnp.dot(a, w2b[...], preferred_element_type=F32))

    @pl.when(valid <= 0)
    def _():
        y_ref[...] = jnp.zeros_like(y_ref)


def _moe_experts(xs, block_expert, block_valid, layer, w1, w3, w2):
    n_rows, dp = xs.shape
    br = MOE_ROW_BLOCK
    d, ff = w1.shape[-2:]
    return pl.pallas_call(
        _expert_body,
        out_shape=jax.ShapeDtypeStruct((n_rows, dp), jnp.uint32),
        grid_spec=pltpu.PrefetchScalarGridSpec(
            num_scalar_prefetch=2,
            grid=(n_rows // br,),
            in_specs=[pl.BlockSpec((br, dp), lambda i, be, bv: (i, 0)),
                      pl.BlockSpec((1, 1, d, ff), lambda i, be, bv: (layer, be[i], 0, 0)),
                      pl.BlockSpec((1, 1, d, ff), lambda i, be, bv: (layer, be[i], 0, 0)),
                      pl.BlockSpec((1, 1, ff, d), lambda i, be, bv: (layer, be[i], 0, 0))],
            out_specs=pl.BlockSpec((br, dp), lambda i, be, bv: (i, 0)),
            scratch_shapes=[pltpu.VMEM((d, ff), BF16), pltpu.VMEM((d, ff), BF16), pltpu.VMEM((ff, d), BF16)]),
        compiler_params=_params("arbitrary"),
        name="moe_experts",
    )(block_expert, block_valid, xs, w1, w3, w2)


def _combine_body(base_ref, nchunk_ref, loff_ref, h_ref, q_ref, g_ref, ys_hbm, sw1_ref, sw3_ref, sw2_ref,
                  x1_ref, mod_ref, ln_ref, out_ref, stage_ref, sem):
    i = pl.program_id(0)

    @pl.when(i == 0)
    def _():
        stage_ref[...] = jnp.zeros_like(stage_ref)

    def copy(local_row, global_row):
        return pltpu.make_async_copy(ys_hbm.at[pl.ds(global_row, MOE_CHUNK)], stage_ref.at[pl.ds(local_row, MOE_CHUNK)], sem)

    total = _chunk_loops(i, base_ref, nchunk_ref, loff_ref, copy)

    h = h_ref[...].astype(BF16)
    h1 = jnp.dot(h, sw1_ref[...], preferred_element_type=F32)
    h3 = jnp.dot(h, sw3_ref[...], preferred_element_type=F32)
    y = jnp.dot((h1 * jax.nn.sigmoid(h1) * h3).astype(BF16), sw2_ref[...], preferred_element_type=F32)
    tm, rows = h_ref.shape[0], stage_ref.shape[0]
    q8, g8 = q_ref[0], g_ref[0]
    pos = lax.broadcasted_iota(jnp.int32, (tm, rows), 1)
    gate_mat = jnp.zeros((tm, rows), F32)
    for j in range(TOP_K):
        gate_mat = jnp.where(pos == q8[:, j:j + 1], g8[:, j:j + 1], gate_mat)

    def drain(c, carry):
        copy(0, 0).wait()
        return carry

    lax.fori_loop(0, total, drain, 0)
    y = y + jnp.dot(gate_mat.astype(BF16), _unpack_bf16_pairs(stage_ref[...]), preferred_element_type=F32)
    out_ref[...] = _layer_norm_rows(DN_ALPHA * x1_ref[...] + mod_ref[0, 0, 5:6, :] * y, ln_ref[0:1, :], ln_ref[1:2, :])


def _moe_combine(tokens, q_cols, g_cols, dest_base, nchunk, loff, ys, sw1, sw3, sw2, x1, modtab, ln, tiles_per_batch,
                 first_tile):
    n, d = tokens.shape
    tm = TOKEN_TILE
    ff = sw1.shape[-1]
    whole = lambda shape: pl.BlockSpec(shape, lambda i, *_: (0,) * len(shape))
    rows = pl.BlockSpec((tm, d), lambda i, *_: (i, 0))
    slots = pl.BlockSpec((1, tm, MOE_SLOTS), lambda i, *_: (i, 0, 0))
    return pl.pallas_call(
        _combine_body,
        out_shape=jax.ShapeDtypeStruct((n, d), F32),
        grid_spec=pltpu.PrefetchScalarGridSpec(
            num_scalar_prefetch=3,
            grid=(n // tm,),
            in_specs=[rows, slots, slots, pl.BlockSpec(memory_space=pl.ANY),
                      whole((d, ff)), whole((d, ff)), whole((ff, d)),
                      rows, _mod_spec(tiles_per_batch, first_tile, True), whole(ln.shape)],
            out_specs=rows,
            scratch_shapes=[pltpu.VMEM((MOE_STAGE_ROWS, d // 2), jnp.uint32), pltpu.SemaphoreType.DMA]),
        compiler_params=_params("arbitrary"),
        name="moe_combine",
    )(dest_base, nchunk, loff, tokens, q_cols, g_cols, ys, sw1, sw3, sw2, x1, modtab, ln)


def _moe_block(h2, x1, modtab, ln, first_tile, layer, router_w, router_b, w1, w3, w2, sw1, sw3, sw2):
    B, Tp, d = h2.shape
    tokens = h2.reshape(B * Tp, d)
    n = B * Tp
    tm, br = TOKEN_TILE, MOE_ROW_BLOCK
    q_rows, g_rows, cnt = _moe_route(tokens, router_w, router_b)
    cnt = cnt[:, :, 0].astype(jnp.int32)
    nchunk = (cnt + MOE_CHUNK - 1) // MOE_CHUNK
    run = nchunk * MOE_CHUNK
    total = jnp.sum(run, axis=0)
    padded = (total + br - 1) // br * br
    p_end = jnp.cumsum(padded)
    offs = p_end - padded
    dest_base = (offs[None, :] + jnp.cumsum(run, axis=0) - run).reshape(-1).astype(jnp.int32)
    loff = jnp.cumsum(run, axis=1) - run
    nchunk, loff = nchunk.reshape(-1).astype(jnp.int32), loff.reshape(-1).astype(jnp.int32)
    n_blocks = -(-(n * TOP_K + (n // tm) * N_EXPERTS * (MOE_CHUNK - 1) + N_EXPERTS * (br - 1)) // br)
    blk_start = jnp.arange(n_blocks, dtype=jnp.int32) * br
    block_expert = jnp.minimum(jnp.sum(blk_start[:, None] >= p_end[None, :], axis=1), N_EXPERTS - 1).astype(jnp.int32)
    block_valid = jnp.clip(total[block_expert] - (blk_start - offs[block_expert]), 0, br).astype(jnp.int32)
    xs = _moe_dispatch(tokens, q_rows, dest_base, nchunk, loff, block_valid)
    ys = _moe_experts(xs, block_expert, block_valid, layer, w1, w3, w2)
    q_cols, g_cols = jnp.swapaxes(q_rows, 1, 2), jnp.swapaxes(g_rows, 1, 2)
    out = _moe_combine(tokens, q_cols, g_cols, dest_base, nchunk, loff, ys,
                       sw1.astype(BF16), sw3.astype(BF16), sw2.astype(BF16),
                       x1.reshape(n, d), modtab, ln, Tp // tm, first_tile)
    return out.reshape(B, Tp, d)


def _low_half():
    return lax.broadcasted_iota(jnp.int32, (1, LANES), 1) < LANES // 2


def _pair_attention(q, parts):
    low = _low_half()
    outs = []
    q = q * jnp.asarray(HEAD_DIM ** -0.5, q.dtype)
    for use_low in (True, False):
        qm = jnp.where(low == use_low, q, jnp.zeros_like(q))
        scores = []
        for k, _, b_lo, b_hi in parts:
            s = lax.dot_general(qm, k, (((1,), (1,)), ((), ())), preferred_element_type=F32)
            b = b_lo if use_low else b_hi
            scores.append(s if b is None else s + b)
        m = functools.reduce(jnp.maximum, [jnp.max(s, axis=-1, keepdims=True) for s in scores])
        den = 0.0
        num = 0.0
        for s, (_, v, _, _) in zip(scores, parts):
            p = jnp.exp(s - m)
            den = den + jnp.sum(p, axis=-1, keepdims=True)
            num = num + jnp.dot(p.astype(BF16), v, preferred_element_type=F32)
        outs.append(num / den)
    return jnp.where(low, outs[0], outs[1])


NA_Q_ROWS = 2
NA_BAND_ROWS = NA_WIN_ROWS + NA_Q_ROWS - 1


def _na_body(cls_ref, q_ref, k_ref, v_ref, bias_ref, o_ref, *, n_ctx, rows):
    s = pl.program_id(1)
    tq = q_ref.shape[1]
    ctx_steps = n_ctx // tq
    n_tiles = NA_DIM // LANES
    tile = lambda t: slice(t * LANES, (t + 1) * LANES)

    @pl.when(s < ctx_steps)
    def _():
        for t in range(n_tiles):
            part = (k_ref[0, 0:n_ctx, tile(t)], v_ref[0, 0:n_ctx, tile(t)], None, None)
            o_ref[0, :, tile(t)] = _pair_attention(q_ref[0, :, tile(t)], [part]).astype(o_ref.dtype)

    @pl.when(s >= ctx_steps)
    def _():
        first = (s - ctx_steps) * NA_Q_ROWS
        start = jnp.clip(first - NA_WIN_ROWS // 2, 0, rows - NA_BAND_ROWS)
        off = pl.multiple_of(n_ctx + start * GRID_W, GRID_W)
        band = pl.ds(off, NA_BAND_ROWS * GRID_W)
        for t in range(n_tiles):
            parts = [(k_ref[0, band, tile(t)], v_ref[0, band, tile(t)], bias_ref[0, 2 * t], bias_ref[0, 2 * t + 1]),
                     (k_ref[0, 0:n_ctx, tile(t)], v_ref[0, 0:n_ctx, tile(t)], None, None)]
            o_ref[0, :, tile(t)] = _pair_attention(q_ref[0, :, tile(t)], parts).astype(o_ref.dtype)


def _na_bias_table(rpb, rows):
    import numpy as np
    kc, W = NA_WIN_COLS, GRID_W
    cidx = np.arange(W)
    col_start = np.clip(cidx - kc // 2, 0, W - kc)
    col_in = (cidx[None, :] >= col_start[:, None]) & (cidx[None, :] < col_start[:, None] + kc)
    d_col = np.clip(cidx[None, :] - cidx[:, None], -(kc - 1), kc - 1) + kc - 1
    classes, class_of, d_rows, allowed = {}, [], [], []
    for step in range(rows // NA_Q_ROWS):
        first = step * NA_Q_ROWS
        band0 = min(max(first - NA_WIN_ROWS // 2, 0), rows - NA_BAND_ROWS)
        d_row = np.zeros((NA_Q_ROWS, NA_BAND_ROWS), np.int32)
        ok = np.zeros((NA_Q_ROWS, NA_BAND_ROWS), bool)
        for p in range(NA_Q_ROWS):
            win0 = min(max(first + p - NA_WIN_ROWS // 2, 0), rows - NA_WIN_ROWS)
            for j in range(NA_BAND_ROWS):
                ok[p, j] = win0 <= band0 + j < win0 + NA_WIN_ROWS
                d_row[p, j] = min(max(band0 + j - (first + p) + NA_WIN_ROWS - 1, 0), 2 * NA_WIN_ROWS - 2)
        key = (d_row.tobytes(), ok.tobytes())
        if key not in classes:
            classes[key] = len(classes)
            d_rows.append(d_row)
            allowed.append(ok)
        class_of.append(classes[key])
    d_rows, allowed = np.stack(d_rows), np.stack(allowed)
    tab = rpb[:, d_rows][..., d_col]
    mask = allowed[None, :, :, :, None, None] & col_in[None, None, None, None]
    tab = jnp.where(mask, tab, -jnp.inf)
    tab = jnp.transpose(tab, (1, 0, 2, 4, 3, 5))
    return tab.reshape(len(classes), NA_HEADS, NA_Q_ROWS * W, NA_BAND_ROWS * W), class_of


def _na_attention(pa, rpb, n_ctx):
    B, T, _ = pa.shape
    rows = (T - n_ctx) // GRID_W
    tq = NA_Q_ROWS * GRID_W
    assert rows % NA_Q_ROWS == 0 and rows >= NA_BAND_ROWS and n_ctx % tq == 0
    table, class_of = _na_bias_table(rpb, rows)
    step_class = jnp.array([0] * (n_ctx // tq) + class_of, jnp.int32)
    return pl.pallas_call(
        functools.partial(_na_body, n_ctx=n_ctx, rows=rows),
        out_shape=jax.ShapeDtypeStruct((B, T, NA_DIM), BF16),
        grid_spec=pltpu.PrefetchScalarGridSpec(
            num_scalar_prefetch=1,
            grid=(B, T // tq),
            in_specs=[pl.BlockSpec((1, tq, NA_DIM), lambda b, s, cls: (b, s, 0)),
                      pl.BlockSpec((1, T, NA_DIM), lambda b, s, cls: (b, 0, 1)),
                      pl.BlockSpec((1, T, NA_DIM), lambda b, s, cls: (b, 0, 2)),
                      pl.BlockSpec((1,) + table.shape[1:], lambda b, s, cls: (cls[s], 0, 0, 0))],
            out_specs=pl.BlockSpec((1, tq, NA_DIM), lambda b, s, cls: (b, s, 0))),
        compiler_params=_params("arbitrary", "arbitrary"),
        name="na_attention",
    )(step_class, pa, pa, pa, table)


def _rms_pair(x, gain):
    low = _low_half()
    sq = x * x
    s_lo = jnp.sum(jnp.where(low, sq, 0.0), axis=-1, keepdims=True)
    s_hi = jnp.sum(jnp.where(low, 0.0, sq), axis=-1, keepdims=True)
    ms = jnp.where(low, s_lo, s_hi) * (1.0 / HEAD_DIM)
    return x * lax.rsqrt(ms + NORM_EPS) * gain


def _rope_pair(x, cos, sin_signed):
    even = lax.broadcasted_iota(jnp.int32, (1, LANES), 1) % 2 == 0
    partner = jnp.where(even, pltpu.roll(x, LANES - 1, axis=1), pltpu.roll(x, 1, axis=1))
    return x * cos + partner * sin_signed


def _gqa_body(q_ref, k_ref, v_ref, cos_q, sin_q, cos_k, sin_k, gain_ref, o_ref, kn_ref, vn_ref, *, n_ctx):
    s = pl.program_id(1)
    tq = q_ref.shape[1]

    @pl.when(s == 0)
    def _():
        kn = _rope_pair(_rms_pair(k_ref[0], gain_ref[1:2, :]), cos_k[...], sin_k[...])
        vn = v_ref[0]
        low = _low_half()
        for src, dst in ((kn, kn_ref), (vn, vn_ref)):
            swapped = pltpu.roll(src, LANES // 2, axis=1)
            dst[0] = jnp.where(low, src, swapped).astype(BF16)
            dst[1] = jnp.where(low, swapped, src).astype(BF16)

    def run(n_keys):
        n_tiles = GQA_Q_DIM // LANES
        for t in range(n_tiles):
            g = t * GQA_KV_HEADS // n_tiles
            k, v = kn_ref[g, 0:n_keys, :], vn_ref[g, 0:n_keys, :]
            q = q_ref[0, :, t * LANES:(t + 1) * LANES]
            qn = _rope_pair(_rms_pair(q, gain_ref[0:1, :]), cos_q[...], sin_q[...]).astype(BF16)
            o_ref[0, :, t * LANES:(t + 1) * LANES] = _pair_attention(qn, [(k, v, None, None)]).astype(o_ref.dtype)

    @pl.when(s < n_ctx // tq)
    def _():
        run(n_ctx)

    @pl.when(s >= n_ctx // tq)
    def _():
        run(kn_ref.shape[1])


def _axial_rope(n_tokens):
    t = jnp.arange(n_tokens)
    row = (t // GRID_W).astype(F32)
    col = (t % GRID_W).astype(F32)
    inv = ROPE_THETA ** (-jnp.arange(0, ROPE_AXIS_DIM, 2, dtype=F32) / ROPE_AXIS_DIM)
    ang = jnp.concatenate([row[:, None] * inv, col[:, None] * inv], -1)
    return jnp.cos(ang), jnp.sin(ang)


def _gqa_rope_tables(T, n_ctx):
    cos, sin = _axial_rope(T - n_ctx)
    cos = jnp.concatenate([jnp.ones((n_ctx, ROPE_AXIS_DIM), F32), cos], 0)
    sin = jnp.concatenate([jnp.zeros((n_ctx, ROPE_AXIS_DIM), F32), sin], 0)
    cos = jnp.tile(jnp.repeat(cos, 2, axis=-1), (1, 2))
    sign = jnp.tile(jnp.array([-1.0, 1.0], F32), LANES // 2)
    sin = jnp.tile(jnp.repeat(sin, 2, axis=-1), (1, 2)) * sign
    return cos, sin


def _gqa_attention(pb, qk_gain, n_ctx):
    B, T, _ = pb.shape
    tq = GQA_Q_TILE
    cos, sin = _gqa_rope_tables(T, n_ctx)
    gain = jnp.tile(qk_gain, (1, 2))
    kv_blk = GQA_Q_DIM // GQA_KV_DIM
    return pl.pallas_call(
        functools.partial(_gqa_body, n_ctx=n_ctx),
        out_shape=jax.ShapeDtypeStruct((B, T, GQA_Q_DIM), BF16),
        grid=(B, T // tq),
        in_specs=[pl.BlockSpec((1, tq, GQA_Q_DIM), lambda b, s: (b, s, 0)),
                  pl.BlockSpec((1, T, GQA_KV_DIM), lambda b, s: (b, 0, kv_blk)),
                  pl.BlockSpec((1, T, GQA_KV_DIM), lambda b, s: (b, 0, kv_blk + 1)),
                  pl.BlockSpec((tq, LANES), lambda b, s: (s, 0)),
                  pl.BlockSpec((tq, LANES), lambda b, s: (s, 0)),
                  pl.BlockSpec((T, LANES), lambda b, s: (0, 0)),
                  pl.BlockSpec((T, LANES), lambda b, s: (0, 0)),
                  pl.BlockSpec((2, LANES), lambda b, s: (0, 0))],
        out_specs=pl.BlockSpec((1, tq, GQA_Q_DIM), lambda b, s: (b, s, 0)),
        scratch_shapes=[pltpu.VMEM((GQA_KV_HEADS, T, GQA_KV_DIM), BF16), pltpu.VMEM((GQA_KV_HEADS, T, GQA_KV_DIM), BF16)],
        compiler_params=_params("arbitrary", "arbitrary"),
        name="gqa_attention",
    )(pb, pb, pb, cos, sin, cos, sin, gain)


def _attention_mixers(x, modtab, w_in, rpb, qk_gain, n_ctx):
    w = w_in.astype(BF16)
    pa, pb = _in_proj(x, modtab, w, ((0, 3 * NA_DIM), (3 * NA_DIM, w.shape[1])), (BF16, F32))
    return jnp.concatenate([_na_attention(pa, rpb, n_ctx), _gqa_attention(pb, qk_gain, n_ctx)], -1)


def _log_sigmoid(x):
    return jnp.minimum(x, 0.0) - jnp.log(1.0 + jnp.exp(-jnp.abs(x)))


def _mlstm_body(xf, gcf, grf, ktf, xb, gcb, grb, ktb, bias_c, bias_r, hf_ref, hb_ref, c_ref, m_ref):
    @pl.when(pl.program_id(1) == 0)
    def _():
        c_ref[...] = jnp.zeros_like(c_ref)
        m_ref[...] = jnp.zeros_like(m_ref)

    L, H, W = MLSTM_CHUNK, MLSTM_HEADS, MLSTM_HEAD
    row = lax.broadcasted_iota(jnp.int32, (L, L), 0)
    col = lax.broadcasted_iota(jnp.int32, (L, L), 1)
    hi = lax.Precision.HIGHEST
    dirs = ((xf, gcf, grf, hf_ref, ktf), (xb, gcb, grb, hb_ref, ktb))
    part = lambda x_ref, j, h: x_ref[0, :, j * MLSTM_DIM + h * W:j * MLSTM_DIM + (h + 1) * W]
    combos = [(d, h) for d in range(2) for h in range(H)]
    head = lambda h: slice(h * W, (h + 1) * W)
    lanes = lambda column: jnp.broadcast_to(column, (L, LANES))
    gates = []
    for d, (_, gc_ref, gr_ref, _, _) in enumerate(dirs):
        seen = (col <= row) if d == 0 else (col >= row)
        seen_f = jnp.where(seen, 1.0, 0.0)
        g_col = gc_ref[0] + bias_c[...]
        g_row = gr_ref[0, 0] + bias_r[...]
        b_col = jnp.dot(seen_f, _log_sigmoid(g_col), preferred_element_type=F32, precision=hi)
        b_row = lax.dot_general(_log_sigmoid(g_row), seen_f, (((1,), (1,)), ((), ())), preferred_element_type=F32,
                                precision=hi)
        b_under_i = pltpu.roll(b_col, LANES - H, axis=1)
        run = g_col - b_under_i
        tok = lax.broadcasted_iota(jnp.int32, (L, LANES), 0)
        shift = 1
        while shift < L:
            if d == 0:
                run = jnp.maximum(run, jnp.where(tok >= shift, pltpu.roll(run, shift, axis=0), -jnp.inf))
            else:
                run = jnp.maximum(run, jnp.where(tok < L - shift, pltpu.roll(run, L - shift, axis=0), -jnp.inf))
            shift *= 2
        gates.append((seen, g_col, g_row, b_col, b_row, b_under_i + run))
    first = {}
    for d, h in combos:
        x_ref = dirs[d][0]
        s_idx = d * H + h
        q = (part(x_ref, 0, h) * W ** -0.5).astype(BF16)
        state = c_ref[s_idx]
        qk = lax.dot_general(q, part(x_ref, 1, h).astype(BF16), (((1,), (1,)), ((), ())), preferred_element_type=F32)
        q_state = jnp.dot(q, state.astype(BF16), preferred_element_type=F32)
        first[d, h] = (dirs[d][4][0, 0, head(h), :].astype(BF16), state, qk, q_state)
    second = {}
    for d, h in combos:
        seen, g_col, g_row, b_col, b_row, m_intra = gates[d]
        _, _, qk, q_state = first[d, h]
        v = part(dirs[d][0], 2, h)
        gi, gf = d * 2 * H + h, d * 2 * H + H + h
        s_idx = d * H + h
        m_state = m_ref[s_idx:s_idx + 1, :]
        b_t, i_t, mi_t = lanes(b_col[:, gf:gf + 1]), lanes(g_col[:, gi:gi + 1]), lanes(m_intra[:, gi:gi + 1])
        b_end = b_t[L - 1:L, :] if d == 0 else b_t[0:1, :]
        d_inter = b_t + m_state
        m_t = jnp.maximum(d_inter, mi_t)
        d_intra = jnp.where(seen, b_t[:, :L] - b_row[gf:gf + 1, :] + g_row[gi:gi + 1, :], -jnp.inf)
        s = qk * jnp.exp(d_intra - m_t[:, :L])
        w_inter = jnp.exp(d_inter - m_t)
        d_state = b_end - b_t + i_t
        m_new = jnp.maximum(b_end + m_state, jnp.max(d_state, axis=0, keepdims=True))
        w_s = jnp.exp(d_state - m_new)
        w_c = jnp.exp(b_end + m_state - m_new)
        ones = jnp.ones((L, LANES), BF16)
        second[d, h] = (s.astype(BF16), jnp.concatenate([v.astype(BF16), ones], axis=1),
                        jnp.concatenate([(v * w_s).astype(BF16), w_s.astype(BF16)], axis=1),
                        w_inter, jnp.exp(-m_t), jnp.concatenate([w_c, w_c], axis=1), m_new)
    writes = []
    for d, h in combos:
        s16, v_ones, vw_ws, w_inter, floor, w_c, m_new = second[d, h]
        kt16, state, _, q_state = first[d, h]
        s_idx = d * H + h
        sv = jnp.dot(s16, v_ones, preferred_element_type=F32)
        num = sv[:, :W] + w_inter * q_state[:, :W]
        den = sv[:, W:] + w_inter * q_state[:, W:]
        writes.append((dirs[d][3].at[0, :, head(h)], num / jnp.maximum(jnp.abs(den), floor)))
        writes.append((c_ref.at[s_idx], w_c * state + jnp.dot(kt16, vw_ws, preferred_element_type=F32)))
        writes.append((m_ref.at[s_idx:s_idx + 1, :], m_new))
    for ref, value in writes:
        ref[...] = value


def _mlstm(pm, gates, gate_b, n_ctx):
    B, T, _ = pm.shape
    L = MLSTM_CHUNK
    nc, ncc = T // L, n_ctx // L
    ng = 4 * MLSTM_HEADS
    g_cols = jnp.pad(gates, ((0, 0), (0, 0), (0, LANES - ng)))
    g_rows = jnp.swapaxes(gates.reshape(B, nc, L, ng), 2, 3)
    k_t = jnp.swapaxes(pm[..., MLSTM_DIM:2 * MLSTM_DIM].reshape(B, nc, L, MLSTM_DIM), 2, 3)
    ktr = lambda order: pl.BlockSpec((1, 1, MLSTM_DIM, L), lambda b, c: (b, order(c), 0, 0))
    bias = gate_b.reshape(ng)
    bias_c = jnp.pad(bias, (0, LANES - ng)).reshape(1, LANES)
    bias_r = bias.reshape(ng, 1)
    fwd = lambda c: c
    bwd = lambda c: jnp.where(c < ncc, ncc - 1 - c, nc - 1 - (c - ncc))
    blk = (1, L, MLSTM_DIM)
    seq = lambda order: pl.BlockSpec((1, L, 3 * MLSTM_DIM), lambda b, c: (b, order(c), 0))
    gcol = lambda order: pl.BlockSpec((1, L, LANES), lambda b, c: (b, order(c), 0))
    grow = lambda order: pl.BlockSpec((1, 1, ng, L), lambda b, c: (b, order(c), 0, 0))
    n_state = 2 * MLSTM_HEADS
    out_sds = jax.ShapeDtypeStruct((B, T, MLSTM_DIM), F32)
    return pl.pallas_call(
        _mlstm_body,
        out_shape=(out_sds, out_sds),
        grid=(B, nc),
        in_specs=[seq(fwd), gcol(fwd), grow(fwd), ktr(fwd), seq(bwd), gcol(bwd), grow(bwd), ktr(bwd),
                  pl.BlockSpec((1, LANES), lambda b, c: (0, 0)), pl.BlockSpec((ng, 1), lambda b, c: (0, 0))],
        out_specs=(pl.BlockSpec(blk, lambda b, c: (b, fwd(c), 0)), pl.BlockSpec(blk, lambda b, c: (b, bwd(c), 0))),
        scratch_shapes=[pltpu.VMEM((n_state, MLSTM_HEAD, MLSTM_HEAD + LANES), F32), pltpu.VMEM((n_state, LANES), F32)],
        compiler_params=_params("arbitrary", "arbitrary"),
        name="mlstm",
    )(pm, g_cols, g_rows, k_t, pm, g_cols, g_rows, k_t, bias_c, bias_r)


def _pair_sums(x):
    low = _low_half()
    s_lo = jnp.sum(jnp.where(low, x, 0.0), axis=-1, keepdims=True)
    s_hi = jnp.sum(jnp.where(low, 0.0, x), axis=-1, keepdims=True)
    return jnp.where(low, s_lo, s_hi)


def _rwkv_prep_body(p_ref, prev_ref, next_ref, tab_ref, tab2_ref, w1_ref, w2_ref, a1_ref, a2_ref, g1_ref, g2_ref,
                    rv_ref, n_ref, w_ref, k_ref, b_ref, gate_ref, *, n_ctx):
    s = pl.program_id(1)
    tm = p_ref.shape[1]
    C = RWKV_DIM
    ctx_tiles = n_ctx // tm
    x = p_ref[0]
    has_prev = jnp.logical_and(s != 0, s != ctx_tiles)
    has_next = jnp.logical_and(s != ctx_tiles - 1, s != pl.num_programs(1) - 1)
    prev_row = jnp.where(has_prev, prev_ref[0, SUBLANES - 1:SUBLANES, :], 0.0)
    next_row = jnp.where(has_next, next_ref[0, 0:1, :], 0.0)
    rowid = lax.broadcasted_iota(jnp.int32, (tm, 1), 0)
    up = jnp.where(rowid == 0, prev_row, pltpu.roll(x, 1, axis=0))
    dn = jnp.where(rowid == tm - 1, next_row, pltpu.roll(x, tm - 1, axis=0))
    d = 0.5 * (up + dn) - x
    part = lambda a, i: a[:, i * C:(i + 1) * C]
    mu = lambda i: tab_ref[i:i + 1, :]
    r = part(x, 0) + part(d, 0) * mu(0)
    k = part(x, 1) + part(d, 1) * mu(1)
    v = part(x, 2) + part(d, 2) * mu(2)
    z, dz = part(x, 3), part(d, 3)
    z_w, z_a, z_g = (z + dz * mu(3)).astype(BF16), (z + dz * mu(4)).astype(BF16), (z + dz * mu(5)).astype(BF16)
    lora = lambda t, w: jnp.dot(t.astype(BF16), w[...], preferred_element_type=F32)
    w_pre = tab2_ref[0:1, :] + lora(jnp.tanh(lora(z_w, w1_ref)), w2_ref)
    neg = -w_pre
    softplus = jnp.maximum(neg, 0.0) + jnp.log(1.0 + jnp.exp(-jnp.abs(neg)))
    decay = jnp.exp(-jnp.exp(-softplus - 0.5))
    iclr = jax.nn.sigmoid(tab2_ref[1:2, :] + lora(lora(z_a, a1_ref), a2_ref))
    gate_ref[0] = lora(jax.nn.sigmoid(lora(z_g, g1_ref)), g2_ref)
    kk = k * tab_ref[6:7, :]
    kk = jnp.concatenate(
        [kk[:, t * LANES:(t + 1) * LANES]
         * lax.rsqrt(jnp.maximum(_pair_sums(jnp.square(kk[:, t * LANES:(t + 1) * LANES])), 1e-24))
         for t in range(C // LANES)], axis=1)
    k2, kk2 = jnp.concatenate([k, k], axis=1), jnp.concatenate([kk, kk], axis=1)
    rv_ref[0] = jnp.concatenate([r, v], axis=1)
    n_ref[0] = -kk2
    w_ref[0] = decay
    k_ref[0] = k2 * (1.0 + (iclr - 1.0) * tab2_ref[2:3, :])
    b_ref[0] = kk2 * iclr


def _rwkv_prep(pr, n_ctx, mu, w0, w1, w2, a0, a1, a2, g1, g2, kvec):
    B, T, _ = pr.shape
    tm, C = TOKEN_TILE, RWKV_DIM
    per_tile = tm // SUBLANES
    tab = jnp.concatenate([mu, kvec[0:1], jnp.zeros((1, C), F32)], 0)
    cat = lambda a: jnp.concatenate([a[0], a[1]], -1)
    tab2 = jnp.concatenate([cat(w0)[None], cat(a0)[None], jnp.tile(kvec[1], 2)[None], jnp.zeros((5, 2 * C), F32)], 0)
    blockdiag = lambda a: jnp.concatenate([jnp.pad(a[0], ((0, 0), (0, C))), jnp.pad(a[1], ((0, 0), (C, 0)))], 0)
    gl = g1.shape[1]
    consts = (tab, tab2, cat(w1).astype(BF16), blockdiag(w2).astype(BF16), cat(a1).astype(BF16),
              blockdiag(a2).astype(BF16), jnp.pad(g1, ((0, 0), (0, LANES - gl))).astype(BF16),
              jnp.pad(g2, ((0, LANES - gl), (0, 0))).astype(BF16))
    const = lambda a: pl.BlockSpec(a.shape, lambda b, s: (0, 0))
    one = jax.ShapeDtypeStruct((B, T, C), F32)
    two = jax.ShapeDtypeStruct((B, T, 2 * C), F32)
    spec1 = pl.BlockSpec((1, tm, C), lambda b, s: (b, s, 0))
    spec2 = pl.BlockSpec((1, tm, 2 * C), lambda b, s: (b, s, 0))
    return pl.pallas_call(
        functools.partial(_rwkv_prep_body, n_ctx=n_ctx),
        out_shape=(two, two, two, two, two, one),
        grid=(B, T // tm),
        in_specs=[pl.BlockSpec((1, tm, 4 * C), lambda b, s: (b, s, 0)),
                  pl.BlockSpec((1, SUBLANES, 4 * C), lambda b, s: (b, jnp.maximum(s * per_tile - 1, 0), 0)),
                  pl.BlockSpec((1, SUBLANES, 4 * C), lambda b, s: (b, jnp.minimum((s + 1) * per_tile, T // SUBLANES - 1), 0))]
        + [const(a) for a in consts],
        out_specs=(spec2, spec2, spec2, spec2, spec2, spec1),
        compiler_params=_params("arbitrary", "arbitrary"),
        name="rwkv_prep",
    )(pr, pr, pr, *consts)


def _rec_post_body(y_ref, r_ref, v_ref, k_ref, gate_ref, hf_ref, hb_ref, o_ref, tab_ref, out_ref):
    C = RWKV_DIM
    for t in range(C // LANES):
        sl = slice(t * LANES, (t + 1) * LANES)
        y = y_ref[0, :, sl]
        yc = y - _pair_sums(y) * (1.0 / RWKV_HEAD)
        var = _pair_sums(yc * yc) * (1.0 / RWKV_HEAD)
        yn = yc * lax.rsqrt(var + RWKV_GN_EPS) * tab_ref[0:1, sl] + tab_ref[1:2, sl]
        k_sum = k_ref[0, :, sl] + k_ref[0, :, C + t * LANES:C + (t + 1) * LANES]
        bonus = _pair_sums(r_ref[0, :, sl] * k_sum * tab_ref[2:3, sl]) * v_ref[0, :, sl]
        out_ref[0, :, sl] = ((yn + bonus) * gate_ref[0, :, sl]).astype(out_ref.dtype)
    for t in range(MLSTM_HEADS):
        sl = slice(t * MLSTM_HEAD, (t + 1) * MLSTM_HEAD)
        h = hf_ref[0, :, sl] + hb_ref[0, :, sl]
        hn = h * lax.rsqrt(jnp.mean(h * h, axis=-1, keepdims=True) + NORM_EPS) * tab_ref[3:4, sl]
        out_ref[0, :, C + t * MLSTM_HEAD:C + (t + 1) * MLSTM_HEAD] = (hn * jax.nn.sigmoid(o_ref[0, :, sl])).astype(out_ref.dtype)


def _rec_post(y, rv, k_eff, gate, h_f, h_b, pm, gn, r_k, norm_g):
    B, T, C = y.shape
    tm = TOKEN_TILE
    tab = jnp.concatenate([gn, r_k.reshape(1, C), norm_g.reshape(1, C), jnp.zeros((4, C), F32)], 0)
    spec1 = pl.BlockSpec((1, tm, C), lambda b, s: (b, s, 0))
    return pl.pallas_call(
        _rec_post_body,
        out_shape=jax.ShapeDtypeStruct((B, T, C + MLSTM_DIM), BF16),
        grid=(B, T // tm),
        in_specs=[spec1, spec1, pl.BlockSpec((1, tm, C), lambda b, s: (b, s, 1)),
                  pl.BlockSpec((1, tm, 2 * C), lambda b, s: (b, s, 0)), spec1, spec1, spec1,
                  pl.BlockSpec((1, tm, MLSTM_DIM), lambda b, s: (b, s, 3)), pl.BlockSpec(tab.shape, lambda b, s: (0, 0))],
        out_specs=pl.BlockSpec((1, tm, C + MLSTM_DIM), lambda b, s: (b, s, 0)),
        compiler_params=_params("arbitrary", "arbitrary"),
        name="rec_post",
    )(y, rv, rv, k_eff, gate, h_f, h_b, pm, tab)


def _to_state_lanes(x):
    B, T, _ = x.shape
    return jnp.transpose(x.reshape(B, T, 2, RWKV_HEADS, RWKV_HEAD), (1, 4, 2, 0, 3)).reshape(T, RWKV_HEAD, LANES)


def _recurrent_mixers(x, modtab, w_in, n_ctx, mu, w0, w1, w2, a0, a1, a2, g1, g2, kvec, r_k, gn, gate_b, norm_g):
    B, T, _ = x.shape
    assert 2 * B * RWKV_HEADS == LANES
    n_main = RWKV_IN + 4 * MLSTM_DIM
    w = jnp.pad(w_in, ((0, 0), (0, n_main + LANES - w_in.shape[1]))).astype(BF16)
    pr, pm, pg = _in_proj(x, modtab, w, ((0, RWKV_IN), (RWKV_IN, n_main), (n_main, n_main + LANES)), (F32, F32, F32))
    rv, kkn, decay, k_eff, kka, gate = _rwkv_prep(pr, n_ctx, mu, w0, w1, w2, a0, a1, a2, g1, g2, kvec)
    scan_in = tuple(_to_state_lanes(a) for a in (rv, decay, k_eff, kkn, kka))
    h_f, h_b = _mlstm(pm, pg[..., :4 * MLSTM_HEADS], gate_b, n_ctx)
    scan_in, h_f, h_b = lax.optimization_barrier((scan_in, h_f, h_b))
    yf, yb = _rwkv_scan(scan_in, n_ctx)
    half = LANES // 2
    y = yf[:, :, :half] + yb[:, :, half:]
    y = jnp.transpose(y.reshape(T, RWKV_HEAD, B, RWKV_HEADS), (2, 0, 3, 1)).reshape(B, T, RWKV_DIM)
    return _rec_post(y, rv, k_eff, gate, h_f, h_b, pm, gn, r_k, norm_g)


def kernel(x, c, ctx, c_ctx, ada_w, ada_b, ln_g, ln_b, mix_w_out, att_w_in, na_rpb, qk_gain, rec_w_in, rwkv_mu, rwkv_w0, rwkv_w1, rwkv_w2, rwkv_a0, rwkv_a1, rwkv_a2, rwkv_g1, rwkv_g2, rwkv_kvec, rwkv_rk, rwkv_gn, mlstm_gate_b, mlstm_norm, moe_router, moe_bias, moe_w1, moe_w3, moe_w2, shared_w1, shared_w3, shared_w2):
    B, S, D = x.shape
    n_ctx = ctx.shape[1]
    assert D == D_MODEL and n_ctx % TOKEN_TILE == 0 and S % TOKEN_TILE == 0
    xs = jnp.concatenate([ctx, x], axis=1)
    mods = _ada_modulation(c, c_ctx, ada_w, ada_b)
    for i in range(DEPTH):
        last = i == DEPTH - 1
        j = i // 2
        mod = mods[i, :B].reshape(B, 6, D)
        mod_c = jnp.broadcast_to(mods[i, B].reshape(1, 6, D), (B, 6, D))
        modtab = jnp.pad(jnp.stack([mod_c, mod], axis=1), ((0, 0), (0, 0), (0, MOD_ROWS - 6), (0, 0)))
        if i % 2 == 0:
            m = _attention_mixers(xs, modtab, att_w_in[j], na_rpb[j], qk_gain[j], n_ctx)
        else:
            m = _recurrent_mixers(xs, modtab, rec_w_in[j], n_ctx, rwkv_mu[j], rwkv_w0[j], rwkv_w1[j], rwkv_w2[j],
                                  rwkv_a0[j], rwkv_a1[j], rwkv_a2[j], rwkv_g1[j], rwkv_g2[j], rwkv_kvec[j],
                                  rwkv_rk[j], rwkv_gn[j], mlstm_gate_b[j], mlstm_norm[j])
        first_tile = n_ctx // TOKEN_TILE if last else 0
        x1, h2 = _out_proj(m, xs, modtab, mix_w_out[i].astype(BF16), jnp.stack([ln_g[i, 0], ln_b[i, 0]]), first_tile)
        xs = _moe_block(h2, x1, modtab, jnp.stack([ln_g[i, 1], ln_b[i, 1]]), first_tile, i, moe_router[i], moe_bias[i],
                        moe_w1, moe_w3, moe_w2, shared_w1[i], shared_w3[i], shared_w2[i])
    return xs
```

```python
import functools

import jax
import jax.numpy as jnp
from jax import lax
from jax.experimental import pallas as pl
from jax.experimental.pallas import tpu as pltpu

D_MODEL = 1024
DEPTH = 4
GRID_W = 64
HEAD_DIM = 64
NA_HEADS = 8
NA_WIN_ROWS = 8
NA_WIN_COLS = 16
GQA_Q_HEADS = 8
GQA_KV_HEADS = 2
ROPE_THETA = 10000.0
ROPE_AXIS_DIM = HEAD_DIM // 2
NA_DIM = NA_HEADS * HEAD_DIM
GQA_Q_DIM = GQA_Q_HEADS * HEAD_DIM
GQA_KV_DIM = GQA_KV_HEADS * HEAD_DIM
RWKV_HEADS = 8
RWKV_HEAD = 64
RWKV_DIM = RWKV_HEADS * RWKV_HEAD
RWKV_GN_EPS = 64e-5
RWKV_IN = 4 * RWKV_DIM
MLSTM_HEADS = 4
MLSTM_HEAD = 128
MLSTM_DIM = MLSTM_HEADS * MLSTM_HEAD
MLSTM_CHUNK = 64
N_EXPERTS = 64
TOP_K = 6
ROUTED_SCALE = 2.5
DN_ALPHA = (2 * DEPTH) ** 0.25
LN_EPS = 1e-5
NORM_EPS = 1e-6
F32 = jnp.float32
BF16 = jnp.bfloat16

LANES = 128
SUBLANES = 8
VMEM_LIMIT = 48 * 1024 * 1024
TOKEN_TILE = 256
RWKV_TIME_BLOCK = 32
GQA_Q_TILE = 256
MOE_ROW_BLOCK = 512
MOE_SLOTS = 8
MOE_CHUNK = SUBLANES
MOE_STAGE_ROWS = TOKEN_TILE * TOP_K + N_EXPERTS * MOE_CHUNK
MOD_ROWS = 8


def _params(*semantics):
    return pltpu.CompilerParams(dimension_semantics=semantics, vmem_limit_bytes=VMEM_LIMIT)


def _mod_spec(tiles_per_batch, first_tile, flat):
    if flat:
        idx = lambda i, *_: (i // tiles_per_batch, jnp.minimum(i % tiles_per_batch + first_tile, 1), 0, 0)
    else:
        idx = lambda b, s, *_: (b, jnp.minimum(s + first_tile, 1), 0, 0)
    return pl.BlockSpec((1, 1, MOD_ROWS, D_MODEL), idx)


def _layer_norm_rows(z, g, b):
    mu = jnp.mean(z, axis=-1, keepdims=True)
    zc = z - mu
    var = jnp.mean(zc * zc, axis=-1, keepdims=True)
    return zc * lax.rsqrt(var + LN_EPS) * g + b


ADA_ROWS = 16


def _ada_body(c_ref, w_ref, b_ref, o_ref):
    x = c_ref[...]
    a = (x * jax.nn.sigmoid(x)).astype(BF16)
    o_ref[0] = jnp.dot(a, w_ref[0].astype(BF16), preferred_element_type=F32) + b_ref[0]


def _ada_modulation(c, c_ctx, ada_w, ada_b):
    B, D = c.shape
    depth, _, n = ada_w.shape
    cond = jnp.concatenate([c, c_ctx[None], jnp.zeros((ADA_ROWS - B - 1, D), F32)], 0)
    return pl.pallas_call(
        _ada_body,
        out_shape=jax.ShapeDtypeStruct((depth, ADA_ROWS, n), F32),
        grid=(depth, n // D),
        in_specs=[pl.BlockSpec((ADA_ROWS, D), lambda l, j: (0, 0)),
                  pl.BlockSpec((1, D, D), lambda l, j: (l, 0, j)),
                  pl.BlockSpec((1, 1, D), lambda l, j: (l, 0, j))],
        out_specs=pl.BlockSpec((1, ADA_ROWS, D), lambda l, j: (l, 0, j)),
        compiler_params=_params("arbitrary", "arbitrary"),
        name="ada_modulation",
    )(cond, ada_w, ada_b.reshape(depth, 1, n))
def _in_proj_body(x_ref, mod_ref, w_ref, *out_refs, splits):
    h = (x_ref[0] * (1.0 + mod_ref[0, 0, 1:2, :]) + mod_ref[0, 0, 0:1, :]).astype(BF16)
    for o_ref, (c0, c1) in zip(out_refs, splits):
        o_ref[0] = jnp.dot(h, w_ref[:, c0:c1], preferred_element_type=F32).astype(o_ref.dtype)


def _in_proj(x, modtab, w, splits, dtypes):
    B, T, D = x.shape
    tm = TOKEN_TILE
    outs = tuple(jax.ShapeDtypeStruct((B, T, c1 - c0), dt) for (c0, c1), dt in zip(splits, dtypes))
    return pl.pallas_call(
        functools.partial(_in_proj_body, splits=splits),
        out_shape=outs,
        grid=(B, T // tm),
        in_specs=[pl.BlockSpec((1, tm, D), lambda b, s: (b, s, 0)), _mod_spec(T // tm, 0, False),
                  pl.BlockSpec(w.shape, lambda b, s: (0, 0))],
        out_specs=tuple(pl.BlockSpec((1, tm, c1 - c0), lambda b, s: (b, s, 0)) for c0, c1 in splits),
        compiler_params=_params("arbitrary", "arbitrary"),
        name="in_proj",
    )(x, modtab, w)


def _out_proj_body(m_ref, x_ref, mod_ref, w_ref, ln_ref, x1_ref, h2_ref):
    y = jnp.dot(m_ref[0], w_ref[...], preferred_element_type=F32)
    mod = mod_ref[0, 0]
    x1 = _layer_norm_rows(DN_ALPHA * x_ref[0] + mod[2:3, :] * y, ln_ref[0:1, :], ln_ref[1:2, :])
    x1_ref[0] = x1
    h2_ref[0] = x1 * (1.0 + mod[4:5, :]) + mod[3:4, :]


def _out_proj(m, x, modtab, w, ln, first_tile):
    B, T, D = x.shape
    tm = TOKEN_TILE
    n_tiles = T // tm - first_tile
    rows = lambda b, s: (b, s + first_tile, 0)
    out_sds = jax.ShapeDtypeStruct((B, n_tiles * tm, D), F32)
    return pl.pallas_call(
        _out_proj_body,
        out_shape=(out_sds, out_sds),
        grid=(B, n_tiles),
        in_specs=[pl.BlockSpec((1, tm, m.shape[-1]), rows), pl.BlockSpec((1, tm, D), rows),
                  _mod_spec(T // tm, first_tile, False),
                  pl.BlockSpec(w.shape, lambda b, s: (0, 0)), pl.BlockSpec(ln.shape, lambda b, s: (0, 0))],
        out_specs=(pl.BlockSpec((1, tm, D), lambda b, s: (b, s, 0)),) * 2,
        compiler_params=_params("arbitrary", "arbitrary"),
        name="out_proj",
    )(m, x, modtab, w, ln)


def _rwkv_scan_body(rvf, wf, kf, nf, bf, rvb, wb, kb, nb, bb, yf_ref, yb_ref, state_ref, *, tc):
    @pl.when(pl.program_id(0) == 0)
    def _():
        state_ref[...] = jnp.zeros_like(state_ref)

    half = LANES // 2
    fwd_lane = lax.broadcasted_iota(jnp.int32, (RWKV_HEAD, LANES), 1) < half

    def step(j, carry):
        jb = tc - 1 - j

        def sel(f, b):
            return jnp.where(fwd_lane, f[j], b[jb])

        w, k, kkn, bv = sel(wf, wb), sel(kf, kb), sel(nf, nb), sel(bf, bb)
        rv_f, rv_b = rvf[j], rvb[jb]
        r = jnp.where(fwd_lane, rv_f, pltpu.roll(rv_b, half, axis=1))
        v = jnp.where(fwd_lane, pltpu.roll(rv_f, half, axis=1), rv_b)
        for vi in range(RWKV_HEAD):
            s = state_ref[vi]
            sa = jnp.sum(s * kkn, axis=0, keepdims=True)
            s2 = s * w + sa * bv + v[vi:vi + 1, :] * k
            state_ref[vi] = s2
            yrow = jnp.sum(s2 * r, axis=0, keepdims=True)
            yf_ref[j, pl.ds(vi, 1), :] = yrow
            yb_ref[jb, pl.ds(vi, 1), :] = yrow
        return carry

    lax.fori_loop(0, tc, step, 0)


def _rwkv_scan(xs, n_ctx):
    T = xs[0].shape[0]
    tc = RWKV_TIME_BLOCK
    assert T % tc == 0 and n_ctx % tc == 0
    nc, ncc = T // tc, n_ctx // tc
    blk = (tc, RWKV_HEAD, LANES)
    fwd = lambda c: (c, 0, 0)
    bwd = lambda c: (jnp.where(c < ncc, ncc - 1 - c, nc - 1 - (c - ncc)), 0, 0)
    out_sds = jax.ShapeDtypeStruct((T, RWKV_HEAD, LANES), F32)
    return pl.pallas_call(
        functools.partial(_rwkv_scan_body, tc=tc),
        out_shape=(out_sds, out_sds),
        grid=(nc,),
        in_specs=[pl.BlockSpec(blk, fwd)] * len(xs) + [pl.BlockSpec(blk, bwd)] * len(xs),
        out_specs=(pl.BlockSpec(blk, fwd), pl.BlockSpec(blk, bwd)),
        scratch_shapes=[pltpu.VMEM((RWKV_HEAD, RWKV_HEAD, LANES), F32)],
        compiler_params=_params("arbitrary"),
        name="rwkv_scan",
    )(*xs, *xs)


def _route_body(x_ref, rwt_ref, rb_ref, upper_ref, lower_ref, q_ref, g_ref, cnt_ref):
    tm = x_ref.shape[0]
    logits = lax.dot_general(rwt_ref[...], x_ref[...], (((1,), (1,)), ((), ())), preferred_element_type=F32,
                             precision=lax.Precision.HIGHEST)
    scores = jax.nn.sigmoid(logits)
    sel = scores + rb_ref[...]
    eidx = lax.broadcasted_iota(jnp.int32, (N_EXPERTS, tm), 0)
    slot = lax.broadcasted_iota(jnp.int32, (MOE_SLOTS, tm), 0)
    onehots = []
    for _ in range(TOP_K):
        m = jnp.max(sel, axis=0, keepdims=True)
        ij = jnp.min(jnp.where(sel == m, eidx, N_EXPERTS), axis=0, keepdims=True)
        oh = eidx == ij
        onehots.append(oh)
        sel = jnp.where(oh, -jnp.inf, sel)
    mask = functools.reduce(jnp.logical_or, onehots)
    maskf = jnp.where(mask, 1.0, 0.0)
    gsum = jnp.sum(jnp.where(mask, scores, 0.0), axis=0, keepdims=True)
    gates = scores / gsum * ROUTED_SCALE
    cnt = jnp.sum(maskf, axis=1, keepdims=True)
    cnt_pad = jnp.ceil(cnt * (1.0 / MOE_CHUNK)) * MOE_CHUNK
    lrank = jnp.dot(maskf.astype(BF16), upper_ref[...], preferred_element_type=F32)
    loff = jnp.dot(lower_ref[...], jnp.broadcast_to(cnt_pad, (N_EXPERTS, LANES)).astype(BF16),
                   preferred_element_type=F32)[:, 0:1]
    q = loff + lrank
    q8 = jnp.full((MOE_SLOTS, tm), -1.0, F32)
    g8 = jnp.zeros((MOE_SLOTS, tm), F32)
    for j, oh in enumerate(onehots):
        q8 = jnp.where(slot == j, jnp.sum(jnp.where(oh, q, 0.0), axis=0, keepdims=True), q8)
        g8 = jnp.where(slot == j, jnp.sum(jnp.where(oh, gates, 0.0), axis=0, keepdims=True), g8)
    q_ref[0] = q8.astype(jnp.int32)
    g_ref[0] = g8
    cnt_ref[0] = jnp.broadcast_to(cnt, (N_EXPERTS, LANES))


def _moe_route(tokens, router_w, router_b):
    n, d = tokens.shape
    tm = TOKEN_TILE
    nt = n // tm
    ar = jnp.arange(tm)
    upper = (ar[:, None] < ar[None, :]).astype(BF16)
    ae = jnp.arange(N_EXPERTS)
    lower = (ae[:, None] > ae[None, :]).astype(BF16)
    const = lambda shape: pl.BlockSpec(shape, lambda i: (0,) * len(shape))
    return pl.pallas_call(
        _route_body,
        out_shape=(jax.ShapeDtypeStruct((nt, MOE_SLOTS, tm), jnp.int32), jax.ShapeDtypeStruct((nt, MOE_SLOTS, tm), F32),
                   jax.ShapeDtypeStruct((nt, N_EXPERTS, LANES), F32)),
        grid=(nt,),
        in_specs=[pl.BlockSpec((tm, d), lambda i: (i, 0)), const((N_EXPERTS, d)), const((N_EXPERTS, 1)),
                  const((tm, tm)), const((N_EXPERTS, N_EXPERTS))],
        out_specs=(pl.BlockSpec((1, MOE_SLOTS, tm), lambda i: (i, 0, 0)),
                   pl.BlockSpec((1, MOE_SLOTS, tm), lambda i: (i, 0, 0)),
                   pl.BlockSpec((1, N_EXPERTS, LANES), lambda i: (i, 0, 0))),
        compiler_params=_params("arbitrary"),
        name="moe_router",
    )(tokens, router_w.T, router_b.reshape(N_EXPERTS, 1), upper, lower)


def _pack_bf16_pairs(x, exact=False):
    m = x.shape[1] // 2
    rounded = (lambda a: a) if exact else (lambda a: a.astype(BF16).astype(F32))
    bits = lambda a: lax.bitcast_convert_type(rounded(a), jnp.uint32)
    return (bits(x[:, :m]) & jnp.uint32(0xFFFF0000)) | (bits(x[:, m:]) >> 16)


def _unpack_bf16_pairs(u):
    hi = lax.bitcast_convert_type(u & jnp.uint32(0xFFFF0000), F32)
    lo = lax.bitcast_convert_type(u << 16, F32)
    return jnp.concatenate([hi.astype(BF16), lo.astype(BF16)], axis=1)


def _chunk_loops(i, base_ref, nchunk_ref, loff_ref, copy):
    def per_expert(e, total):
        k = i * N_EXPERTS + e
        n, base, lo = nchunk_ref[k], base_ref[k], loff_ref[k]

        def piece(c, carry):
            copy(pl.multiple_of(lo + c * MOE_CHUNK, MOE_CHUNK), pl.multiple_of(base + c * MOE_CHUNK, MOE_CHUNK)).start()
            return carry

        lax.fori_loop(0, n, piece, 0)
        return total + n

    return lax.fori_loop(0, N_EXPERTS, per_expert, 0)


def _dispatch_body(base_ref, nchunk_ref, loff_ref, bv_ref, x_ref, q_ref, xs_hbm, stage_ref, zero_ref, sem_z, sem):
    i = pl.program_id(0)
    br = zero_ref.shape[0]
    n_blocks = xs_hbm.shape[0] // br

    @pl.when(i == 0)
    def _():
        zero_ref[...] = jnp.zeros_like(zero_ref)

        def fill_copy(b):
            return pltpu.make_async_copy(zero_ref, xs_hbm.at[pl.ds(b * br, br)], sem_z)

        def fill(b, carry):
            @pl.when(bv_ref[b] < br)
            def _():
                fill_copy(b).start()
            return carry

        def fill_wait(b, carry):
            @pl.when(bv_ref[b] < br)
            def _():
                fill_copy(b).wait()
            return carry

        lax.fori_loop(0, n_blocks, fill, 0)
        lax.fori_loop(0, n_blocks, fill_wait, 0)

    rows, tm = stage_ref.shape[0], x_ref.shape[0]
    q8 = q_ref[0]
    pos = lax.broadcasted_iota(jnp.int32, (rows, tm), 0)
    hit = functools.reduce(jnp.logical_or, [pos == q8[j:j + 1, :] for j in range(TOP_K)])
    perm = jnp.where(hit, 1.0, 0.0).astype(BF16)
    stage_ref[...] = _pack_bf16_pairs(jnp.dot(perm, x_ref[...].astype(BF16), preferred_element_type=F32), exact=True)

    def copy(local_row, global_row):
        return pltpu.make_async_copy(stage_ref.at[pl.ds(local_row, MOE_CHUNK)], xs_hbm.at[pl.ds(global_row, MOE_CHUNK)], sem)

    total = _chunk_loops(i, base_ref, nchunk_ref, loff_ref, copy)

    def drain(c, carry):
        copy(0, 0).wait()
        return carry

    lax.fori_loop(0, total, drain, 0)


def _moe_dispatch(tokens, q_rows, dest_base, nchunk, loff, block_valid):
    n, d = tokens.shape
    tm, br = TOKEN_TILE, MOE_ROW_BLOCK
    n_rows = block_valid.shape[0] * br
    return pl.pallas_call(
        _dispatch_body,
        out_shape=jax.ShapeDtypeStruct((n_rows, d // 2), jnp.uint32),
        grid_spec=pltpu.PrefetchScalarGridSpec(
            num_scalar_prefetch=4,
            grid=(n // tm,),
            in_specs=[pl.BlockSpec((tm, d), lambda i, *_: (i, 0)),
                      pl.BlockSpec((1, MOE_SLOTS, tm), lambda i, *_: (i, 0, 0))],
            out_specs=pl.BlockSpec(memory_space=pl.ANY),
            scratch_shapes=[pltpu.VMEM((MOE_STAGE_ROWS, d // 2), jnp.uint32), pltpu.VMEM((br, d // 2), jnp.uint32),
                            pltpu.SemaphoreType.DMA, pltpu.SemaphoreType.DMA]),
        compiler_params=_params("arbitrary"),
        name="moe_dispatch",
    )(dest_base, nchunk, loff, block_valid, tokens, q_rows)


def _expert_body(be_ref, bv_ref, x_ref, w1_ref, w3_ref, w2_ref, y_ref, w1b, w3b, w2b):
    i = pl.program_id(0)
    valid = bv_ref[i]

    @pl.when(jnp.logical_or(i == 0, be_ref[i] != be_ref[jnp.maximum(i - 1, 0)]))
    def _():
        w1b[...] = w1_ref[0, 0].astype(BF16)
        w3b[...] = w3_ref[0, 0].astype(BF16)
        w2b[...] = w2_ref[0, 0].astype(BF16)

    @pl.when(valid > 0)
    def _():
        x = _unpack_bf16_pairs(x_ref[...])
        h1 = jnp.dot(x, w1b[...], preferred_element_type=F32)
        h3 = jnp.dot(x, w3b[...], preferred_element_type=F32)
        a = (h1 * jax.nn.sigmoid(h1) * h3).astype(BF16)
        y_ref[...] = _pack_bf16_pairs(jnp.dot(a, w2b[...], preferred_element_type=F32))

    @pl.when(valid <= 0)
    def _():
        y_ref[...] = jnp.zeros_like(y_ref)


def _moe_experts(xs, block_expert, block_valid, layer, w1, w3, w2):
    n_rows, dp = xs.shape
    br = MOE_ROW_BLOCK
    d, ff = w1.shape[-2:]
    return pl.pallas_call(
        _expert_body,
        out_shape=jax.ShapeDtypeStruct((n_rows, dp), jnp.uint32),
        grid_spec=pltpu.PrefetchScalarGridSpec(
            num_scalar_prefetch=2,
            grid=(n_rows // br,),
            in_specs=[pl.BlockSpec((br, dp), lambda i, be, bv: (i, 0)),
                      pl.BlockSpec((1, 1, d, ff), lambda i, be, bv: (layer, be[i], 0, 0)),
                      pl.BlockSpec((1, 1, d, ff), lambda i, be, bv: (layer, be[i], 0, 0)),
                      pl.BlockSpec((1, 1, ff, d), lambda i, be, bv: (layer, be[i], 0, 0))],
            out_specs=pl.BlockSpec((br, dp), lambda i, be, bv: (i, 0)),
            scratch_shapes=[pltpu.VMEM((d, ff), BF16), pltpu.VMEM((d, ff), BF16), pltpu.VMEM((ff, d), BF16)]),
        compiler_params=_params("arbitrary"),
        name="moe_experts",
    )(block_expert, block_valid, xs, w1, w3, w2)


def _combine_body(base_ref, nchunk_ref, loff_ref, h_ref, q_ref, g_ref, ys_hbm, sw1_ref, sw3_ref, sw2_ref,
                  x1_ref, mod_ref, ln_ref, out_ref, stage_ref, sem):
    i = pl.program_id(0)

    @pl.when(i == 0)
    def _():
        stage_ref[...] = jnp.zeros_like(stage_ref)

    def copy(local_row, global_row):
        return pltpu.make_async_copy(ys_hbm.at[pl.ds(global_row, MOE_CHUNK)], stage_ref.at[pl.ds(local_row, MOE_CHUNK)], sem)

    total = _chunk_loops(i, base_ref, nchunk_ref, loff_ref, copy)

    h = h_ref[...].astype(BF16)
    h1 = jnp.dot(h, sw1_ref[...], preferred_element_type=F32)
    h3 = jnp.dot(h, sw3_ref[...], preferred_element_type=F32)
    y = jnp.dot((h1 * jax.nn.sigmoid(h1) * h3).astype(BF16), sw2_ref[...], preferred_element_type=F32)
    tm, rows = h_ref.shape[0], stage_ref.shape[0]
    q8, g8 = q_ref[0], g_ref[0]
    pos = lax.broadcasted_iota(jnp.int32, (tm, rows), 1)
    gate_mat = jnp.zeros((tm, rows), F32)
    for j in range(TOP_K):
        gate_mat = jnp.where(pos == q8[:, j:j + 1], g8[:, j:j + 1], gate_mat)

    def drain(c, carry):
        copy(0, 0).wait()
        return carry

    lax.fori_loop(0, total, drain, 0)
    y = y + jnp.dot(gate_mat.astype(BF16), _unpack_bf16_pairs(stage_ref[...]), preferred_element_type=F32)
    out_ref[...] = _layer_norm_rows(DN_ALPHA * x1_ref[...] + mod_ref[0, 0, 5:6, :] * y, ln_ref[0:1, :], ln_ref[1:2, :])


def _moe_combine(tokens, q_cols, g_cols, dest_base, nchunk, loff, ys, sw1, sw3, sw2, x1, modtab, ln, tiles_per_batch,
                 first_tile):
    n, d = tokens.shape
    tm = TOKEN_TILE
    ff = sw1.shape[-1]
    whole = lambda shape: pl.BlockSpec(shape, lambda i, *_: (0,) * len(shape))
    rows = pl.BlockSpec((tm, d), lambda i, *_: (i, 0))
    slots = pl.BlockSpec((1, tm, MOE_SLOTS), lambda i, *_: (i, 0, 0))
    return pl.pallas_call(
        _combine_body,
        out_shape=jax.ShapeDtypeStruct((n, d), F32),
        grid_spec=pltpu.PrefetchScalarGridSpec(
            num_scalar_prefetch=3,
            grid=(n // tm,),
            in_specs=[rows, slots, slots, pl.BlockSpec(memory_space=pl.ANY),
                      whole((d, ff)), whole((d, ff)), whole((ff, d)),
                      rows, _mod_spec(tiles_per_batch, first_tile, True), whole(ln.shape)],
            out_specs=rows,
            scratch_shapes=[pltpu.VMEM((MOE_STAGE_ROWS, d // 2), jnp.uint32), pltpu.SemaphoreType.DMA]),
        compiler_params=_params("arbitrary"),
        name="moe_combine",
    )(dest_base, nchunk, loff, tokens, q_cols, g_cols, ys, sw1, sw3, sw2, x1, modtab, ln)


def _moe_block(h2, x1, modtab, ln, first_tile, layer, router_w, router_b, w1, w3, w2, sw1, sw3, sw2):
    B, Tp, d = h2.shape
    tokens = h2.reshape(B * Tp, d)
    n = B * Tp
    tm, br = TOKEN_TILE, MOE_ROW_BLOCK
    q_rows, g_rows, cnt = _moe_route(tokens, router_w, router_b)
    cnt = cnt[:, :, 0].astype(jnp.int32)
    nchunk = (cnt + MOE_CHUNK - 1) // MOE_CHUNK
    run = nchunk * MOE_CHUNK
    total = jnp.sum(run, axis=0)
    padded = (total + br - 1) // br * br
    p_end = jnp.cumsum(padded)
    offs = p_end - padded
    dest_base = (offs[None, :] + jnp.cumsum(run, axis=0) - run).reshape(-1).astype(jnp.int32)
    loff = jnp.cumsum(run, axis=1) - run
    nchunk, loff = nchunk.reshape(-1).astype(jnp.int32), loff.reshape(-1).astype(jnp.int32)
    n_blocks = -(-(n * TOP_K + (n // tm) * N_EXPERTS * (MOE_CHUNK - 1) + N_EXPERTS * (br - 1)) // br)
    blk_start = jnp.arange(n_blocks, dtype=jnp.int32) * br
    block_expert = jnp.minimum(jnp.sum(blk_start[:, None] >= p_end[None, :], axis=1), N_EXPERTS - 1).astype(jnp.int32)
    block_valid = jnp.clip(total[block_expert] - (blk_start - offs[block_expert]), 0, br).astype(jnp.int32)
    xs = _moe_dispatch(tokens, q_rows, dest_base, nchunk, loff, block_valid)
    ys = _moe_experts(xs, block_expert, block_valid, layer, w1, w3, w2)
    q_cols, g_cols = jnp.swapaxes(q_rows, 1, 2), jnp.swapaxes(g_rows, 1, 2)
    out = _moe_combine(tokens, q_cols, g_cols, dest_base, nchunk, loff, ys,
                       sw1.astype(BF16), sw3.astype(BF16), sw2.astype(BF16),
                       x1.reshape(n, d), modtab, ln, Tp // tm, first_tile)
    return out.reshape(B, Tp, d)


def _low_half():
    return lax.broadcasted_iota(jnp.int32, (1, LANES), 1) < LANES // 2


def _pair_attention(q, parts):
    low = _low_half()
    outs = []
    q = q * jnp.asarray(HEAD_DIM ** -0.5, q.dtype)
    for use_low in (True, False):
        qm = jnp.where(low == use_low, q, jnp.zeros_like(q))
        scores = []
        for k, _, b_lo, b_hi in parts:
            s = lax.dot_general(qm, k, (((1,), (1,)), ((), ())), preferred_element_type=F32)
            b = b_lo if use_low else b_hi
            scores.append(s if b is None else s + b)
        m = functools.reduce(jnp.maximum, [jnp.max(s, axis=-1, keepdims=True) for s in scores])
        den = 0.0
        num = 0.0
        for s, (_, v, _, _) in zip(scores, parts):
            p = jnp.exp(s - m)
            den = den + jnp.sum(p, axis=-1, keepdims=True)
            num = num + jnp.dot(p.astype(BF16), v, preferred_element_type=F32)
        outs.append(num / den)
    return jnp.where(low, outs[0], outs[1])


NA_Q_ROWS = 4
NA_BAND_ROWS = NA_WIN_ROWS + NA_Q_ROWS - 1


def _na_body(cls_ref, q_ref, k_ref, v_ref, bias_ref, o_ref, *, n_ctx, rows):
    s = pl.program_id(1)
    tq = q_ref.shape[1]
    ctx_steps = n_ctx // tq
    n_tiles = NA_DIM // LANES
    tile = lambda t: slice(t * LANES, (t + 1) * LANES)

    @pl.when(s < ctx_steps)
    def _():
        for t in range(n_tiles):
            part = (k_ref[0, 0:n_ctx, tile(t)], v_ref[0, 0:n_ctx, tile(t)], None, None)
            o_ref[0, :, tile(t)] = _pair_attention(q_ref[0, :, tile(t)], [part]).astype(o_ref.dtype)

    @pl.when(s >= ctx_steps)
    def _():
        first = (s - ctx_steps) * NA_Q_ROWS
        start = jnp.clip(first - NA_WIN_ROWS // 2, 0, rows - NA_BAND_ROWS)
        off = pl.multiple_of(n_ctx + start * GRID_W, GRID_W)
        band = pl.ds(off, NA_BAND_ROWS * GRID_W)
        for t in range(n_tiles):
            parts = [(k_ref[0, band, tile(t)], v_ref[0, band, tile(t)], bias_ref[0, 2 * t], bias_ref[0, 2 * t + 1]),
                     (k_ref[0, 0:n_ctx, tile(t)], v_ref[0, 0:n_ctx, tile(t)], None, None)]
            o_ref[0, :, tile(t)] = _pair_attention(q_ref[0, :, tile(t)], parts).astype(o_ref.dtype)


def _na_bias_table(rpb, rows):
    import numpy as np
    kc, W = NA_WIN_COLS, GRID_W
    cidx = np.arange(W)
    col_start = np.clip(cidx - kc // 2, 0, W - kc)
    col_in = (cidx[None, :] >= col_start[:, None]) & (cidx[None, :] < col_start[:, None] + kc)
    d_col = np.clip(cidx[None, :] - cidx[:, None], -(kc - 1), kc - 1) + kc - 1
    classes, class_of, d_rows, allowed = {}, [], [], []
    for step in range(rows // NA_Q_ROWS):
        first = step * NA_Q_ROWS
        band0 = min(max(first - NA_WIN_ROWS // 2, 0), rows - NA_BAND_ROWS)
        d_row = np.zeros((NA_Q_ROWS, NA_BAND_ROWS), np.int32)
        ok = np.zeros((NA_Q_ROWS, NA_BAND_ROWS), bool)
        for p in range(NA_Q_ROWS):
            win0 = min(max(first + p - NA_WIN_ROWS // 2, 0), rows - NA_WIN_ROWS)
            for j in range(NA_BAND_ROWS):
                ok[p, j] = win0 <= band0 + j < win0 + NA_WIN_ROWS
                d_row[p, j] = min(max(band0 + j - (first + p) + NA_WIN_ROWS - 1, 0), 2 * NA_WIN_ROWS - 2)
        key = (d_row.tobytes(), ok.tobytes())
        if key not in classes:
            classes[key] = len(classes)
            d_rows.append(d_row)
            allowed.append(ok)
        class_of.append(classes[key])
    d_rows, allowed = np.stack(d_rows), np.stack(allowed)
    tab = rpb[:, d_rows][..., d_col]
    mask = allowed[None, :, :, :, None, None] & col_in[None, None, None, None]
    tab = jnp.where(mask, tab, -jnp.inf)
    tab = jnp.transpose(tab, (1, 0, 2, 4, 3, 5))
    return tab.reshape(len(classes), NA_HEADS, NA_Q_ROWS * W, NA_BAND_ROWS * W), class_of


def _na_attention(pa, rpb, n_ctx):
    B, T, _ = pa.shape
    rows = (T - n_ctx) // GRID_W
    tq = NA_Q_ROWS * GRID_W
    assert rows % NA_Q_ROWS == 0 and rows >= NA_BAND_ROWS and n_ctx % tq == 0
    table, class_of = _na_bias_table(rpb, rows)
    step_class = jnp.array([0] * (n_ctx // tq) + class_of, jnp.int32)
    return pl.pallas_call(
        functools.partial(_na_body, n_ctx=n_ctx, rows=rows),
        out_shape=jax.ShapeDtypeStruct((B, T, NA_DIM), BF16),
        grid_spec=pltpu.PrefetchScalarGridSpec(
            num_scalar_prefetch=1,
            grid=(B, T // tq),
            in_specs=[pl.BlockSpec((1, tq, NA_DIM), lambda b, s, cls: (b, s, 0)),
                      pl.BlockSpec((1, T, NA_DIM), lambda b, s, cls: (b, 0, 1)),
                      pl.BlockSpec((1, T, NA_DIM), lambda b, s, cls: (b, 0, 2)),
                      pl.BlockSpec((1,) + table.shape[1:], lambda b, s, cls: (cls[s], 0, 0, 0))],
            out_specs=pl.BlockSpec((1, tq, NA_DIM), lambda b, s, cls: (b, s, 0))),
        compiler_params=_params("arbitrary", "arbitrary"),
        name="na_attention",
    )(step_class, pa, pa, pa, table)


def _rms_pair(x, gain):
    low = _low_half()
    sq = x * x
    s_lo = jnp.sum(jnp.where(low, sq, 0.0), axis=-1, keepdims=True)
    s_hi = jnp.sum(jnp.where(low, 0.0, sq), axis=-1, keepdims=True)
    ms = jnp.where(low, s_lo, s_hi) * (1.0 / HEAD_DIM)
    return x * lax.rsqrt(ms + NORM_EPS) * gain


def _rope_pair(x, cos, sin_signed):
    even = lax.broadcasted_iota(jnp.int32, (1, LANES), 1) % 2 == 0
    partner = jnp.where(even, pltpu.roll(x, LANES - 1, axis=1), pltpu.roll(x, 1, axis=1))
    return x * cos + partner * sin_signed


def _gqa_body(q_ref, k_ref, v_ref, cos_q, sin_q, cos_k, sin_k, gain_ref, o_ref, kn_ref, vn_ref, *, n_ctx):
    s = pl.program_id(1)
    tq = q_ref.shape[1]

    @pl.when(s == 0)
    def _():
        kn = _rope_pair(_rms_pair(k_ref[0], gain_ref[1:2, :]), cos_k[...], sin_k[...])
        vn = v_ref[0]
        low = _low_half()
        for src, dst in ((kn, kn_ref), (vn, vn_ref)):
            swapped = pltpu.roll(src, LANES // 2, axis=1)
            dst[0] = jnp.where(low, src, swapped).astype(BF16)
            dst[1] = jnp.where(low, swapped, src).astype(BF16)

    def run(n_keys):
        n_tiles = GQA_Q_DIM // LANES
        for t in range(n_tiles):
            g = t * GQA_KV_HEADS // n_tiles
            k, v = kn_ref[g, 0:n_keys, :], vn_ref[g, 0:n_keys, :]
            q = q_ref[0, :, t * LANES:(t + 1) * LANES]
            qn = _rope_pair(_rms_pair(q, gain_ref[0:1, :]), cos_q[...], sin_q[...]).astype(BF16)
            o_ref[0, :, t * LANES:(t + 1) * LANES] = _pair_attention(qn, [(k, v, None, None)]).astype(o_ref.dtype)

    @pl.when(s < n_ctx // tq)
    def _():
        run(n_ctx)

    @pl.when(s >= n_ctx // tq)
    def _():
        run(kn_ref.shape[1])


def _axial_rope(n_tokens):
    t = jnp.arange(n_tokens)
    row = (t // GRID_W).astype(F32)
    col = (t % GRID_W).astype(F32)
    inv = ROPE_THETA ** (-jnp.arange(0, ROPE_AXIS_DIM, 2, dtype=F32) / ROPE_AXIS_DIM)
    ang = jnp.concatenate([row[:, None] * inv, col[:, None] * inv], -1)
    return jnp.cos(ang), jnp.sin(ang)


def _gqa_rope_tables(T, n_ctx):
    cos, sin = _axial_rope(T - n_ctx)
    cos = jnp.concatenate([jnp.ones((n_ctx, ROPE_AXIS_DIM), F32), cos], 0)
    sin = jnp.concatenate([jnp.zeros((n_ctx, ROPE_AXIS_DIM), F32), sin], 0)
    cos = jnp.tile(jnp.repeat(cos, 2, axis=-1), (1, 2))
    sign = jnp.tile(jnp.array([-1.0, 1.0], F32), LANES // 2)
    sin = jnp.tile(jnp.repeat(sin, 2, axis=-1), (1, 2)) * sign
    return cos, sin


def _gqa_attention(pb, qk_gain, n_ctx):
    B, T, _ = pb.shape
    tq = GQA_Q_TILE
    cos, sin = _gqa_rope_tables(T, n_ctx)
    gain = jnp.tile(qk_gain, (1, 2))
    kv_blk = GQA_Q_DIM // GQA_KV_DIM
    return pl.pallas_call(
        functools.partial(_gqa_body, n_ctx=n_ctx),
        out_shape=jax.ShapeDtypeStruct((B, T, GQA_Q_DIM), BF16),
        grid=(B, T // tq),
        in_specs=[pl.BlockSpec((1, tq, GQA_Q_DIM), lambda b, s: (b, s, 0)),
                  pl.BlockSpec((1, T, GQA_KV_DIM), lambda b, s: (b, 0, kv_blk)),
                  pl.BlockSpec((1, T, GQA_KV_DIM), lambda b, s: (b, 0, kv_blk + 1)),
                  pl.BlockSpec((tq, LANES), lambda b, s: (s, 0)),
                  pl.BlockSpec((tq, LANES), lambda b, s: (s, 0)),
                  pl.BlockSpec((T, LANES), lambda b, s: (0, 0)),
                  pl.BlockSpec((T, LANES), lambda b, s: (0, 0)),
                  pl.BlockSpec((2, LANES), lambda b, s: (0, 0))],
        out_specs=pl.BlockSpec((1, tq, GQA_Q_DIM), lambda b, s: (b, s, 0)),
        scratch_shapes=[pltpu.VMEM((GQA_KV_HEADS, T, GQA_KV_DIM), BF16), pltpu.VMEM((GQA_KV_HEADS, T, GQA_KV_DIM), BF16)],
        compiler_params=_params("arbitrary", "arbitrary"),
        name="gqa_attention",
    )(pb, pb, pb, cos, sin, cos, sin, gain)


def _attention_mixers(x, modtab, w_in, rpb, qk_gain, n_ctx):
    w = w_in.astype(BF16)
    pa, pb = _in_proj(x, modtab, w, ((0, 3 * NA_DIM), (3 * NA_DIM, w.shape[1])), (BF16, F32))
    return jnp.concatenate([_na_attention(pa, rpb, n_ctx), _gqa_attention(pb, qk_gain, n_ctx)], -1)


def _log_sigmoid(x):
    return jnp.minimum(x, 0.0) - jnp.log(1.0 + jnp.exp(-jnp.abs(x)))


def _mlstm_body(xf, gcf, grf, ktf, xb, gcb, grb, ktb, bias_c, bias_r, hf_ref, hb_ref, c_ref, m_ref):
    @pl.when(pl.program_id(1) == 0)
    def _():
        c_ref[...] = jnp.zeros_like(c_ref)
        m_ref[...] = jnp.zeros_like(m_ref)

    L, H, W = MLSTM_CHUNK, MLSTM_HEADS, MLSTM_HEAD
    row = lax.broadcasted_iota(jnp.int32, (L, L), 0)
    col = lax.broadcasted_iota(jnp.int32, (L, L), 1)
    hi = lax.Precision.HIGHEST
    dirs = ((xf, gcf, grf, hf_ref, ktf), (xb, gcb, grb, hb_ref, ktb))
    part = lambda x_ref, j, h: x_ref[0, :, j * MLSTM_DIM + h * W:j * MLSTM_DIM + (h + 1) * W]
    combos = [(d, h) for d in range(2) for h in range(H)]
    head = lambda h: slice(h * W, (h + 1) * W)
    lanes = lambda column: jnp.broadcast_to(column, (L, LANES))
    gates = []
    for d, (_, gc_ref, gr_ref, _, _) in enumerate(dirs):
        seen = (col <= row) if d == 0 else (col >= row)
        seen_f = jnp.where(seen, 1.0, 0.0)
        g_col = gc_ref[0] + bias_c[...]
        g_row = gr_ref[0, 0] + bias_r[...]
        b_col = jnp.dot(seen_f, _log_sigmoid(g_col), preferred_element_type=F32, precision=hi)
        b_row = lax.dot_general(_log_sigmoid(g_row), seen_f, (((1,), (1,)), ((), ())), preferred_element_type=F32,
                                precision=hi)
        b_under_i = pltpu.roll(b_col, LANES - H, axis=1)
        run = g_col - b_under_i
        tok = lax.broadcasted_iota(jnp.int32, (L, LANES), 0)
        shift = 1
        while shift < L:
            if d == 0:
                run = jnp.maximum(run, jnp.where(tok >= shift, pltpu.roll(run, shift, axis=0), -jnp.inf))
            else:
                run = jnp.maximum(run, jnp.where(tok < L - shift, pltpu.roll(run, L - shift, axis=0), -jnp.inf))
            shift *= 2
        gates.append((seen, g_col, g_row, b_col, b_row, b_under_i + run))
    first = {}
    for d, h in combos:
        x_ref = dirs[d][0]
        s_idx = d * H + h
        q = (part(x_ref, 0, h) * W ** -0.5).astype(BF16)
        state = c_ref[s_idx]
        qk = lax.dot_general(q, part(x_ref, 1, h).astype(BF16), (((1,), (1,)), ((), ())), preferred_element_type=F32)
        q_state = jnp.dot(q, state.astype(BF16), preferred_element_type=F32)
        first[d, h] = (dirs[d][4][0, 0, head(h), :].astype(BF16), state, qk, q_state)
    second = {}
    for d, h in combos:
        seen, g_col, g_row, b_col, b_row, m_intra = gates[d]
        _, _, qk, q_state = first[d, h]
        v = part(dirs[d][0], 2, h)
        gi, gf = d * 2 * H + h, d * 2 * H + H + h
        s_idx = d * H + h
        m_state = m_ref[s_idx:s_idx + 1, :]
        b_t, i_t, mi_t = lanes(b_col[:, gf:gf + 1]), lanes(g_col[:, gi:gi + 1]), lanes(m_intra[:, gi:gi + 1])
        b_end = b_t[L - 1:L, :] if d == 0 else b_t[0:1, :]
        d_inter = b_t + m_state
        m_t = jnp.maximum(d_inter, mi_t)
        d_intra = jnp.where(seen, b_t[:, :L] - b_row[gf:gf + 1, :] + g_row[gi:gi + 1, :], -jnp.inf)
        s = qk * jnp.exp(d_intra - m_t[:, :L])
        w_inter = jnp.exp(d_inter - m_t)
        d_state = b_end - b_t + i_t
        m_new = jnp.maximum(b_end + m_state, jnp.max(d_state, axis=0, keepdims=True))
        w_s = jnp.exp(d_state - m_new)
        w_c = jnp.exp(b_end + m_state - m_new)
        ones = jnp.ones((L, LANES), BF16)
        second[d, h] = (s.astype(BF16), jnp.concatenate([v.astype(BF16), ones], axis=1),
                        jnp.concatenate([(v * w_s).astype(BF16), w_s.astype(BF16)], axis=1),
                        w_inter, jnp.exp(-m_t), jnp.concatenate([w_c, w_c], axis=1), m_new)
    writes = []
    for d, h in combos:
        s16, v_ones, vw_ws, w_inter, floor, w_c, m_new = second[d, h]
        kt16, state, _, q_state = first[d, h]
        s_idx = d * H + h
        sv = jnp.dot(s16, v_ones, preferred_element_type=F32)
        num = sv[:, :W] + w_inter * q_state[:, :W]
        den = sv[:, W:] + w_inter * q_state[:, W:]
        writes.append((dirs[d][3].at[0, :, head(h)], num / jnp.maximum(jnp.abs(den), floor)))
        writes.append((c_ref.at[s_idx], w_c * state + jnp.dot(kt16, vw_ws, preferred_element_type=F32)))
        writes.append((m_ref.at[s_idx:s_idx + 1, :], m_new))
    for ref, value in writes:
        ref[...] = value


def _mlstm(pm, gates, gate_b, n_ctx):
    B, T, _ = pm.shape
    L = MLSTM_CHUNK
    nc, ncc = T // L, n_ctx // L
    ng = 4 * MLSTM_HEADS
    g_cols = jnp.pad(gates, ((0, 0), (0, 0), (0, LANES - ng)))
    g_rows = jnp.swapaxes(gates.reshape(B, nc, L, ng), 2, 3)
    k_t = jnp.swapaxes(pm[..., MLSTM_DIM:2 * MLSTM_DIM].reshape(B, nc, L, MLSTM_DIM), 2, 3)
    ktr = lambda order: pl.BlockSpec((1, 1, MLSTM_DIM, L), lambda b, c: (b, order(c), 0, 0))
    bias = gate_b.reshape(ng)
    bias_c = jnp.pad(bias, (0, LANES - ng)).reshape(1, LANES)
    bias_r = bias.reshape(ng, 1)
    fwd = lambda c: c
    bwd = lambda c: jnp.where(c < ncc, ncc - 1 - c, nc - 1 - (c - ncc))
    blk = (1, L, MLSTM_DIM)
    seq = lambda order: pl.BlockSpec((1, L, 3 * MLSTM_DIM), lambda b, c: (b, order(c), 0))
    gcol = lambda order: pl.BlockSpec((1, L, LANES), lambda b, c: (b, order(c), 0))
    grow = lambda order: pl.BlockSpec((1, 1, ng, L), lambda b, c: (b, order(c), 0, 0))
    n_state = 2 * MLSTM_HEADS
    out_sds = jax.ShapeDtypeStruct((B, T, MLSTM_DIM), F32)
    return pl.pallas_call(
        _mlstm_body,
        out_shape=(out_sds, out_sds),
        grid=(B, nc),
        in_specs=[seq(fwd), gcol(fwd), grow(fwd), ktr(fwd), seq(bwd), gcol(bwd), grow(bwd), ktr(bwd),
                  pl.BlockSpec((1, LANES), lambda b, c: (0, 0)), pl.BlockSpec((ng, 1), lambda b, c: (0, 0))],
        out_specs=(pl.BlockSpec(blk, lambda b, c: (b, fwd(c), 0)), pl.BlockSpec(blk, lambda b, c: (b, bwd(c), 0))),
        scratch_shapes=[pltpu.VMEM((n_state, MLSTM_HEAD, MLSTM_HEAD + LANES), F32), pltpu.VMEM((n_state, LANES), F32)],
        compiler_params=_params("arbitrary", "arbitrary"),
        name="mlstm",
    )(pm, g_cols, g_rows, k_t, pm, g_cols, g_rows, k_t, bias_c, bias_r)


def _pair_sums(x):
    low = _low_half()
    s_lo = jnp.sum(jnp.where(low, x, 0.0), axis=-1, keepdims=True)
    s_hi = jnp.sum(jnp.where(low, 0.0, x), axis=-1, keepdims=True)
    return jnp.where(low, s_lo, s_hi)


def _rwkv_prep_body(p_ref, prev_ref, next_ref, tab_ref, tab2_ref, w1_ref, w2_ref, a1_ref, a2_ref, g1_ref, g2_ref,
                    rv_ref, n_ref, w_ref, k_ref, b_ref, gate_ref, *, n_ctx):
    s = pl.program_id(1)
    tm = p_ref.shape[1]
    C = RWKV_DIM
    ctx_tiles = n_ctx // tm
    x = p_ref[0]
    has_prev = jnp.logical_and(s != 0, s != ctx_tiles)
    has_next = jnp.logical_and(s != ctx_tiles - 1, s != pl.num_programs(1) - 1)
    prev_row = jnp.where(has_prev, prev_ref[0, SUBLANES - 1:SUBLANES, :], 0.0)
    next_row = jnp.where(has_next, next_ref[0, 0:1, :], 0.0)
    rowid = lax.broadcasted_iota(jnp.int32, (tm, 1), 0)
    up = jnp.where(rowid == 0, prev_row, pltpu.roll(x, 1, axis=0))
    dn = jnp.where(rowid == tm - 1, next_row, pltpu.roll(x, tm - 1, axis=0))
    d = 0.5 * (up + dn) - x
    part = lambda a, i: a[:, i * C:(i + 1) * C]
    mu = lambda i: tab_ref[i:i + 1, :]
    r = part(x, 0) + part(d, 0) * mu(0)
    k = part(x, 1) + part(d, 1) * mu(1)
    v = part(x, 2) + part(d, 2) * mu(2)
    z, dz = part(x, 3), part(d, 3)
    z_w, z_a, z_g = (z + dz * mu(3)).astype(BF16), (z + dz * mu(4)).astype(BF16), (z + dz * mu(5)).astype(BF16)
    lora = lambda t, w: jnp.dot(t.astype(BF16), w[...], preferred_element_type=F32)
    w_pre = tab2_ref[0:1, :] + lora(jnp.tanh(lora(z_w, w1_ref)), w2_ref)
    neg = -w_pre
    softplus = jnp.maximum(neg, 0.0) + jnp.log(1.0 + jnp.exp(-jnp.abs(neg)))
    decay = jnp.exp(-jnp.exp(-softplus - 0.5))
    iclr = jax.nn.sigmoid(tab2_ref[1:2, :] + lora(lora(z_a, a1_ref), a2_ref))
    gate_ref[0] = lora(jax.nn.sigmoid(lora(z_g, g1_ref)), g2_ref)
    kk = k * tab_ref[6:7, :]
    kk = jnp.concatenate(
        [kk[:, t * LANES:(t + 1) * LANES]
         * lax.rsqrt(jnp.maximum(_pair_sums(jnp.square(kk[:, t * LANES:(t + 1) * LANES])), 1e-24))
         for t in range(C // LANES)], axis=1)
    k2, kk2 = jnp.concatenate([k, k], axis=1), jnp.concatenate([kk, kk], axis=1)
    rv_ref[0] = jnp.concatenate([r, v], axis=1)
    n_ref[0] = -kk2
    w_ref[0] = decay
    k_ref[0] = k2 * (1.0 + (iclr - 1.0) * tab2_ref[2:3, :])
    b_ref[0] = kk2 * iclr


def _rwkv_prep(pr, n_ctx, mu, w0, w1, w2, a0, a1, a2, g1, g2, kvec):
    B, T, _ = pr.shape
    tm, C = TOKEN_TILE, RWKV_DIM
    per_tile = tm // SUBLANES
    tab = jnp.concatenate([mu, kvec[0:1], jnp.zeros((1, C), F32)], 0)
    cat = lambda a: jnp.concatenate([a[0], a[1]], -1)
    tab2 = jnp.concatenate([cat(w0)[None], cat(a0)[None], jnp.tile(kvec[1], 2)[None], jnp.zeros((5, 2 * C), F32)], 0)
    blockdiag = lambda a: jnp.concatenate([jnp.pad(a[0], ((0, 0), (0, C))), jnp.pad(a[1], ((0, 0), (C, 0)))], 0)
    gl = g1.shape[1]
    consts = (tab, tab2, cat(w1).astype(BF16), blockdiag(w2).astype(BF16), cat(a1).astype(BF16),
              blockdiag(a2).astype(BF16), jnp.pad(g1, ((0, 0), (0, LANES - gl))).astype(BF16),
              jnp.pad(g2, ((0, LANES - gl), (0, 0))).astype(BF16))
    const = lambda a: pl.BlockSpec(a.shape, lambda b, s: (0, 0))
    one = jax.ShapeDtypeStruct((B, T, C), F32)
    two = jax.ShapeDtypeStruct((B, T, 2 * C), F32)
    spec1 = pl.BlockSpec((1, tm, C), lambda b, s: (b, s, 0))
    spec2 = pl.BlockSpec((1, tm, 2 * C), lambda b, s: (b, s, 0))
    return pl.pallas_call(
        functools.partial(_rwkv_prep_body, n_ctx=n_ctx),
        out_shape=(two, two, two, two, two, one),
        grid=(B, T // tm),
        in_specs=[pl.BlockSpec((1, tm, 4 * C), lambda b, s: (b, s, 0)),
                  pl.BlockSpec((1, SUBLANES, 4 * C), lambda b, s: (b, jnp.maximum(s * per_tile - 1, 0), 0)),
                  pl.BlockSpec((1, SUBLANES, 4 * C), lambda b, s: (b, jnp.minimum((s + 1) * per_tile, T // SUBLANES - 1), 0))]
        + [const(a) for a in consts],
        out_specs=(spec2, spec2, spec2, spec2, spec2, spec1),
        compiler_params=_params("arbitrary", "arbitrary"),
        name="rwkv_prep",
    )(pr, pr, pr, *consts)


def _rec_post_body(y_ref, r_ref, v_ref, k_ref, gate_ref, hf_ref, hb_ref, o_ref, tab_ref, out_ref):
    C = RWKV_DIM
    for t in range(C // LANES):
        sl = slice(t * LANES, (t + 1) * LANES)
        y = y_ref[0, :, sl]
        yc = y - _pair_sums(y) * (1.0 / RWKV_HEAD)
        var = _pair_sums(yc * yc) * (1.0 / RWKV_HEAD)
        yn = yc * lax.rsqrt(var + RWKV_GN_EPS) * tab_ref[0:1, sl] + tab_ref[1:2, sl]
        k_sum = k_ref[0, :, sl] + k_ref[0, :, C + t * LANES:C + (t + 1) * LANES]
        bonus = _pair_sums(r_ref[0, :, sl] * k_sum * tab_ref[2:3, sl]) * v_ref[0, :, sl]
        out_ref[0, :, sl] = ((yn + bonus) * gate_ref[0, :, sl]).astype(out_ref.dtype)
    for t in range(MLSTM_HEADS):
        sl = slice(t * MLSTM_HEAD, (t + 1) * MLSTM_HEAD)
        h = hf_ref[0, :, sl] + hb_ref[0, :, sl]
        hn = h * lax.rsqrt(jnp.mean(h * h, axis=-1, keepdims=True) + NORM_EPS) * tab_ref[3:4, sl]
        out_ref[0, :, C + t * MLSTM_HEAD:C + (t + 1) * MLSTM_HEAD] = (hn * jax.nn.sigmoid(o_ref[0, :, sl])).astype(out_ref.dtype)


def _rec_post(y, rv, k_eff, gate, h_f, h_b, pm, gn, r_k, norm_g):
    B, T, C = y.shape
    tm = TOKEN_TILE
    tab = jnp.concatenate([gn, r_k.reshape(1, C), norm_g.reshape(1, C), jnp.zeros((4, C), F32)], 0)
    spec1 = pl.BlockSpec((1, tm, C), lambda b, s: (b, s, 0))
    return pl.pallas_call(
        _rec_post_body,
        out_shape=jax.ShapeDtypeStruct((B, T, C + MLSTM_DIM), BF16),
        grid=(B, T // tm),
        in_specs=[spec1, spec1, pl.BlockSpec((1, tm, C), lambda b, s: (b, s, 1)),
                  pl.BlockSpec((1, tm, 2 * C), lambda b, s: (b, s, 0)), spec1, spec1, spec1,
                  pl.BlockSpec((1, tm, MLSTM_DIM), lambda b, s: (b, s, 3)), pl.BlockSpec(tab.shape, lambda b, s: (0, 0))],
        out_specs=pl.BlockSpec((1, tm, C + MLSTM_DIM), lambda b, s: (b, s, 0)),
        compiler_params=_params("arbitrary", "arbitrary"),
        name="rec_post",
    )(y, rv, rv, k_eff, gate, h_f, h_b, pm, tab)


def _to_state_lanes(x):
    B, T, _ = x.shape
    return jnp.transpose(x.reshape(B, T, 2, RWKV_HEADS, RWKV_HEAD), (1, 4, 2, 0, 3)).reshape(T, RWKV_HEAD, LANES)


def _recurrent_mixers(x, modtab, w_in, n_ctx, mu, w0, w1, w2, a0, a1, a2, g1, g2, kvec, r_k, gn, gate_b, norm_g):
    B, T, _ = x.shape
    assert 2 * B * RWKV_HEADS == LANES
    n_main = RWKV_IN + 4 * MLSTM_DIM
    w = jnp.pad(w_in, ((0, 0), (0, n_main + LANES - w_in.shape[1]))).astype(BF16)
    pr, pm, pg = _in_proj(x, modtab, w, ((0, RWKV_IN), (RWKV_IN, n_main), (n_main, n_main + LANES)), (F32, F32, F32))
    rv, kkn, decay, k_eff, kka, gate = _rwkv_prep(pr, n_ctx, mu, w0, w1, w2, a0, a1, a2, g1, g2, kvec)
    scan_in = tuple(_to_state_lanes(a) for a in (rv, decay, k_eff, kkn, kka))
    h_f, h_b = _mlstm(pm, pg[..., :4 * MLSTM_HEADS], gate_b, n_ctx)
    scan_in, h_f, h_b = lax.optimization_barrier((scan_in, h_f, h_b))
    yf, yb = _rwkv_scan(scan_in, n_ctx)
    half = LANES // 2
    y = yf[:, :, :half] + yb[:, :, half:]
    y = jnp.transpose(y.reshape(T, RWKV_HEAD, B, RWKV_HEADS), (2, 0, 3, 1)).reshape(B, T, RWKV_DIM)
    return _rec_post(y, rv, k_eff, gate, h_f, h_b, pm, gn, r_k, norm_g)


def kernel(x, c, ctx, c_ctx, ada_w, ada_b, ln_g, ln_b, mix_w_out, att_w_in, na_rpb, qk_gain, rec_w_in, rwkv_mu, rwkv_w0, rwkv_w1, rwkv_w2, rwkv_a0, rwkv_a1, rwkv_a2, rwkv_g1, rwkv_g2, rwkv_kvec, rwkv_rk, rwkv_gn, mlstm_gate_b, mlstm_norm, moe_router, moe_bias, moe_w1, moe_w3, moe_w2, shared_w1, shared_w3, shared_w2):
    B, S, D = x.shape
    n_ctx = ctx.shape[1]
    assert D == D_MODEL and n_ctx % TOKEN_TILE == 0 and S % TOKEN_TILE == 0
    xs = jnp.concatenate([ctx, x], axis=1)
    mods = _ada_modulation(c, c_ctx, ada_w, ada_b)
    for i in range(DEPTH):
        last = i == DEPTH - 1
        j = i // 2
        mod = mods[i, :B].reshape(B, 6, D)
        mod_c = jnp.broadcast_to(mods[i, B].reshape(1, 6, D), (B, 6, D))
        modtab = jnp.pad(jnp.stack([mod_c, mod], axis=1), ((0, 0), (0, 0), (0, MOD_ROWS - 6), (0, 0)))
        if i % 2 == 0:
            m = _attention_mixers(xs, modtab, att_w_in[j], na_rpb[j], qk_gain[j], n_ctx)
        else:
            m = _recurrent_mixers(xs, modtab, rec_w_in[j], n_ctx, rwkv_mu[j], rwkv_w0[j], rwkv_w1[j], rwkv_w2[j],
                                  rwkv_a0[j], rwkv_a1[j], rwkv_a2[j], rwkv_g1[j], rwkv_g2[j], rwkv_kvec[j],
                                  rwkv_rk[j], rwkv_gn[j], mlstm_gate_b[j], mlstm_norm[j])
        first_tile = n_ctx // TOKEN_TILE if last else 0
        x1, h2 = _out_proj(m, xs, modtab, mix_w_out[i].astype(BF16), jnp.stack([ln_g[i, 0], ln_b[i, 0]]), first_tile)
        xs = _moe_block(h2, x1, modtab, jnp.stack([ln_g[i, 1], ln_b[i, 1]]), first_tile, i, moe_router[i], moe_bias[i],
                        moe_w1, moe_w3, moe_w2, shared_w1[i], shared_w3[i], shared_w2[i])
    return xs
```

```python
import functools

import jax
import jax.numpy as jnp
from jax import lax
from jax.experimental import pallas as pl
from jax.experimental.pallas import tpu as pltpu

D_MODEL = 1024
DEPTH = 4
GRID_W = 64
HEAD_DIM = 64
NA_HEADS = 8
NA_WIN_ROWS = 8
NA_WIN_COLS = 16
GQA_Q_HEADS = 8
GQA_KV_HEADS = 2
ROPE_THETA = 10000.0
ROPE_AXIS_DIM = HEAD_DIM // 2
NA_DIM = NA_HEADS * HEAD_DIM
GQA_Q_DIM = GQA_Q_HEADS * HEAD_DIM
GQA_KV_DIM = GQA_KV_HEADS * HEAD_DIM
RWKV_HEADS = 8
RWKV_HEAD = 64
RWKV_DIM = RWKV_HEADS * RWKV_HEAD
RWKV_GN_EPS = 64e-5
RWKV_IN = 4 * RWKV_DIM
MLSTM_HEADS = 4
MLSTM_HEAD = 128
MLSTM_DIM = MLSTM_HEADS * MLSTM_HEAD
MLSTM_CHUNK = 64
N_EXPERTS = 64
TOP_K = 6
ROUTED_SCALE = 2.5
DN_ALPHA = (2 * DEPTH) ** 0.25
LN_EPS = 1e-5
NORM_EPS = 1e-6
F32 = jnp.float32
BF16 = jnp.bfloat16

LANES = 128
SUBLANES = 8
VMEM_LIMIT = 48 * 1024 * 1024
TOKEN_TILE = 256
RWKV_TIME_BLOCK = 32
GQA_Q_TILE = 256
MLSTM_BATCH = 4
MOE_ROW_BLOCK = 512
MOE_SLOTS = 8
MOE_CHUNK = SUBLANES
MOE_STAGE_ROWS = TOKEN_TILE * TOP_K + N_EXPERTS * MOE_CHUNK
MOD_ROWS = 8


def _params(*semantics):
    return pltpu.CompilerParams(dimension_semantics=semantics, vmem_limit_bytes=VMEM_LIMIT)


def _mod_spec(tiles_per_batch, first_tile, flat):
    if flat:
        idx = lambda i, *_: (i // tiles_per_batch, jnp.minimum(i % tiles_per_batch + first_tile, 1), 0, 0)
    else:
        idx = lambda b, s, *_: (b, jnp.minimum(s + first_tile, 1), 0, 0)
    return pl.BlockSpec((1, 1, MOD_ROWS, D_MODEL), idx)


def _layer_norm_rows(z, g, b):
    mu = jnp.mean(z, axis=-1, keepdims=True)
    zc = z - mu
    var = jnp.mean(zc * zc, axis=-1, keepdims=True)
    return zc * lax.rsqrt(var + LN_EPS) * g + b


ADA_ROWS = 16


def _ada_body(c_ref, w_ref, b_ref, o_ref):
    x = c_ref[...]
    a = (x * jax.nn.sigmoid(x)).astype(BF16)
    o_ref[0] = jnp.dot(a, w_ref[0].astype(BF16), preferred_element_type=F32) + b_ref[0]


def _ada_modulation(c, c_ctx, ada_w, ada_b):
    B, D = c.shape
    depth, _, n = ada_w.shape
    cond = jnp.concatenate([c, c_ctx[None], jnp.zeros((ADA_ROWS - B - 1, D), F32)], 0)
    return pl.pallas_call(
        _ada_body,
        out_shape=jax.ShapeDtypeStruct((depth, ADA_ROWS, n), F32),
        grid=(depth, n // D),
        in_specs=[pl.BlockSpec((ADA_ROWS, D), lambda l, j: (0, 0)),
                  pl.BlockSpec((1, D, D), lambda l, j: (l, 0, j)),
                  pl.BlockSpec((1, 1, D), lambda l, j: (l, 0, j))],
        out_specs=pl.BlockSpec((1, ADA_ROWS, D), lambda l, j: (l, 0, j)),
        compiler_params=_params("arbitrary", "arbitrary"),
        name="ada_modulation",
    )(cond, ada_w, ada_b.reshape(depth, 1, n))
def _in_proj_body(x_ref, mod_ref, w_ref, *out_refs, splits):
    h = (x_ref[0] * (1.0 + mod_ref[0, 0, 1:2, :]) + mod_ref[0, 0, 0:1, :]).astype(BF16)
    for o_ref, (c0, c1) in zip(out_refs, splits):
        o_ref[0] = jnp.dot(h, w_ref[:, c0:c1], preferred_element_type=F32).astype(o_ref.dtype)


def _in_proj(x, modtab, w, splits, dtypes):
    B, T, D = x.shape
    tm = TOKEN_TILE
    outs = tuple(jax.ShapeDtypeStruct((B, T, c1 - c0), dt) for (c0, c1), dt in zip(splits, dtypes))
    return pl.pallas_call(
        functools.partial(_in_proj_body, splits=splits),
        out_shape=outs,
        grid=(B, T // tm),
        in_specs=[pl.BlockSpec((1, tm, D), lambda b, s: (b, s, 0)), _mod_spec(T // tm, 0, False),
                  pl.BlockSpec(w.shape, lambda b, s: (0, 0))],
        out_specs=tuple(pl.BlockSpec((1, tm, c1 - c0), lambda b, s: (b, s, 0)) for c0, c1 in splits),
        compiler_params=_params("arbitrary", "arbitrary"),
        name="in_proj",
    )(x, modtab, w)


def _out_proj_body(m_ref, x_ref, mod_ref, w_ref, ln_ref, x1_ref, h2_ref):
    y = jnp.dot(m_ref[0], w_ref[...], preferred_element_type=F32)
    mod = mod_ref[0, 0]
    x1 = _layer_norm_rows(DN_ALPHA * x_ref[0] + mod[2:3, :] * y, ln_ref[0:1, :], ln_ref[1:2, :])
    x1_ref[0] = x1
    h2_ref[0] = x1 * (1.0 + mod[4:5, :]) + mod[3:4, :]


def _out_proj(m, x, modtab, w, ln, first_tile):
    B, T, D = x.shape
    tm = TOKEN_TILE
    n_tiles = T // tm - first_tile
    rows = lambda b, s: (b, s + first_tile, 0)
    out_sds = jax.ShapeDtypeStruct((B, n_tiles * tm, D), F32)
    return pl.pallas_call(
        _out_proj_body,
        out_shape=(out_sds, out_sds),
        grid=(B, n_tiles),
        in_specs=[pl.BlockSpec((1, tm, m.shape[-1]), rows), pl.BlockSpec((1, tm, D), rows),
                  _mod_spec(T // tm, first_tile, False),
                  pl.BlockSpec(w.shape, lambda b, s: (0, 0)), pl.BlockSpec(ln.shape, lambda b, s: (0, 0))],
        out_specs=(pl.BlockSpec((1, tm, D), lambda b, s: (b, s, 0)),) * 2,
        compiler_params=_params("arbitrary", "arbitrary"),
        name="out_proj",
    )(m, x, modtab, w, ln)


def _rwkv_scan_body(rvf, wf, kf, nf, bf, rvb, wb, kb, nb, bb, yf_ref, yb_ref, state_ref, *, tc):
    @pl.when(pl.program_id(0) == 0)
    def _():
        state_ref[...] = jnp.zeros_like(state_ref)

    half = LANES // 2
    fwd_lane = lax.broadcasted_iota(jnp.int32, (RWKV_HEAD, LANES), 1) < half

    def step(j, carry):
        jb = tc - 1 - j

        def sel(f, b):
            return jnp.where(fwd_lane, f[j], b[jb])

        w, k, kkn, bv = sel(wf, wb), sel(kf, kb), sel(nf, nb), sel(bf, bb)
        rv_f, rv_b = rvf[j], rvb[jb]
        r = jnp.where(fwd_lane, rv_f, pltpu.roll(rv_b, half, axis=1))
        v = jnp.where(fwd_lane, pltpu.roll(rv_f, half, axis=1), rv_b)
        for vi in range(RWKV_HEAD):
            s = state_ref[vi]
            sa = jnp.sum(s * kkn, axis=0, keepdims=True)
            s2 = s * w + sa * bv + v[vi:vi + 1, :] * k
            state_ref[vi] = s2
            yrow = jnp.sum(s2 * r, axis=0, keepdims=True)
            yf_ref[j, pl.ds(vi, 1), :] = yrow
            yb_ref[jb, pl.ds(vi, 1), :] = yrow
        return carry

    lax.fori_loop(0, tc, step, 0)


def _rwkv_scan(xs, n_ctx):
    T = xs[0].shape[0]
    tc = RWKV_TIME_BLOCK
    assert T % tc == 0 and n_ctx % tc == 0
    nc, ncc = T // tc, n_ctx // tc
    blk = (tc, RWKV_HEAD, LANES)
    fwd = lambda c: (c, 0, 0)
    bwd = lambda c: (jnp.where(c < ncc, ncc - 1 - c, nc - 1 - (c - ncc)), 0, 0)
    out_sds = jax.ShapeDtypeStruct((T, RWKV_HEAD, LANES), F32)
    return pl.pallas_call(
        functools.partial(_rwkv_scan_body, tc=tc),
        out_shape=(out_sds, out_sds),
        grid=(nc,),
        in_specs=[pl.BlockSpec(blk, fwd)] * len(xs) + [pl.BlockSpec(blk, bwd)] * len(xs),
        out_specs=(pl.BlockSpec(blk, fwd), pl.BlockSpec(blk, bwd)),
        scratch_shapes=[pltpu.VMEM((RWKV_HEAD, RWKV_HEAD, LANES), F32)],
        compiler_params=_params("arbitrary"),
        name="rwkv_scan",
    )(*xs, *xs)


def _route_body(x_ref, rwt_ref, rb_ref, upper_ref, lower_ref, q_ref, g_ref, cnt_ref):
    tm = x_ref.shape[0]
    logits = lax.dot_general(rwt_ref[...], x_ref[...], (((1,), (1,)), ((), ())), preferred_element_type=F32,
                             precision=lax.Precision.HIGHEST)
    scores = jax.nn.sigmoid(logits)
    sel = scores + rb_ref[...]
    eidx = lax.broadcasted_iota(jnp.int32, (N_EXPERTS, tm), 0)
    slot = lax.broadcasted_iota(jnp.int32, (MOE_SLOTS, tm), 0)
    onehots = []
    for _ in range(TOP_K):
        m = jnp.max(sel, axis=0, keepdims=True)
        ij = jnp.min(jnp.where(sel == m, eidx, N_EXPERTS), axis=0, keepdims=True)
        oh = eidx == ij
        onehots.append(oh)
        sel = jnp.where(oh, -jnp.inf, sel)
    mask = functools.reduce(jnp.logical_or, onehots)
    maskf = jnp.where(mask, 1.0, 0.0)
    gsum = jnp.sum(jnp.where(mask, scores, 0.0), axis=0, keepdims=True)
    gates = scores / gsum * ROUTED_SCALE
    cnt = jnp.sum(maskf, axis=1, keepdims=True)
    cnt_pad = jnp.ceil(cnt * (1.0 / MOE_CHUNK)) * MOE_CHUNK
    lrank = jnp.dot(maskf.astype(BF16), upper_ref[...], preferred_element_type=F32)
    loff = jnp.dot(lower_ref[...], jnp.broadcast_to(cnt_pad, (N_EXPERTS, LANES)).astype(BF16),
                   preferred_element_type=F32)[:, 0:1]
    q = loff + lrank
    q8 = jnp.full((MOE_SLOTS, tm), -1.0, F32)
    g8 = jnp.zeros((MOE_SLOTS, tm), F32)
    for j, oh in enumerate(onehots):
        q8 = jnp.where(slot == j, jnp.sum(jnp.where(oh, q, 0.0), axis=0, keepdims=True), q8)
        g8 = jnp.where(slot == j, jnp.sum(jnp.where(oh, gates, 0.0), axis=0, keepdims=True), g8)
    q_ref[0] = q8.astype(jnp.int32)
    g_ref[0] = g8
    cnt_ref[0] = jnp.broadcast_to(cnt, (N_EXPERTS, LANES))


def _moe_route(tokens, router_w, router_b):
    n, d = tokens.shape
    tm = TOKEN_TILE
    nt = n // tm
    ar = jnp.arange(tm)
    upper = (ar[:, None] < ar[None, :]).astype(BF16)
    ae = jnp.arange(N_EXPERTS)
    lower = (ae[:, None] > ae[None, :]).astype(BF16)
    const = lambda shape: pl.BlockSpec(shape, lambda i: (0,) * len(shape))
    return pl.pallas_call(
        _route_body,
        out_shape=(jax.ShapeDtypeStruct((nt, MOE_SLOTS, tm), jnp.int32), jax.ShapeDtypeStruct((nt, MOE_SLOTS, tm), F32),
                   jax.ShapeDtypeStruct((nt, N_EXPERTS, LANES), F32)),
        grid=(nt,),
        in_specs=[pl.BlockSpec((tm, d), lambda i: (i, 0)), const((N_EXPERTS, d)), const((N_EXPERTS, 1)),
                  const((tm, tm)), const((N_EXPERTS, N_EXPERTS))],
        out_specs=(pl.BlockSpec((1, MOE_SLOTS, tm), lambda i: (i, 0, 0)),
                   pl.BlockSpec((1, MOE_SLOTS, tm), lambda i: (i, 0, 0)),
                   pl.BlockSpec((1, N_EXPERTS, LANES), lambda i: (i, 0, 0))),
        compiler_params=_params("arbitrary"),
        name="moe_router",
    )(tokens, router_w.T, router_b.reshape(N_EXPERTS, 1), upper, lower)


def _chunk_loops(i, base_ref, nchunk_ref, loff_ref, copy):
    def per_expert(e, total):
        k = i * N_EXPERTS + e
        n, base, lo = nchunk_ref[k], base_ref[k], loff_ref[k]

        def piece(c, carry):
            copy(pl.multiple_of(lo + c * MOE_CHUNK, MOE_CHUNK), pl.multiple_of(base + c * MOE_CHUNK, MOE_CHUNK)).start()
            return carry

        lax.fori_loop(0, n, piece, 0)
        return total + n

    return lax.fori_loop(0, N_EXPERTS, per_expert, 0)


def _dispatch_body(base_ref, nchunk_ref, loff_ref, bv_ref, x_ref, q_ref, xs_hbm, stage_ref, zero_ref, sem_z, sem):
    i = pl.program_id(0)
    br = zero_ref.shape[0]
    n_blocks = xs_hbm.shape[0] // br

    @pl.when(i == 0)
    def _():
        zero_ref[...] = jnp.zeros_like(zero_ref)

        def fill_copy(b):
            return pltpu.make_async_copy(zero_ref, xs_hbm.at[pl.ds(b * br, br)], sem_z)

        def fill(b, carry):
            @pl.when(bv_ref[b] < br)
            def _():
                fill_copy(b).start()
            return carry

        def fill_wait(b, carry):
            @pl.when(bv_ref[b] < br)
            def _():
                fill_copy(b).wait()
            return carry

        lax.fori_loop(0, n_blocks, fill, 0)
        lax.fori_loop(0, n_blocks, fill_wait, 0)

    rows, tm = stage_ref.shape[0], x_ref.shape[0]
    q8 = q_ref[0]
    pos = lax.broadcasted_iota(jnp.int32, (rows, tm), 0)
    hit = functools.reduce(jnp.logical_or, [pos == q8[j:j + 1, :] for j in range(TOP_K)])
    perm = jnp.where(hit, 1.0, 0.0).astype(BF16)
    stage_ref[...] = jnp.dot(perm, x_ref[...].astype(BF16), preferred_element_type=F32)

    def copy(local_row, global_row):
        return pltpu.make_async_copy(stage_ref.at[pl.ds(local_row, MOE_CHUNK)], xs_hbm.at[pl.ds(global_row, MOE_CHUNK)], sem)

    total = _chunk_loops(i, base_ref, nchunk_ref, loff_ref, copy)

    def drain(c, carry):
        copy(0, 0).wait()
        return carry

    lax.fori_loop(0, total, drain, 0)


def _moe_dispatch(tokens, q_rows, dest_base, nchunk, loff, block_valid):
    n, d = tokens.shape
    tm, br = TOKEN_TILE, MOE_ROW_BLOCK
    n_rows = block_valid.shape[0] * br
    return pl.pallas_call(
        _dispatch_body,
        out_shape=jax.ShapeDtypeStruct((n_rows, d), F32),
        grid_spec=pltpu.PrefetchScalarGridSpec(
            num_scalar_prefetch=4,
            grid=(n // tm,),
            in_specs=[pl.BlockSpec((tm, d), lambda i, *_: (i, 0)),
                      pl.BlockSpec((1, MOE_SLOTS, tm), lambda i, *_: (i, 0, 0))],
            out_specs=pl.BlockSpec(memory_space=pl.ANY),
            scratch_shapes=[pltpu.VMEM((MOE_STAGE_ROWS, d), F32), pltpu.VMEM((br, d), F32),
                            pltpu.SemaphoreType.DMA, pltpu.SemaphoreType.DMA]),
        compiler_params=_params("arbitrary"),
        name="moe_dispatch",
    )(dest_base, nchunk, loff, block_valid, tokens, q_rows)


def _expert_body(be_ref, bv_ref, x_ref, w1_ref, w3_ref, w2_ref, y_ref, w1b, w3b, w2b):
    i = pl.program_id(0)
    valid = bv_ref[i]

    @pl.when(jnp.logical_or(i == 0, be_ref[i] != be_ref[jnp.maximum(i - 1, 0)]))
    def _():
        w1b[...] = w1_ref[0, 0].astype(BF16)
        w3b[...] = w3_ref[0, 0].astype(BF16)
        w2b[...] = w2_ref[0, 0].astype(BF16)

    @pl.when(valid > 0)
    def _():
        x = x_ref[...].astype(BF16)
        h1 = jnp.dot(x, w1b[...], preferred_element_type=F32)
        h3 = jnp.dot(x, w3b[...], preferred_element_type=F32)
        a = (h1 * jax.nn.sigmoid(h1) * h3).astype(BF16)
        y_ref[...] = jnp.dot(a, w2b[...], preferred_element_type=F32)

    @pl.when(valid <= 0)
    def _():
        y_ref[...] = jnp.zeros_like(y_ref)


def _moe_experts(xs, block_expert, block_valid, layer, w1, w3, w2):
    n_rows, dp = xs.shape
    br = MOE_ROW_BLOCK
    d, ff = w1.shape[-2:]
    return pl.pallas_call(
        _expert_body,
        out_shape=jax.ShapeDtypeStruct((n_rows, dp), F32),
        grid_spec=pltpu.PrefetchScalarGridSpec(
            num_scalar_prefetch=2,
            grid=(n_rows // br,),
            in_specs=[pl.BlockSpec((br, dp), lambda i, be, bv: (i, 0)),
                      pl.BlockSpec((1, 1, d, ff), lambda i, be, bv: (layer, be[i], 0, 0)),
                      pl.BlockSpec((1, 1, d, ff), lambda i, be, bv: (layer, be[i], 0, 0)),
                      pl.BlockSpec((1, 1, ff, d), lambda i, be, bv: (layer, be[i], 0, 0))],
            out_specs=pl.BlockSpec((br, dp), lambda i, be, bv: (i, 0)),
            scratch_shapes=[pltpu.VMEM((d, ff), BF16), pltpu.VMEM((d, ff), BF16), pltpu.VMEM((ff, d), BF16)]),
        compiler_params=_params("arbitrary"),
        name="moe_experts",
    )(block_expert, block_valid, xs, w1, w3, w2)


def _combine_body(base_ref, nchunk_ref, loff_ref, h_ref, q_ref, g_ref, ys_hbm, sw1_ref, sw3_ref, sw2_ref,
                  x1_ref, mod_ref, ln_ref, out_ref, stage_ref, sem):
    i = pl.program_id(0)

    @pl.when(i == 0)
    def _():
        stage_ref[...] = jnp.zeros_like(stage_ref)

    def copy(local_row, global_row):
        return pltpu.make_async_copy(ys_hbm.at[pl.ds(global_row, MOE_CHUNK)], stage_ref.at[pl.ds(local_row, MOE_CHUNK)], sem)

    total = _chunk_loops(i, base_ref, nchunk_ref, loff_ref, copy)

    h = h_ref[...].astype(BF16)
    h1 = jnp.dot(h, sw1_ref[...], preferred_element_type=F32)
    h3 = jnp.dot(h, sw3_ref[...], preferred_element_type=F32)
    y = jnp.dot((h1 * jax.nn.sigmoid(h1) * h3).astype(BF16), sw2_ref[...], preferred_element_type=F32)
    tm, rows = h_ref.shape[0], stage_ref.shape[0]
    q8, g8 = q_ref[0], g_ref[0]
    pos = lax.broadcasted_iota(jnp.int32, (tm, rows), 1)
    gate_mat = jnp.zeros((tm, rows), F32)
    for j in range(TOP_K):
        gate_mat = jnp.where(pos == q8[:, j:j + 1], g8[:, j:j + 1], gate_mat)

    def drain(c, carry):
        copy(0, 0).wait()
        return carry

    lax.fori_loop(0, total, drain, 0)
    y = y + jnp.dot(gate_mat.astype(BF16), stage_ref[...].astype(BF16), preferred_element_type=F32)
    out_ref[...] = _layer_norm_rows(DN_ALPHA * x1_ref[...] + mod_ref[0, 0, 5:6, :] * y, ln_ref[0:1, :], ln_ref[1:2, :])


def _moe_combine(tokens, q_cols, g_cols, dest_base, nchunk, loff, ys, sw1, sw3, sw2, x1, modtab, ln, tiles_per_batch,
                 first_tile):
    n, d = tokens.shape
    tm = TOKEN_TILE
    ff = sw1.shape[-1]
    whole = lambda shape: pl.BlockSpec(shape, lambda i, *_: (0,) * len(shape))
    rows = pl.BlockSpec((tm, d), lambda i, *_: (i, 0))
    slots = pl.BlockSpec((1, tm, MOE_SLOTS), lambda i, *_: (i, 0, 0))
    return pl.pallas_call(
        _combine_body,
        out_shape=jax.ShapeDtypeStruct((n, d), F32),
        grid_spec=pltpu.PrefetchScalarGridSpec(
            num_scalar_prefetch=3,
            grid=(n // tm,),
            in_specs=[rows, slots, slots, pl.BlockSpec(memory_space=pl.ANY),
                      whole((d, ff)), whole((d, ff)), whole((ff, d)),
                      rows, _mod_spec(tiles_per_batch, first_tile, True), whole(ln.shape)],
            out_specs=rows,
            scratch_shapes=[pltpu.VMEM((MOE_STAGE_ROWS, d), F32), pltpu.SemaphoreType.DMA]),
        compiler_params=_params("arbitrary"),
        name="moe_combine",
    )(dest_base, nchunk, loff, tokens, q_cols, g_cols, ys, sw1, sw3, sw2, x1, modtab, ln)


def _moe_block(h2, x1, modtab, ln, first_tile, layer, router_w, router_b, w1, w3, w2, sw1, sw3, sw2):
    B, Tp, d = h2.shape
    tokens = h2.reshape(B * Tp, d)
    n = B * Tp
    tm, br = TOKEN_TILE, MOE_ROW_BLOCK
    q_rows, g_rows, cnt = _moe_route(tokens, router_w, router_b)
    cnt = cnt[:, :, 0].astype(jnp.int32)
    nchunk = (cnt + MOE_CHUNK - 1) // MOE_CHUNK
    run = nchunk * MOE_CHUNK
    total = jnp.sum(run, axis=0)
    padded = (total + br - 1) // br * br
    p_end = jnp.cumsum(padded)
    offs = p_end - padded
    dest_base = (offs[None, :] + jnp.cumsum(run, axis=0) - run).reshape(-1).astype(jnp.int32)
    loff = jnp.cumsum(run, axis=1) - run
    nchunk, loff = nchunk.reshape(-1).astype(jnp.int32), loff.reshape(-1).astype(jnp.int32)
    n_blocks = -(-(n * TOP_K + (n // tm) * N_EXPERTS * (MOE_CHUNK - 1) + N_EXPERTS * (br - 1)) // br)
    blk_start = jnp.arange(n_blocks, dtype=jnp.int32) * br
    block_expert = jnp.minimum(jnp.sum(blk_start[:, None] >= p_end[None, :], axis=1), N_EXPERTS - 1).astype(jnp.int32)
    block_valid = jnp.clip(total[block_expert] - (blk_start - offs[block_expert]), 0, br).astype(jnp.int32)
    xs = _moe_dispatch(tokens, q_rows, dest_base, nchunk, loff, block_valid)
    ys = _moe_experts(xs, block_expert, block_valid, layer, w1, w3, w2)
    q_cols, g_cols = jnp.swapaxes(q_rows, 1, 2), jnp.swapaxes(g_rows, 1, 2)
    out = _moe_combine(tokens, q_cols, g_cols, dest_base, nchunk, loff, ys,
                       sw1.astype(BF16), sw3.astype(BF16), sw2.astype(BF16),
                       x1.reshape(n, d), modtab, ln, Tp // tm, first_tile)
    return out.reshape(B, Tp, d)


def _low_half():
    return lax.broadcasted_iota(jnp.int32, (1, LANES), 1) < LANES // 2


def _pair_attention(q, parts):
    low = _low_half()
    outs = []
    q = q * jnp.asarray(HEAD_DIM ** -0.5, q.dtype)
    for use_low in (True, False):
        qm = jnp.where(low == use_low, q, jnp.zeros_like(q))
        scores = []
        for k, _, b_lo, b_hi in parts:
            s = lax.dot_general(qm, k, (((1,), (1,)), ((), ())), preferred_element_type=F32)
            b = b_lo if use_low else b_hi
            scores.append(s if b is None else s + b)
        m = functools.reduce(jnp.maximum, [jnp.max(s, axis=-1, keepdims=True) for s in scores])
        den = 0.0
        num = 0.0
        for s, (_, v, _, _) in zip(scores, parts):
            p = jnp.exp(s - m)
            den = den + jnp.sum(p, axis=-1, keepdims=True)
            num = num + jnp.dot(p.astype(BF16), v, preferred_element_type=F32)
        outs.append(num / den)
    return jnp.where(low, outs[0], outs[1])


NA_Q_ROWS = 4
NA_BAND_ROWS = NA_WIN_ROWS + NA_Q_ROWS - 1


def _na_body(cls_ref, q_ref, k_ref, v_ref, bias_ref, o_ref, *, n_ctx, rows):
    s = pl.program_id(1)
    tq = q_ref.shape[1]
    ctx_steps = n_ctx // tq
    n_tiles = NA_DIM // LANES
    tile = lambda t: slice(t * LANES, (t + 1) * LANES)

    @pl.when(s < ctx_steps)
    def _():
        for t in range(n_tiles):
            part = (k_ref[0, 0:n_ctx, tile(t)], v_ref[0, 0:n_ctx, tile(t)], None, None)
            o_ref[0, :, tile(t)] = _pair_attention(q_ref[0, :, tile(t)], [part]).astype(o_ref.dtype)

    @pl.when(s >= ctx_steps)
    def _():
        first = (s - ctx_steps) * NA_Q_ROWS
        start = jnp.clip(first - NA_WIN_ROWS // 2, 0, rows - NA_BAND_ROWS)
        off = pl.multiple_of(n_ctx + start * GRID_W, GRID_W)
        band = pl.ds(off, NA_BAND_ROWS * GRID_W)
        for t in range(n_tiles):
            parts = [(k_ref[0, band, tile(t)], v_ref[0, band, tile(t)], bias_ref[0, 2 * t], bias_ref[0, 2 * t + 1]),
                     (k_ref[0, 0:n_ctx, tile(t)], v_ref[0, 0:n_ctx, tile(t)], None, None)]
            o_ref[0, :, tile(t)] = _pair_attention(q_ref[0, :, tile(t)], parts).astype(o_ref.dtype)


def _na_bias_table(rpb, rows):
    import numpy as np
    kc, W = NA_WIN_COLS, GRID_W
    cidx = np.arange(W)
    col_start = np.clip(cidx - kc // 2, 0, W - kc)
    col_in = (cidx[None, :] >= col_start[:, None]) & (cidx[None, :] < col_start[:, None] + kc)
    d_col = np.clip(cidx[None, :] - cidx[:, None], -(kc - 1), kc - 1) + kc - 1
    classes, class_of, d_rows, allowed = {}, [], [], []
    for step in range(rows // NA_Q_ROWS):
        first = step * NA_Q_ROWS
        band0 = min(max(first - NA_WIN_ROWS // 2, 0), rows - NA_BAND_ROWS)
        d_row = np.zeros((NA_Q_ROWS, NA_BAND_ROWS), np.int32)
        ok = np.zeros((NA_Q_ROWS, NA_BAND_ROWS), bool)
        for p in range(NA_Q_ROWS):
            win0 = min(max(first + p - NA_WIN_ROWS // 2, 0), rows - NA_WIN_ROWS)
            for j in range(NA_BAND_ROWS):
                ok[p, j] = win0 <= band0 + j < win0 + NA_WIN_ROWS
                d_row[p, j] = min(max(band0 + j - (first + p) + NA_WIN_ROWS - 1, 0), 2 * NA_WIN_ROWS - 2)
        key = (d_row.tobytes(), ok.tobytes())
        if key not in classes:
            classes[key] = len(classes)
            d_rows.append(d_row)
            allowed.append(ok)
        class_of.append(classes[key])
    d_rows, allowed = np.stack(d_rows), np.stack(allowed)
    tab = rpb[:, d_rows][..., d_col]
    mask = allowed[None, :, :, :, None, None] & col_in[None, None, None, None]
    tab = jnp.where(mask, tab, -jnp.inf)
    tab = jnp.transpose(tab, (1, 0, 2, 4, 3, 5))
    return tab.reshape(len(classes), NA_HEADS, NA_Q_ROWS * W, NA_BAND_ROWS * W), class_of


def _na_attention(pa, rpb, n_ctx):
    B, T, _ = pa.shape
    rows = (T - n_ctx) // GRID_W
    tq = NA_Q_ROWS * GRID_W
    assert rows % NA_Q_ROWS == 0 and rows >= NA_BAND_ROWS and n_ctx % tq == 0
    table, class_of = _na_bias_table(rpb, rows)
    step_class = jnp.array([0] * (n_ctx // tq) + class_of, jnp.int32)
    return pl.pallas_call(
        functools.partial(_na_body, n_ctx=n_ctx, rows=rows),
        out_shape=jax.ShapeDtypeStruct((B, T, NA_DIM), BF16),
        grid_spec=pltpu.PrefetchScalarGridSpec(
            num_scalar_prefetch=1,
            grid=(B, T // tq),
            in_specs=[pl.BlockSpec((1, tq, NA_DIM), lambda b, s, cls: (b, s, 0)),
                      pl.BlockSpec((1, T, NA_DIM), lambda b, s, cls: (b, 0, 1)),
                      pl.BlockSpec((1, T, NA_DIM), lambda b, s, cls: (b, 0, 2)),
                      pl.BlockSpec((1,) + table.shape[1:], lambda b, s, cls: (cls[s], 0, 0, 0))],
            out_specs=pl.BlockSpec((1, tq, NA_DIM), lambda b, s, cls: (b, s, 0))),
        compiler_params=_params("arbitrary", "arbitrary"),
        name="na_attention",
    )(step_class, pa, pa, pa, table)


def _rms_pair(x, gain):
    low = _low_half()
    sq = x * x
    s_lo = jnp.sum(jnp.where(low, sq, 0.0), axis=-1, keepdims=True)
    s_hi = jnp.sum(jnp.where(low, 0.0, sq), axis=-1, keepdims=True)
    ms = jnp.where(low, s_lo, s_hi) * (1.0 / HEAD_DIM)
    return x * lax.rsqrt(ms + NORM_EPS) * gain


def _rope_pair(x, cos, sin_signed):
    even = lax.broadcasted_iota(jnp.int32, (1, LANES), 1) % 2 == 0
    partner = jnp.where(even, pltpu.roll(x, LANES - 1, axis=1), pltpu.roll(x, 1, axis=1))
    return x * cos + partner * sin_signed


def _gqa_body(q_ref, k_ref, v_ref, cos_q, sin_q, cos_k, sin_k, gain_ref, o_ref, kn_ref, vn_ref, *, n_ctx):
    s = pl.program_id(1)
    tq = q_ref.shape[1]

    @pl.when(s == 0)
    def _():
        kn = _rope_pair(_rms_pair(k_ref[0], gain_ref[1:2, :]), cos_k[...], sin_k[...])
        vn = v_ref[0]
        low = _low_half()
        for src, dst in ((kn, kn_ref), (vn, vn_ref)):
            swapped = pltpu.roll(src, LANES // 2, axis=1)
            dst[0] = jnp.where(low, src, swapped).astype(BF16)
            dst[1] = jnp.where(low, swapped, src).astype(BF16)

    def run(n_keys):
        n_tiles = GQA_Q_DIM // LANES
        for t in range(n_tiles):
            g = t * GQA_KV_HEADS // n_tiles
            k, v = kn_ref[g, 0:n_keys, :], vn_ref[g, 0:n_keys, :]
            q = q_ref[0, :, t * LANES:(t + 1) * LANES]
            qn = _rope_pair(_rms_pair(q, gain_ref[0:1, :]), cos_q[...], sin_q[...]).astype(BF16)
            o_ref[0, :, t * LANES:(t + 1) * LANES] = _pair_attention(qn, [(k, v, None, None)]).astype(o_ref.dtype)

    @pl.when(s < n_ctx // tq)
    def _():
        run(n_ctx)

    @pl.when(s >= n_ctx // tq)
    def _():
        run(kn_ref.shape[1])


def _axial_rope(n_tokens):
    t = jnp.arange(n_tokens)
    row = (t // GRID_W).astype(F32)
    col = (t % GRID_W).astype(F32)
    inv = ROPE_THETA ** (-jnp.arange(0, ROPE_AXIS_DIM, 2, dtype=F32) / ROPE_AXIS_DIM)
    ang = jnp.concatenate([row[:, None] * inv, col[:, None] * inv], -1)
    return jnp.cos(ang), jnp.sin(ang)


def _gqa_rope_tables(T, n_ctx):
    cos, sin = _axial_rope(T - n_ctx)
    cos = jnp.concatenate([jnp.ones((n_ctx, ROPE_AXIS_DIM), F32), cos], 0)
    sin = jnp.concatenate([jnp.zeros((n_ctx, ROPE_AXIS_DIM), F32), sin], 0)
    cos = jnp.tile(jnp.repeat(cos, 2, axis=-1), (1, 2))
    sign = jnp.tile(jnp.array([-1.0, 1.0], F32), LANES // 2)
    sin = jnp.tile(jnp.repeat(sin, 2, axis=-1), (1, 2)) * sign
    return cos, sin


def _gqa_attention(pb, qk_gain, n_ctx):
    B, T, _ = pb.shape
    tq = GQA_Q_TILE
    cos, sin = _gqa_rope_tables(T, n_ctx)
    gain = jnp.tile(qk_gain, (1, 2))
    kv_blk = GQA_Q_DIM // GQA_KV_DIM
    return pl.pallas_call(
        functools.partial(_gqa_body, n_ctx=n_ctx),
        out_shape=jax.ShapeDtypeStruct((B, T, GQA_Q_DIM), BF16),
        grid=(B, T // tq),
        in_specs=[pl.BlockSpec((1, tq, GQA_Q_DIM), lambda b, s: (b, s, 0)),
                  pl.BlockSpec((1, T, GQA_KV_DIM), lambda b, s: (b, 0, kv_blk)),
                  pl.BlockSpec((1, T, GQA_KV_DIM), lambda b, s: (b, 0, kv_blk + 1)),
                  pl.BlockSpec((tq, LANES), lambda b, s: (s, 0)),
                  pl.BlockSpec((tq, LANES), lambda b, s: (s, 0)),
                  pl.BlockSpec((T, LANES), lambda b, s: (0, 0)),
                  pl.BlockSpec((T, LANES), lambda b, s: (0, 0)),
                  pl.BlockSpec((2, LANES), lambda b, s: (0, 0))],
        out_specs=pl.BlockSpec((1, tq, GQA_Q_DIM), lambda b, s: (b, s, 0)),
        scratch_shapes=[pltpu.VMEM((GQA_KV_HEADS, T, GQA_KV_DIM), BF16), pltpu.VMEM((GQA_KV_HEADS, T, GQA_KV_DIM), BF16)],
        compiler_params=_params("arbitrary", "arbitrary"),
        name="gqa_attention",
    )(pb, pb, pb, cos, sin, cos, sin, gain)


def _attention_mixers(x, modtab, w_in, rpb, qk_gain, n_ctx):
    w = w_in.astype(BF16)
    pa, pb = _in_proj(x, modtab, w, ((0, 3 * NA_DIM), (3 * NA_DIM, w.shape[1])), (BF16, F32))
    return jnp.concatenate([_na_attention(pa, rpb, n_ctx), _gqa_attention(pb, qk_gain, n_ctx)], -1)


def _log_sigmoid(x):
    return jnp.minimum(x, 0.0) - jnp.log(1.0 + jnp.exp(-jnp.abs(x)))


def _mlstm_body(xf, gcf, grf, ktf, xb, gcb, grb, ktb, bias_c, bias_r, hf_ref, hb_ref, c_ref, m_ref):
    @pl.when(pl.program_id(1) == 0)
    def _():
        c_ref[...] = jnp.zeros_like(c_ref)
        m_ref[...] = jnp.zeros_like(m_ref)

    L, H, W = MLSTM_CHUNK, MLSTM_HEADS, MLSTM_HEAD
    row = lax.broadcasted_iota(jnp.int32, (L, L), 0)
    col = lax.broadcasted_iota(jnp.int32, (L, L), 1)
    hi = lax.Precision.HIGHEST
    dirs = ((xf, gcf, grf, hf_ref, ktf), (xb, gcb, grb, hb_ref, ktb))
    nb = xf.shape[0]
    part = lambda x_ref, bi, j, h: x_ref[bi, :, j * MLSTM_DIM + h * W:j * MLSTM_DIM + (h + 1) * W]
    combos = [(bi, d, h) for bi in range(nb) for d in range(2) for h in range(H)]
    head = lambda h: slice(h * W, (h + 1) * W)
    lanes = lambda column: jnp.broadcast_to(column, (L, LANES))
    gates = {}
    for bi, (d, (_, gc_ref, gr_ref, _, _)) in [(bi, dd) for bi in range(nb) for dd in enumerate(dirs)]:
        seen = (col <= row) if d == 0 else (col >= row)
        seen_f = jnp.where(seen, 1.0, 0.0)
        g_col = gc_ref[bi] + bias_c[...]
        g_row = gr_ref[bi, 0] + bias_r[...]
        b_col = jnp.dot(seen_f, _log_sigmoid(g_col), preferred_element_type=F32, precision=hi)
        b_row = lax.dot_general(_log_sigmoid(g_row), seen_f, (((1,), (1,)), ((), ())), preferred_element_type=F32,
                                precision=hi)
        b_under_i = pltpu.roll(b_col, LANES - H, axis=1)
        run = g_col - b_under_i
        tok = lax.broadcasted_iota(jnp.int32, (L, LANES), 0)
        shift = 1
        while shift < L:
            if d == 0:
                run = jnp.maximum(run, jnp.where(tok >= shift, pltpu.roll(run, shift, axis=0), -jnp.inf))
            else:
                run = jnp.maximum(run, jnp.where(tok < L - shift, pltpu.roll(run, L - shift, axis=0), -jnp.inf))
            shift *= 2
        gates[bi, d] = (seen, g_col, g_row, b_col, b_row, b_under_i + run)
    first = {}
    for bi, d, h in combos:
        x_ref = dirs[d][0]
        s_idx = (bi * 2 + d) * H + h
        q = (part(x_ref, bi, 0, h) * W ** -0.5).astype(BF16)
        state = c_ref[s_idx]
        qk = lax.dot_general(q, part(x_ref, bi, 1, h).astype(BF16), (((1,), (1,)), ((), ())),
                             preferred_element_type=F32)
        q_state = jnp.dot(q, state.astype(BF16), preferred_element_type=F32)
        first[bi, d, h] = (dirs[d][4][bi, 0, head(h), :].astype(BF16), state, qk, q_state)
    second = {}
    for bi, d, h in combos:
        seen, g_col, g_row, b_col, b_row, m_intra = gates[bi, d]
        _, _, qk, q_state = first[bi, d, h]
        v = part(dirs[d][0], bi, 2, h)
        gi, gf = d * 2 * H + h, d * 2 * H + H + h
        s_idx = (bi * 2 + d) * H + h
        m_state = m_ref[s_idx:s_idx + 1, :]
        b_t, i_t, mi_t = lanes(b_col[:, gf:gf + 1]), lanes(g_col[:, gi:gi + 1]), lanes(m_intra[:, gi:gi + 1])
        b_end = b_t[L - 1:L, :] if d == 0 else b_t[0:1, :]
        d_inter = b_t + m_state
        m_t = jnp.maximum(d_inter, mi_t)
        d_intra = jnp.where(seen, b_t[:, :L] - b_row[gf:gf + 1, :] + g_row[gi:gi + 1, :], -jnp.inf)
        s = qk * jnp.exp(d_intra - m_t[:, :L])
        w_inter = jnp.exp(d_inter - m_t)
        d_state = b_end - b_t + i_t
        m_new = jnp.maximum(b_end + m_state, jnp.max(d_state, axis=0, keepdims=True))
        w_s = jnp.exp(d_state - m_new)
        w_c = jnp.exp(b_end + m_state - m_new)
        ones = jnp.ones((L, LANES), BF16)
        second[bi, d, h] = (s.astype(BF16), jnp.concatenate([v.astype(BF16), ones], axis=1),
                            jnp.concatenate([(v * w_s).astype(BF16), w_s.astype(BF16)], axis=1),
                            w_inter, jnp.exp(-m_t), jnp.concatenate([w_c, w_c], axis=1), m_new)
    writes = []
    for bi, d, h in combos:
        s16, v_ones, vw_ws, w_inter, floor, w_c, m_new = second[bi, d, h]
        kt16, state, _, q_state = first[bi, d, h]
        s_idx = (bi * 2 + d) * H + h
        sv = jnp.dot(s16, v_ones, preferred_element_type=F32)
        num = sv[:, :W] + w_inter * q_state[:, :W]
        den = sv[:, W:] + w_inter * q_state[:, W:]
        writes.append((dirs[d][3].at[bi, :, head(h)], num / jnp.maximum(jnp.abs(den), floor)))
        writes.append((c_ref.at[s_idx], w_c * state + jnp.dot(kt16, vw_ws, preferred_element_type=F32)))
        writes.append((m_ref.at[s_idx:s_idx + 1, :], m_new))
    for ref, value in writes:
        ref[...] = value


def _mlstm(pm, gates, gate_b, n_ctx):
    B, T, _ = pm.shape
    L = MLSTM_CHUNK
    nc, ncc = T // L, n_ctx // L
    ng = 4 * MLSTM_HEADS
    g_cols = jnp.pad(gates, ((0, 0), (0, 0), (0, LANES - ng)))
    g_rows = jnp.swapaxes(gates.reshape(B, nc, L, ng), 2, 3)
    k_t = jnp.swapaxes(pm[..., MLSTM_DIM:2 * MLSTM_DIM].reshape(B, nc, L, MLSTM_DIM), 2, 3)
    nb = MLSTM_BATCH
    assert B % nb == 0
    ktr = lambda order: pl.BlockSpec((nb, 1, MLSTM_DIM, L), lambda b, c: (b, order(c), 0, 0))
    bias = gate_b.reshape(ng)
    bias_c = jnp.pad(bias, (0, LANES - ng)).reshape(1, LANES)
    bias_r = bias.reshape(ng, 1)
    fwd = lambda c: c
    bwd = lambda c: jnp.where(c < ncc, ncc - 1 - c, nc - 1 - (c - ncc))
    blk = (nb, L, MLSTM_DIM)
    seq = lambda order: pl.BlockSpec((nb, L, 3 * MLSTM_DIM), lambda b, c: (b, order(c), 0))
    gcol = lambda order: pl.BlockSpec((nb, L, LANES), lambda b, c: (b, order(c), 0))
    grow = lambda order: pl.BlockSpec((nb, 1, ng, L), lambda b, c: (b, order(c), 0, 0))
    n_state = nb * 2 * MLSTM_HEADS
    out_sds = jax.ShapeDtypeStruct((B, T, MLSTM_DIM), F32)
    return pl.pallas_call(
        _mlstm_body,
        out_shape=(out_sds, out_sds),
        grid=(B // nb, nc),
        in_specs=[seq(fwd), gcol(fwd), grow(fwd), ktr(fwd), seq(bwd), gcol(bwd), grow(bwd), ktr(bwd),
                  pl.BlockSpec((1, LANES), lambda b, c: (0, 0)), pl.BlockSpec((ng, 1), lambda b, c: (0, 0))],
        out_specs=(pl.BlockSpec(blk, lambda b, c: (b, fwd(c), 0)), pl.BlockSpec(blk, lambda b, c: (b, bwd(c), 0))),
        scratch_shapes=[pltpu.VMEM((n_state, MLSTM_HEAD, MLSTM_HEAD + LANES), F32), pltpu.VMEM((n_state, LANES), F32)],
        compiler_params=_params("arbitrary", "arbitrary"),
        name="mlstm",
    )(pm, g_cols, g_rows, k_t, pm, g_cols, g_rows, k_t, bias_c, bias_r)


def _pair_sums(x):
    low = _low_half()
    s_lo = jnp.sum(jnp.where(low, x, 0.0), axis=-1, keepdims=True)
    s_hi = jnp.sum(jnp.where(low, 0.0, x), axis=-1, keepdims=True)
    return jnp.where(low, s_lo, s_hi)


def _rwkv_prep_body(p_ref, prev_ref, next_ref, tab_ref, tab2_ref, w1_ref, w2_ref, a1_ref, a2_ref, g1_ref, g2_ref,
                    rv_ref, n_ref, w_ref, k_ref, b_ref, gate_ref, *, n_ctx):
    s = pl.program_id(1)
    tm = p_ref.shape[1]
    C = RWKV_DIM
    ctx_tiles = n_ctx // tm
    x = p_ref[0]
    has_prev = jnp.logical_and(s != 0, s != ctx_tiles)
    has_next = jnp.logical_and(s != ctx_tiles - 1, s != pl.num_programs(1) - 1)
    prev_row = jnp.where(has_prev, prev_ref[0, SUBLANES - 1:SUBLANES, :], 0.0)
    next_row = jnp.where(has_next, next_ref[0, 0:1, :], 0.0)
    rowid = lax.broadcasted_iota(jnp.int32, (tm, 1), 0)
    up = jnp.where(rowid == 0, prev_row, pltpu.roll(x, 1, axis=0))
    dn = jnp.where(rowid == tm - 1, next_row, pltpu.roll(x, tm - 1, axis=0))
    d = 0.5 * (up + dn) - x
    part = lambda a, i: a[:, i * C:(i + 1) * C]
    mu = lambda i: tab_ref[i:i + 1, :]
    r = part(x, 0) + part(d, 0) * mu(0)
    k = part(x, 1) + part(d, 1) * mu(1)
    v = part(x, 2) + part(d, 2) * mu(2)
    z, dz = part(x, 3), part(d, 3)
    z_w, z_a, z_g = (z + dz * mu(3)).astype(BF16), (z + dz * mu(4)).astype(BF16), (z + dz * mu(5)).astype(BF16)
    lora = lambda t, w: jnp.dot(t.astype(BF16), w[...], preferred_element_type=F32)
    w_pre = tab2_ref[0:1, :] + lora(jnp.tanh(lora(z_w, w1_ref)), w2_ref)
    neg = -w_pre
    softplus = jnp.maximum(neg, 0.0) + jnp.log(1.0 + jnp.exp(-jnp.abs(neg)))
    decay = jnp.exp(-jnp.exp(-softplus - 0.5))
    iclr = jax.nn.sigmoid(tab2_ref[1:2, :] + lora(lora(z_a, a1_ref), a2_ref))
    gate_ref[0] = lora(jax.nn.sigmoid(lora(z_g, g1_ref)), g2_ref)
    kk = k * tab_ref[6:7, :]
    kk = jnp.concatenate(
        [kk[:, t * LANES:(t + 1) * LANES]
         * lax.rsqrt(jnp.maximum(_pair_sums(jnp.square(kk[:, t * LANES:(t + 1) * LANES])), 1e-24))
         for t in range(C // LANES)], axis=1)
    k2, kk2 = jnp.concatenate([k, k], axis=1), jnp.concatenate([kk, kk], axis=1)
    rv_ref[0] = jnp.concatenate([r, v], axis=1)
    n_ref[0] = -kk2
    w_ref[0] = decay
    k_ref[0] = k2 * (1.0 + (iclr - 1.0) * tab2_ref[2:3, :])
    b_ref[0] = kk2 * iclr


def _rwkv_prep(pr, n_ctx, mu, w0, w1, w2, a0, a1, a2, g1, g2, kvec):
    B, T, _ = pr.shape
    tm, C = TOKEN_TILE, RWKV_DIM
    per_tile = tm // SUBLANES
    tab = jnp.concatenate([mu, kvec[0:1], jnp.zeros((1, C), F32)], 0)
    cat = lambda a: jnp.concatenate([a[0], a[1]], -1)
    tab2 = jnp.concatenate([cat(w0)[None], cat(a0)[None], jnp.tile(kvec[1], 2)[None], jnp.zeros((5, 2 * C), F32)], 0)
    blockdiag = lambda a: jnp.concatenate([jnp.pad(a[0], ((0, 0), (0, C))), jnp.pad(a[1], ((0, 0), (C, 0)))], 0)
    gl = g1.shape[1]
    consts = (tab, tab2, cat(w1).astype(BF16), blockdiag(w2).astype(BF16), cat(a1).astype(BF16),
              blockdiag(a2).astype(BF16), jnp.pad(g1, ((0, 0), (0, LANES - gl))).astype(BF16),
              jnp.pad(g2, ((0, LANES - gl), (0, 0))).astype(BF16))
    const = lambda a: pl.BlockSpec(a.shape, lambda b, s: (0, 0))
    one = jax.ShapeDtypeStruct((B, T, C), F32)
    two = jax.ShapeDtypeStruct((B, T, 2 * C), F32)
    spec1 = pl.BlockSpec((1, tm, C), lambda b, s: (b, s, 0))
    spec2 = pl.BlockSpec((1, tm, 2 * C), lambda b, s: (b, s, 0))
    return pl.pallas_call(
        functools.partial(_rwkv_prep_body, n_ctx=n_ctx),
        out_shape=(two, two, two, two, two, one),
        grid=(B, T // tm),
        in_specs=[pl.BlockSpec((1, tm, 4 * C), lambda b, s: (b, s, 0)),
                  pl.BlockSpec((1, SUBLANES, 4 * C), lambda b, s: (b, jnp.maximum(s * per_tile - 1, 0), 0)),
                  pl.BlockSpec((1, SUBLANES, 4 * C), lambda b, s: (b, jnp.minimum((s + 1) * per_tile, T // SUBLANES - 1), 0))]
        + [const(a) for a in consts],
        out_specs=(spec2, spec2, spec2, spec2, spec2, spec1),
        compiler_params=_params("arbitrary", "arbitrary"),
        name="rwkv_prep",
    )(pr, pr, pr, *consts)


def _rec_post_body(y_ref, r_ref, v_ref, k_ref, gate_ref, hf_ref, hb_ref, o_ref, tab_ref, out_ref):
    C = RWKV_DIM
    for t in range(C // LANES):
        sl = slice(t * LANES, (t + 1) * LANES)
        y = y_ref[0, :, sl]
        yc = y - _pair_sums(y) * (1.0 / RWKV_HEAD)
        var = _pair_sums(yc * yc) * (1.0 / RWKV_HEAD)
        yn = yc * lax.rsqrt(var + RWKV_GN_EPS) * tab_ref[0:1, sl] + tab_ref[1:2, sl]
        k_sum = k_ref[0, :, sl] + k_ref[0, :, C + t * LANES:C + (t + 1) * LANES]
        bonus = _pair_sums(r_ref[0, :, sl] * k_sum * tab_ref[2:3, sl]) * v_ref[0, :, sl]
        out_ref[0, :, sl] = ((yn + bonus) * gate_ref[0, :, sl]).astype(out_ref.dtype)
    for t in range(MLSTM_HEADS):
        sl = slice(t * MLSTM_HEAD, (t + 1) * MLSTM_HEAD)
        h = hf_ref[0, :, sl] + hb_ref[0, :, sl]
        hn = h * lax.rsqrt(jnp.mean(h * h, axis=-1, keepdims=True) + NORM_EPS) * tab_ref[3:4, sl]
        out_ref[0, :, C + t * MLSTM_HEAD:C + (t + 1) * MLSTM_HEAD] = (hn * jax.nn.sigmoid(o_ref[0, :, sl])).astype(out_ref.dtype)


def _rec_post(y, rv, k_eff, gate, h_f, h_b, pm, gn, r_k, norm_g):
    B, T, C = y.shape
    tm = TOKEN_TILE
    tab = jnp.concatenate([gn, r_k.reshape(1, C), norm_g.reshape(1, C), jnp.zeros((4, C), F32)], 0)
    spec1 = pl.BlockSpec((1, tm, C), lambda b, s: (b, s, 0))
    return pl.pallas_call(
        _rec_post_body,
        out_shape=jax.ShapeDtypeStruct((B, T, C + MLSTM_DIM), BF16),
        grid=(B, T // tm),
        in_specs=[spec1, spec1, pl.BlockSpec((1, tm, C), lambda b, s: (b, s, 1)),
                  pl.BlockSpec((1, tm, 2 * C), lambda b, s: (b, s, 0)), spec1, spec1, spec1,
                  pl.BlockSpec((1, tm, MLSTM_DIM), lambda b, s: (b, s, 3)), pl.BlockSpec(tab.shape, lambda b, s: (0, 0))],
        out_specs=pl.BlockSpec((1, tm, C + MLSTM_DIM), lambda b, s: (b, s, 0)),
        compiler_params=_params("arbitrary", "arbitrary"),
        name="rec_post",
    )(y, rv, rv, k_eff, gate, h_f, h_b, pm, tab)


def _to_state_lanes(x):
    B, T, _ = x.shape
    return jnp.transpose(x.reshape(B, T, 2, RWKV_HEADS, RWKV_HEAD), (1, 4, 2, 0, 3)).reshape(T, RWKV_HEAD, LANES)


def _recurrent_mixers(x, modtab, w_in, n_ctx, mu, w0, w1, w2, a0, a1, a2, g1, g2, kvec, r_k, gn, gate_b, norm_g):
    B, T, _ = x.shape
    assert 2 * B * RWKV_HEADS == LANES
    n_main = RWKV_IN + 4 * MLSTM_DIM
    w = jnp.pad(w_in, ((0, 0), (0, n_main + LANES - w_in.shape[1]))).astype(BF16)
    pr, pm, pg = _in_proj(x, modtab, w, ((0, RWKV_IN), (RWKV_IN, n_main), (n_main, n_main + LANES)), (F32, F32, F32))
    rv, kkn, decay, k_eff, kka, gate = _rwkv_prep(pr, n_ctx, mu, w0, w1, w2, a0, a1, a2, g1, g2, kvec)
    scan_in = tuple(_to_state_lanes(a) for a in (rv, decay, k_eff, kkn, kka))
    h_f, h_b = _mlstm(pm, pg[..., :4 * MLSTM_HEADS], gate_b, n_ctx)
    scan_in, h_f, h_b = lax.optimization_barrier((scan_in, h_f, h_b))
    yf, yb = _rwkv_scan(scan_in, n_ctx)
    half = LANES // 2
    y = yf[:, :, :half] + yb[:, :, half:]
    y = jnp.transpose(y.reshape(T, RWKV_HEAD, B, RWKV_HEADS), (2, 0, 3, 1)).reshape(B, T, RWKV_DIM)
    return _rec_post(y, rv, k_eff, gate, h_f, h_b, pm, gn, r_k, norm_g)


def kernel(x, c, ctx, c_ctx, ada_w, ada_b, ln_g, ln_b, mix_w_out, att_w_in, na_rpb, qk_gain, rec_w_in, rwkv_mu, rwkv_w0, rwkv_w1, rwkv_w2, rwkv_a0, rwkv_a1, rwkv_a2, rwkv_g1, rwkv_g2, rwkv_kvec, rwkv_rk, rwkv_gn, mlstm_gate_b, mlstm_norm, moe_router, moe_bias, moe_w1, moe_w3, moe_w2, shared_w1, shared_w3, shared_w2):
    B, S, D = x.shape
    n_ctx = ctx.shape[1]
    assert D == D_MODEL and n_ctx % TOKEN_TILE == 0 and S % TOKEN_TILE == 0
    xs = jnp.concatenate([ctx, x], axis=1)
    mods = _ada_modulation(c, c_ctx, ada_w, ada_b)
    for i in range(DEPTH):
        last = i == DEPTH - 1
        j = i // 2
        mod = mods[i, :B].reshape(B, 6, D)
        mod_c = jnp.broadcast_to(mods[i, B].reshape(1, 6, D), (B, 6, D))
        modtab = jnp.pad(jnp.stack([mod_c, mod], axis=1), ((0, 0), (0, 0), (0, MOD_ROWS - 6), (0, 0)))
        if i % 2 == 0:
            m = _attention_mixers(xs, modtab, att_w_in[j], na_rpb[j], qk_gain[j], n_ctx)
        else:
            m = _recurrent_mixers(xs, modtab, rec_w_in[j], n_ctx, rwkv_mu[j], rwkv_w0[j], rwkv_w1[j], rwkv_w2[j],
                                  rwkv_a0[j], rwkv_a1[j], rwkv_a2[j], rwkv_g1[j], rwkv_g2[j], rwkv_kvec[j],
                                  rwkv_rk[j], rwkv_gn[j], mlstm_gate_b[j], mlstm_norm[j])
        first_tile = n_ctx // TOKEN_TILE if last else 0
        x1, h2 = _out_proj(m, xs, modtab, mix_w_out[i].astype(BF16), jnp.stack([ln_g[i, 0], ln_b[i, 0]]), first_tile)
        xs = _moe_block(h2, x1, modtab, jnp.stack([ln_g[i, 1], ln_b[i, 1]]), first_tile, i, moe_router[i], moe_bias[i],
                        moe_w1, moe_w3, moe_w2, shared_w1[i], shared_w3[i], shared_w2[i])
    return xs
```
